```python
import math
import jax, jax.numpy as jnp
from jax import lax
import numpy as np

D_MODEL = 4096
BATCH = 4
SEQ = 2048
DEPTH = 2
DEC_BATCH = 8
DEC_SEQ = 8
PAST_LEN = 16384
PAGE_SIZE = 128

HEAD_DIM = 128
MIX_WIDTH = D_MODEL
RET_WIDTH = MIX_WIDTH // 4
RET_HEADS = RET_WIDTH // HEAD_DIM
RET_CHUNK = 128
S5_WIDTH = MIX_WIDTH // 4
S5_GROUP = 16
S5_GROUPS = S5_WIDTH // S5_GROUP
S5_STATE = 64
NSA_WIDTH = MIX_WIDTH - RET_WIDTH - S5_WIDTH
NSA_HEADS = NSA_WIDTH // HEAD_DIM
NSA_KV_HEADS = 4
NSA_HPG = NSA_HEADS // NSA_KV_HEADS
KV_WIDTH = NSA_KV_HEADS * HEAD_DIM
CMP_BLOCK = 32
CMP_STRIDE = 16
SEL_BLOCK = 64
SEL_TOP = 16
WINDOW = 512
SEL_Q_BLOCK = 32
WIN_Q_BLOCK = 128
FORCE_SCORE = 1e4
NEG = -1e30
N_BUCKETS = 32
MAX_DISTANCE = 128
D_FF = 4 * D_MODEL
EPS = 1e-6
IN_SPLITS = (RET_WIDTH, RET_WIDTH, RET_WIDTH, RET_WIDTH, S5_WIDTH, NSA_WIDTH,
             KV_WIDTH, KV_WIDTH, KV_WIDTH, KV_WIDTH, KV_WIDTH, KV_WIDTH, 3 * NSA_HEADS)
IN_COLS = sum(IN_SPLITS)

kernel_name = "hymba_retnet_s5_nsa_decode_step"

F32 = jnp.float32


def rms_norm(x, gain):
    xf = x.astype(F32)
    xf = xf * lax.rsqrt(jnp.mean(xf * xf, axis=-1, keepdims=True) + EPS)
    return xf * gain.astype(F32)


def largest_divisor(n, cap):
    return max(d for d in range(1, min(n, cap) + 1) if n % d == 0)


def t5_bucket(dist):
    max_exact = N_BUCKETS // 2
    d = jnp.maximum(dist, 0)
    large = max_exact + (jnp.log(jnp.maximum(d, 1).astype(F32) / max_exact)
                         / math.log(MAX_DISTANCE / max_exact) * (N_BUCKETS - max_exact)).astype(jnp.int32)
    return jnp.where(d < max_exact, d, jnp.minimum(large, N_BUCKETS - 1))


def rotary(x, pos):
    half = HEAD_DIM // 2
    inv = 1.0 / (10000.0 ** jnp.linspace(0.0, 1.0, half, dtype=F32))
    ang = pos.astype(F32)[:, None] * inv[None]
    cos, sin = jnp.cos(ang)[None, :, None, :], jnp.sin(ang)[None, :, None, :]
    x1, x2 = x[..., :half], x[..., half:]
    return jnp.concatenate([x1 * cos - x2 * sin, x1 * sin + x2 * cos], axis=-1)


def retention(q, k, v, s0, q0):
    B, T = q.shape[:2]
    pos = q0 + jnp.arange(T)
    q = rotary(q, pos)
    k = rotary(k, pos) * HEAD_DIM ** -0.5
    lg = jnp.log1p(-(2.0 ** (-5.0 - jnp.arange(RET_HEADS, dtype=F32))))
    c = largest_divisor(T, RET_CHUNK)
    n = T // c
    i = jnp.arange(c)
    rel = i[:, None] - i[None, :]
    decay = jnp.where(rel[None] >= 0, jnp.exp(jnp.maximum(rel, 0)[None] * lg[:, None, None]), 0.0)
    q_dec = jnp.exp((i + 1)[:, None] * lg[None])[None, :, :, None]
    k_dec = jnp.exp((c - 1 - i)[:, None] * lg[None])[None, :, :, None]
    chunk_dec = jnp.exp(c * lg)[None, :, None, None]

    def to_chunks(a):
        return a.reshape(B, n, c, RET_HEADS, HEAD_DIM).swapaxes(0, 1)

    def step(s, qkv):
        qc, kc, vc = qkv
        inner = jnp.einsum('bihd,bjhd->bhij', qc, kc) * decay
        o = (jnp.einsum('bhij,bjhv->bihv', inner, vc)
             + jnp.einsum('bihd,bhdv->bihv', qc, s) * q_dec)
        s = s * chunk_dec + jnp.einsum('bjhd,bjhv->bhdv', kc * k_dec, vc)
        return s, o

    s, o = lax.scan(step, s0.astype(F32), (to_chunks(q), to_chunks(k), to_chunks(v)))
    return o.swapaxes(0, 1).reshape(B, T, RET_HEADS, HEAD_DIM), s


def s5_scan(u, x0, lam_re, lam_im, log_step, b_re, b_im, c_re, c_im, d):
    B, T = u.shape[:2]
    ug = u.reshape(B, T, S5_GROUPS, S5_GROUP).astype(jnp.complex64)
    lam = lax.complex(lam_re.astype(F32), lam_im.astype(F32))
    a_bar = jnp.exp(lam * jnp.exp(log_step.astype(F32))[:, None])
    b_bar = ((a_bar - 1.0) / lam)[..., None] * lax.complex(b_re.astype(F32), b_im.astype(F32))
    bu = jnp.einsum('gpc,btgc->btgp', b_bar, ug)
    init = lax.complex(x0[..., 0].astype(F32), x0[..., 1].astype(F32))
    bu = bu.at[:, 0].add(a_bar[None] * init)
    a = jnp.broadcast_to(a_bar, bu.shape)

    def combine(e1, e2):
        a1, b1 = e1
        a2, b2 = e2
        return a1 * a2, a2 * b1 + b2

    _, xs = lax.associative_scan(combine, (a, bu), axis=1)
    cm = lax.complex(c_re.astype(F32), c_im.astype(F32))
    y = jnp.einsum('gcp,btgp->btgc', cm, xs).real.reshape(B, T, S5_WIDTH) + d.astype(F32) * u
    x_last = xs[:, -1]
    return y, jnp.stack([x_last.real, x_last.imag], axis=-1)


def compress(x, pe, w1, w2):
    B, L = x.shape[:2]
    r = CMP_BLOCK // CMP_STRIDE
    n_full = L // CMP_STRIDE
    n_cmp = n_full - r + 1
    xs = x[:, :n_full * CMP_STRIDE].reshape(B, n_full, CMP_STRIDE, NSA_KV_HEADS, HEAD_DIM)
    w1 = w1.astype(F32)
    w1r = w1.reshape(r, CMP_STRIDE, HEAD_DIM, HEAD_DIM)
    h = jnp.einsum('ld,ldh->h', pe.astype(F32), w1)
    for j in range(r):
        h = h + jnp.einsum('bnsgd,sdh->bngh', xs[:, j:j + n_cmp], w1r[j])
    return jnp.einsum('bngh,he->bnge', jax.nn.gelu(h), w2.astype(F32))


def nsa(q, k_cmp, v_cmp, k_slc, v_slc, k_win, v_win, gates, q0, rel_bias,
        pe_k, w1_k, w2_k, pe_v, w1_v, w2_v):
    B, T = q.shape[:2]
    G, J = NSA_KV_HEADS, NSA_HPG
    Lk = q0 + T
    qg = (q.astype(F32) * HEAD_DIM ** -0.5).reshape(B, T, G, J, HEAD_DIM)
    t_pos = q0 + jnp.arange(T)
    table = rel_bias.astype(F32).reshape(N_BUCKETS, G, J)

    kc = compress(k_cmp.astype(F32), pe_k, w1_k, w2_k)
    vc = compress(v_cmp.astype(F32), pe_v, w1_v, w2_v)
    n_cmp = kc.shape[1]
    c_start = jnp.arange(n_cmp) * CMP_STRIDE
    dist = t_pos[:, None] - (c_start + CMP_BLOCK - 1)[None]
    valid = (dist >= 0)[:, None, None, :]
    bias = jnp.transpose(table[t5_bucket(dist)], (0, 2, 3, 1))
    s = jnp.einsum('btgjd,bngd->btgjn', qg, kc) + bias
    p_cmp = jax.nn.softmax(jnp.where(valid, s, NEG), axis=-1) * valid
    o_cmp = jnp.einsum('btgjn,bngd->btgjd', p_cmp, vc)

    n_sel = -(-Lk // SEL_BLOCK)
    s_start = jnp.arange(n_sel) * SEL_BLOCK
    overlap = ((c_start[:, None] < s_start[None] + SEL_BLOCK)
               & (c_start[:, None] + CMP_BLOCK > s_start[None])).astype(F32)
    p_sel = jnp.einsum('btgn,nm->btgm', p_cmp.sum(axis=3), overlap)
    cur = t_pos // SEL_BLOCK
    blk = jnp.arange(n_sel)
    forced = (blk[None] == 0) | (blk[None] == cur[:, None]) | (blk[None] == cur[:, None] - 1)
    blk_ok = (blk[None] <= cur[:, None])[None, :, None, :]
    score = jnp.where(blk_ok, p_sel + FORCE_SCORE * forced[None, :, None, :], NEG)
    n_top = min(SEL_TOP, n_sel)
    _, idx = lax.top_k(score, n_top)

    pad = n_sel * SEL_BLOCK - Lk

    def blocks(a):
        a = jnp.pad(a.astype(F32), ((0, 0), (0, pad), (0, 0), (0, 0)))
        return a.reshape(B, n_sel, SEL_BLOCK, G, HEAD_DIM).transpose(0, 3, 1, 2, 4)

    kb, vb = blocks(k_slc), blocks(v_slc)
    qb = largest_divisor(T, SEL_Q_BLOCK)
    nq = T // qb
    b_ix = jnp.arange(B)[:, None, None, None]
    g_ix = jnp.arange(G)[None, None, :, None]
    g_ix5 = g_ix[..., None]
    off = jnp.arange(SEL_BLOCK)

    def sel_block(args):
        qc, ic, tc = args
        kg = kb[b_ix, g_ix, ic]
        vg = vb[b_ix, g_ix, ic]
        kpos = ic[..., None] * SEL_BLOCK + off
        d = tc[None, :, None, None, None] - kpos
        ok = (d >= 0) & (ic <= (tc // SEL_BLOCK)[None, :, None, None])[..., None]
        bias_s = jnp.moveaxis(table[t5_bucket(d), g_ix5], -1, 3)
        sc = jnp.einsum('bqgjd,bqgkld->bqgjkl', qc, kg) + bias_s
        sc = jnp.where(ok[:, :, :, None], sc, NEG).reshape(B, qb, G, J, n_top * SEL_BLOCK)
        pr = jax.nn.softmax(sc, axis=-1).reshape(B, qb, G, J, n_top, SEL_BLOCK)
        return jnp.einsum('bqgjkl,bqgkld->bqgjd', pr, vg)

    o_slc = lax.map(sel_block, (qg.reshape(B, nq, qb, G, J, HEAD_DIM).swapaxes(0, 1),
                                idx.reshape(B, nq, qb, G, n_top).swapaxes(0, 1),
                                t_pos.reshape(nq, qb)))
    o_slc = o_slc.swapaxes(0, 1).reshape(B, T, G, J, HEAD_DIM)

    qw = largest_divisor(T, WIN_Q_BLOCK)
    nw = T // qw
    rows = jnp.arange(nw)[:, None] * qw + jnp.arange(qw + WINDOW)[None]
    kwin = k_win.astype(F32)[:, rows]
    vwin = v_win.astype(F32)[:, rows]
    kpos = q0 - WINDOW + rows
    dw = t_pos.reshape(nw, qw)[:, :, None] - kpos[:, None, :]
    okw = (kpos[:, None, :] >= 0) & (dw >= 0) & (dw <= WINDOW)
    bias_w = jnp.transpose(table[t5_bucket(dw)], (0, 3, 4, 1, 2))
    sw = jnp.einsum('bcqgjd,bckgd->bcgjqk', qg.reshape(B, nw, qw, G, J, HEAD_DIM), kwin) + bias_w
    pw = jax.nn.softmax(jnp.where(okw[:, None, None], sw, NEG), axis=-1)
    o_win = jnp.einsum('bcgjqk,bckgd->bcqgjd', pw, vwin).reshape(B, T, G, J, HEAD_DIM)

    g = jax.nn.sigmoid(gates.astype(F32)).reshape(B, T, G, J, 3)
    o = g[..., 0:1] * o_cmp + g[..., 1:2] * o_slc + g[..., 2:3] * o_win
    return o.reshape(B, T, NSA_WIDTH)


def block(x, q0, win_buf, ret_s0, s5_s0, win_k0, win_v0, past_cmp_k, past_cmp_v,
          past_slc_k, past_slc_v, lw, rel_bias):
    B, T, _ = x.shape
    G = NSA_KV_HEADS
    h = rms_norm(x, lw['norm_mix']).astype(x.dtype)
    proj = jnp.einsum('btd,de->bte', h, lw['w_in'])
    (rq, rk, rv, rg, su, nq_, kc, vc, ks, vs, kw, vw, ng) = jnp.split(
        proj, np.cumsum(IN_SPLITS)[:-1].tolist(), axis=-1)

    def heads(a, n):
        return a.reshape(B, T, n, HEAD_DIM)

    ro, ret_s = retention(heads(rq, RET_HEADS).astype(F32), heads(rk, RET_HEADS).astype(F32),
                          heads(rv, RET_HEADS).astype(F32), ret_s0, q0)
    ro = ro * lax.rsqrt(jnp.mean(ro * ro, axis=-1, keepdims=True) + EPS)
    ro = jax.nn.silu(rg.astype(F32)) * ro.reshape(B, T, RET_WIDTH)

    sy, s5_s = s5_scan(su.astype(F32), s5_s0, lw['s5_lambda_re'], lw['s5_lambda_im'], lw['s5_log_step'],
                       lw['s5_b_re'], lw['s5_b_im'], lw['s5_c_re'], lw['s5_c_im'], lw['s5_d'])
    sy = jax.nn.gelu(sy)
    so = sy * jax.nn.sigmoid(jnp.einsum('btc,ce->bte', sy, lw['s5_w_glu'].astype(F32)))

    kc, vc, ks, vs, kw, vw = (heads(a, G) for a in (kc, vc, ks, vs, kw, vw))
    k_cmp_all = jnp.concatenate([past_cmp_k.astype(kc.dtype), kc], axis=1)
    v_cmp_all = jnp.concatenate([past_cmp_v.astype(vc.dtype), vc], axis=1)
    k_slc_all = jnp.concatenate([past_slc_k.astype(ks.dtype), ks], axis=1)
    v_slc_all = jnp.concatenate([past_slc_v.astype(vs.dtype), vs], axis=1)
    k_win_all = jnp.concatenate([win_k0.astype(kw.dtype), kw], axis=1)
    v_win_all = jnp.concatenate([win_v0.astype(vw.dtype), vw], axis=1)
    no = nsa(heads(nq_, NSA_HEADS), k_cmp_all, v_cmp_all, k_slc_all, v_slc_all, k_win_all, v_win_all,
             ng.reshape(B, T, NSA_HEADS, 3), q0, rel_bias,
             lw['cmp_pe_k'], lw['cmp_w1_k'], lw['cmp_w2_k'], lw['cmp_pe_v'], lw['cmp_w1_v'], lw['cmp_w2_v'])

    bn = lw['branch_norm']
    mix = jnp.concatenate([rms_norm(ro, bn[:RET_WIDTH]),
                           rms_norm(so, bn[RET_WIDTH:RET_WIDTH + S5_WIDTH]),
                           rms_norm(no, bn[RET_WIDTH + S5_WIDTH:])], axis=-1).astype(x.dtype)
    x = x + jnp.einsum('btm,md->btd', mix, lw['w_out']).astype(x.dtype)

    h = rms_norm(x, lw['norm_ffn']).astype(x.dtype)
    up = jnp.einsum('btd,df->btf', h, lw['w_up'])
    x = x + jnp.einsum('btf,fd->btd', jnp.square(jax.nn.relu(up)), lw['w_down']).astype(x.dtype)

    cmp_rows = jnp.stack([kc, vc], axis=1)
    slc_rows = jnp.stack([ks, vs], axis=1)
    win_new = jnp.stack([k_win_all[:, -win_buf:], v_win_all[:, -win_buf:]], axis=1)
    return x, cmp_rows, slc_rows, win_new, ret_s, s5_s


def gather_pages(pool, page_table, layer):
    pages = pool[page_table, layer]
    db, npg = page_table.shape
    pages = pages.transpose(0, 2, 1, 3, 4, 5).reshape(db, 2, npg * PAGE_SIZE, NSA_KV_HEADS, HEAD_DIM)
    return pages[:, 0], pages[:, 1]


def setup_inputs(seed: int = 0) -> dict:
    key = jax.random.key(seed)
    ks = iter(jax.random.split(key, 48))
    nrm = lambda shape, s: jax.random.normal(next(ks), shape, F32) * s
    n_pages = PAST_LEN // PAGE_SIZE
    n_pool = (5 * DEC_BATCH * n_pages + 3) // 4
    win_buf = min(WINDOW, PAST_LEN)
    page_table = jax.random.permutation(next(ks), n_pool)[:DEC_BATCH * n_pages].reshape(
        DEC_BATCH, n_pages).astype(jnp.int32)
    G, HD = NSA_KV_HEADS, HEAD_DIM
    n_idx = jnp.arange(S5_STATE, dtype=F32)
    return {
        'x_prompt': nrm((BATCH, SEQ, D_MODEL), 1.0),
        'x_sample': nrm((DEC_BATCH, DEC_SEQ, D_MODEL), 1.0),
        'cache_cmp': nrm((n_pool, DEPTH, 2, PAGE_SIZE, G, HD), 1.0),
        'cache_slc': nrm((n_pool, DEPTH, 2, PAGE_SIZE, G, HD), 1.0),
        'cache_win': nrm((DEPTH, DEC_BATCH, 2, win_buf, G, HD), 1.0),
        'state_ret': nrm((DEPTH, DEC_BATCH, RET_HEADS, HD, HD), 1.0),
        'state_s5': nrm((DEPTH, DEC_BATCH, S5_GROUPS, S5_STATE, 2), 0.5),
        'page_table': page_table,
        'rel_bias': nrm((N_BUCKETS, NSA_HEADS), 0.1),
        'norm_mix': 1.0 + nrm((DEPTH, D_MODEL), 0.01),
        'w_in': nrm((DEPTH, D_MODEL, IN_COLS), D_MODEL ** -0.5),
        's5_lambda_re': -0.5 + nrm((DEPTH, S5_GROUPS, S5_STATE), 0.01),
        's5_lambda_im': math.pi * n_idx + nrm((DEPTH, S5_GROUPS, S5_STATE), 0.01),
        's5_log_step': jax.random.uniform(next(ks), (DEPTH, S5_GROUPS), F32, math.log(1e-3), math.log(1e-1)),
        's5_b_re': nrm((DEPTH, S5_GROUPS, S5_STATE, S5_GROUP), (2 * S5_GROUP) ** -0.5),
        's5_b_im': nrm((DEPTH, S5_GROUPS, S5_STATE, S5_GROUP), (2 * S5_GROUP) ** -0.5),
        's5_c_re': nrm((DEPTH, S5_GROUPS, S5_GROUP, S5_STATE), (2 * S5_STATE) ** -0.5),
        's5_c_im': nrm((DEPTH, S5_GROUPS, S5_GROUP, S5_STATE), (2 * S5_STATE) ** -0.5),
        's5_d': nrm((DEPTH, S5_WIDTH), 1.0),
        's5_w_glu': nrm((DEPTH, S5_WIDTH, S5_WIDTH), S5_WIDTH ** -0.5),
        'cmp_pe_k': nrm((DEPTH, CMP_BLOCK, HD), 0.02),
        'cmp_w1_k': nrm((DEPTH, CMP_BLOCK, HD, HD), (CMP_BLOCK * HD) ** -0.5),
        'cmp_w2_k': nrm((DEPTH, HD, HD), HD ** -0.5),
        'cmp_pe_v': nrm((DEPTH, CMP_BLOCK, HD), 0.02),
        'cmp_w1_v': nrm((DEPTH, CMP_BLOCK, HD, HD), (CMP_BLOCK * HD) ** -0.5),
        'cmp_w2_v': nrm((DEPTH, HD, HD), HD ** -0.5),
        'branch_norm': 1.0 + nrm((DEPTH, MIX_WIDTH), 0.01),
        'w_out': nrm((DEPTH, MIX_WIDTH, D_MODEL), MIX_WIDTH ** -0.5),
        'norm_ffn': 1.0 + nrm((DEPTH, D_MODEL), 0.01),
        'w_up': nrm((DEPTH, D_MODEL, D_FF), D_MODEL ** -0.5),
        'w_down': nrm((DEPTH, D_FF, D_MODEL), D_FF ** -0.5),
        'norm_final': 1.0 + nrm((D_MODEL,), 0.01),
    }


def reference(x_prompt, x_sample, cache_cmp, cache_slc, cache_win, state_ret, state_s5, page_table,
              rel_bias, norm_mix, w_in, s5_lambda_re, s5_lambda_im, s5_log_step, s5_b_re, s5_b_im,
              s5_c_re, s5_c_im, s5_d, s5_w_glu, cmp_pe_k, cmp_w1_k, cmp_w2_k, cmp_pe_v, cmp_w1_v,
              cmp_w2_v, branch_norm, w_out, norm_ffn, w_up, w_down, norm_final):
    past_len = page_table.shape[1] * PAGE_SIZE
    win_buf = cache_win.shape[3]
    bp = x_prompt.shape[0]
    bs = x_sample.shape[0]
    G, HD = NSA_KV_HEADS, HEAD_DIM
    xp, xs = x_prompt, x_sample
    empty = jnp.zeros((bp, 0, G, HD), x_prompt.dtype)
    zero_win = jnp.zeros((bp, WINDOW, G, HD), x_prompt.dtype)
    zero_ret = jnp.zeros((bp, RET_HEADS, HD, HD), F32)
    zero_s5 = jnp.zeros((bp, S5_GROUPS, S5_STATE, 2), F32)
    cmp_p, cmp_s, slc_p, slc_s, win_p, win_s, ret_p, ret_s, s5_p, s5_s = ([] for _ in range(10))
    for l in range(DEPTH):
        lw = {
            'norm_mix': norm_mix[l], 'w_in': w_in[l],
            's5_lambda_re': s5_lambda_re[l], 's5_lambda_im': s5_lambda_im[l], 's5_log_step': s5_log_step[l],
            's5_b_re': s5_b_re[l], 's5_b_im': s5_b_im[l], 's5_c_re': s5_c_re[l], 's5_c_im': s5_c_im[l],
            's5_d': s5_d[l], 's5_w_glu': s5_w_glu[l],
            'cmp_pe_k': cmp_pe_k[l], 'cmp_w1_k': cmp_w1_k[l], 'cmp_w2_k': cmp_w2_k[l],
            'cmp_pe_v': cmp_pe_v[l], 'cmp_w1_v': cmp_w1_v[l], 'cmp_w2_v': cmp_w2_v[l],
            'branch_norm': branch_norm[l], 'w_out': w_out[l],
            'norm_ffn': norm_ffn[l], 'w_up': w_up[l], 'w_down': w_down[l],
        }
        xp, c_rows, s_rows, w_new, r_new, s_new = block(
            xp, 0, win_buf, zero_ret, zero_s5, zero_win, zero_win, empty, empty, empty, empty, lw, rel_bias)
        cmp_p.append(c_rows); slc_p.append(s_rows); win_p.append(w_new); ret_p.append(r_new); s5_p.append(s_new)
        pck, pcv = gather_pages(cache_cmp, page_table, l)
        psk, psv = gather_pages(cache_slc, page_table, l)
        front = ((0, 0), (WINDOW - win_buf, 0), (0, 0), (0, 0))
        wk0 = jnp.pad(cache_win[l, :, 0], front)
        wv0 = jnp.pad(cache_win[l, :, 1], front)
        xs, c_rows, s_rows, w_new, r_new, s_new = block(
            xs, past_len, win_buf, state_ret[l], state_s5[l], wk0, wv0, pck, pcv, psk, psv, lw, rel_bias)
        cmp_s.append(c_rows); slc_s.append(s_rows); win_s.append(w_new); ret_s.append(r_new); s5_s.append(s_new)
    y_prompt = rms_norm(xp, norm_final).astype(x_prompt.dtype)
    y_sample = rms_norm(xs, norm_final).astype(x_sample.dtype)
    return (y_prompt, y_sample,
            jnp.stack(cmp_p, axis=1), jnp.stack(cmp_s, axis=1),
            jnp.stack(slc_p, axis=1), jnp.stack(slc_s, axis=1),
            jnp.stack(win_p, axis=0), jnp.stack(win_s, axis=0),
            jnp.stack(ret_p, axis=0), jnp.stack(ret_s, axis=0),
            jnp.stack(s5_p, axis=0), jnp.stack(s5_s, axis=0))
```

```python
import functools
import math

import jax
import jax.numpy as jnp
import numpy as np
from jax import lax
from jax.experimental import pallas as pl
from jax.experimental.pallas import tpu as pltpu

F32 = jnp.float32
BF16 = jnp.bfloat16

D_MODEL = 4096
DEPTH = 2
PAGE_SIZE = 128
HEAD_DIM = 128
RET_WIDTH = 1024
RET_HEADS = 8
RET_CHUNK = 128
S5_WIDTH = 1024
S5_GROUP = 16
S5_GROUPS = 64
S5_STATE = 64
NSA_WIDTH = 2048
NSA_HEADS = 16
NSA_KV_HEADS = 4
NSA_HPG = 4
KV_WIDTH = 512
CMP_BLOCK = 32
CMP_STRIDE = 16
SEL_BLOCK = 64
SEL_TOP = 16
WINDOW = 512
SEL_Q_BLOCK = 32
WIN_Q_BLOCK = 128
FORCE_SCORE = 1e4
NEG = -1e30
N_BUCKETS = 32
MAX_DISTANCE = 128
D_FF = 4 * D_MODEL
EPS = 1e-6
IN_SPLITS = (RET_WIDTH, RET_WIDTH, RET_WIDTH, RET_WIDTH, S5_WIDTH, NSA_WIDTH,
             KV_WIDTH, KV_WIDTH, KV_WIDTH, KV_WIDTH, KV_WIDTH, KV_WIDTH, 3 * NSA_HEADS)
IN_COLS = sum(IN_SPLITS)

V7X_VMEM_LIMIT_BYTES = 48 * 1024 * 1024
LANE = 128


def _round_up(n, m):
    return -(-n // m) * m


def _rmsnorm_kernel(x_ref, g_ref, o_ref):
    x = x_ref[...].astype(F32)
    ms = jnp.mean(x * x, axis=-1, keepdims=True)
    o_ref[...] = (x * lax.rsqrt(ms + EPS) * g_ref[...].astype(F32)).astype(o_ref.dtype)


def rmsnorm(x2d, gain, out_dtype):
    m, d = x2d.shape
    tm = min(m, 256)
    assert m % tm == 0
    return pl.pallas_call(
        _rmsnorm_kernel,
        grid=(m // tm,),
        in_specs=[pl.BlockSpec((tm, d), lambda i: (i, 0)),
                  pl.BlockSpec((1, d), lambda i: (0, 0))],
        out_specs=pl.BlockSpec((tm, d), lambda i: (i, 0)),
        out_shape=jax.ShapeDtypeStruct((m, d), out_dtype),
        compiler_params=pltpu.CompilerParams(dimension_semantics=("parallel",),
                                             vmem_limit_bytes=V7X_VMEM_LIMIT_BYTES),
        name="rmsnorm",
    )(x2d, gain.reshape(1, d))


def _mm_kernel(*refs, nk, act, has_res):
    if has_res:
        a_ref, w_ref, r_ref, o_ref, acc_ref = refs
    else:
        a_ref, w_ref, o_ref, acc_ref = refs
        r_ref = None
    k = pl.program_id(2)

    @pl.when(k == 0)
    def _():
        acc_ref[...] = jnp.zeros_like(acc_ref)

    acc_ref[...] += jnp.dot(a_ref[...], w_ref[...], preferred_element_type=F32)

    @pl.when(k == nk - 1)
    def _():
        acc = acc_ref[...]
        if act == "relu2":
            acc = jnp.square(jnp.maximum(acc, 0.0))
        if has_res:
            acc = acc + r_ref[...].astype(F32)
        o_ref[...] = acc.astype(o_ref.dtype)


def matmul(a, w, *, res=None, act=None, out_dtype=F32):
    m, kdim = a.shape
    n = w.shape[1]
    tm = min(m, 1024)
    tn = 512 if n % 512 == 0 else (256 if n % 256 == 0 else 128)
    tk = min(kdim, 4096)
    assert m % tm == 0 and n % tn == 0 and kdim % tk == 0
    nk = kdim // tk
    in_specs = [pl.BlockSpec((tm, tk), lambda i, j, k: (i, k)),
                pl.BlockSpec((tk, tn), lambda i, j, k: (k, j))]
    args = [a, w]
    if res is not None:
        in_specs.append(pl.BlockSpec((tm, tn), lambda i, j, k: (i, j)))
        args.append(res)
    return pl.pallas_call(
        functools.partial(_mm_kernel, nk=nk, act=act, has_res=res is not None),
        grid=(m // tm, n // tn, nk),
        in_specs=in_specs,
        out_specs=pl.BlockSpec((tm, tn), lambda i, j, k: (i, j)),
        out_shape=jax.ShapeDtypeStruct((m, n), out_dtype),
        scratch_shapes=[pltpu.VMEM((tm, tn), F32)],
        compiler_params=pltpu.CompilerParams(
            dimension_semantics=("parallel", "parallel", "arbitrary"),
            vmem_limit_bytes=V7X_VMEM_LIMIT_BYTES),
        name="matmul",
    )(*args)


def _largest_divisor(n, cap):
    return max(d for d in range(1, min(n, cap) + 1) if n % d == 0)


def _t5_bucket(dist):
    max_exact = N_BUCKETS // 2
    d = jnp.maximum(dist, 0)
    large = max_exact + (jnp.log(jnp.maximum(d, 1).astype(F32) / max_exact)
                         / math.log(MAX_DISTANCE / max_exact) * (N_BUCKETS - max_exact)).astype(jnp.int32)
    return jnp.where(d < max_exact, d, jnp.minimum(large, N_BUCKETS - 1))


def _rotary(x, pos):
    half = HEAD_DIM // 2
    inv = 1.0 / (10000.0 ** jnp.linspace(0.0, 1.0, half, dtype=F32))
    ang = pos.astype(F32)[:, None] * inv[None]
    cos, sin = jnp.cos(ang)[None, :, None, :], jnp.sin(ang)[None, :, None, :]
    x1, x2 = x[..., :half], x[..., half:]
    return jnp.concatenate([x1 * cos - x2 * sin, x1 * sin + x2 * cos], axis=-1)


def _retention(q, k, v, s0, q0):
    B, T = q.shape[:2]
    pos = q0 + jnp.arange(T)
    q = _rotary(q, pos)
    k = _rotary(k, pos) * HEAD_DIM ** -0.5
    lg = jnp.log1p(-(2.0 ** (-5.0 - jnp.arange(RET_HEADS, dtype=F32))))
    c = _largest_divisor(T, RET_CHUNK)
    n = T // c
    i = jnp.arange(c)
    rel = i[:, None] - i[None, :]
    decay = jnp.where(rel[None] >= 0, jnp.exp(jnp.maximum(rel, 0)[None] * lg[:, None, None]), 0.0)
    q_dec = jnp.exp((i + 1)[:, None] * lg[None])[None, :, :, None]
    k_dec = jnp.exp((c - 1 - i)[:, None] * lg[None])[None, :, :, None]
    chunk_dec = jnp.exp(c * lg)[None, :, None, None]

    def to_chunks(a):
        return a.reshape(B, n, c, RET_HEADS, HEAD_DIM).swapaxes(0, 1)

    def step(s, qkv):
        qc, kc, vc = qkv
        inner = jnp.einsum('bihd,bjhd->bhij', qc, kc) * decay
        o = (jnp.einsum('bhij,bjhv->bihv', inner, vc)
             + jnp.einsum('bihd,bhdv->bihv', qc, s) * q_dec)
        s = s * chunk_dec + jnp.einsum('bjhd,bjhv->bhdv', kc * k_dec, vc)
        return s, o

    s, o = lax.scan(step, s0.astype(F32), (to_chunks(q), to_chunks(k), to_chunks(v)))
    return o.swapaxes(0, 1).reshape(B, T, RET_HEADS, HEAD_DIM), s


def _s5_scan(u, x0, lam_re, lam_im, log_step, b_re, b_im, c_re, c_im, d):
    B, T = u.shape[:2]
    ug = u.reshape(B, T, S5_GROUPS, S5_GROUP).astype(jnp.complex64)
    lam = lax.complex(lam_re.astype(F32), lam_im.astype(F32))
    a_bar = jnp.exp(lam * jnp.exp(log_step.astype(F32))[:, None])
    b_bar = ((a_bar - 1.0) / lam)[..., None] * lax.complex(b_re.astype(F32), b_im.astype(F32))
    bu = jnp.einsum('gpc,btgc->btgp', b_bar, ug)
    init = lax.complex(x0[..., 0].astype(F32), x0[..., 1].astype(F32))
    bu = bu.at[:, 0].add(a_bar[None] * init)
    a = jnp.broadcast_to(a_bar, bu.shape)

    def combine(e1, e2):
        a1, b1 = e1
        a2, b2 = e2
        return a1 * a2, a2 * b1 + b2

    _, xs = lax.associative_scan(combine, (a, bu), axis=1)
    cm = lax.complex(c_re.astype(F32), c_im.astype(F32))
    y = jnp.einsum('gcp,btgp->btgc', cm, xs).real.reshape(B, T, S5_WIDTH) + d.astype(F32) * u
    x_last = xs[:, -1]
    return y, jnp.stack([x_last.real, x_last.imag], axis=-1)


def _compress(x, pe, w1, w2):
    B, L = x.shape[:2]
    r = CMP_BLOCK // CMP_STRIDE
    n_full = L // CMP_STRIDE
    n_cmp = n_full - r + 1
    xs = x[:, :n_full * CMP_STRIDE].reshape(B, n_full, CMP_STRIDE, NSA_KV_HEADS, HEAD_DIM)
    w1 = w1.astype(F32)
    w1r = w1.reshape(r, CMP_STRIDE, HEAD_DIM, HEAD_DIM)
    h = jnp.einsum('ld,ldh->h', pe.astype(F32), w1)
    for j in range(r):
        h = h + jnp.einsum('bnsgd,sdh->bngh', xs[:, j:j + n_cmp], w1r[j])
    return jnp.einsum('bngh,he->bnge', jax.nn.gelu(h), w2.astype(F32))


def _nsa(q, k_cmp, v_cmp, k_slc, v_slc, k_win, v_win, gates, q0, rel_bias,
         pe_k, w1_k, w2_k, pe_v, w1_v, w2_v):
    B, T = q.shape[:2]
    G, J = NSA_KV_HEADS, NSA_HPG
    Lk = q0 + T
    qg = (q.astype(F32) * HEAD_DIM ** -0.5).reshape(B, T, G, J, HEAD_DIM)
    t_pos = q0 + jnp.arange(T)
    table = rel_bias.astype(F32).reshape(N_BUCKETS, G, J)

    kc = _compress(k_cmp.astype(F32), pe_k, w1_k, w2_k)
    vc = _compress(v_cmp.astype(F32), pe_v, w1_v, w2_v)
    n_cmp = kc.shape[1]
    c_start = jnp.arange(n_cmp) * CMP_STRIDE
    dist = t_pos[:, None] - (c_start + CMP_BLOCK - 1)[None]
    valid = (dist >= 0)[:, None, None, :]
    bias = jnp.transpose(table[_t5_bucket(dist)], (0, 2, 3, 1))
    s = jnp.einsum('btgjd,bngd->btgjn', qg, kc) + bias
    p_cmp = jax.nn.softmax(jnp.where(valid, s, NEG), axis=-1) * valid
    o_cmp = jnp.einsum('btgjn,bngd->btgjd', p_cmp, vc)

    n_sel = -(-Lk // SEL_BLOCK)
    s_start = jnp.arange(n_sel) * SEL_BLOCK
    overlap = ((c_start[:, None] < s_start[None] + SEL_BLOCK)
               & (c_start[:, None] + CMP_BLOCK > s_start[None])).astype(F32)
    p_sel = jnp.einsum('btgn,nm->btgm', p_cmp.sum(axis=3), overlap)
    cur = t_pos // SEL_BLOCK
    blk = jnp.arange(n_sel)
    forced = (blk[None] == 0) | (blk[None] == cur[:, None]) | (blk[None] == cur[:, None] - 1)
    blk_ok = (blk[None] <= cur[:, None])[None, :, None, :]
    score = jnp.where(blk_ok, p_sel + FORCE_SCORE * forced[None, :, None, :], NEG)
    n_top = min(SEL_TOP, n_sel)
    _, idx = lax.top_k(score, n_top)

    pad = n_sel * SEL_BLOCK - Lk

    def blocks(a):
        a = jnp.pad(a.astype(F32), ((0, 0), (0, pad), (0, 0), (0, 0)))
        return a.reshape(B, n_sel, SEL_BLOCK, G, HEAD_DIM).transpose(0, 3, 1, 2, 4)

    kb, vb = blocks(k_slc), blocks(v_slc)
    qb = _largest_divisor(T, SEL_Q_BLOCK)
    nq = T // qb
    b_ix = jnp.arange(B)[:, None, None, None]
    g_ix = jnp.arange(G)[None, None, :, None]
    g_ix5 = g_ix[..., None]
    off = jnp.arange(SEL_BLOCK)

    def sel_block(args):
        qc, ic, tc = args
        kg = kb[b_ix, g_ix, ic]
        vg = vb[b_ix, g_ix, ic]
        kpos = ic[..., None] * SEL_BLOCK + off
        d = tc[None, :, None, None, None] - kpos
        ok = (d >= 0) & (ic <= (tc // SEL_BLOCK)[None, :, None, None])[..., None]
        bias_s = jnp.moveaxis(table[_t5_bucket(d), g_ix5], -1, 3)
        sc = jnp.einsum('bqgjd,bqgkld->bqgjkl', qc, kg) + bias_s
        sc = jnp.where(ok[:, :, :, None], sc, NEG).reshape(B, qb, G, J, n_top * SEL_BLOCK)
        pr = jax.nn.softmax(sc, axis=-1).reshape(B, qb, G, J, n_top, SEL_BLOCK)
        return jnp.einsum('bqgjkl,bqgkld->bqgjd', pr, vg)

    o_slc = lax.map(sel_block, (qg.reshape(B, nq, qb, G, J, HEAD_DIM).swapaxes(0, 1),
                                idx.reshape(B, nq, qb, G, n_top).swapaxes(0, 1),
                                t_pos.reshape(nq, qb)))
    o_slc = o_slc.swapaxes(0, 1).reshape(B, T, G, J, HEAD_DIM)

    qw = _largest_divisor(T, WIN_Q_BLOCK)
    nw = T // qw
    rows = jnp.arange(nw)[:, None] * qw + jnp.arange(qw + WINDOW)[None]
    kwin = k_win.astype(F32)[:, rows]
    vwin = v_win.astype(F32)[:, rows]
    kpos = q0 - WINDOW + rows
    dw = t_pos.reshape(nw, qw)[:, :, None] - kpos[:, None, :]
    okw = (kpos[:, None, :] >= 0) & (dw >= 0) & (dw <= WINDOW)
    bias_w = jnp.transpose(table[_t5_bucket(dw)], (0, 3, 4, 1, 2))
    sw = jnp.einsum('bcqgjd,bckgd->bcgjqk', qg.reshape(B, nw, qw, G, J, HEAD_DIM), kwin) + bias_w
    pw = jax.nn.softmax(jnp.where(okw[:, None, None], sw, NEG), axis=-1)
    o_win = jnp.einsum('bcgjqk,bckgd->bcqgjd', pw, vwin).reshape(B, T, G, J, HEAD_DIM)

    g = jax.nn.sigmoid(gates.astype(F32)).reshape(B, T, G, J, 3)
    o = g[..., 0:1] * o_cmp + g[..., 1:2] * o_slc + g[..., 2:3] * o_win
    return o.reshape(B, T, NSA_WIDTH)


def _jnp_rms(x, gain):
    xf = x.astype(F32)
    xf = xf * lax.rsqrt(jnp.mean(xf * xf, axis=-1, keepdims=True) + EPS)
    return xf * gain.astype(F32)


def _block(x, q0, win_buf, ret_s0, s5_s0, win_k0, win_v0, past_cmp_k, past_cmp_v,
           past_slc_k, past_slc_v, lw, rel_bias):
    B, T, _ = x.shape
    G = NSA_KV_HEADS
    M = B * T
    x2 = x.reshape(M, D_MODEL)
    h = rmsnorm(x2, lw['norm_mix'], BF16)
    proj = matmul(h, lw['w_in'])[:, :IN_COLS].reshape(B, T, IN_COLS)
    (rq, rk, rv, rg, su, nq_, kc, vc, ks, vs, kw, vw, ng) = jnp.split(
        proj, np.cumsum(IN_SPLITS)[:-1].tolist(), axis=-1)

    def heads(a, n):
        return a.reshape(B, T, n, HEAD_DIM)

    ro, ret_s = _retention(heads(rq, RET_HEADS), heads(rk, RET_HEADS), heads(rv, RET_HEADS), ret_s0, q0)
    ro = ro * lax.rsqrt(jnp.mean(ro * ro, axis=-1, keepdims=True) + EPS)
    ro = jax.nn.silu(rg) * ro.reshape(B, T, RET_WIDTH)

    sy, s5_s = _s5_scan(su, s5_s0, lw['s5_lambda_re'], lw['s5_lambda_im'], lw['s5_log_step'],
                        lw['s5_b_re'], lw['s5_b_im'], lw['s5_c_re'], lw['s5_c_im'], lw['s5_d'])
    sy = jax.nn.gelu(sy)
    so = sy * jax.nn.sigmoid(jnp.einsum('btc,ce->bte', sy, lw['s5_w_glu']))

    kc, vc, ks, vs, kw, vw = (heads(a, G) for a in (kc, vc, ks, vs, kw, vw))
    k_cmp_all = jnp.concatenate([past_cmp_k, kc], axis=1)
    v_cmp_all = jnp.concatenate([past_cmp_v, vc], axis=1)
    k_slc_all = jnp.concatenate([past_slc_k, ks], axis=1)
    v_slc_all = jnp.concatenate([past_slc_v, vs], axis=1)
    k_win_all = jnp.concatenate([win_k0, kw], axis=1)
    v_win_all = jnp.concatenate([win_v0, vw], axis=1)
    no = _nsa(heads(nq_, NSA_HEADS), k_cmp_all, v_cmp_all, k_slc_all, v_slc_all, k_win_all, v_win_all,
              ng.reshape(B, T, NSA_HEADS, 3), q0, rel_bias,
              lw['cmp_pe_k'], lw['cmp_w1_k'], lw['cmp_w2_k'], lw['cmp_pe_v'], lw['cmp_w1_v'], lw['cmp_w2_v'])

    bn = lw['branch_norm']
    mix = jnp.concatenate([_jnp_rms(ro, bn[:RET_WIDTH]),
                           _jnp_rms(so, bn[RET_WIDTH:RET_WIDTH + S5_WIDTH]),
                           _jnp_rms(no, bn[RET_WIDTH + S5_WIDTH:])], axis=-1).astype(BF16)
    x2 = matmul(mix.reshape(M, D_MODEL), lw['w_out'], res=x2)

    h = rmsnorm(x2, lw['norm_ffn'], BF16)
    up = matmul(h, lw['w_up'], act="relu2", out_dtype=BF16)
    x2 = matmul(up, lw['w_down'], res=x2)

    cmp_rows = jnp.stack([kc, vc], axis=1)
    slc_rows = jnp.stack([ks, vs], axis=1)
    win_new = jnp.stack([k_win_all[:, -win_buf:], v_win_all[:, -win_buf:]], axis=1)
    return x2.reshape(B, T, D_MODEL), cmp_rows, slc_rows, win_new, ret_s, s5_s


def _gather_pages(pool, page_table, layer):
    pages = pool[page_table, layer]
    db, npg = page_table.shape
    pages = pages.transpose(0, 2, 1, 3, 4, 5).reshape(db, 2, npg * PAGE_SIZE, NSA_KV_HEADS, HEAD_DIM)
    return pages[:, 0], pages[:, 1]


def kernel(x_prompt, x_sample, cache_cmp, cache_slc, cache_win, state_ret, state_s5, page_table,
           rel_bias, norm_mix, w_in, s5_lambda_re, s5_lambda_im, s5_log_step, s5_b_re, s5_b_im,
           s5_c_re, s5_c_im, s5_d, s5_w_glu, cmp_pe_k, cmp_w1_k, cmp_w2_k, cmp_pe_v, cmp_w1_v,
           cmp_w2_v, branch_norm, w_out, norm_ffn, w_up, w_down, norm_final):
    past_len = page_table.shape[1] * PAGE_SIZE
    win_buf = cache_win.shape[3]
    bp = x_prompt.shape[0]
    G, HD = NSA_KV_HEADS, HEAD_DIM
    xp, xs = x_prompt, x_sample
    empty = jnp.zeros((bp, 0, G, HD), x_prompt.dtype)
    zero_win = jnp.zeros((bp, WINDOW, G, HD), x_prompt.dtype)
    zero_ret = jnp.zeros((bp, RET_HEADS, HD, HD), F32)
    zero_s5 = jnp.zeros((bp, S5_GROUPS, S5_STATE, 2), F32)
    in_pad = _round_up(IN_COLS, 512) - IN_COLS
    cmp_p, cmp_s, slc_p, slc_s, win_p, win_s, ret_p, ret_s, s5_p, s5_s = ([] for _ in range(10))
    for l in range(DEPTH):
        lw = {
            'norm_mix': norm_mix[l],
            'w_in': jnp.pad(w_in[l].astype(BF16), ((0, 0), (0, in_pad))),
            's5_lambda_re': s5_lambda_re[l], 's5_lambda_im': s5_lambda_im[l], 's5_log_step': s5_log_step[l],
            's5_b_re': s5_b_re[l], 's5_b_im': s5_b_im[l], 's5_c_re': s5_c_re[l], 's5_c_im': s5_c_im[l],
            's5_d': s5_d[l], 's5_w_glu': s5_w_glu[l],
            'cmp_pe_k': cmp_pe_k[l], 'cmp_w1_k': cmp_w1_k[l], 'cmp_w2_k': cmp_w2_k[l],
            'cmp_pe_v': cmp_pe_v[l], 'cmp_w1_v': cmp_w1_v[l], 'cmp_w2_v': cmp_w2_v[l],
            'branch_norm': branch_norm[l], 'w_out': w_out[l].astype(BF16),
            'norm_ffn': norm_ffn[l], 'w_up': w_up[l].astype(BF16), 'w_down': w_down[l].astype(BF16),
        }
        xp, c_rows, s_rows, w_new, r_new, s_new = _block(
            xp, 0, win_buf, zero_ret, zero_s5, zero_win, zero_win, empty, empty, empty, empty, lw, rel_bias)
        cmp_p.append(c_rows); slc_p.append(s_rows); win_p.append(w_new); ret_p.append(r_new); s5_p.append(s_new)
        pck, pcv = _gather_pages(cache_cmp, page_table, l)
        psk, psv = _gather_pages(cache_slc, page_table, l)
        front = ((0, 0), (WINDOW - win_buf, 0), (0, 0), (0, 0))
        wk0 = jnp.pad(cache_win[l, :, 0], front)
        wv0 = jnp.pad(cache_win[l, :, 1], front)
        xs, c_rows, s_rows, w_new, r_new, s_new = _block(
            xs, past_len, win_buf, state_ret[l], state_s5[l], wk0, wv0, pck, pcv, psk, psv, lw, rel_bias)
        cmp_s.append(c_rows); slc_s.append(s_rows); win_s.append(w_new); ret_s.append(r_new); s5_s.append(s_new)
    y_prompt = rmsnorm(xp.reshape(-1, D_MODEL), norm_final, F32).reshape(xp.shape)
    y_sample = rmsnorm(xs.reshape(-1, D_MODEL), norm_final, F32).reshape(xs.shape)
    return (y_prompt, y_sample,
            jnp.stack(cmp_p, axis=1), jnp.stack(cmp_s, axis=1),
            jnp.stack(slc_p, axis=1), jnp.stack(slc_s, axis=1),
            jnp.stack(win_p, axis=0), jnp.stack(win_s, axis=0),
            jnp.stack(ret_p, axis=0), jnp.stack(ret_s, axis=0),
            jnp.stack(s5_p, axis=0), jnp.stack(s5_s, axis=0))
```

```python
import functools
import math

import jax
import jax.numpy as jnp
import numpy as np
from jax import lax
from jax.experimental import pallas as pl
from jax.experimental.pallas import tpu as pltpu

F32 = jnp.float32
BF16 = jnp.bfloat16

D_MODEL = 4096
DEPTH = 2
PAGE_SIZE = 128
HEAD_DIM = 128
RET_WIDTH = 1024
RET_HEADS = 8
RET_CHUNK = 128
S5_WIDTH = 1024
S5_GROUP = 16
S5_GROUPS = 64
S5_STATE = 64
NSA_WIDTH = 2048
NSA_HEADS = 16
NSA_KV_HEADS = 4
NSA_HPG = 4
KV_WIDTH = 512
CMP_BLOCK = 32
CMP_STRIDE = 16
SEL_BLOCK = 64
SEL_TOP = 16
WINDOW = 512
SEL_Q_BLOCK = 32
WIN_Q_BLOCK = 128
FORCE_SCORE = 1e4
NEG = -1e30
N_BUCKETS = 32
MAX_DISTANCE = 128
D_FF = 4 * D_MODEL
EPS = 1e-6
IN_SPLITS = (RET_WIDTH, RET_WIDTH, RET_WIDTH, RET_WIDTH, S5_WIDTH, NSA_WIDTH,
             KV_WIDTH, KV_WIDTH, KV_WIDTH, KV_WIDTH, KV_WIDTH, KV_WIDTH, 3 * NSA_HEADS)
IN_COLS = sum(IN_SPLITS)

V7X_VMEM_LIMIT_BYTES = 48 * 1024 * 1024
LANE = 128


def _round_up(n, m):
    return -(-n // m) * m


def _rmsnorm_kernel(x_ref, g_ref, o_ref):
    x = x_ref[...].astype(F32)
    ms = jnp.mean(x * x, axis=-1, keepdims=True)
    o_ref[...] = (x * lax.rsqrt(ms + EPS) * g_ref[...].astype(F32)).astype(o_ref.dtype)


def rmsnorm(x2d, gain, out_dtype):
    m, d = x2d.shape
    tm = min(m, 256)
    assert m % tm == 0
    return pl.pallas_call(
        _rmsnorm_kernel,
        grid=(m // tm,),
        in_specs=[pl.BlockSpec((tm, d), lambda i: (i, 0)),
                  pl.BlockSpec((1, d), lambda i: (0, 0))],
        out_specs=pl.BlockSpec((tm, d), lambda i: (i, 0)),
        out_shape=jax.ShapeDtypeStruct((m, d), out_dtype),
        compiler_params=pltpu.CompilerParams(dimension_semantics=("parallel",),
                                             vmem_limit_bytes=V7X_VMEM_LIMIT_BYTES),
        name="rmsnorm",
    )(x2d, gain.reshape(1, d))


def _mm_kernel(*refs, nk, act, has_res):
    if has_res:
        a_ref, w_ref, r_ref, o_ref, acc_ref = refs
    else:
        a_ref, w_ref, o_ref, acc_ref = refs
        r_ref = None
    k = pl.program_id(2)

    @pl.when(k == 0)
    def _():
        acc_ref[...] = jnp.zeros_like(acc_ref)

    acc_ref[...] += jnp.dot(a_ref[...].astype(BF16), w_ref[...], preferred_element_type=F32)

    @pl.when(k == nk - 1)
    def _():
        acc = acc_ref[...]
        if act == "relu2":
            acc = jnp.square(jnp.maximum(acc, 0.0))
        if act == "glu":
            acc = r_ref[...].astype(F32) * jax.nn.sigmoid(acc)
        elif has_res:
            acc = acc + r_ref[...].astype(F32)
        o_ref[...] = acc.astype(o_ref.dtype)


def matmul(a, w, *, res=None, act=None, out_dtype=F32):
    m, kdim = a.shape
    n = w.shape[1]
    tm = min(m, 1024)
    tn = 512 if n % 512 == 0 else (256 if n % 256 == 0 else 128)
    tk = min(kdim, 4096)
    assert m % tm == 0 and n % tn == 0 and kdim % tk == 0
    nk = kdim // tk
    in_specs = [pl.BlockSpec((tm, tk), lambda i, j, k: (i, k)),
                pl.BlockSpec((tk, tn), lambda i, j, k: (k, j))]
    args = [a, w]
    if res is not None:
        in_specs.append(pl.BlockSpec((tm, tn), lambda i, j, k: (i, j)))
        args.append(res)
    return pl.pallas_call(
        functools.partial(_mm_kernel, nk=nk, act=act, has_res=res is not None),
        grid=(m // tm, n // tn, nk),
        in_specs=in_specs,
        out_specs=pl.BlockSpec((tm, tn), lambda i, j, k: (i, j)),
        out_shape=jax.ShapeDtypeStruct((m, n), out_dtype),
        scratch_shapes=[pltpu.VMEM((tm, tn), F32)],
        compiler_params=pltpu.CompilerParams(
            dimension_semantics=("parallel", "parallel", "arbitrary"),
            vmem_limit_bytes=V7X_VMEM_LIMIT_BYTES),
        name="matmul",
    )(*args)


def _gelu_tanh(x):
    return 0.5 * x * (1.0 + jnp.tanh(math.sqrt(2.0 / math.pi) * (x + 0.044715 * (x * x * x))))


def _dot_nt(a, b):
    return lax.dot_general(a, b, (((1,), (1,)), ((), ())), preferred_element_type=F32)


def _dot_tn(a, b):
    return lax.dot_general(a, b, (((0,), (0,)), ((), ())), preferred_element_type=F32)


def _compress_rows(x_ref, n_full, pe_ref, w1_ref, w2_ref):
    pieces = [x_ref[0, pl.ds(s, n_full, stride=CMP_STRIDE), :] for s in range(CMP_STRIDE)]
    x = jnp.concatenate(pieces, axis=1).astype(BF16)
    half = CMP_STRIDE * HEAD_DIM
    z0 = jnp.dot(x, w1_ref[:half, :], preferred_element_type=F32)
    z1 = jnp.dot(x, w1_ref[half:, :], preferred_element_type=F32)
    z1 = pltpu.roll(z1, n_full - 1, 0)
    pe = jnp.broadcast_to(pe_ref[...], (8, CMP_BLOCK * HEAD_DIM)).astype(BF16)
    h0 = jnp.dot(pe, w1_ref[...], preferred_element_type=F32)[0:1, :]
    h = _gelu_tanh(z0 + z1 + h0)
    return jnp.dot(h.astype(BF16), w2_ref[...], preferred_element_type=F32)


def _compress_prompt_kernel(xk_ref, xv_ref, pek_ref, w1k_ref, w2k_ref, pev_ref, w1v_ref, w2v_ref,
                            kc_ref, vc_ref, *, n_full):
    for x_ref, pe_ref, w1_ref, w2_ref, o_ref in ((xk_ref, pek_ref, w1k_ref, w2k_ref, kc_ref),
                                                  (xv_ref, pev_ref, w1v_ref, w2v_ref, vc_ref)):
        out = _compress_rows(x_ref, n_full, pe_ref, w1_ref, w2_ref).astype(o_ref.dtype)
        n_pad = o_ref.shape[2]
        o_ref[0, 0, :n_full, :] = out
        if n_pad > n_full:
            o_ref[0, 0, n_full:, :] = jnp.zeros((n_pad - n_full, HEAD_DIM), o_ref.dtype)


def compress_prompt(proj3, col_k, col_v, pe_k, w1_k, w2_k, pe_v, w1_v, w2_v):
    B, T, _ = proj3.shape
    n_full = T // CMP_STRIDE
    n_pad = _round_up(n_full, LANE)
    flat = CMP_BLOCK * HEAD_DIM
    wspec = [pl.BlockSpec((1, flat), lambda b, g: (0, 0)),
             pl.BlockSpec((flat, HEAD_DIM), lambda b, g: (0, 0)),
             pl.BlockSpec((HEAD_DIM, HEAD_DIM), lambda b, g: (0, 0))]
    out_sds = jax.ShapeDtypeStruct((B, NSA_KV_HEADS, n_pad, HEAD_DIM), BF16)
    ospec = pl.BlockSpec((1, 1, n_pad, HEAD_DIM), lambda b, g: (b, g, 0, 0))
    return pl.pallas_call(
        functools.partial(_compress_prompt_kernel, n_full=n_full),
        grid=(B, NSA_KV_HEADS),
        in_specs=[pl.BlockSpec((1, T, HEAD_DIM), lambda b, g: (b, 0, col_k // HEAD_DIM + g)),
                  pl.BlockSpec((1, T, HEAD_DIM), lambda b, g: (b, 0, col_v // HEAD_DIM + g))] + wspec + wspec,
        out_specs=(ospec, ospec),
        out_shape=(out_sds, out_sds),
        compiler_params=pltpu.CompilerParams(dimension_semantics=("parallel", "parallel"),
                                             vmem_limit_bytes=V7X_VMEM_LIMIT_BYTES),
        name="nsa_compress_prompt",
    )(proj3, proj3,
      pe_k.reshape(1, flat), w1_k.reshape(flat, HEAD_DIM).astype(BF16), w2_k.astype(BF16),
      pe_v.reshape(1, flat), w1_v.reshape(flat, HEAD_DIM).astype(BF16), w2_v.astype(BF16))


ATT_TILE = 128
N_BIAS_TILES = 4


def _bucket_table():
    d = np.arange(MAX_DISTANCE)
    max_exact = N_BUCKETS // 2
    large = max_exact + (np.log(np.maximum(d, 1).astype(np.float32) / np.float32(max_exact))
                         / np.float32(math.log(MAX_DISTANCE / max_exact))
                         * np.float32(N_BUCKETS - max_exact)).astype(np.int32)
    return np.where(d < max_exact, d, np.minimum(large, N_BUCKETS - 1)).astype(np.int32)


def _bias_of_distance(rel_bias, dist):
    bt = _bucket_table()
    buckets = bt[np.clip(dist, 0, MAX_DISTANCE - 1)]
    b = jnp.moveaxis(rel_bias.astype(F32)[buckets], -1, 0)
    return jnp.where(jnp.asarray(dist >= 0)[None], b, NEG)


def _softmax_step(s, m_ref, l_ref, acc_ref, v):
    m_old = m_ref[...]
    m_new = jnp.maximum(m_old, jnp.max(s, axis=-1, keepdims=True))
    alpha = jnp.exp(m_old - m_new)
    p = jnp.exp(s - m_new)
    l_ref[...] = alpha * l_ref[...] + jnp.sum(p, axis=-1, keepdims=True)
    acc_ref[...] = alpha * acc_ref[...] + jnp.dot(p.astype(BF16), v, preferred_element_type=F32)
    m_ref[...] = m_new


def _nsa_prompt_kernel(q_ref, kc_ref, vc_ref, ks_ref, vs_ref, kw_ref, vw_ref, gate_ref,
                       bcmp_ref, btile_ref, ovl_ref, exp_ref, o_ref,
                       ksb, vsb, kwb, vwb, m_ref, l_ref, acc_ref, mask_ref, *, n_sel, n_top):
    tq = ATT_TILE
    qi = pl.program_id(2)
    n_kt = mask_ref.shape[0]
    n_sel_pad = exp_ref.shape[0]

    @pl.when(qi == 0)
    def _():
        ksb[...] = ks_ref[0].astype(BF16)
        vsb[...] = vs_ref[0].astype(BF16)
        kwb[...] = kw_ref[0].astype(BF16)
        vwb[...] = vw_ref[0].astype(BF16)

    row = lax.broadcasted_iota(jnp.int32, (tq, LANE), 0)
    lane = lax.broadcasted_iota(jnp.int32, (tq, LANE), 1)
    t_pos = qi * tq + row
    gates = jax.nn.sigmoid(gate_ref[0, 0])

    valid = t_pos >= lane * CMP_STRIDE + (CMP_BLOCK - 1)
    kc = kc_ref[0, 0]
    vc = vc_ref[0, 0]
    psum = jnp.zeros((tq, LANE), F32)
    qs = []
    for j in range(NSA_HPG):
        q = (q_ref[0, :, j * HEAD_DIM:(j + 1) * HEAD_DIM] * HEAD_DIM ** -0.5).astype(BF16)
        qs.append(q)
        s = jnp.where(valid, _dot_nt(q, kc) + bcmp_ref[j], NEG)
        e = jnp.where(valid, jnp.exp(s - jnp.max(s, axis=-1, keepdims=True)), 0.0)
        den = jnp.sum(e, axis=-1, keepdims=True)
        p = e / jnp.where(den > 0.0, den, 1.0)
        psum = psum + p
        o_ref[0, :, j * HEAD_DIM:(j + 1) * HEAD_DIM] = gates[:, 3 * j:3 * j + 1] * jnp.dot(
            p.astype(BF16), vc, preferred_element_type=F32)

    hi = psum.astype(BF16)
    r1 = psum - hi.astype(F32)
    mid = r1.astype(BF16)
    lo = (r1 - mid.astype(F32)).astype(BF16)
    ovl = ovl_ref[...]
    p_sel = _dot_nt(ovl, hi) + _dot_nt(ovl, mid) + _dot_nt(ovl, lo)
    blk = lax.broadcasted_iota(jnp.int32, (n_sel_pad, tq), 0)
    cur = (qi * tq + lax.broadcasted_iota(jnp.int32, (n_sel_pad, tq), 1)) // SEL_BLOCK
    forced = (blk == 0) | (blk == cur) | (blk == cur - 1)
    score = jnp.where(blk <= cur, p_sel + jnp.where(forced, FORCE_SCORE, 0.0), NEG)
    rank = jnp.zeros((n_sel_pad, tq), F32)
    for jb in range(n_sel):
        other = score[jb:jb + 1, :]
        beats = (other > score) | ((other == score) & (blk > jb))
        rank = rank + jnp.where(beats, 1.0, 0.0)
    sel = jnp.where((rank < n_top) & (blk <= cur) & (blk < n_sel), 1.0, 0.0).astype(BF16)
    for kt in range(n_kt):
        mask_ref[kt] = _dot_tn(sel, exp_ref[:, kt * tq:(kt + 1) * tq])

    def attend(j, kb, vb, kt, bias_idx, use_mask):
        off = pl.multiple_of(kt * tq, tq)
        s = _dot_nt(qs[j], kb[pl.ds(off, tq), :]) + btile_ref[j, bias_idx]
        if use_mask:
            s = jnp.where(mask_ref[kt] > 0.5, s, NEG)
        _softmax_step(s, m_ref, l_ref, acc_ref, vb[pl.ds(off, tq), :])

    def reset():
        m_ref[...] = jnp.full(m_ref.shape, NEG, F32)
        l_ref[...] = jnp.zeros(l_ref.shape, F32)
        acc_ref[...] = jnp.zeros(acc_ref.shape, F32)

    for j in range(NSA_HPG):
        cols = slice(j * HEAD_DIM, (j + 1) * HEAD_DIM)
        reset()

        def slc_body(kt, c, j=j):
            attend(j, ksb, vsb, kt, jnp.minimum(qi - kt, 2), True)
            return c

        lax.fori_loop(0, qi + 1, slc_body, 0)
        o_ref[0, :, cols] = o_ref[0, :, cols] + gates[:, 3 * j + 1:3 * j + 2] * (
            acc_ref[...] / l_ref[...])

        reset()

        def win_body(kt, c, j=j):
            dd = qi - kt
            attend(j, kwb, vwb, kt, jnp.where(dd == WINDOW // ATT_TILE, 3, jnp.minimum(dd, 2)), False)
            return c

        lax.fori_loop(jnp.maximum(qi - WINDOW // ATT_TILE, 0), qi + 1, win_body, 0)
        o_ref[0, :, cols] = o_ref[0, :, cols] + gates[:, 3 * j + 2:3 * j + 3] * (
            acc_ref[...] / l_ref[...])


def nsa_prompt(proj3, cols, kcmp, vcmp, rel_bias):
    B, T, _ = proj3.shape
    tq = ATT_TILE
    assert T % tq == 0 and T % SEL_BLOCK == 0
    G, J = NSA_KV_HEADS, NSA_HPG
    nq = T // tq
    n_sel = T // SEL_BLOCK
    n_top = min(SEL_TOP, n_sel)
    n_sel_pad = _round_up(n_sel, 16)
    n_pad = kcmp.shape[2]
    assert n_pad == LANE, "one lane tile of compressed blocks"

    t = np.arange(T)[:, None]
    n = np.arange(n_pad)[None, :]
    bcmp = _bias_of_distance(rel_bias, np.maximum(t - (n * CMP_STRIDE + CMP_BLOCK - 1), 0))
    r = np.arange(tq)[:, None]
    c = np.arange(tq)[None, :]
    far = np.full((tq, tq), MAX_DISTANCE)
    edge = np.where(r <= c, MAX_DISTANCE, -1)
    btile = _bias_of_distance(rel_bias, np.stack([r - c, tq + r - c, far, edge]))
    cmp_start = np.arange(n_pad) * CMP_STRIDE
    sel_start = np.arange(n_sel_pad) * SEL_BLOCK
    ovl = ((cmp_start[None, :] < sel_start[:, None] + SEL_BLOCK)
           & (cmp_start[None, :] + CMP_BLOCK > sel_start[:, None])
           & (np.arange(n_pad)[None, :] < T // CMP_STRIDE - 1))
    ovl = jnp.asarray(ovl, BF16)
    expand = jnp.asarray(np.arange(T)[None, :] // SEL_BLOCK == np.arange(n_sel_pad)[:, None], BF16)
    gates = proj3[:, :, cols['ng']:cols['ng'] + 3 * NSA_HEADS].reshape(B, T, G, 3 * J).transpose(0, 2, 1, 3)

    kv_spec = lambda name: pl.BlockSpec((1, T, HEAD_DIM),
                                        lambda b, g, i, o=cols[name] // HEAD_DIM: (b, 0, o + g))
    cmp_spec = pl.BlockSpec((1, 1, n_pad, HEAD_DIM), lambda b, g, i: (b, g, 0, 0))
    return pl.pallas_call(
        functools.partial(_nsa_prompt_kernel, n_sel=n_sel, n_top=n_top),
        grid=(B, G, nq),
        in_specs=[pl.BlockSpec((1, tq, J * HEAD_DIM),
                               lambda b, g, i, o=cols['nq'] // (J * HEAD_DIM): (b, i, o + g)),
                  cmp_spec, cmp_spec,
                  kv_spec('ks'), kv_spec('vs'), kv_spec('kw'), kv_spec('vw'),
                  pl.BlockSpec((1, 1, tq, 3 * J), lambda b, g, i: (b, g, i, 0)),
                  pl.BlockSpec((J, tq, n_pad), lambda b, g, i: (g, i, 0)),
                  pl.BlockSpec((J, N_BIAS_TILES, tq, tq), lambda b, g, i: (g, 0, 0, 0)),
                  pl.BlockSpec((n_sel_pad, n_pad), lambda b, g, i: (0, 0)),
                  pl.BlockSpec((n_sel_pad, T), lambda b, g, i: (0, 0))],
        out_specs=pl.BlockSpec((1, tq, J * HEAD_DIM), lambda b, g, i: (b, i, g)),
        out_shape=jax.ShapeDtypeStruct((B, T, NSA_WIDTH), F32),
        scratch_shapes=[pltpu.VMEM((T, HEAD_DIM), BF16)] * 4 + [
            pltpu.VMEM((tq, LANE), F32), pltpu.VMEM((tq, LANE), F32), pltpu.VMEM((tq, HEAD_DIM), F32),
            pltpu.VMEM((nq, tq, tq), F32)],
        compiler_params=pltpu.CompilerParams(
            dimension_semantics=("parallel", "parallel", "arbitrary"),
            vmem_limit_bytes=V7X_VMEM_LIMIT_BYTES),
        name="nsa_prompt",
    )(proj3, kcmp, vcmp, proj3, proj3, proj3, proj3, gates, bcmp, btile, ovl, expand)


def _retention_tables(T, q0):
    c = _largest_divisor(T, RET_CHUNK)
    lg = np.log1p(-(2.0 ** (-5.0 - np.arange(RET_HEADS, dtype=np.float32)))).astype(np.float32)
    i = np.arange(c)
    rel = i[:, None] - i[None, :]
    decay = np.where(rel[None] >= 0, np.exp(np.maximum(rel, 0)[None] * lg[:, None, None]), 0.0)
    q_dec = np.broadcast_to(np.exp((i + 1)[None, :, None] * lg[:, None, None]), (RET_HEADS, c, HEAD_DIM))
    k_dec = np.broadcast_to(np.exp((c - 1 - i)[None, :, None] * lg[:, None, None]), (RET_HEADS, c, HEAD_DIM))
    chunk_dec = np.broadcast_to(np.exp(c * lg)[:, None, None], (RET_HEADS, 8, HEAD_DIM))
    half = HEAD_DIM // 2
    inv = (1.0 / (10000.0 ** np.linspace(0.0, 1.0, half, dtype=np.float32))).astype(np.float32)
    ang = (q0 + np.arange(T)).astype(np.float32)[:, None] * inv[None]
    cos, sin = np.cos(ang), np.sin(ang)
    cosf = np.concatenate([cos, cos], axis=1)
    sinf = np.concatenate([-sin, sin], axis=1)
    f = lambda a: jnp.asarray(a, F32)
    return c, f(decay), f(q_dec), f(k_dec), f(chunk_dec), f(cosf), f(sinf)


def _retention_prompt_kernel(q_ref, k_ref, v_ref, g_ref, cos_ref, sin_ref, dec_ref, qd_ref, kd_ref, cd_ref,
                             o_ref, s_ref, *, c, n):
    decay = dec_ref[0]
    q_dec = qd_ref[0]
    k_dec = kd_ref[0]
    chunk_dec = cd_ref[0, 0:1, :]
    half = HEAD_DIM // 2

    def rot(x, cos, sin):
        return x * cos + pltpu.roll(x, half, 1) * sin

    def body(i, s):
        rows = pl.ds(pl.multiple_of(i * c, c), c)
        cos, sin = cos_ref[rows, :], sin_ref[rows, :]
        q = rot(q_ref[0, rows, :], cos, sin)
        k = rot(k_ref[0, rows, :], cos, sin) * HEAD_DIM ** -0.5
        v = v_ref[0, rows, :].astype(BF16)
        qb = q.astype(BF16)
        inner = _dot_nt(qb, k.astype(BF16)) * decay
        o = (jnp.dot(inner.astype(BF16), v, preferred_element_type=F32)
             + jnp.dot(qb, s.astype(BF16), preferred_element_type=F32) * q_dec)
        s = s * chunk_dec + _dot_tn((k * k_dec).astype(BF16), v)
        o = o * lax.rsqrt(jnp.mean(o * o, axis=-1, keepdims=True) + EPS)
        g = g_ref[0, rows, :]
        o_ref[0, rows, :] = g * jax.nn.sigmoid(g) * o
        return s

    s_ref[0, 0] = lax.fori_loop(0, n, body, jnp.zeros((HEAD_DIM, HEAD_DIM), F32))


def retention_prompt(proj3, cols):
    B, T, _ = proj3.shape
    c, decay, q_dec, k_dec, chunk_dec, cosf, sinf = _retention_tables(T, 0)
    assert c == RET_CHUNK
    col = lambda name: pl.BlockSpec((1, T, HEAD_DIM), lambda b, h, o=cols[name] // HEAD_DIM: (b, 0, o + h))
    tab = lambda r: pl.BlockSpec((1, r, HEAD_DIM), lambda b, h: (h, 0, 0))
    full = pl.BlockSpec((T, HEAD_DIM), lambda b, h: (0, 0))
    return pl.pallas_call(
        functools.partial(_retention_prompt_kernel, c=c, n=T // c),
        grid=(B, RET_HEADS),
        in_specs=[col('rq'), col('rk'), col('rv'), col('rg'), full, full, tab(c), tab(c), tab(c), tab(8)],
        out_specs=(pl.BlockSpec((1, T, HEAD_DIM), lambda b, h: (b, 0, h)),
                   pl.BlockSpec((1, 1, HEAD_DIM, HEAD_DIM), lambda b, h: (b, h, 0, 0))),
        out_shape=(jax.ShapeDtypeStruct((B, T, RET_WIDTH), F32),
                   jax.ShapeDtypeStruct((B, RET_HEADS, HEAD_DIM, HEAD_DIM), F32)),
        compiler_params=pltpu.CompilerParams(dimension_semantics=("parallel", "parallel"),
                                             vmem_limit_bytes=V7X_VMEM_LIMIT_BYTES),
        name="retention_prompt",
    )(proj3, proj3, proj3, proj3, cosf, sinf, decay, q_dec, k_dec, chunk_dec)


S5_BLK_GROUPS = 8
S5_BLK_STATE = S5_BLK_GROUPS * S5_STATE
S5_BLK_CH = S5_BLK_GROUPS * S5_GROUP
S5_SCAN_ROWS = 8


def _s5_params(lam_re, lam_im, log_step, b_re, b_im, c_re, c_im):
    nb = S5_GROUPS // S5_BLK_GROUPS
    lam = lax.complex(lam_re.astype(F32), lam_im.astype(F32))
    step = jnp.exp(log_step.astype(F32))[:, None]
    a_bar = jnp.exp(lam * step)
    b_bar = ((a_bar - 1.0) / lam)[..., None] * lax.complex(b_re.astype(F32), b_im.astype(F32))
    r = np.arange(S5_SCAN_ROWS)

    def powers(k, keep):
        p = jnp.exp(lam[None] * step[None] * jnp.asarray(k, F32)[:, None, None])
        return jnp.where(jnp.asarray(keep)[:, None, None], p, 0.0)

    tabs = [powers(np.full(S5_SCAN_ROWS, k), r >= k) for k in (1, 2, 4)]
    tabs.append(powers(r + 1, r >= 0))
    tab = jnp.stack(tabs)
    tab = tab.reshape(4, S5_SCAN_ROWS, nb, S5_BLK_STATE).transpose(2, 0, 1, 3)
    atab = jnp.concatenate([tab.real, tab.imag], axis=1)

    eye = jnp.eye(S5_BLK_GROUPS, dtype=F32)
    bb = b_bar.reshape(nb, S5_BLK_GROUPS, S5_STATE, S5_GROUP)

    def in_mat(x):
        return jnp.einsum('ngpc,gh->ngchp', x, eye).reshape(nb, S5_BLK_CH, S5_BLK_STATE)

    bmat = jnp.concatenate([in_mat(bb.real), in_mat(bb.imag)], axis=-1)
    b_hi = bmat.astype(BF16)
    b_lo = (bmat - b_hi.astype(F32)).astype(BF16)
    cr = c_re.astype(F32).reshape(nb, S5_BLK_GROUPS, S5_GROUP, S5_STATE)
    ci = c_im.astype(F32).reshape(nb, S5_BLK_GROUPS, S5_GROUP, S5_STATE)

    def out_mat(x):
        return jnp.einsum('ngcp,gh->ngphc', x, eye).reshape(nb, S5_BLK_STATE, S5_BLK_CH)

    cmat = jnp.concatenate([out_mat(cr), -out_mat(ci)], axis=1).astype(BF16)
    return atab, b_hi, b_lo, cmat


def _s5_scan_tile(xr, xi, cr, ci, atab_ref):
    for idx, k in enumerate((1, 2, 4)):
        pr, pi = atab_ref[0, idx], atab_ref[0, 4 + idx]
        sr, si = pltpu.roll(xr, k, 0), pltpu.roll(xi, k, 0)
        xr, xi = xr + pr * sr - pi * si, xi + pr * si + pi * sr
    pr, pi = atab_ref[0, 3], atab_ref[0, 7]
    xr, xi = xr + pr * cr - pi * ci, xi + pr * ci + pi * cr
    last = S5_SCAN_ROWS - 1
    cr = jnp.broadcast_to(xr[last:last + 1, :], xr.shape)
    ci = jnp.broadcast_to(xi[last:last + 1, :], xi.shape)
    return xr, xi, cr, ci


def _s5_prompt_kernel(u_ref, x0_ref, atab_ref, bhi_ref, blo_ref, c_ref, d_ref, y_ref, st_ref, xs_ref, *, T):
    u = u_ref[0]
    u_hi = u.astype(BF16)
    u_lo = (u - u_hi.astype(F32)).astype(BF16)
    b_hi = bhi_ref[0]
    xs_ref[...] = (jnp.dot(u_hi, b_hi, preferred_element_type=F32)
                   + jnp.dot(u_lo, b_hi, preferred_element_type=F32)
                   + jnp.dot(u_hi, blo_ref[0], preferred_element_type=F32))
    n = S5_BLK_STATE
    R = S5_SCAN_ROWS

    def body(i, carry):
        cr, ci = carry
        rows = pl.ds(pl.multiple_of(i * R, R), R)
        xr, xi, cr, ci = _s5_scan_tile(xs_ref[rows, :n], xs_ref[rows, n:], cr, ci, atab_ref)
        xs_ref[rows, :n] = xr
        xs_ref[rows, n:] = xi
        return cr, ci

    x0 = x0_ref[0, 0]
    cr0 = jnp.broadcast_to(x0[0:1, :], (R, n))
    ci0 = jnp.broadcast_to(x0[1:2, :], (R, n))
    cr, ci = lax.fori_loop(0, T // R, body, (cr0, ci0))
    st_ref[0, 0] = jnp.concatenate([cr[0:1], ci[0:1]], axis=0)
    y = jnp.dot(xs_ref[...].astype(BF16), c_ref[0], preferred_element_type=F32) + d_ref[...] * u
    y_ref[0] = _gelu_tanh(y)


def s5_prompt(proj3, cols, x0, lam_re, lam_im, log_step, b_re, b_im, c_re, c_im, d):
    B, T, _ = proj3.shape
    assert T % S5_SCAN_ROWS == 0
    nb = S5_GROUPS // S5_BLK_GROUPS
    atab, b_hi, b_lo, cmat = _s5_params(lam_re, lam_im, log_step, b_re, b_im, c_re, c_im)
    x0b = x0.astype(F32).reshape(B, nb, S5_BLK_STATE, 2).transpose(0, 1, 3, 2)
    blk3 = lambda shape: pl.BlockSpec((1,) + shape, lambda b, j: (j, 0, 0))
    y, st = pl.pallas_call(
        functools.partial(_s5_prompt_kernel, T=T),
        grid=(B, nb),
        in_specs=[pl.BlockSpec((1, T, S5_BLK_CH), lambda b, j, o=cols['su'] // S5_BLK_CH: (b, 0, o + j)),
                  pl.BlockSpec((1, 1, 2, S5_BLK_STATE), lambda b, j: (b, j, 0, 0)),
                  pl.BlockSpec((1, 8, S5_SCAN_ROWS, S5_BLK_STATE), lambda b, j: (j, 0, 0, 0)),
                  blk3((S5_BLK_CH, 2 * S5_BLK_STATE)), blk3((S5_BLK_CH, 2 * S5_BLK_STATE)),
                  blk3((2 * S5_BLK_STATE, S5_BLK_CH)),
                  pl.BlockSpec((1, S5_BLK_CH), lambda b, j: (0, j))],
        out_specs=(pl.BlockSpec((1, T, S5_BLK_CH), lambda b, j: (b, 0, j)),
                   pl.BlockSpec((1, 1, 2, S5_BLK_STATE), lambda b, j: (b, j, 0, 0))),
        out_shape=(jax.ShapeDtypeStruct((B, T, S5_WIDTH), F32),
                   jax.ShapeDtypeStruct((B, nb, 2, S5_BLK_STATE), F32)),
        scratch_shapes=[pltpu.VMEM((T, 2 * S5_BLK_STATE), F32)],
        compiler_params=pltpu.CompilerParams(dimension_semantics=("parallel", "parallel"),
                                             vmem_limit_bytes=V7X_VMEM_LIMIT_BYTES),
        name="s5_prompt",
    )(proj3, x0b, atab, b_hi, b_lo, cmat, d.astype(F32).reshape(1, S5_WIDTH))
    st = st.transpose(0, 1, 3, 2).reshape(B, S5_GROUPS, S5_STATE, 2)
    return y, st


def _branch_norm_kernel(ro_ref, so_ref, no_ref, bn_ref, o_ref):
    off = 0
    for ref in (ro_ref, so_ref, no_ref):
        x = ref[...]
        w = x.shape[-1]
        y = x * lax.rsqrt(jnp.mean(x * x, axis=-1, keepdims=True) + EPS) * bn_ref[:, off:off + w]
        o_ref[:, off:off + w] = y.astype(o_ref.dtype)
        off += w


def branch_norm(ro, so, no, bn):
    m = ro.shape[0]
    tm = min(m, 256)
    assert m % tm == 0
    spec = lambda w: pl.BlockSpec((tm, w), lambda i: (i, 0))
    return pl.pallas_call(
        _branch_norm_kernel,
        grid=(m // tm,),
        in_specs=[spec(RET_WIDTH), spec(S5_WIDTH), spec(NSA_WIDTH), pl.BlockSpec((1, D_MODEL), lambda i: (0, 0))],
        out_specs=spec(D_MODEL),
        out_shape=jax.ShapeDtypeStruct((m, D_MODEL), BF16),
        compiler_params=pltpu.CompilerParams(dimension_semantics=("parallel",),
                                             vmem_limit_bytes=V7X_VMEM_LIMIT_BYTES),
        name="branch_norm",
    )(ro, so, no, bn.astype(F32).reshape(1, D_MODEL))


def _largest_divisor(n, cap):
    return max(d for d in range(1, min(n, cap) + 1) if n % d == 0)


def _t5_bucket(dist):
    max_exact = N_BUCKETS // 2
    d = jnp.maximum(dist, 0)
    large = max_exact + (jnp.log(jnp.maximum(d, 1).astype(F32) / max_exact)
                         / math.log(MAX_DISTANCE / max_exact) * (N_BUCKETS - max_exact)).astype(jnp.int32)
    return jnp.where(d < max_exact, d, jnp.minimum(large, N_BUCKETS - 1))


def _rotary(x, pos):
    half = HEAD_DIM // 2
    inv = 1.0 / (10000.0 ** jnp.linspace(0.0, 1.0, half, dtype=F32))
    ang = pos.astype(F32)[:, None] * inv[None]
    cos, sin = jnp.cos(ang)[None, :, None, :], jnp.sin(ang)[None, :, None, :]
    x1, x2 = x[..., :half], x[..., half:]
    return jnp.concatenate([x1 * cos - x2 * sin, x1 * sin + x2 * cos], axis=-1)


def _retention(q, k, v, s0, q0):
    B, T = q.shape[:2]
    pos = q0 + jnp.arange(T)
    q = _rotary(q, pos)
    k = _rotary(k, pos) * HEAD_DIM ** -0.5
    lg = jnp.log1p(-(2.0 ** (-5.0 - jnp.arange(RET_HEADS, dtype=F32))))
    c = _largest_divisor(T, RET_CHUNK)
    n = T // c
    i = jnp.arange(c)
    rel = i[:, None] - i[None, :]
    decay = jnp.where(rel[None] >= 0, jnp.exp(jnp.maximum(rel, 0)[None] * lg[:, None, None]), 0.0)
    q_dec = jnp.exp((i + 1)[:, None] * lg[None])[None, :, :, None]
    k_dec = jnp.exp((c - 1 - i)[:, None] * lg[None])[None, :, :, None]
    chunk_dec = jnp.exp(c * lg)[None, :, None, None]

    def to_chunks(a):
        return a.reshape(B, n, c, RET_HEADS, HEAD_DIM).swapaxes(0, 1)

    def step(s, qkv):
        qc, kc, vc = qkv
        inner = jnp.einsum('bihd,bjhd->bhij', qc, kc) * decay
        o = (jnp.einsum('bhij,bjhv->bihv', inner, vc)
             + jnp.einsum('bihd,bhdv->bihv', qc, s) * q_dec)
        s = s * chunk_dec + jnp.einsum('bjhd,bjhv->bhdv', kc * k_dec, vc)
        return s, o

    s, o = lax.scan(step, s0.astype(F32), (to_chunks(q), to_chunks(k), to_chunks(v)))
    return o.swapaxes(0, 1).reshape(B, T, RET_HEADS, HEAD_DIM), s


def _s5_scan(u, x0, lam_re, lam_im, log_step, b_re, b_im, c_re, c_im, d):
    B, T = u.shape[:2]
    ug = u.reshape(B, T, S5_GROUPS, S5_GROUP).astype(jnp.complex64)
    lam = lax.complex(lam_re.astype(F32), lam_im.astype(F32))
    a_bar = jnp.exp(lam * jnp.exp(log_step.astype(F32))[:, None])
    b_bar = ((a_bar - 1.0) / lam)[..., None] * lax.complex(b_re.astype(F32), b_im.astype(F32))
    bu = jnp.einsum('gpc,btgc->btgp', b_bar, ug)
    init = lax.complex(x0[..., 0].astype(F32), x0[..., 1].astype(F32))
    bu = bu.at[:, 0].add(a_bar[None] * init)
    a = jnp.broadcast_to(a_bar, bu.shape)

    def combine(e1, e2):
        a1, b1 = e1
        a2, b2 = e2
        return a1 * a2, a2 * b1 + b2

    _, xs = lax.associative_scan(combine, (a, bu), axis=1)
    cm = lax.complex(c_re.astype(F32), c_im.astype(F32))
    y = jnp.einsum('gcp,btgp->btgc', cm, xs).real.reshape(B, T, S5_WIDTH) + d.astype(F32) * u
    x_last = xs[:, -1]
    return y, jnp.stack([x_last.real, x_last.imag], axis=-1)


def _compress(x, pe, w1, w2):
    B, L = x.shape[:2]
    r = CMP_BLOCK // CMP_STRIDE
    n_full = L // CMP_STRIDE
    n_cmp = n_full - r + 1
    xs = x[:, :n_full * CMP_STRIDE].reshape(B, n_full, CMP_STRIDE, NSA_KV_HEADS, HEAD_DIM)
    w1 = w1.astype(F32)
    w1r = w1.reshape(r, CMP_STRIDE, HEAD_DIM, HEAD_DIM)
    h = jnp.einsum('ld,ldh->h', pe.astype(F32), w1)
    for j in range(r):
        h = h + jnp.einsum('bnsgd,sdh->bngh', xs[:, j:j + n_cmp], w1r[j])
    return jnp.einsum('bngh,he->bnge', jax.nn.gelu(h), w2.astype(F32))


def _nsa(q, k_cmp, v_cmp, k_slc, v_slc, k_win, v_win, gates, q0, rel_bias,
         pe_k, w1_k, w2_k, pe_v, w1_v, w2_v):
    B, T = q.shape[:2]
    G, J = NSA_KV_HEADS, NSA_HPG
    Lk = q0 + T
    qg = (q.astype(F32) * HEAD_DIM ** -0.5).reshape(B, T, G, J, HEAD_DIM)
    t_pos = q0 + jnp.arange(T)
    table = rel_bias.astype(F32).reshape(N_BUCKETS, G, J)

    kc = _compress(k_cmp.astype(F32), pe_k, w1_k, w2_k)
    vc = _compress(v_cmp.astype(F32), pe_v, w1_v, w2_v)
    n_cmp = kc.shape[1]
    c_start = jnp.arange(n_cmp) * CMP_STRIDE
    dist = t_pos[:, None] - (c_start + CMP_BLOCK - 1)[None]
    valid = (dist >= 0)[:, None, None, :]
    bias = jnp.transpose(table[_t5_bucket(dist)], (0, 2, 3, 1))
    s = jnp.einsum('btgjd,bngd->btgjn', qg, kc) + bias
    p_cmp = jax.nn.softmax(jnp.where(valid, s, NEG), axis=-1) * valid
    o_cmp = jnp.einsum('btgjn,bngd->btgjd', p_cmp, vc)

    n_sel = -(-Lk // SEL_BLOCK)
    s_start = jnp.arange(n_sel) * SEL_BLOCK
    overlap = ((c_start[:, None] < s_start[None] + SEL_BLOCK)
               & (c_start[:, None] + CMP_BLOCK > s_start[None])).astype(F32)
    p_sel = jnp.einsum('btgn,nm->btgm', p_cmp.sum(axis=3), overlap)
    cur = t_pos // SEL_BLOCK
    blk = jnp.arange(n_sel)
    forced = (blk[None] == 0) | (blk[None] == cur[:, None]) | (blk[None] == cur[:, None] - 1)
    blk_ok = (blk[None] <= cur[:, None])[None, :, None, :]
    score = jnp.where(blk_ok, p_sel + FORCE_SCORE * forced[None, :, None, :], NEG)
    n_top = min(SEL_TOP, n_sel)
    _, idx = lax.top_k(score, n_top)

    pad = n_sel * SEL_BLOCK - Lk

    def blocks(a):
        a = jnp.pad(a.astype(F32), ((0, 0), (0, pad), (0, 0), (0, 0)))
        return a.reshape(B, n_sel, SEL_BLOCK, G, HEAD_DIM).transpose(0, 3, 1, 2, 4)

    kb, vb = blocks(k_slc), blocks(v_slc)
    qb = _largest_divisor(T, SEL_Q_BLOCK)
    nq = T // qb
    b_ix = jnp.arange(B)[:, None, None, None]
    g_ix = jnp.arange(G)[None, None, :, None]
    g_ix5 = g_ix[..., None]
    off = jnp.arange(SEL_BLOCK)

    def sel_block(args):
        qc, ic, tc = args
        kg = kb[b_ix, g_ix, ic]
        vg = vb[b_ix, g_ix, ic]
        kpos = ic[..., None] * SEL_BLOCK + off
        d = tc[None, :, None, None, None] - kpos
        ok = (d >= 0) & (ic <= (tc // SEL_BLOCK)[None, :, None, None])[..., None]
        bias_s = jnp.moveaxis(table[_t5_bucket(d), g_ix5], -1, 3)
        sc = jnp.einsum('bqgjd,bqgkld->bqgjkl', qc, kg) + bias_s
        sc = jnp.where(ok[:, :, :, None], sc, NEG).reshape(B, qb, G, J, n_top * SEL_BLOCK)
        pr = jax.nn.softmax(sc, axis=-1).reshape(B, qb, G, J, n_top, SEL_BLOCK)
        return jnp.einsum('bqgjkl,bqgkld->bqgjd', pr, vg)

    o_slc = lax.map(sel_block, (qg.reshape(B, nq, qb, G, J, HEAD_DIM).swapaxes(0, 1),
                                idx.reshape(B, nq, qb, G, n_top).swapaxes(0, 1),
                                t_pos.reshape(nq, qb)))
    o_slc = o_slc.swapaxes(0, 1).reshape(B, T, G, J, HEAD_DIM)

    qw = _largest_divisor(T, WIN_Q_BLOCK)
    nw = T // qw
    rows = jnp.arange(nw)[:, None] * qw + jnp.arange(qw + WINDOW)[None]
    kwin = k_win.astype(F32)[:, rows]
    vwin = v_win.astype(F32)[:, rows]
    kpos = q0 - WINDOW + rows
    dw = t_pos.reshape(nw, qw)[:, :, None] - kpos[:, None, :]
    okw = (kpos[:, None, :] >= 0) & (dw >= 0) & (dw <= WINDOW)
    bias_w = jnp.transpose(table[_t5_bucket(dw)], (0, 3, 4, 1, 2))
    sw = jnp.einsum('bcqgjd,bckgd->bcgjqk', qg.reshape(B, nw, qw, G, J, HEAD_DIM), kwin) + bias_w
    pw = jax.nn.softmax(jnp.where(okw[:, None, None], sw, NEG), axis=-1)
    o_win = jnp.einsum('bcgjqk,bckgd->bcqgjd', pw, vwin).reshape(B, T, G, J, HEAD_DIM)

    g = jax.nn.sigmoid(gates.astype(F32)).reshape(B, T, G, J, 3)
    o = g[..., 0:1] * o_cmp + g[..., 1:2] * o_slc + g[..., 2:3] * o_win
    return o.reshape(B, T, NSA_WIDTH)


def _jnp_rms(x, gain):
    xf = x.astype(F32)
    xf = xf * lax.rsqrt(jnp.mean(xf * xf, axis=-1, keepdims=True) + EPS)
    return xf * gain.astype(F32)


def _block(x, q0, win_buf, ret_s0, s5_s0, win_k0, win_v0, past_cmp_k, past_cmp_v,
           past_slc_k, past_slc_v, lw, rel_bias):
    B, T, _ = x.shape
    G = NSA_KV_HEADS
    M = B * T
    x2 = x.reshape(M, D_MODEL)
    h = rmsnorm(x2, lw['norm_mix'], BF16)
    proj = matmul(h, lw['w_in'])[:, :IN_COLS].reshape(B, T, IN_COLS)
    (rq, rk, rv, rg, su, nq_, kc, vc, ks, vs, kw, vw, ng) = jnp.split(
        proj, np.cumsum(IN_SPLITS)[:-1].tolist(), axis=-1)

    def heads(a, n):
        return a.reshape(B, T, n, HEAD_DIM)

    ro, ret_s = _retention(heads(rq, RET_HEADS), heads(rk, RET_HEADS), heads(rv, RET_HEADS), ret_s0, q0)
    ro = ro * lax.rsqrt(jnp.mean(ro * ro, axis=-1, keepdims=True) + EPS)
    ro = jax.nn.silu(rg) * ro.reshape(B, T, RET_WIDTH)

    sy, s5_s = _s5_scan(su, s5_s0, lw['s5_lambda_re'], lw['s5_lambda_im'], lw['s5_log_step'],
                        lw['s5_b_re'], lw['s5_b_im'], lw['s5_c_re'], lw['s5_c_im'], lw['s5_d'])
    sy = jax.nn.gelu(sy)
    so = sy * jax.nn.sigmoid(jnp.einsum('btc,ce->bte', sy, lw['s5_w_glu']))

    kc, vc, ks, vs, kw, vw = (heads(a, G) for a in (kc, vc, ks, vs, kw, vw))
    k_cmp_all = jnp.concatenate([past_cmp_k, kc], axis=1)
    v_cmp_all = jnp.concatenate([past_cmp_v, vc], axis=1)
    k_slc_all = jnp.concatenate([past_slc_k, ks], axis=1)
    v_slc_all = jnp.concatenate([past_slc_v, vs], axis=1)
    k_win_all = jnp.concatenate([win_k0, kw], axis=1)
    v_win_all = jnp.concatenate([win_v0, vw], axis=1)
    no = _nsa(heads(nq_, NSA_HEADS), k_cmp_all, v_cmp_all, k_slc_all, v_slc_all, k_win_all, v_win_all,
              ng.reshape(B, T, NSA_HEADS, 3), q0, rel_bias,
              lw['cmp_pe_k'], lw['cmp_w1_k'], lw['cmp_w2_k'], lw['cmp_pe_v'], lw['cmp_w1_v'], lw['cmp_w2_v'])

    bn = lw['branch_norm']
    mix = jnp.concatenate([_jnp_rms(ro, bn[:RET_WIDTH]),
                           _jnp_rms(so, bn[RET_WIDTH:RET_WIDTH + S5_WIDTH]),
                           _jnp_rms(no, bn[RET_WIDTH + S5_WIDTH:])], axis=-1).astype(BF16)
    x2 = matmul(mix.reshape(M, D_MODEL), lw['w_out'], res=x2)

    h = rmsnorm(x2, lw['norm_ffn'], BF16)
    up = matmul(h, lw['w_up'], act="relu2", out_dtype=BF16)
    x2 = matmul(up, lw['w_down'], res=x2)

    cmp_rows = jnp.stack([kc, vc], axis=1)
    slc_rows = jnp.stack([ks, vs], axis=1)
    win_new = jnp.stack([k_win_all[:, -win_buf:], v_win_all[:, -win_buf:]], axis=1)
    return x2.reshape(B, T, D_MODEL), cmp_rows, slc_rows, win_new, ret_s, s5_s


_COL_NAMES = ('rq', 'rk', 'rv', 'rg', 'su', 'nq', 'kc', 'vc', 'ks', 'vs', 'kw', 'vw', 'ng')
COLS = {name: int(off) for name, off in zip(_COL_NAMES, np.concatenate([[0], np.cumsum(IN_SPLITS)]))}


def _block_prompt(x, win_buf, lw, rel_bias):
    B, T, _ = x.shape
    G = NSA_KV_HEADS
    M = B * T
    assert T >= win_buf
    x2 = x.reshape(M, D_MODEL)
    h = rmsnorm(x2, lw['norm_mix'], BF16)
    proj3 = matmul(h, lw['w_in']).reshape(B, T, -1)

    ro, ret_s = retention_prompt(proj3, COLS)
    zero_s5 = jnp.zeros((B, S5_GROUPS, S5_STATE, 2), F32)
    sy, s5_s = s5_prompt(proj3, COLS, zero_s5, lw['s5_lambda_re'], lw['s5_lambda_im'], lw['s5_log_step'],
                         lw['s5_b_re'], lw['s5_b_im'], lw['s5_c_re'], lw['s5_c_im'], lw['s5_d'])
    sy2 = sy.reshape(M, S5_WIDTH)
    so = matmul(sy2, lw['s5_w_glu'], res=sy2, act="glu")
    kcmp, vcmp = compress_prompt(proj3, COLS['kc'], COLS['vc'], lw['cmp_pe_k'], lw['cmp_w1_k'], lw['cmp_w2_k'],
                                 lw['cmp_pe_v'], lw['cmp_w1_v'], lw['cmp_w2_v'])
    no = nsa_prompt(proj3, COLS, kcmp, vcmp, rel_bias)

    mix = branch_norm(ro.reshape(M, RET_WIDTH), so, no.reshape(M, NSA_WIDTH), lw['branch_norm'])
    x2 = matmul(mix, lw['w_out'], res=x2)
    h = rmsnorm(x2, lw['norm_ffn'], BF16)
    up = matmul(h, lw['w_up'], act="relu2", out_dtype=BF16)
    x2 = matmul(up, lw['w_down'], res=x2)

    rows = lambda name: proj3[:, :, COLS[name]:COLS[name] + KV_WIDTH].reshape(B, T, G, HEAD_DIM)
    cmp_rows = jnp.stack([rows('kc'), rows('vc')], axis=1)
    slc_rows = jnp.stack([rows('ks'), rows('vs')], axis=1)
    win_new = jnp.stack([rows('kw')[:, T - win_buf:], rows('vw')[:, T - win_buf:]], axis=1)
    return x2.reshape(B, T, D_MODEL), cmp_rows, slc_rows, win_new, ret_s, s5_s


def _gather_pages(pool, page_table, layer):
    pages = pool[page_table, layer]
    db, npg = page_table.shape
    pages = pages.transpose(0, 2, 1, 3, 4, 5).reshape(db, 2, npg * PAGE_SIZE, NSA_KV_HEADS, HEAD_DIM)
    return pages[:, 0], pages[:, 1]


def kernel(x_prompt, x_sample, cache_cmp, cache_slc, cache_win, state_ret, state_s5, page_table,
           rel_bias, norm_mix, w_in, s5_lambda_re, s5_lambda_im, s5_log_step, s5_b_re, s5_b_im,
           s5_c_re, s5_c_im, s5_d, s5_w_glu, cmp_pe_k, cmp_w1_k, cmp_w2_k, cmp_pe_v, cmp_w1_v,
           cmp_w2_v, branch_norm, w_out, norm_ffn, w_up, w_down, norm_final):
    past_len = page_table.shape[1] * PAGE_SIZE
    win_buf = cache_win.shape[3]
    bp = x_prompt.shape[0]
    G, HD = NSA_KV_HEADS, HEAD_DIM
    xp, xs = x_prompt, x_sample
    empty = jnp.zeros((bp, 0, G, HD), x_prompt.dtype)
    zero_win = jnp.zeros((bp, WINDOW, G, HD), x_prompt.dtype)
    zero_ret = jnp.zeros((bp, RET_HEADS, HD, HD), F32)
    zero_s5 = jnp.zeros((bp, S5_GROUPS, S5_STATE, 2), F32)
    in_pad = _round_up(IN_COLS, 512) - IN_COLS
    cmp_p, cmp_s, slc_p, slc_s, win_p, win_s, ret_p, ret_s, s5_p, s5_s = ([] for _ in range(10))
    for l in range(DEPTH):
        lw = {
            'norm_mix': norm_mix[l],
            'w_in': jnp.pad(w_in[l].astype(BF16), ((0, 0), (0, in_pad))),
            's5_lambda_re': s5_lambda_re[l], 's5_lambda_im': s5_lambda_im[l], 's5_log_step': s5_log_step[l],
            's5_b_re': s5_b_re[l], 's5_b_im': s5_b_im[l], 's5_c_re': s5_c_re[l], 's5_c_im': s5_c_im[l],
            's5_d': s5_d[l], 's5_w_glu': s5_w_glu[l],
            'cmp_pe_k': cmp_pe_k[l], 'cmp_w1_k': cmp_w1_k[l], 'cmp_w2_k': cmp_w2_k[l],
            'cmp_pe_v': cmp_pe_v[l], 'cmp_w1_v': cmp_w1_v[l], 'cmp_w2_v': cmp_w2_v[l],
            'branch_norm': branch_norm[l], 'w_out': w_out[l].astype(BF16),
            'norm_ffn': norm_ffn[l], 'w_up': w_up[l].astype(BF16), 'w_down': w_down[l].astype(BF16),
        }
        xp, c_rows, s_rows, w_new, r_new, s_new = _block_prompt(xp, win_buf, lw, rel_bias)
        cmp_p.append(c_rows); slc_p.append(s_rows); win_p.append(w_new); ret_p.append(r_new); s5_p.append(s_new)
        pck, pcv = _gather_pages(cache_cmp, page_table, l)
        psk, psv = _gather_pages(cache_slc, page_table, l)
        front = ((0, 0), (WINDOW - win_buf, 0), (0, 0), (0, 0))
        wk0 = jnp.pad(cache_win[l, :, 0], front)
        wv0 = jnp.pad(cache_win[l, :, 1], front)
        xs, c_rows, s_rows, w_new, r_new, s_new = _block(
            xs, past_len, win_buf, state_ret[l], state_s5[l], wk0, wv0, pck, pcv, psk, psv, lw, rel_bias)
        cmp_s.append(c_rows); slc_s.append(s_rows); win_s.append(w_new); ret_s.append(r_new); s5_s.append(s_new)
    y_prompt = rmsnorm(xp.reshape(-1, D_MODEL), norm_final, F32).reshape(xp.shape)
    y_sample = rmsnorm(xs.reshape(-1, D_MODEL), norm_final, F32).reshape(xs.shape)
    return (y_prompt, y_sample,
            jnp.stack(cmp_p, axis=1), jnp.stack(cmp_s, axis=1),
            jnp.stack(slc_p, axis=1), jnp.stack(slc_s, axis=1),
            jnp.stack(win_p, axis=0), jnp.stack(win_s, axis=0),
            jnp.stack(ret_p, axis=0), jnp.stack(ret_s, axis=0),
            jnp.stack(s5_p, axis=0), jnp.stack(s5_s, axis=0))
```

```python
import functools
import math

import jax
import jax.numpy as jnp
import numpy as np
from jax import lax
from jax.experimental import pallas as pl
from jax.experimental.pallas import tpu as pltpu

F32 = jnp.float32
BF16 = jnp.bfloat16

D_MODEL = 4096
DEPTH = 2
PAGE_SIZE = 128
HEAD_DIM = 128
RET_WIDTH = 1024
RET_HEADS = 8
RET_CHUNK = 128
S5_WIDTH = 1024
S5_GROUP = 16
S5_GROUPS = 64
S5_STATE = 64
NSA_WIDTH = 2048
NSA_HEADS = 16
NSA_KV_HEADS = 4
NSA_HPG = 4
KV_WIDTH = 512
CMP_BLOCK = 32
CMP_STRIDE = 16
SEL_BLOCK = 64
SEL_TOP = 16
WINDOW = 512
SEL_Q_BLOCK = 32
WIN_Q_BLOCK = 128
FORCE_SCORE = 1e4
NEG = -1e30
N_BUCKETS = 32
MAX_DISTANCE = 128
D_FF = 4 * D_MODEL
EPS = 1e-6
IN_SPLITS = (RET_WIDTH, RET_WIDTH, RET_WIDTH, RET_WIDTH, S5_WIDTH, NSA_WIDTH,
             KV_WIDTH, KV_WIDTH, KV_WIDTH, KV_WIDTH, KV_WIDTH, KV_WIDTH, 3 * NSA_HEADS)
IN_COLS = sum(IN_SPLITS)

V7X_VMEM_LIMIT_BYTES = 48 * 1024 * 1024
LANE = 128


def _round_up(n, m):
    return -(-n // m) * m


def _rmsnorm_kernel(x_ref, g_ref, o_ref):
    x = x_ref[...].astype(F32)
    ms = jnp.mean(x * x, axis=-1, keepdims=True)
    o_ref[...] = (x * lax.rsqrt(ms + EPS) * g_ref[...].astype(F32)).astype(o_ref.dtype)


def rmsnorm(x2d, gain, out_dtype):
    m, d = x2d.shape
    tm = min(m, 256)
    assert m % tm == 0
    return pl.pallas_call(
        _rmsnorm_kernel,
        grid=(m // tm,),
        in_specs=[pl.BlockSpec((tm, d), lambda i: (i, 0)),
                  pl.BlockSpec((1, d), lambda i: (0, 0))],
        out_specs=pl.BlockSpec((tm, d), lambda i: (i, 0)),
        out_shape=jax.ShapeDtypeStruct((m, d), out_dtype),
        compiler_params=pltpu.CompilerParams(dimension_semantics=("parallel",),
                                             vmem_limit_bytes=V7X_VMEM_LIMIT_BYTES),
        name="rmsnorm",
    )(x2d, gain.reshape(1, d))


def _mm_kernel(*refs, nk, act, has_res):
    if has_res:
        a_ref, w_ref, r_ref, o_ref, acc_ref = refs
    else:
        a_ref, w_ref, o_ref, acc_ref = refs
        r_ref = None
    k = pl.program_id(2)

    @pl.when(k == 0)
    def _():
        acc_ref[...] = jnp.zeros_like(acc_ref)

    acc_ref[...] += jnp.dot(a_ref[...].astype(BF16), w_ref[...], preferred_element_type=F32)

    @pl.when(k == nk - 1)
    def _():
        acc = acc_ref[...]
        if act == "relu2":
            acc = jnp.square(jnp.maximum(acc, 0.0))
        if act == "glu":
            acc = r_ref[...].astype(F32) * jax.nn.sigmoid(acc)
        elif has_res:
            acc = acc + r_ref[...].astype(F32)
        o_ref[...] = acc.astype(o_ref.dtype)


def matmul(a, w, *, res=None, act=None, out_dtype=F32):
    m, kdim = a.shape
    n = w.shape[1]
    tm = min(m, 1024)
    tn = 512 if n % 512 == 0 else (256 if n % 256 == 0 else 128)
    tk = min(kdim, 4096)
    assert m % tm == 0 and n % tn == 0 and kdim % tk == 0
    nk = kdim // tk
    in_specs = [pl.BlockSpec((tm, tk), lambda i, j, k: (i, k)),
                pl.BlockSpec((tk, tn), lambda i, j, k: (k, j))]
    args = [a, w]
    if res is not None:
        in_specs.append(pl.BlockSpec((tm, tn), lambda i, j, k: (i, j)))
        args.append(res)
    return pl.pallas_call(
        functools.partial(_mm_kernel, nk=nk, act=act, has_res=res is not None),
        grid=(m // tm, n // tn, nk),
        in_specs=in_specs,
        out_specs=pl.BlockSpec((tm, tn), lambda i, j, k: (i, j)),
        out_shape=jax.ShapeDtypeStruct((m, n), out_dtype),
        scratch_shapes=[pltpu.VMEM((tm, tn), F32)],
        compiler_params=pltpu.CompilerParams(
            dimension_semantics=("parallel", "parallel", "arbitrary"),
            vmem_limit_bytes=V7X_VMEM_LIMIT_BYTES),
        name="matmul",
    )(*args)


def _gelu_tanh(x):
    return 0.5 * x * (1.0 + jnp.tanh(math.sqrt(2.0 / math.pi) * (x + 0.044715 * (x * x * x))))


def _dot_nt(a, b):
    return lax.dot_general(a, b, (((1,), (1,)), ((), ())), preferred_element_type=F32)


def _dot_tn(a, b):
    return lax.dot_general(a, b, (((0,), (0,)), ((), ())), preferred_element_type=F32)


def _compress_rows(x_ref, n_full, pe_ref, w1_ref, w2_ref):
    pieces = [x_ref[0, pl.ds(s, n_full, stride=CMP_STRIDE), :] for s in range(CMP_STRIDE)]
    x = jnp.concatenate(pieces, axis=1).astype(BF16)
    half = CMP_STRIDE * HEAD_DIM
    z0 = jnp.dot(x, w1_ref[:half, :], preferred_element_type=F32)
    z1 = jnp.dot(x, w1_ref[half:, :], preferred_element_type=F32)
    z1 = pltpu.roll(z1, n_full - 1, 0)
    pe = jnp.broadcast_to(pe_ref[...], (8, CMP_BLOCK * HEAD_DIM)).astype(BF16)
    h0 = jnp.dot(pe, w1_ref[...], preferred_element_type=F32)[0:1, :]
    h = _gelu_tanh(z0 + z1 + h0)
    return jnp.dot(h.astype(BF16), w2_ref[...], preferred_element_type=F32)


def _compress_prompt_kernel(xk_ref, xv_ref, pek_ref, w1k_ref, w2k_ref, pev_ref, w1v_ref, w2v_ref,
                            kc_ref, vc_ref, *, n_full):
    for x_ref, pe_ref, w1_ref, w2_ref, o_ref in ((xk_ref, pek_ref, w1k_ref, w2k_ref, kc_ref),
                                                  (xv_ref, pev_ref, w1v_ref, w2v_ref, vc_ref)):
        out = _compress_rows(x_ref, n_full, pe_ref, w1_ref, w2_ref).astype(o_ref.dtype)
        n_pad = o_ref.shape[2]
        o_ref[0, 0, :n_full, :] = out
        if n_pad > n_full:
            o_ref[0, 0, n_full:, :] = jnp.zeros((n_pad - n_full, HEAD_DIM), o_ref.dtype)


def compress_prompt(proj3, col_k, col_v, pe_k, w1_k, w2_k, pe_v, w1_v, w2_v):
    B, T, _ = proj3.shape
    n_full = T // CMP_STRIDE
    n_pad = _round_up(n_full, LANE)
    flat = CMP_BLOCK * HEAD_DIM
    wspec = [pl.BlockSpec((1, flat), lambda b, g: (0, 0)),
             pl.BlockSpec((flat, HEAD_DIM), lambda b, g: (0, 0)),
             pl.BlockSpec((HEAD_DIM, HEAD_DIM), lambda b, g: (0, 0))]
    out_sds = jax.ShapeDtypeStruct((B, NSA_KV_HEADS, n_pad, HEAD_DIM), BF16)
    ospec = pl.BlockSpec((1, 1, n_pad, HEAD_DIM), lambda b, g: (b, g, 0, 0))
    return pl.pallas_call(
        functools.partial(_compress_prompt_kernel, n_full=n_full),
        grid=(B, NSA_KV_HEADS),
        in_specs=[pl.BlockSpec((1, T, HEAD_DIM), lambda b, g: (b, 0, col_k // HEAD_DIM + g)),
                  pl.BlockSpec((1, T, HEAD_DIM), lambda b, g: (b, 0, col_v // HEAD_DIM + g))] + wspec + wspec,
        out_specs=(ospec, ospec),
        out_shape=(out_sds, out_sds),
        compiler_params=pltpu.CompilerParams(dimension_semantics=("parallel", "parallel"),
                                             vmem_limit_bytes=V7X_VMEM_LIMIT_BYTES),
        name="nsa_compress_prompt",
    )(proj3, proj3,
      pe_k.reshape(1, flat), w1_k.reshape(flat, HEAD_DIM).astype(BF16), w2_k.astype(BF16),
      pe_v.reshape(1, flat), w1_v.reshape(flat, HEAD_DIM).astype(BF16), w2_v.astype(BF16))


ATT_TILE = 128
N_BIAS_TILES = 4


def _bucket_table():
    d = np.arange(MAX_DISTANCE)
    max_exact = N_BUCKETS // 2
    large = max_exact + (np.log(np.maximum(d, 1).astype(np.float32) / np.float32(max_exact))
                         / np.float32(math.log(MAX_DISTANCE / max_exact))
                         * np.float32(N_BUCKETS - max_exact)).astype(np.int32)
    return np.where(d < max_exact, d, np.minimum(large, N_BUCKETS - 1)).astype(np.int32)


def _bias_of_distance(rel_bias, dist):
    bt = _bucket_table()
    buckets = bt[np.clip(dist, 0, MAX_DISTANCE - 1)]
    b = jnp.moveaxis(rel_bias.astype(F32)[buckets], -1, 0)
    return jnp.where(jnp.asarray(dist >= 0)[None], b, NEG)


def _softmax_step(s, m_ref, l_ref, acc_ref, v):
    m_old = m_ref[...]
    m_new = jnp.maximum(m_old, jnp.max(s, axis=-1, keepdims=True))
    alpha = jnp.exp(m_old - m_new)
    p = jnp.exp(s - m_new)
    l_ref[...] = alpha * l_ref[...] + jnp.sum(p, axis=-1, keepdims=True)
    acc_ref[...] = alpha * acc_ref[...] + jnp.dot(p.astype(BF16), v, preferred_element_type=F32)
    m_ref[...] = m_new


def _nsa_prompt_kernel(q_ref, kc_ref, vc_ref, ks_ref, vs_ref, kw_ref, vw_ref, gate_ref,
                       bcmp_ref, btile_ref, ovl_ref, exp_ref, o_ref,
                       ksb, vsb, kwb, vwb, m_ref, l_ref, acc_ref, mask_ref, *, n_sel, n_top):
    tq = ATT_TILE
    qi = pl.program_id(2)
    n_kt = mask_ref.shape[0]
    n_sel_pad = exp_ref.shape[0]

    @pl.when(qi == 0)
    def _():
        ksb[...] = ks_ref[0].astype(BF16)
        vsb[...] = vs_ref[0].astype(BF16)
        kwb[...] = kw_ref[0].astype(BF16)
        vwb[...] = vw_ref[0].astype(BF16)

    row = lax.broadcasted_iota(jnp.int32, (tq, LANE), 0)
    lane = lax.broadcasted_iota(jnp.int32, (tq, LANE), 1)
    t_pos = qi * tq + row
    gates = jax.nn.sigmoid(gate_ref[0, 0])

    valid = t_pos >= lane * CMP_STRIDE + (CMP_BLOCK - 1)
    kc = kc_ref[0, 0]
    vc = vc_ref[0, 0]
    psum = jnp.zeros((tq, LANE), F32)
    qs = []
    for j in range(NSA_HPG):
        q = (q_ref[0, :, j * HEAD_DIM:(j + 1) * HEAD_DIM] * HEAD_DIM ** -0.5).astype(BF16)
        qs.append(q)
        s = jnp.where(valid, _dot_nt(q, kc) + bcmp_ref[j], NEG)
        e = jnp.where(valid, jnp.exp(s - jnp.max(s, axis=-1, keepdims=True)), 0.0)
        den = jnp.sum(e, axis=-1, keepdims=True)
        p = e / jnp.where(den > 0.0, den, 1.0)
        psum = psum + p
        o_ref[0, :, j * HEAD_DIM:(j + 1) * HEAD_DIM] = gates[:, 3 * j:3 * j + 1] * jnp.dot(
            p.astype(BF16), vc, preferred_element_type=F32)

    hi = psum.astype(BF16)
    r1 = psum - hi.astype(F32)
    mid = r1.astype(BF16)
    lo = (r1 - mid.astype(F32)).astype(BF16)
    ovl = ovl_ref[...]
    p_sel = _dot_nt(ovl, hi) + _dot_nt(ovl, mid) + _dot_nt(ovl, lo)
    blk = lax.broadcasted_iota(jnp.int32, (n_sel_pad, tq), 0)
    cur = (qi * tq + lax.broadcasted_iota(jnp.int32, (n_sel_pad, tq), 1)) // SEL_BLOCK
    forced = (blk == 0) | (blk == cur) | (blk == cur - 1)
    score = jnp.where(blk <= cur, p_sel + jnp.where(forced, FORCE_SCORE, 0.0), NEG)
    rank = jnp.zeros((n_sel_pad, tq), F32)
    for jb in range(n_sel):
        other = score[jb:jb + 1, :]
        beats = (other > score) | ((other == score) & (blk > jb))
        rank = rank + jnp.where(beats, 1.0, 0.0)
    sel = jnp.where((rank < n_top) & (blk <= cur) & (blk < n_sel), 1.0, 0.0).astype(BF16)
    for kt in range(n_kt):
        mask_ref[kt] = _dot_tn(sel, exp_ref[:, kt * tq:(kt + 1) * tq])

    def attend(j, kb, vb, kt, bias_idx, use_mask):
        off = pl.multiple_of(kt * tq, tq)
        s = _dot_nt(qs[j], kb[pl.ds(off, tq), :]) + btile_ref[j, bias_idx]
        if use_mask:
            s = jnp.where(mask_ref[kt] > 0.5, s, NEG)
        _softmax_step(s, m_ref, l_ref, acc_ref, vb[pl.ds(off, tq), :])

    def reset():
        m_ref[...] = jnp.full(m_ref.shape, NEG, F32)
        l_ref[...] = jnp.zeros(l_ref.shape, F32)
        acc_ref[...] = jnp.zeros(acc_ref.shape, F32)

    for j in range(NSA_HPG):
        cols = slice(j * HEAD_DIM, (j + 1) * HEAD_DIM)
        reset()

        def slc_body(kt, c, j=j):
            attend(j, ksb, vsb, kt, jnp.minimum(qi - kt, 2), True)
            return c

        lax.fori_loop(0, qi + 1, slc_body, 0)
        o_ref[0, :, cols] = o_ref[0, :, cols] + gates[:, 3 * j + 1:3 * j + 2] * (
            acc_ref[...] / l_ref[...])

        reset()

        def win_body(kt, c, j=j):
            dd = qi - kt
            attend(j, kwb, vwb, kt, jnp.where(dd == WINDOW // ATT_TILE, 3, jnp.minimum(dd, 2)), False)
            return c

        lax.fori_loop(jnp.maximum(qi - WINDOW // ATT_TILE, 0), qi + 1, win_body, 0)
        o_ref[0, :, cols] = o_ref[0, :, cols] + gates[:, 3 * j + 2:3 * j + 3] * (
            acc_ref[...] / l_ref[...])


def nsa_prompt(proj3, cols, kcmp, vcmp, rel_bias):
    B, T, _ = proj3.shape
    tq = ATT_TILE
    assert T % tq == 0 and T % SEL_BLOCK == 0
    G, J = NSA_KV_HEADS, NSA_HPG
    nq = T // tq
    n_sel = T // SEL_BLOCK
    n_top = min(SEL_TOP, n_sel)
    n_sel_pad = _round_up(n_sel, 16)
    n_pad = kcmp.shape[2]
    assert n_pad == LANE, "one lane tile of compressed blocks"

    t = np.arange(T)[:, None]
    n = np.arange(n_pad)[None, :]
    bcmp = _bias_of_distance(rel_bias, np.maximum(t - (n * CMP_STRIDE + CMP_BLOCK - 1), 0))
    r = np.arange(tq)[:, None]
    c = np.arange(tq)[None, :]
    far = np.full((tq, tq), MAX_DISTANCE)
    edge = np.where(r <= c, MAX_DISTANCE, -1)
    btile = _bias_of_distance(rel_bias, np.stack([r - c, tq + r - c, far, edge]))
    cmp_start = np.arange(n_pad) * CMP_STRIDE
    sel_start = np.arange(n_sel_pad) * SEL_BLOCK
    ovl = ((cmp_start[None, :] < sel_start[:, None] + SEL_BLOCK)
           & (cmp_start[None, :] + CMP_BLOCK > sel_start[:, None])
           & (np.arange(n_pad)[None, :] < T // CMP_STRIDE - 1))
    ovl = jnp.asarray(ovl, BF16)
    expand = jnp.asarray(np.arange(T)[None, :] // SEL_BLOCK == np.arange(n_sel_pad)[:, None], BF16)
    gates = proj3[:, :, cols['ng']:cols['ng'] + 3 * NSA_HEADS].reshape(B, T, G, 3 * J).transpose(0, 2, 1, 3)

    kv_spec = lambda name: pl.BlockSpec((1, T, HEAD_DIM),
                                        lambda b, g, i, o=cols[name] // HEAD_DIM: (b, 0, o + g))
    cmp_spec = pl.BlockSpec((1, 1, n_pad, HEAD_DIM), lambda b, g, i: (b, g, 0, 0))
    return pl.pallas_call(
        functools.partial(_nsa_prompt_kernel, n_sel=n_sel, n_top=n_top),
        grid=(B, G, nq),
        in_specs=[pl.BlockSpec((1, tq, J * HEAD_DIM),
                               lambda b, g, i, o=cols['nq'] // (J * HEAD_DIM): (b, i, o + g)),
                  cmp_spec, cmp_spec,
                  kv_spec('ks'), kv_spec('vs'), kv_spec('kw'), kv_spec('vw'),
                  pl.BlockSpec((1, 1, tq, 3 * J), lambda b, g, i: (b, g, i, 0)),
                  pl.BlockSpec((J, tq, n_pad), lambda b, g, i: (g, i, 0)),
                  pl.BlockSpec((J, N_BIAS_TILES, tq, tq), lambda b, g, i: (g, 0, 0, 0)),
                  pl.BlockSpec((n_sel_pad, n_pad), lambda b, g, i: (0, 0)),
                  pl.BlockSpec((n_sel_pad, T), lambda b, g, i: (0, 0))],
        out_specs=pl.BlockSpec((1, tq, J * HEAD_DIM), lambda b, g, i: (b, i, g)),
        out_shape=jax.ShapeDtypeStruct((B, T, NSA_WIDTH), F32),
        scratch_shapes=[pltpu.VMEM((T, HEAD_DIM), BF16)] * 4 + [
            pltpu.VMEM((tq, LANE), F32), pltpu.VMEM((tq, LANE), F32), pltpu.VMEM((tq, HEAD_DIM), F32),
            pltpu.VMEM((nq, tq, tq), F32)],
        compiler_params=pltpu.CompilerParams(
            dimension_semantics=("parallel", "parallel", "arbitrary"),
            vmem_limit_bytes=V7X_VMEM_LIMIT_BYTES),
        name="nsa_prompt",
    )(proj3, kcmp, vcmp, proj3, proj3, proj3, proj3, gates, bcmp, btile, ovl, expand)


SLC_CLASS_TILES = 4


def _rows_softmax_pv(s_tiles, v_tiles):
    m = s_tiles[0]
    for s in s_tiles[1:]:
        m = jnp.maximum(m, s)
    m = jnp.max(m, axis=-1, keepdims=True)
    l = None
    o = None
    for s, v in zip(s_tiles, v_tiles):
        p = jnp.exp(s - m)
        l = p if l is None else l + p
        pv = jnp.dot(p.astype(BF16), v, preferred_element_type=F32)
        o = pv if o is None else o + pv
    return o / jnp.sum(l, axis=-1, keepdims=True)


def _nsa_prompt2_kernel(q_ref, kc_ref, vc_ref, ks_ref, vs_ref, kw_ref, vw_ref, gate_ref,
                        bcmp_ref, btile_ref, ovl_ref, pick_ref, cvec_ref, kaug_s_ref, kaug_w_ref, o_ref,
                        ksb, vsb, kwb, vwb, s_ref, *, n_sel, n_top, T):
    tq = ATT_TILE
    J = NSA_HPG
    qi = pl.program_id(2)
    n_sel_pad = pick_ref.shape[0]
    nq = T // tq

    @pl.when(qi == 0)
    def _():
        ksb[:, :HEAD_DIM] = ks_ref[0].astype(BF16)
        ksb[:, HEAD_DIM:] = kaug_s_ref[...]
        vsb[...] = vs_ref[0].astype(BF16)
        kwb[:WINDOW, :HEAD_DIM] = jnp.zeros((WINDOW, HEAD_DIM), BF16)
        kwb[WINDOW:, :HEAD_DIM] = kw_ref[0].astype(BF16)
        kwb[:, HEAD_DIM:] = kaug_w_ref[...]
        vwb[:WINDOW, :] = jnp.zeros((WINDOW, HEAD_DIM), BF16)
        vwb[WINDOW:, :] = vw_ref[0].astype(BF16)

    def stack(f):
        return jnp.concatenate([f(j) for j in range(J)], axis=0)

    def put(vals, first):
        for j in range(J):
            cols = slice(j * HEAD_DIM, (j + 1) * HEAD_DIM)
            v = vals[j * tq:(j + 1) * tq]
            o_ref[0, :, cols] = v if first else o_ref[0, :, cols] + v

    gates = jax.nn.sigmoid(gate_ref[0, 0])
    gate = lambda c: stack(lambda j: gates[:, 3 * j + c:3 * j + c + 1])
    q_all = stack(lambda j: q_ref[0, :, j * HEAD_DIM:(j + 1) * HEAD_DIM] * HEAD_DIM ** -0.5).astype(BF16)

    row = lax.broadcasted_iota(jnp.int32, (tq, LANE), 0)
    lane = lax.broadcasted_iota(jnp.int32, (tq, LANE), 1)
    valid1 = qi * tq + row >= lane * CMP_STRIDE + (CMP_BLOCK - 1)
    valid = stack(lambda j: valid1)
    s = jnp.where(valid, _dot_nt(q_all, kc_ref[0, 0]) + stack(lambda j: bcmp_ref[j]), NEG)
    e = jnp.where(valid, jnp.exp(s - jnp.max(s, axis=-1, keepdims=True)), 0.0)
    den = jnp.sum(e, axis=-1, keepdims=True)
    p = e / jnp.where(den > 0.0, den, 1.0)
    put(gate(0) * jnp.dot(p.astype(BF16), vc_ref[0, 0], preferred_element_type=F32), True)
    psum = p[0:tq]
    for j in range(1, J):
        psum = psum + p[j * tq:(j + 1) * tq]

    hi = psum.astype(BF16)
    r1 = psum - hi.astype(F32)
    mid = r1.astype(BF16)
    lo = (r1 - mid.astype(F32)).astype(BF16)
    ovl = ovl_ref[...]
    p_sel = _dot_nt(ovl, hi) + _dot_nt(ovl, mid) + _dot_nt(ovl, lo)
    blk = lax.broadcasted_iota(jnp.int32, (n_sel_pad, tq), 0)
    cur = (qi * tq + lax.broadcasted_iota(jnp.int32, (n_sel_pad, tq), 1)) // SEL_BLOCK
    forced = (blk == 0) | (blk == cur) | (blk == cur - 1)
    score = jnp.where(blk <= cur, p_sel + jnp.where(forced, FORCE_SCORE, 0.0), NEG)
    rank = jnp.zeros((n_sel_pad, tq), F32)
    for jb in range(n_sel):
        other = score[jb:jb + 1, :]
        beats = (other > score) | ((other == score) & (blk > jb))
        rank = rank + jnp.where(beats, 1.0, 0.0)
    sel = jnp.where((rank < n_top) & (blk <= cur) & (blk < n_sel), 1.0, 0.0).astype(BF16)
    aug = (_dot_tn(sel, pick_ref[...]) - cvec_ref[...]).astype(BF16)
    q_aug = jnp.concatenate([q_all, stack(lambda j: aug)], axis=1)

    bt = lambda i: stack(lambda j: btile_ref[j, i])

    for cls in range(-(-nq // SLC_CLASS_TILES)):
        n_t = min((cls + 1) * SLC_CLASS_TILES, nq)

        @pl.when(qi // SLC_CLASS_TILES == cls)
        def _(n_t=n_t):
            for kt in range(n_t):
                s_ref[kt] = _dot_nt(q_aug, ksb[kt * tq:(kt + 1) * tq, :])
            s_ref[qi] = s_ref[qi] + bt(0)

            @pl.when(qi > 0)
            def _():
                s_ref[qi - 1] = s_ref[qi - 1] + bt(1)

            o = _rows_softmax_pv([s_ref[kt] for kt in range(n_t)],
                                 [vsb[kt * tq:(kt + 1) * tq, :] for kt in range(n_t)])
            put(gate(1) * o, False)

    n_w = WINDOW // tq + 1
    s_tiles, v_tiles = [], []
    for w in range(n_w):
        rows = pl.ds(pl.multiple_of((qi + w) * tq, tq), tq)
        s = _dot_nt(q_aug, kwb[rows, :])
        if w == 0:
            s = s + bt(3)
        elif w == n_w - 2:
            s = s + bt(1)
        elif w == n_w - 1:
            s = s + bt(0)
        s_tiles.append(s)
        v_tiles.append(vwb[rows, :])
    put(gate(2) * _rows_softmax_pv(s_tiles, v_tiles), False)


def nsa_prompt2(proj3, cols, kcmp, vcmp, rel_bias):
    B, T, _ = proj3.shape
    tq = ATT_TILE
    assert T % tq == 0 and T % SEL_BLOCK == 0 and WINDOW % tq == 0
    G, J = NSA_KV_HEADS, NSA_HPG
    nq = T // tq
    n_sel = T // SEL_BLOCK
    n_top = min(SEL_TOP, n_sel)
    n_sel_pad = _round_up(n_sel, 16)
    assert n_sel_pad < LANE
    n_pad = kcmp.shape[2]
    assert n_pad == LANE, "one lane tile of compressed blocks"

    t = np.arange(T)[:, None]
    n = np.arange(n_pad)[None, :]
    bcmp = _bias_of_distance(rel_bias, np.maximum(t - (n * CMP_STRIDE + CMP_BLOCK - 1), 0))
    r = np.arange(tq)[:, None]
    c = np.arange(tq)[None, :]
    far = np.full((tq, tq), MAX_DISTANCE)
    edge = np.where(r <= c, MAX_DISTANCE, -1)
    dist = np.stack([r - c, tq + r - c, far, edge])
    btile = _bias_of_distance(rel_bias, dist)
    b_far = rel_bias.astype(F32)[_bucket_table()[MAX_DISTANCE - 1]][:, None, None, None]
    btile = jnp.where(jnp.asarray(dist >= 0)[None], btile - b_far, NEG)
    cmp_start = np.arange(n_pad) * CMP_STRIDE
    sel_start = np.arange(n_sel_pad) * SEL_BLOCK
    ovl = ((cmp_start[None, :] < sel_start[:, None] + SEL_BLOCK)
           & (cmp_start[None, :] + CMP_BLOCK > sel_start[:, None])
           & (np.arange(n_pad)[None, :] < T // CMP_STRIDE - 1))
    ovl = jnp.asarray(ovl, BF16)
    pick = jnp.asarray(np.arange(LANE)[None, :] == np.arange(n_sel_pad)[:, None], BF16)
    lane_i = np.arange(LANE)
    cvec = jnp.asarray(((lane_i < n_sel) | (lane_i == n_sel_pad))[None, :], F32)
    big = -NEG
    kaug_s = jnp.asarray(np.where(np.arange(T)[:, None] // SEL_BLOCK == lane_i[None, :], big, 0.0), BF16)
    kaug_w = jnp.asarray(np.where((np.arange(T + WINDOW)[:, None] < WINDOW) & (lane_i[None, :] == n_sel_pad),
                                  big, 0.0), BF16)
    gates = proj3[:, :, cols['ng']:cols['ng'] + 3 * NSA_HEADS].reshape(B, T, G, 3 * J).transpose(0, 2, 1, 3)

    kv_spec = lambda name: pl.BlockSpec((1, T, HEAD_DIM),
                                        lambda b, g, i, o=cols[name] // HEAD_DIM: (b, 0, o + g))
    cmp_spec = pl.BlockSpec((1, 1, n_pad, HEAD_DIM), lambda b, g, i: (b, g, 0, 0))
    const2 = lambda shape: pl.BlockSpec(shape, lambda b, g, i: (0, 0))
    return pl.pallas_call(
        functools.partial(_nsa_prompt2_kernel, n_sel=n_sel, n_top=n_top, T=T),
        grid=(B, G, nq),
        in_specs=[pl.BlockSpec((1, tq, J * HEAD_DIM),
                               lambda b, g, i, o=cols['nq'] // (J * HEAD_DIM): (b, i, o + g)),
                  cmp_spec, cmp_spec,
                  kv_spec('ks'), kv_spec('vs'), kv_spec('kw'), kv_spec('vw'),
                  pl.BlockSpec((1, 1, tq, 3 * J), lambda b, g, i: (b, g, i, 0)),
                  pl.BlockSpec((J, tq, n_pad), lambda b, g, i: (g, i, 0)),
                  pl.BlockSpec((J, N_BIAS_TILES, tq, tq), lambda b, g, i: (g, 0, 0, 0)),
                  const2((n_sel_pad, n_pad)), const2((n_sel_pad, LANE)), const2((1, LANE)),
                  const2((T, LANE)), const2((T + WINDOW, LANE))],
        out_specs=pl.BlockSpec((1, tq, J * HEAD_DIM), lambda b, g, i: (b, i, g)),
        out_shape=jax.ShapeDtypeStruct((B, T, NSA_WIDTH), F32),
        scratch_shapes=[pltpu.VMEM((T, 2 * HEAD_DIM), BF16), pltpu.VMEM((T, HEAD_DIM), BF16),
                        pltpu.VMEM((T + WINDOW, 2 * HEAD_DIM), BF16), pltpu.VMEM((T + WINDOW, HEAD_DIM), BF16),
                        pltpu.VMEM((nq, J * tq, tq), F32)],
        compiler_params=pltpu.CompilerParams(
            dimension_semantics=("parallel", "parallel", "arbitrary"),
            vmem_limit_bytes=V7X_VMEM_LIMIT_BYTES),
        name="nsa_prompt",
    )(proj3, kcmp, vcmp, proj3, proj3, proj3, proj3, gates, bcmp, btile, ovl, pick, cvec, kaug_s, kaug_w)


PAGE_ROWS = PAGE_SIZE * NSA_KV_HEADS
STRIDES_PER_PAGE = PAGE_SIZE // CMP_STRIDE
PACK_ROWS = 16


def _cache_rows(cache):
    return cache.reshape(-1, HEAD_DIM)


def _cmp_partial_kernel(pt_ref, page_ref, w1k_ref, w1v_ref, z_ref):
    G = NSA_KV_HEADS
    half = CMP_STRIDE * HEAD_DIM
    for kv, w1_ref in ((0, w1k_ref), (1, w1v_ref)):
        xs = []
        for g in range(G):
            pieces = [page_ref[pl.ds(kv * PAGE_ROWS + s * G + g, STRIDES_PER_PAGE, stride=CMP_STRIDE * G), :]
                      for s in range(CMP_STRIDE)]
            xs.append(jnp.concatenate(pieces, axis=1))
        x = jnp.concatenate(xs, axis=0).astype(BF16)
        for j in range(CMP_BLOCK // CMP_STRIDE):
            z = jnp.dot(x, w1_ref[j * half:(j + 1) * half, :], preferred_element_type=F32)
            for g in range(G):
                z_ref[0, kv, j, g] = z[g * STRIDES_PER_PAGE:(g + 1) * STRIDES_PER_PAGE]


def cmp_partial(cache_rows, page_table, layer, w1_k, w1_v):
    B, n_pages = page_table.shape
    flat = CMP_BLOCK * HEAD_DIM
    n_str = n_pages * STRIDES_PER_PAGE
    wspec = pl.BlockSpec((flat, HEAD_DIM), lambda b, p, pt: (0, 0))
    return pl.pallas_call(
        _cmp_partial_kernel,
        grid_spec=pltpu.PrefetchScalarGridSpec(
            num_scalar_prefetch=1,
            grid=(B, n_pages),
            in_specs=[pl.BlockSpec((2 * PAGE_ROWS, HEAD_DIM), lambda b, p, pt: (pt[b, p] * DEPTH + layer, 0)),
                      wspec, wspec],
            out_specs=pl.BlockSpec((1, 2, 2, NSA_KV_HEADS, STRIDES_PER_PAGE, HEAD_DIM),
                                   lambda b, p, pt: (b, 0, 0, 0, p, 0))),
        out_shape=jax.ShapeDtypeStruct((B, 2, 2, NSA_KV_HEADS, n_str, HEAD_DIM), F32),
        compiler_params=pltpu.CompilerParams(dimension_semantics=("parallel", "arbitrary"),
                                             vmem_limit_bytes=V7X_VMEM_LIMIT_BYTES),
        name="nsa_cmp_partial",
    )(page_table, cache_rows, w1_k.reshape(flat, HEAD_DIM).astype(BF16), w1_v.reshape(flat, HEAD_DIM).astype(BF16))


def _pad_rows(x, rows):
    extra = rows - x.shape[0]
    return jnp.concatenate([x, jnp.zeros((extra, x.shape[1]), x.dtype)], axis=0) if extra else x


def _cmp_attn_sample_kernel(z_ref, pek_ref, w1k_ref, w2k_ref, pev_ref, w1v_ref, w2v_ref, q_ref, bias_ref,
                            ovl_ref, o_ref, psel_ref, *, q0):
    T = q_ref.shape[1]
    J = NSA_HPG
    n_str = z_ref.shape[4]

    def finish(kv, pe_ref, w1_ref, w2_ref):
        z1 = pltpu.roll(z_ref[0, kv, 1, 0], n_str - 1, 0)
        pe = jnp.broadcast_to(pe_ref[...], (PACK_ROWS, CMP_BLOCK * HEAD_DIM)).astype(BF16)
        h0 = jnp.dot(pe, w1_ref[...], preferred_element_type=F32)[0:1, :]
        h = _gelu_tanh(z_ref[0, kv, 0, 0] + z1 + h0)
        return jnp.dot(h.astype(BF16), w2_ref[...], preferred_element_type=F32).astype(BF16)

    kc = finish(0, pek_ref, w1k_ref, w2k_ref)
    vc = finish(1, pev_ref, w1v_ref, w2v_ref)
    stack = lambda f: jnp.concatenate([f(j) for j in range(J)], axis=0)
    q_all = stack(lambda j: q_ref[0, :, j * HEAD_DIM:(j + 1) * HEAD_DIM] * HEAD_DIM ** -0.5).astype(BF16)
    row = lax.broadcasted_iota(jnp.int32, (T, n_str), 0)
    lane = lax.broadcasted_iota(jnp.int32, (T, n_str), 1)
    valid1 = (q0 + row >= lane * CMP_STRIDE + (CMP_BLOCK - 1)) & (lane < n_str - 1)
    valid = stack(lambda j: valid1)
    s = jnp.where(valid, _dot_nt(q_all, kc) + stack(lambda j: bias_ref[j]), NEG)
    e = jnp.where(valid, jnp.exp(s - jnp.max(s, axis=-1, keepdims=True)), 0.0)
    den = jnp.sum(e, axis=-1, keepdims=True)
    p = e / jnp.where(den > 0.0, den, 1.0)
    o_ref[0, 0] = jnp.dot(p.astype(BF16), vc, preferred_element_type=F32)
    psum = p[0:T]
    for j in range(1, J):
        psum = psum + p[j * T:(j + 1) * T]
    psum = _pad_rows(psum, PACK_ROWS)
    hi = psum.astype(BF16)
    r1 = psum - hi.astype(F32)
    mid = r1.astype(BF16)
    lo = (r1 - mid.astype(F32)).astype(BF16)
    ovl = ovl_ref[...]
    p_sel = (jnp.dot(hi, ovl, preferred_element_type=F32) + jnp.dot(mid, ovl, preferred_element_type=F32)
             + jnp.dot(lo, ovl, preferred_element_type=F32))
    psel_ref[0, 0] = p_sel[:T]


def _rank_select_kernel(psel_ref, tpos_ref, out_ref, score_ref, rank_ref, *, n_sel, n_top):
    shape = psel_ref.shape
    blk = lax.broadcasted_iota(jnp.int32, shape, 0)
    cur = jnp.broadcast_to(tpos_ref[...], shape) // SEL_BLOCK
    forced = (blk == 0) | (blk == cur) | (blk == cur - 1)
    ok = (blk <= cur) & (blk < n_sel)
    score = jnp.where(ok, psel_ref[...] + jnp.where(forced, FORCE_SCORE, 0.0), NEG)
    score_ref[...] = score
    rank_ref[...] = jnp.zeros(shape, F32)

    def body(jb, c):
        other = jnp.broadcast_to(score_ref[pl.ds(jb, 1), :], shape)
        beats = (other > score) | ((other == score) & (blk > jb))
        rank_ref[...] = rank_ref[...] + jnp.where(beats, 1.0, 0.0)
        return c

    lax.fori_loop(0, n_sel, body, 0)
    out_ref[...] = jnp.where((rank_ref[...] < n_top) & ok, 0.0, NEG)


def _slc_win_sample_kernel(pt_ref, page_ref, qT_ref, seladd_ref, blast_ref, selnew_ref, ksn_ref, vsn_ref,
                           kwn_ref, vwn_ref, bnew_ref, cw_ref, bwin_ref, ocmp_ref, gate_ref, o_ref,
                           m_ref, l_ref, acc_ref, *, win_buf):
    G = NSA_KV_HEADS
    T = ksn_ref.shape[1]
    p = pl.program_id(1)
    last = pl.num_programs(1) - 1
    lane_group = lax.broadcasted_iota(jnp.int32, (1, LANE), 1) // (LANE // G)

    @pl.when(p == 0)
    def _():
        m_ref[...] = jnp.full(m_ref.shape, NEG, F32)
        l_ref[...] = jnp.zeros(l_ref.shape, F32)
        acc_ref[...] = jnp.zeros(acc_ref.shape, F32)

    def scores(k_of_g):
        s = None
        for g in range(G):
            sg = jnp.dot(k_of_g(g).astype(BF16), qT_ref[0, g], preferred_element_type=F32)
            s = sg if s is None else s + sg
        return s

    def weighted_values(pT, v_of_g):
        o = None
        for g in range(G):
            pg = jnp.where(lane_group == g, pT, 0.0).astype(BF16)
            og = _dot_tn(v_of_g(g).astype(BF16), pg)
            o = og if o is None else o + og
        return o

    def accumulate(sT, v_of_g):
        m_old = m_ref[0:1, :]
        m_new = jnp.maximum(m_old, jnp.max(sT, axis=0, keepdims=True))
        alpha = jnp.exp(m_old - m_new)
        pT = jnp.exp(sT - m_new)
        l_ref[...] = jnp.broadcast_to(alpha * l_ref[0:1, :] + jnp.sum(pT, axis=0, keepdims=True), l_ref.shape)
        acc_ref[...] = alpha * acc_ref[...] + weighted_values(pT, v_of_g)
        m_ref[...] = jnp.broadcast_to(m_new, m_ref.shape)

    k_page = lambda g: page_ref[pl.ds(g, PAGE_SIZE, stride=G), :]
    v_page = lambda g: page_ref[pl.ds(PAGE_ROWS + g, PAGE_SIZE, stride=G), :]
    key = lax.broadcasted_iota(jnp.int32, (PAGE_SIZE, LANE), 0)
    blocks_per_page = PAGE_SIZE // SEL_BLOCK
    mask = seladd_ref[0, 0, blocks_per_page - 1:blocks_per_page, :]
    for i in range(blocks_per_page - 2, -1, -1):
        mask = jnp.where(key < (i + 1) * SEL_BLOCK, seladd_ref[0, 0, i:i + 1, :], mask)
    is_last = (p == last).astype(F32)
    accumulate(scores(k_page) + mask + blast_ref[...] * is_last, v_page)

    @pl.when(p == last)
    def _():
        new = lambda ref: (lambda g: _pad_rows(ref[0, :, g * HEAD_DIM:(g + 1) * HEAD_DIM], PACK_ROWS))
        accumulate(scores(new(ksn_ref)) + bnew_ref[...] + selnew_ref[0], new(vsn_ref))
        o_slc = (acc_ref[...] / l_ref[0:1, :]).T

        k_win = lambda g: cw_ref[pl.ds(g, win_buf, stride=G), :]
        v_win = lambda g: cw_ref[pl.ds(win_buf * G + g, win_buf, stride=G), :]
        s_w = scores(k_win) + bwin_ref[...]
        s_n = scores(new(kwn_ref)) + bnew_ref[...]
        m = jnp.maximum(jnp.max(s_w, axis=0, keepdims=True), jnp.max(s_n, axis=0, keepdims=True))
        p_w = jnp.exp(s_w - m)
        p_n = jnp.exp(s_n - m)
        den = jnp.sum(p_w, axis=0, keepdims=True) + jnp.sum(p_n, axis=0, keepdims=True)
        o_win = ((weighted_values(p_w, v_win) + weighted_values(p_n, new(vwn_ref))) / den).T

        gates = jax.nn.sigmoid(gate_ref[0])
        o_ref[0] = gates[:, 0:1] * ocmp_ref[0] + gates[:, 1:2] * o_slc + gates[:, 2:3] * o_win


def nsa_sample(proj3, cols, cache_cmp, cache_slc, cache_win, page_table, layer, rel_bias,
               pe_k, w1_k, w2_k, pe_v, w1_v, w2_v):
    B, T, _ = proj3.shape
    G, J, H = NSA_KV_HEADS, NSA_HPG, NSA_HEADS
    n_pages = page_table.shape[1]
    past = n_pages * PAGE_SIZE
    q0 = past
    win_buf = cache_win.shape[3]
    L = G * J * T
    assert L == LANE and T <= PACK_ROWS and T < CMP_STRIDE and past % SEL_BLOCK == 0 and T <= SEL_BLOCK
    assert win_buf == min(WINDOW, past)
    n_str = past // CMP_STRIDE
    n_sel = past // SEL_BLOCK + 1
    n_top = min(SEL_TOP, n_sel)
    n_sel_rows = _round_up(n_sel, 8)
    n_sel_lanes = _round_up(n_sel, LANE)
    flat = CMP_BLOCK * HEAD_DIM
    b_far = rel_bias.astype(F32)[_bucket_table()[MAX_DISTANCE - 1]]

    def lanes(x):
        x = jnp.moveaxis(x, 0, -1)
        return jnp.broadcast_to(x[..., None], x.shape + (T,)).reshape(x.shape[:-1] + (L,))

    def lane_bias(dist, ok):
        b = _bias_of_distance(rel_bias, np.maximum(dist, 0)) - b_far[:, None, None]
        b = jnp.where(jnp.asarray(ok)[None], b, NEG)
        return jnp.moveaxis(b.reshape(G, J, dist.shape[0], T), 2, 0).reshape(dist.shape[0], L)

    ti = np.arange(T)[None, :]
    z = cmp_partial(_cache_rows(cache_cmp), page_table, layer, w1_k, w1_v)
    n = np.arange(n_str)[None, :]
    bcmp = _bias_of_distance(rel_bias, np.maximum(q0 + np.arange(T)[:, None] - (n * CMP_STRIDE + CMP_BLOCK - 1), 0))
    cmp_start = np.arange(n_str) * CMP_STRIDE
    sel_start = np.arange(n_sel_lanes) * SEL_BLOCK
    ovl = ((cmp_start[:, None] < sel_start[None, :] + SEL_BLOCK) & (cmp_start[:, None] + CMP_BLOCK > sel_start[None, :])
           & (np.arange(n_str)[:, None] < n_str - 1) & (np.arange(n_sel_lanes)[None, :] < n_sel))
    wspec = [pl.BlockSpec((1, flat), lambda b, g: (0, 0)),
             pl.BlockSpec((flat, HEAD_DIM), lambda b, g: (0, 0)),
             pl.BlockSpec((HEAD_DIM, HEAD_DIM), lambda b, g: (0, 0))]
    o_cmp, p_sel = pl.pallas_call(
        functools.partial(_cmp_attn_sample_kernel, q0=q0),
        grid=(B, G),
        in_specs=[pl.BlockSpec((1, 2, 2, 1, n_str, HEAD_DIM), lambda b, g: (b, 0, 0, g, 0, 0))] + wspec + wspec + [
            pl.BlockSpec((1, T, J * HEAD_DIM), lambda b, g, o=cols['nq'] // (J * HEAD_DIM): (b, 0, o + g)),
            pl.BlockSpec((J, T, n_str), lambda b, g: (g, 0, 0)),
            pl.BlockSpec((n_str, n_sel_lanes), lambda b, g: (0, 0))],
        out_specs=(pl.BlockSpec((1, 1, J * T, HEAD_DIM), lambda b, g: (b, g, 0, 0)),
                   pl.BlockSpec((1, 1, T, n_sel_lanes), lambda b, g: (b, g, 0, 0))),
        out_shape=(jax.ShapeDtypeStruct((B, G, J * T, HEAD_DIM), F32),
                   jax.ShapeDtypeStruct((B, G, T, n_sel_lanes), F32)),
        compiler_params=pltpu.CompilerParams(dimension_semantics=("parallel", "parallel"),
                                             vmem_limit_bytes=V7X_VMEM_LIMIT_BYTES),
        name="nsa_cmp_attn_sample",
    )(z, pe_k.reshape(1, flat), w1_k.reshape(flat, HEAD_DIM).astype(BF16), w2_k.astype(BF16),
      pe_v.reshape(1, flat), w1_v.reshape(flat, HEAD_DIM).astype(BF16), w2_v.astype(BF16),
      proj3, bcmp, jnp.asarray(ovl, BF16))

    n_bgt = B * G * T
    psel_t = p_sel.reshape(n_bgt, n_sel_lanes)[:, :n_sel_rows].T
    tpos = jnp.asarray(np.tile(q0 + np.arange(T), B * G)[None, :], jnp.int32)
    seladd = pl.pallas_call(
        functools.partial(_rank_select_kernel, n_sel=n_sel, n_top=n_top),
        out_shape=jax.ShapeDtypeStruct((n_sel_rows, n_bgt), F32),
        scratch_shapes=[pltpu.VMEM((n_sel_rows, n_bgt), F32), pltpu.VMEM((n_sel_rows, n_bgt), F32)],
        name="nsa_rank_select",
    )(psel_t, tpos)
    seladd = seladd.T.reshape(B, G, 1, T, n_sel_rows)
    seladd = jnp.broadcast_to(seladd, (B, G, J, T, n_sel_rows)).reshape(B, L, n_sel_rows)
    bpp = PAGE_SIZE // SEL_BLOCK
    sel_past = seladd[:, :, :n_sel - 1].reshape(B, L, n_pages, bpp).transpose(0, 2, 3, 1)
    sel_new = seladd[:, :, n_sel - 1].reshape(B, 1, L)

    q = proj3[:, :, cols['nq']:cols['nq'] + NSA_WIDTH].reshape(B, T, G, J, HEAD_DIM) * HEAD_DIM ** -0.5
    q_t = q.transpose(0, 2, 4, 3, 1).reshape(B, G, HEAD_DIM, J * T)
    place = jnp.asarray(np.arange(G)[:, None, None] == (np.arange(L) // (J * T))[None, None, :])
    q_pad = jnp.where(place[None], jnp.tile(q_t, (1, 1, 1, G)), 0.0).astype(BF16)
    ki = np.arange(PAGE_SIZE)[:, None]
    b_last = lane_bias(PAGE_SIZE + ti - ki, np.ones((PAGE_SIZE, T), bool))
    kn = np.arange(PACK_ROWS)[:, None]
    b_new = lane_bias(ti - kn, (ti - kn >= 0) & (kn < T))
    kw = np.arange(win_buf)[:, None]
    b_win = lane_bias(win_buf + ti - kw, win_buf + ti - kw <= WINDOW)
    gates = proj3[:, :, cols['ng']:cols['ng'] + 3 * H].reshape(B, T, G, J, 3).transpose(0, 2, 3, 1, 4).reshape(B, L, 3)
    win_rows = 2 * win_buf * G
    new_spec = lambda name: pl.BlockSpec((1, T, KV_WIDTH), lambda b, p, pt, o=cols[name] // KV_WIDTH: (b, 0, o))
    const = lambda shape: pl.BlockSpec(shape, lambda b, p, pt: (0, 0))
    per_b = lambda shape: pl.BlockSpec((1,) + shape, lambda b, p, pt: (b,) + (0,) * len(shape))
    out = pl.pallas_call(
        functools.partial(_slc_win_sample_kernel, win_buf=win_buf),
        grid_spec=pltpu.PrefetchScalarGridSpec(
            num_scalar_prefetch=1,
            grid=(B, n_pages),
            in_specs=[pl.BlockSpec((2 * PAGE_ROWS, HEAD_DIM), lambda b, p, pt: (pt[b, p] * DEPTH + layer, 0)),
                      per_b((G, HEAD_DIM, L)),
                      pl.BlockSpec((1, 1, bpp, L), lambda b, p, pt: (b, p, 0, 0)),
                      const((PAGE_SIZE, L)), per_b((1, L)),
                      new_spec('ks'), new_spec('vs'), new_spec('kw'), new_spec('vw'),
                      const((PACK_ROWS, L)),
                      pl.BlockSpec((win_rows, HEAD_DIM), lambda b, p, pt: (layer * B + b, 0)),
                      const((win_buf, L)), per_b((L, HEAD_DIM)), per_b((L, 3))],
            out_specs=per_b((L, HEAD_DIM)),
            scratch_shapes=[pltpu.VMEM((8, L), F32), pltpu.VMEM((8, L), F32), pltpu.VMEM((HEAD_DIM, L), F32)]),
        out_shape=jax.ShapeDtypeStruct((B, L, HEAD_DIM), F32),
        compiler_params=pltpu.CompilerParams(dimension_semantics=("parallel", "arbitrary"),
                                             vmem_limit_bytes=V7X_VMEM_LIMIT_BYTES),
        name="nsa_slc_win_sample",
    )(page_table, _cache_rows(cache_slc), q_pad, sel_past, b_last, sel_new, proj3, proj3, proj3, proj3, b_new,
      cache_win.reshape(-1, HEAD_DIM), b_win, o_cmp.reshape(B, L, HEAD_DIM), gates)
    return out.reshape(B, G, J, T, HEAD_DIM).transpose(0, 3, 1, 2, 4).reshape(B, T, NSA_WIDTH)


def _retention_tables(T, q0):
    c = _largest_divisor(T, RET_CHUNK)
    cp = max(c, RET_CHUNK)
    lg = np.log1p(-(2.0 ** (-5.0 - np.arange(RET_HEADS, dtype=np.float32)))).astype(np.float32)
    i = np.arange(cp)
    rel = i[:, None] - i[None, :]
    inside = (i < c)[:, None] & (i < c)[None, :]
    decay = np.where((rel >= 0) & inside, np.exp(np.maximum(rel, 0)[None] * lg[:, None, None]), 0.0)
    q_dec = np.broadcast_to(np.exp((i + 1)[None, :, None] * lg[:, None, None]), (RET_HEADS, cp, HEAD_DIM))
    k_dec = np.where((i < c)[None, :, None], np.exp((c - 1 - i)[None, :, None] * lg[:, None, None]), 0.0)
    k_dec = np.broadcast_to(k_dec, (RET_HEADS, cp, HEAD_DIM))
    chunk_dec = np.broadcast_to(np.exp(c * lg)[:, None, None], (RET_HEADS, 8, HEAD_DIM))
    half = HEAD_DIM // 2
    inv = (1.0 / (10000.0 ** np.linspace(0.0, 1.0, half, dtype=np.float32))).astype(np.float32)
    ang = (q0 + np.arange(T)).astype(np.float32)[:, None] * inv[None]
    cos, sin = np.cos(ang), np.sin(ang)
    cosf = np.concatenate([cos, cos], axis=1)
    sinf = np.concatenate([-sin, sin], axis=1)
    f = lambda a: jnp.asarray(a, F32)
    return c, cp, f(decay), f(q_dec), f(k_dec), f(chunk_dec), f(cosf), f(sinf)


def _retention_kernel(q_ref, k_ref, v_ref, g_ref, s0_ref, cos_ref, sin_ref, dec_ref, qd_ref, kd_ref, cd_ref,
                      o_ref, s_ref, *, c, cp, n):
    decay = dec_ref[0]
    q_dec = qd_ref[0]
    k_dec = kd_ref[0]
    chunk_dec = cd_ref[0, 0:1, :]
    half = HEAD_DIM // 2

    def load(ref, rows):
        x = ref[rows, :]
        if cp > c:
            x = jnp.concatenate([x, jnp.zeros((cp - c, HEAD_DIM), x.dtype)], axis=0)
        return x

    def rot(x, cos, sin):
        return x * cos + pltpu.roll(x, half, 1) * sin

    def body(i, s):
        rows = pl.ds(pl.multiple_of(i * c, c), c)
        cos, sin = load(cos_ref, rows), load(sin_ref, rows)
        q = rot(load(q_ref.at[0], rows), cos, sin)
        k = rot(load(k_ref.at[0], rows), cos, sin) * HEAD_DIM ** -0.5
        v = load(v_ref.at[0], rows).astype(BF16)
        qb = q.astype(BF16)
        inner = _dot_nt(qb, k.astype(BF16)) * decay
        o = (jnp.dot(inner.astype(BF16), v, preferred_element_type=F32)
             + jnp.dot(qb, s.astype(BF16), preferred_element_type=F32) * q_dec)
        s = s * chunk_dec + _dot_tn((k * k_dec).astype(BF16), v)
        o = o * lax.rsqrt(jnp.mean(o * o, axis=-1, keepdims=True) + EPS)
        g = g_ref[0, rows, :]
        o_ref[0, rows, :] = g * jax.nn.sigmoid(g) * o[:c]
        return s

    s_ref[0, 0] = lax.fori_loop(0, n, body, s0_ref[0, 0])


def retention(proj3, cols, s0, q0):
    B, T, _ = proj3.shape
    c, cp, decay, q_dec, k_dec, chunk_dec, cosf, sinf = _retention_tables(T, q0)
    col = lambda name: pl.BlockSpec((1, T, HEAD_DIM), lambda b, h, o=cols[name] // HEAD_DIM: (b, 0, o + h))
    tab = lambda r: pl.BlockSpec((1, r, HEAD_DIM), lambda b, h: (h, 0, 0))
    full = pl.BlockSpec((T, HEAD_DIM), lambda b, h: (0, 0))
    state = pl.BlockSpec((1, 1, HEAD_DIM, HEAD_DIM), lambda b, h: (b, h, 0, 0))
    return pl.pallas_call(
        functools.partial(_retention_kernel, c=c, cp=cp, n=T // c),
        grid=(B, RET_HEADS),
        in_specs=[col('rq'), col('rk'), col('rv'), col('rg'), state, full, full,
                  tab(cp), tab(cp), tab(cp), tab(8)],
        out_specs=(pl.BlockSpec((1, T, HEAD_DIM), lambda b, h: (b, 0, h)), state),
        out_shape=(jax.ShapeDtypeStruct((B, T, RET_WIDTH), F32),
                   jax.ShapeDtypeStruct((B, RET_HEADS, HEAD_DIM, HEAD_DIM), F32)),
        compiler_params=pltpu.CompilerParams(dimension_semantics=("parallel", "parallel"),
                                             vmem_limit_bytes=V7X_VMEM_LIMIT_BYTES),
        name="retention",
    )(proj3, proj3, proj3, proj3, s0.astype(F32), cosf, sinf, decay, q_dec, k_dec, chunk_dec)


S5_BLK_GROUPS = 8
S5_BLK_STATE = S5_BLK_GROUPS * S5_STATE
S5_BLK_CH = S5_BLK_GROUPS * S5_GROUP
S5_SCAN_ROWS = 8


def _s5_params(lam_re, lam_im, log_step, b_re, b_im, c_re, c_im):
    nb = S5_GROUPS // S5_BLK_GROUPS
    lam = lax.complex(lam_re.astype(F32), lam_im.astype(F32))
    step = jnp.exp(log_step.astype(F32))[:, None]
    a_bar = jnp.exp(lam * step)
    b_bar = ((a_bar - 1.0) / lam)[..., None] * lax.complex(b_re.astype(F32), b_im.astype(F32))
    r = np.arange(S5_SCAN_ROWS)

    def powers(k, keep):
        p = jnp.exp(lam[None] * step[None] * jnp.asarray(k, F32)[:, None, None])
        return jnp.where(jnp.asarray(keep)[:, None, None], p, 0.0)

    tabs = [powers(np.full(S5_SCAN_ROWS, k), r >= k) for k in (1, 2, 4)]
    tabs.append(powers(r + 1, r >= 0))
    tab = jnp.stack(tabs)
    tab = tab.reshape(4, S5_SCAN_ROWS, nb, S5_BLK_STATE).transpose(2, 0, 1, 3)
    atab = jnp.concatenate([tab.real, tab.imag], axis=1)

    eye = jnp.eye(S5_BLK_GROUPS, dtype=F32)
    bb = b_bar.reshape(nb, S5_BLK_GROUPS, S5_STATE, S5_GROUP)

    def in_mat(x):
        return jnp.einsum('ngpc,gh->ngchp', x, eye).reshape(nb, S5_BLK_CH, S5_BLK_STATE)

    bmat = jnp.concatenate([in_mat(bb.real), in_mat(bb.imag)], axis=-1)
    b_hi = bmat.astype(BF16)
    b_lo = (bmat - b_hi.astype(F32)).astype(BF16)
    cr = c_re.astype(F32).reshape(nb, S5_BLK_GROUPS, S5_GROUP, S5_STATE)
    ci = c_im.astype(F32).reshape(nb, S5_BLK_GROUPS, S5_GROUP, S5_STATE)

    def out_mat(x):
        return jnp.einsum('ngcp,gh->ngphc', x, eye).reshape(nb, S5_BLK_STATE, S5_BLK_CH)

    cmat = jnp.concatenate([out_mat(cr), -out_mat(ci)], axis=1).astype(BF16)
    return atab, b_hi, b_lo, cmat


def _s5_scan_tile(xr, xi, cr, ci, atab_ref):
    for idx, k in enumerate((1, 2, 4)):
        pr, pi = atab_ref[0, idx], atab_ref[0, 4 + idx]
        sr, si = pltpu.roll(xr, k, 0), pltpu.roll(xi, k, 0)
        xr, xi = xr + pr * sr - pi * si, xi + pr * si + pi * sr
    pr, pi = atab_ref[0, 3], atab_ref[0, 7]
    xr, xi = xr + pr * cr - pi * ci, xi + pr * ci + pi * cr
    last = S5_SCAN_ROWS - 1
    cr = jnp.broadcast_to(xr[last:last + 1, :], xr.shape)
    ci = jnp.broadcast_to(xi[last:last + 1, :], xi.shape)
    return xr, xi, cr, ci


def _s5_prompt_kernel(u_ref, x0_ref, atab_ref, bhi_ref, blo_ref, c_ref, d_ref, y_ref, st_ref, xs_ref, *, T):
    u = u_ref[0]
    pad = (-T) % 16

    def padded(x):
        return jnp.concatenate([x, jnp.zeros((pad, x.shape[1]), x.dtype)], axis=0) if pad else x

    up = padded(u)
    u_hi = up.astype(BF16)
    u_lo = (up - u_hi.astype(F32)).astype(BF16)
    b_hi = bhi_ref[0]
    xs_ref[...] = (jnp.dot(u_hi, b_hi, preferred_element_type=F32)
                   + jnp.dot(u_lo, b_hi, preferred_element_type=F32)
                   + jnp.dot(u_hi, blo_ref[0], preferred_element_type=F32))[:T]
    n = S5_BLK_STATE
    R = S5_SCAN_ROWS

    def body(i, carry):
        cr, ci = carry
        rows = pl.ds(pl.multiple_of(i * R, R), R)
        xr, xi, cr, ci = _s5_scan_tile(xs_ref[rows, :n], xs_ref[rows, n:], cr, ci, atab_ref)
        xs_ref[rows, :n] = xr
        xs_ref[rows, n:] = xi
        return cr, ci

    x0 = x0_ref[0, 0]
    cr0 = jnp.broadcast_to(x0[0:1, :], (R, n))
    ci0 = jnp.broadcast_to(x0[1:2, :], (R, n))
    cr, ci = lax.fori_loop(0, T // R, body, (cr0, ci0))
    st_ref[0, 0] = jnp.concatenate([cr[0:1], ci[0:1]], axis=0)
    y = jnp.dot(padded(xs_ref[...]).astype(BF16), c_ref[0], preferred_element_type=F32)[:T] + d_ref[...] * u
    y_ref[0] = _gelu_tanh(y)


def s5_prompt(proj3, cols, x0, lam_re, lam_im, log_step, b_re, b_im, c_re, c_im, d):
    B, T, _ = proj3.shape
    assert T % S5_SCAN_ROWS == 0
    nb = S5_GROUPS // S5_BLK_GROUPS
    atab, b_hi, b_lo, cmat = _s5_params(lam_re, lam_im, log_step, b_re, b_im, c_re, c_im)
    x0b = x0.astype(F32).reshape(B, nb, S5_BLK_STATE, 2).transpose(0, 1, 3, 2)
    blk3 = lambda shape: pl.BlockSpec((1,) + shape, lambda b, j: (j, 0, 0))
    y, st = pl.pallas_call(
        functools.partial(_s5_prompt_kernel, T=T),
        grid=(B, nb),
        in_specs=[pl.BlockSpec((1, T, S5_BLK_CH), lambda b, j, o=cols['su'] // S5_BLK_CH: (b, 0, o + j)),
                  pl.BlockSpec((1, 1, 2, S5_BLK_STATE), lambda b, j: (b, j, 0, 0)),
                  pl.BlockSpec((1, 8, S5_SCAN_ROWS, S5_BLK_STATE), lambda b, j: (j, 0, 0, 0)),
                  blk3((S5_BLK_CH, 2 * S5_BLK_STATE)), blk3((S5_BLK_CH, 2 * S5_BLK_STATE)),
                  blk3((2 * S5_BLK_STATE, S5_BLK_CH)),
                  pl.BlockSpec((1, S5_BLK_CH), lambda b, j: (0, j))],
        out_specs=(pl.BlockSpec((1, T, S5_BLK_CH), lambda b, j: (b, 0, j)),
                   pl.BlockSpec((1, 1, 2, S5_BLK_STATE), lambda b, j: (b, j, 0, 0))),
        out_shape=(jax.ShapeDtypeStruct((B, T, S5_WIDTH), F32),
                   jax.ShapeDtypeStruct((B, nb, 2, S5_BLK_STATE), F32)),
        scratch_shapes=[pltpu.VMEM((T, 2 * S5_BLK_STATE), F32)],
        compiler_params=pltpu.CompilerParams(dimension_semantics=("parallel", "parallel"),
                                             vmem_limit_bytes=V7X_VMEM_LIMIT_BYTES),
        name="s5_prompt",
    )(proj3, x0b, atab, b_hi, b_lo, cmat, d.astype(F32).reshape(1, S5_WIDTH))
    st = st.transpose(0, 1, 3, 2).reshape(B, S5_GROUPS, S5_STATE, 2)
    return y, st


def _branch_norm_kernel(ro_ref, so_ref, no_ref, bn_ref, o_ref):
    off = 0
    for ref in (ro_ref, so_ref, no_ref):
        x = ref[...]
        w = x.shape[-1]
        y = x * lax.rsqrt(jnp.mean(x * x, axis=-1, keepdims=True) + EPS) * bn_ref[:, off:off + w]
        o_ref[:, off:off + w] = y.astype(o_ref.dtype)
        off += w


def branch_norm(ro, so, no, bn):
    m = ro.shape[0]
    tm = min(m, 256)
    assert m % tm == 0
    spec = lambda w: pl.BlockSpec((tm, w), lambda i: (i, 0))
    return pl.pallas_call(
        _branch_norm_kernel,
        grid=(m // tm,),
        in_specs=[spec(RET_WIDTH), spec(S5_WIDTH), spec(NSA_WIDTH), pl.BlockSpec((1, D_MODEL), lambda i: (0, 0))],
        out_specs=spec(D_MODEL),
        out_shape=jax.ShapeDtypeStruct((m, D_MODEL), BF16),
        compiler_params=pltpu.CompilerParams(dimension_semantics=("parallel",),
                                             vmem_limit_bytes=V7X_VMEM_LIMIT_BYTES),
        name="branch_norm",
    )(ro, so, no, bn.astype(F32).reshape(1, D_MODEL))


def _largest_divisor(n, cap):
    return max(d for d in range(1, min(n, cap) + 1) if n % d == 0)


def _t5_bucket(dist):
    max_exact = N_BUCKETS // 2
    d = jnp.maximum(dist, 0)
    large = max_exact + (jnp.log(jnp.maximum(d, 1).astype(F32) / max_exact)
                         / math.log(MAX_DISTANCE / max_exact) * (N_BUCKETS - max_exact)).astype(jnp.int32)
    return jnp.where(d < max_exact, d, jnp.minimum(large, N_BUCKETS - 1))


def _rotary(x, pos):
    half = HEAD_DIM // 2
    inv = 1.0 / (10000.0 ** jnp.linspace(0.0, 1.0, half, dtype=F32))
    ang = pos.astype(F32)[:, None] * inv[None]
    cos, sin = jnp.cos(ang)[None, :, None, :], jnp.sin(ang)[None, :, None, :]
    x1, x2 = x[..., :half], x[..., half:]
    return jnp.concatenate([x1 * cos - x2 * sin, x1 * sin + x2 * cos], axis=-1)


def _retention(q, k, v, s0, q0):
    B, T = q.shape[:2]
    pos = q0 + jnp.arange(T)
    q = _rotary(q, pos)
    k = _rotary(k, pos) * HEAD_DIM ** -0.5
    lg = jnp.log1p(-(2.0 ** (-5.0 - jnp.arange(RET_HEADS, dtype=F32))))
    c = _largest_divisor(T, RET_CHUNK)
    n = T // c
    i = jnp.arange(c)
    rel = i[:, None] - i[None, :]
    decay = jnp.where(rel[None] >= 0, jnp.exp(jnp.maximum(rel, 0)[None] * lg[:, None, None]), 0.0)
    q_dec = jnp.exp((i + 1)[:, None] * lg[None])[None, :, :, None]
    k_dec = jnp.exp((c - 1 - i)[:, None] * lg[None])[None, :, :, None]
    chunk_dec = jnp.exp(c * lg)[None, :, None, None]

    def to_chunks(a):
        return a.reshape(B, n, c, RET_HEADS, HEAD_DIM).swapaxes(0, 1)

    def step(s, qkv):
        qc, kc, vc = qkv
        inner = jnp.einsum('bihd,bjhd->bhij', qc, kc) * decay
        o = (jnp.einsum('bhij,bjhv->bihv', inner, vc)
             + jnp.einsum('bihd,bhdv->bihv', qc, s) * q_dec)
        s = s * chunk_dec + jnp.einsum('bjhd,bjhv->bhdv', kc * k_dec, vc)
        return s, o

    s, o = lax.scan(step, s0.astype(F32), (to_chunks(q), to_chunks(k), to_chunks(v)))
    return o.swapaxes(0, 1).reshape(B, T, RET_HEADS, HEAD_DIM), s


def _s5_scan(u, x0, lam_re, lam_im, log_step, b_re, b_im, c_re, c_im, d):
    B, T = u.shape[:2]
    ug = u.reshape(B, T, S5_GROUPS, S5_GROUP).astype(jnp.complex64)
    lam = lax.complex(lam_re.astype(F32), lam_im.astype(F32))
    a_bar = jnp.exp(lam * jnp.exp(log_step.astype(F32))[:, None])
    b_bar = ((a_bar - 1.0) / lam)[..., None] * lax.complex(b_re.astype(F32), b_im.astype(F32))
    bu = jnp.einsum('gpc,btgc->btgp', b_bar, ug)
    init = lax.complex(x0[..., 0].astype(F32), x0[..., 1].astype(F32))
    bu = bu.at[:, 0].add(a_bar[None] * init)
    a = jnp.broadcast_to(a_bar, bu.shape)

    def combine(e1, e2):
        a1, b1 = e1
        a2, b2 = e2
        return a1 * a2, a2 * b1 + b2

    _, xs = lax.associative_scan(combine, (a, bu), axis=1)
    cm = lax.complex(c_re.astype(F32), c_im.astype(F32))
    y = jnp.einsum('gcp,btgp->btgc', cm, xs).real.reshape(B, T, S5_WIDTH) + d.astype(F32) * u
    x_last = xs[:, -1]
    return y, jnp.stack([x_last.real, x_last.imag], axis=-1)


def _compress(x, pe, w1, w2):
    B, L = x.shape[:2]
    r = CMP_BLOCK // CMP_STRIDE
    n_full = L // CMP_STRIDE
    n_cmp = n_full - r + 1
    xs = x[:, :n_full * CMP_STRIDE].reshape(B, n_full, CMP_STRIDE, NSA_KV_HEADS, HEAD_DIM)
    w1 = w1.astype(F32)
    w1r = w1.reshape(r, CMP_STRIDE, HEAD_DIM, HEAD_DIM)
    h = jnp.einsum('ld,ldh->h', pe.astype(F32), w1)
    for j in range(r):
        h = h + jnp.einsum('bnsgd,sdh->bngh', xs[:, j:j + n_cmp], w1r[j])
    return jnp.einsum('bngh,he->bnge', jax.nn.gelu(h), w2.astype(F32))


def _nsa(q, k_cmp, v_cmp, k_slc, v_slc, k_win, v_win, gates, q0, rel_bias,
         pe_k, w1_k, w2_k, pe_v, w1_v, w2_v):
    B, T = q.shape[:2]
    G, J = NSA_KV_HEADS, NSA_HPG
    Lk = q0 + T
    qg = (q.astype(F32) * HEAD_DIM ** -0.5).reshape(B, T, G, J, HEAD_DIM)
    t_pos = q0 + jnp.arange(T)
    table = rel_bias.astype(F32).reshape(N_BUCKETS, G, J)

    kc = _compress(k_cmp.astype(F32), pe_k, w1_k, w2_k)
    vc = _compress(v_cmp.astype(F32), pe_v, w1_v, w2_v)
    n_cmp = kc.shape[1]
    c_start = jnp.arange(n_cmp) * CMP_STRIDE
    dist = t_pos[:, None] - (c_start + CMP_BLOCK - 1)[None]
    valid = (dist >= 0)[:, None, None, :]
    bias = jnp.transpose(table[_t5_bucket(dist)], (0, 2, 3, 1))
    s = jnp.einsum('btgjd,bngd->btgjn', qg, kc) + bias
    p_cmp = jax.nn.softmax(jnp.where(valid, s, NEG), axis=-1) * valid
    o_cmp = jnp.einsum('btgjn,bngd->btgjd', p_cmp, vc)

    n_sel = -(-Lk // SEL_BLOCK)
    s_start = jnp.arange(n_sel) * SEL_BLOCK
    overlap = ((c_start[:, None] < s_start[None] + SEL_BLOCK)
               & (c_start[:, None] + CMP_BLOCK > s_start[None])).astype(F32)
    p_sel = jnp.einsum('btgn,nm->btgm', p_cmp.sum(axis=3), overlap)
    cur = t_pos // SEL_BLOCK
    blk = jnp.arange(n_sel)
    forced = (blk[None] == 0) | (blk[None] == cur[:, None]) | (blk[None] == cur[:, None] - 1)
    blk_ok = (blk[None] <= cur[:, None])[None, :, None, :]
    score = jnp.where(blk_ok, p_sel + FORCE_SCORE * forced[None, :, None, :], NEG)
    n_top = min(SEL_TOP, n_sel)
    _, idx = lax.top_k(score, n_top)

    pad = n_sel * SEL_BLOCK - Lk

    def blocks(a):
        a = jnp.pad(a.astype(F32), ((0, 0), (0, pad), (0, 0), (0, 0)))
        return a.reshape(B, n_sel, SEL_BLOCK, G, HEAD_DIM).transpose(0, 3, 1, 2, 4)

    kb, vb = blocks(k_slc), blocks(v_slc)
    qb = _largest_divisor(T, SEL_Q_BLOCK)
    nq = T // qb
    b_ix = jnp.arange(B)[:, None, None, None]
    g_ix = jnp.arange(G)[None, None, :, None]
    g_ix5 = g_ix[..., None]
    off = jnp.arange(SEL_BLOCK)

    def sel_block(args):
        qc, ic, tc = args
        kg = kb[b_ix, g_ix, ic]
        vg = vb[b_ix, g_ix, ic]
        kpos = ic[..., None] * SEL_BLOCK + off
        d = tc[None, :, None, None, None] - kpos
        ok = (d >= 0) & (ic <= (tc // SEL_BLOCK)[None, :, None, None])[..., None]
        bias_s = jnp.moveaxis(table[_t5_bucket(d), g_ix5], -1, 3)
        sc = jnp.einsum('bqgjd,bqgkld->bqgjkl', qc, kg) + bias_s
        sc = jnp.where(ok[:, :, :, None], sc, NEG).reshape(B, qb, G, J, n_top * SEL_BLOCK)
        pr = jax.nn.softmax(sc, axis=-1).reshape(B, qb, G, J, n_top, SEL_BLOCK)
        return jnp.einsum('bqgjkl,bqgkld->bqgjd', pr, vg)

    o_slc = lax.map(sel_block, (qg.reshape(B, nq, qb, G, J, HEAD_DIM).swapaxes(0, 1),
                                idx.reshape(B, nq, qb, G, n_top).swapaxes(0, 1),
                                t_pos.reshape(nq, qb)))
    o_slc = o_slc.swapaxes(0, 1).reshape(B, T, G, J, HEAD_DIM)

    qw = _largest_divisor(T, WIN_Q_BLOCK)
    nw = T // qw
    rows = jnp.arange(nw)[:, None] * qw + jnp.arange(qw + WINDOW)[None]
    kwin = k_win.astype(F32)[:, rows]
    vwin = v_win.astype(F32)[:, rows]
    kpos = q0 - WINDOW + rows
    dw = t_pos.reshape(nw, qw)[:, :, None] - kpos[:, None, :]
    okw = (kpos[:, None, :] >= 0) & (dw >= 0) & (dw <= WINDOW)
    bias_w = jnp.transpose(table[_t5_bucket(dw)], (0, 3, 4, 1, 2))
    sw = jnp.einsum('bcqgjd,bckgd->bcgjqk', qg.reshape(B, nw, qw, G, J, HEAD_DIM), kwin) + bias_w
    pw = jax.nn.softmax(jnp.where(okw[:, None, None], sw, NEG), axis=-1)
    o_win = jnp.einsum('bcgjqk,bckgd->bcqgjd', pw, vwin).reshape(B, T, G, J, HEAD_DIM)

    g = jax.nn.sigmoid(gates.astype(F32)).reshape(B, T, G, J, 3)
    o = g[..., 0:1] * o_cmp + g[..., 1:2] * o_slc + g[..., 2:3] * o_win
    return o.reshape(B, T, NSA_WIDTH)


def _jnp_rms(x, gain):
    xf = x.astype(F32)
    xf = xf * lax.rsqrt(jnp.mean(xf * xf, axis=-1, keepdims=True) + EPS)
    return xf * gain.astype(F32)


def _block(x, q0, win_buf, ret_s0, s5_s0, win_k0, win_v0, past_cmp_k, past_cmp_v,
           past_slc_k, past_slc_v, lw, rel_bias):
    B, T, _ = x.shape
    G = NSA_KV_HEADS
    M = B * T
    x2 = x.reshape(M, D_MODEL)
    h = rmsnorm(x2, lw['norm_mix'], BF16)
    proj = matmul(h, lw['w_in'])[:, :IN_COLS].reshape(B, T, IN_COLS)
    (rq, rk, rv, rg, su, nq_, kc, vc, ks, vs, kw, vw, ng) = jnp.split(
        proj, np.cumsum(IN_SPLITS)[:-1].tolist(), axis=-1)

    def heads(a, n):
        return a.reshape(B, T, n, HEAD_DIM)

    ro, ret_s = _retention(heads(rq, RET_HEADS), heads(rk, RET_HEADS), heads(rv, RET_HEADS), ret_s0, q0)
    ro = ro * lax.rsqrt(jnp.mean(ro * ro, axis=-1, keepdims=True) + EPS)
    ro = jax.nn.silu(rg) * ro.reshape(B, T, RET_WIDTH)

    sy, s5_s = _s5_scan(su, s5_s0, lw['s5_lambda_re'], lw['s5_lambda_im'], lw['s5_log_step'],
                        lw['s5_b_re'], lw['s5_b_im'], lw['s5_c_re'], lw['s5_c_im'], lw['s5_d'])
    sy = jax.nn.gelu(sy)
    so = sy * jax.nn.sigmoid(jnp.einsum('btc,ce->bte', sy, lw['s5_w_glu']))

    kc, vc, ks, vs, kw, vw = (heads(a, G) for a in (kc, vc, ks, vs, kw, vw))
    k_cmp_all = jnp.concatenate([past_cmp_k, kc], axis=1)
    v_cmp_all = jnp.concatenate([past_cmp_v, vc], axis=1)
    k_slc_all = jnp.concatenate([past_slc_k, ks], axis=1)
    v_slc_all = jnp.concatenate([past_slc_v, vs], axis=1)
    k_win_all = jnp.concatenate([win_k0, kw], axis=1)
    v_win_all = jnp.concatenate([win_v0, vw], axis=1)
    no = _nsa(heads(nq_, NSA_HEADS), k_cmp_all, v_cmp_all, k_slc_all, v_slc_all, k_win_all, v_win_all,
              ng.reshape(B, T, NSA_HEADS, 3), q0, rel_bias,
              lw['cmp_pe_k'], lw['cmp_w1_k'], lw['cmp_w2_k'], lw['cmp_pe_v'], lw['cmp_w1_v'], lw['cmp_w2_v'])

    bn = lw['branch_norm']
    mix = jnp.concatenate([_jnp_rms(ro, bn[:RET_WIDTH]),
                           _jnp_rms(so, bn[RET_WIDTH:RET_WIDTH + S5_WIDTH]),
                           _jnp_rms(no, bn[RET_WIDTH + S5_WIDTH:])], axis=-1).astype(BF16)
    x2 = matmul(mix.reshape(M, D_MODEL), lw['w_out'], res=x2)

    h = rmsnorm(x2, lw['norm_ffn'], BF16)
    up = matmul(h, lw['w_up'], act="relu2", out_dtype=BF16)
    x2 = matmul(up, lw['w_down'], res=x2)

    cmp_rows = jnp.stack([kc, vc], axis=1)
    slc_rows = jnp.stack([ks, vs], axis=1)
    win_new = jnp.stack([k_win_all[:, -win_buf:], v_win_all[:, -win_buf:]], axis=1)
    return x2.reshape(B, T, D_MODEL), cmp_rows, slc_rows, win_new, ret_s, s5_s


_COL_NAMES = ('rq', 'rk', 'rv', 'rg', 'su', 'nq', 'kc', 'vc', 'ks', 'vs', 'kw', 'vw', 'ng')
COLS = {name: int(off) for name, off in zip(_COL_NAMES, np.concatenate([[0], np.cumsum(IN_SPLITS)]))}


def _block_prompt(x, win_buf, lw, rel_bias):
    B, T, _ = x.shape
    G = NSA_KV_HEADS
    M = B * T
    assert T >= win_buf
    x2 = x.reshape(M, D_MODEL)
    h = rmsnorm(x2, lw['norm_mix'], BF16)
    proj3 = matmul(h, lw['w_in']).reshape(B, T, -1)

    ro, ret_s = retention(proj3, COLS, jnp.zeros((B, RET_HEADS, HEAD_DIM, HEAD_DIM), F32), 0)
    zero_s5 = jnp.zeros((B, S5_GROUPS, S5_STATE, 2), F32)
    sy, s5_s = s5_prompt(proj3, COLS, zero_s5, lw['s5_lambda_re'], lw['s5_lambda_im'], lw['s5_log_step'],
                         lw['s5_b_re'], lw['s5_b_im'], lw['s5_c_re'], lw['s5_c_im'], lw['s5_d'])
    sy2 = sy.reshape(M, S5_WIDTH)
    so = matmul(sy2, lw['s5_w_glu'], res=sy2, act="glu")
    kcmp, vcmp = compress_prompt(proj3, COLS['kc'], COLS['vc'], lw['cmp_pe_k'], lw['cmp_w1_k'], lw['cmp_w2_k'],
                                 lw['cmp_pe_v'], lw['cmp_w1_v'], lw['cmp_w2_v'])
    no = nsa_prompt2(proj3, COLS, kcmp, vcmp, rel_bias)

    mix = branch_norm(ro.reshape(M, RET_WIDTH), so, no.reshape(M, NSA_WIDTH), lw['branch_norm'])
    x2 = matmul(mix, lw['w_out'], res=x2)
    h = rmsnorm(x2, lw['norm_ffn'], BF16)
    up = matmul(h, lw['w_up'], act="relu2", out_dtype=BF16)
    x2 = matmul(up, lw['w_down'], res=x2)

    rows = lambda name: proj3[:, :, COLS[name]:COLS[name] + KV_WIDTH].reshape(B, T, G, HEAD_DIM)
    cmp_rows = jnp.stack([rows('kc'), rows('vc')], axis=1)
    slc_rows = jnp.stack([rows('ks'), rows('vs')], axis=1)
    win_new = jnp.stack([rows('kw')[:, T - win_buf:], rows('vw')[:, T - win_buf:]], axis=1)
    return x2.reshape(B, T, D_MODEL), cmp_rows, slc_rows, win_new, ret_s, s5_s


def _mixer_tail(x2, ro, so_in, no, lw):
    so = matmul(so_in, lw['s5_w_glu'], res=so_in, act="glu")
    mix = branch_norm(ro, so, no, lw['branch_norm'])
    x2 = matmul(mix, lw['w_out'], res=x2)
    h = rmsnorm(x2, lw['norm_ffn'], BF16)
    up = matmul(h, lw['w_up'], act="relu2", out_dtype=BF16)
    return matmul(up, lw['w_down'], res=x2)


def _block_sample(x, layer, cache_cmp, cache_slc, cache_win, state_ret, state_s5, page_table, lw, rel_bias):
    B, T, _ = x.shape
    G = NSA_KV_HEADS
    M = B * T
    past_len = page_table.shape[1] * PAGE_SIZE
    win_buf = cache_win.shape[3]
    x2 = x.reshape(M, D_MODEL)
    h = rmsnorm(x2, lw['norm_mix'], BF16)
    proj3 = matmul(h, lw['w_in']).reshape(B, T, -1)

    ro, ret_s = retention(proj3, COLS, state_ret[layer], past_len)
    sy, s5_s = s5_prompt(proj3, COLS, state_s5[layer], lw['s5_lambda_re'], lw['s5_lambda_im'], lw['s5_log_step'],
                         lw['s5_b_re'], lw['s5_b_im'], lw['s5_c_re'], lw['s5_c_im'], lw['s5_d'])
    no = nsa_sample(proj3, COLS, cache_cmp, cache_slc, cache_win, page_table, layer, rel_bias,
                    lw['cmp_pe_k'], lw['cmp_w1_k'], lw['cmp_w2_k'], lw['cmp_pe_v'], lw['cmp_w1_v'], lw['cmp_w2_v'])
    x2 = _mixer_tail(x2, ro.reshape(M, RET_WIDTH), sy.reshape(M, S5_WIDTH), no.reshape(M, NSA_WIDTH), lw)

    rows = lambda name: proj3[:, :, COLS[name]:COLS[name] + KV_WIDTH].reshape(B, T, G, HEAD_DIM)
    cmp_rows = jnp.stack([rows('kc'), rows('vc')], axis=1)
    slc_rows = jnp.stack([rows('ks'), rows('vs')], axis=1)
    win_new = jnp.concatenate([cache_win[layer], jnp.stack([rows('kw'), rows('vw')], axis=1)],
                              axis=2)[:, :, -win_buf:]
    return x2.reshape(B, T, D_MODEL), cmp_rows, slc_rows, win_new, ret_s, s5_s


def _gather_pages(pool, page_table, layer):
    pages = pool[page_table, layer]
    db, npg = page_table.shape
    pages = pages.transpose(0, 2, 1, 3, 4, 5).reshape(db, 2, npg * PAGE_SIZE, NSA_KV_HEADS, HEAD_DIM)
    return pages[:, 0], pages[:, 1]


def kernel(x_prompt, x_sample, cache_cmp, cache_slc, cache_win, state_ret, state_s5, page_table,
           rel_bias, norm_mix, w_in, s5_lambda_re, s5_lambda_im, s5_log_step, s5_b_re, s5_b_im,
           s5_c_re, s5_c_im, s5_d, s5_w_glu, cmp_pe_k, cmp_w1_k, cmp_w2_k, cmp_pe_v, cmp_w1_v,
           cmp_w2_v, branch_norm, w_out, norm_ffn, w_up, w_down, norm_final):
    past_len = page_table.shape[1] * PAGE_SIZE
    win_buf = cache_win.shape[3]
    bp = x_prompt.shape[0]
    G, HD = NSA_KV_HEADS, HEAD_DIM
    xp, xs = x_prompt, x_sample
    empty = jnp.zeros((bp, 0, G, HD), x_prompt.dtype)
    zero_win = jnp.zeros((bp, WINDOW, G, HD), x_prompt.dtype)
    zero_ret = jnp.zeros((bp, RET_HEADS, HD, HD), F32)
    zero_s5 = jnp.zeros((bp, S5_GROUPS, S5_STATE, 2), F32)
    in_pad = _round_up(IN_COLS, 512) - IN_COLS
    cmp_p, cmp_s, slc_p, slc_s, win_p, win_s, ret_p, ret_s, s5_p, s5_s = ([] for _ in range(10))
    for l in range(DEPTH):
        lw = {
            'norm_mix': norm_mix[l],
            'w_in': jnp.pad(w_in[l].astype(BF16), ((0, 0), (0, in_pad))),
            's5_lambda_re': s5_lambda_re[l], 's5_lambda_im': s5_lambda_im[l], 's5_log_step': s5_log_step[l],
            's5_b_re': s5_b_re[l], 's5_b_im': s5_b_im[l], 's5_c_re': s5_c_re[l], 's5_c_im': s5_c_im[l],
            's5_d': s5_d[l], 's5_w_glu': s5_w_glu[l],
            'cmp_pe_k': cmp_pe_k[l], 'cmp_w1_k': cmp_w1_k[l], 'cmp_w2_k': cmp_w2_k[l],
            'cmp_pe_v': cmp_pe_v[l], 'cmp_w1_v': cmp_w1_v[l], 'cmp_w2_v': cmp_w2_v[l],
            'branch_norm': branch_norm[l], 'w_out': w_out[l].astype(BF16),
            'norm_ffn': norm_ffn[l], 'w_up': w_up[l].astype(BF16), 'w_down': w_down[l].astype(BF16),
        }
        xp, c_rows, s_rows, w_new, r_new, s_new = _block_prompt(xp, win_buf, lw, rel_bias)
        cmp_p.append(c_rows); slc_p.append(s_rows); win_p.append(w_new); ret_p.append(r_new); s5_p.append(s_new)
        xs, c_rows, s_rows, w_new, r_new, s_new = _block_sample(
            xs, l, cache_cmp, cache_slc, cache_win, state_ret, state_s5, page_table, lw, rel_bias)
        cmp_s.append(c_rows); slc_s.append(s_rows); win_s.append(w_new); ret_s.append(r_new); s5_s.append(s_new)
    y_prompt = rmsnorm(xp.reshape(-1, D_MODEL), norm_final, F32).reshape(xp.shape)
    y_sample = rmsnorm(xs.reshape(-1, D_MODEL), norm_final, F32).reshape(xs.shape)
    return (y_prompt, y_sample,
            jnp.stack(cmp_p, axis=1), jnp.stack(cmp_s, axis=1),
            jnp.stack(slc_p, axis=1), jnp.stack(slc_s, axis=1),
            jnp.stack(win_p, axis=0), jnp.stack(win_s, axis=0),
            jnp.stack(ret_p, axis=0), jnp.stack(ret_s, axis=0),
            jnp.stack(s5_p, axis=0), jnp.stack(s5_s, axis=0))
```

```python
import functools
import math

import jax
import jax.numpy as jnp
import numpy as np
from jax import lax
from jax.experimental import pallas as pl
from jax.experimental.pallas import tpu as pltpu

F32 = jnp.float32
BF16 = jnp.bfloat16

D_MODEL = 4096
DEPTH = 2
PAGE_SIZE = 128
HEAD_DIM = 128
RET_WIDTH = 1024
RET_HEADS = 8
RET_CHUNK = 128
S5_WIDTH = 1024
S5_GROUP = 16
S5_GROUPS = 64
S5_STATE = 64
NSA_WIDTH = 2048
NSA_HEADS = 16
NSA_KV_HEADS = 4
NSA_HPG = 4
KV_WIDTH = 512
CMP_BLOCK = 32
CMP_STRIDE = 16
SEL_BLOCK = 64
SEL_TOP = 16
WINDOW = 512
SEL_Q_BLOCK = 32
WIN_Q_BLOCK = 128
FORCE_SCORE = 1e4
NEG = -1e30
N_BUCKETS = 32
MAX_DISTANCE = 128
D_FF = 4 * D_MODEL
EPS = 1e-6
IN_SPLITS = (RET_WIDTH, RET_WIDTH, RET_WIDTH, RET_WIDTH, S5_WIDTH, NSA_WIDTH,
             KV_WIDTH, KV_WIDTH, KV_WIDTH, KV_WIDTH, KV_WIDTH, KV_WIDTH, 3 * NSA_HEADS)
IN_COLS = sum(IN_SPLITS)

V7X_VMEM_LIMIT_BYTES = 48 * 1024 * 1024
LANE = 128


def _round_up(n, m):
    return -(-n // m) * m


def _rmsnorm_kernel(x_ref, g_ref, o_ref):
    x = x_ref[...].astype(F32)
    ms = jnp.mean(x * x, axis=-1, keepdims=True)
    o_ref[...] = (x * lax.rsqrt(ms + EPS) * g_ref[...].astype(F32)).astype(o_ref.dtype)


def rmsnorm(x2d, gain, out_dtype):
    m, d = x2d.shape
    tm = min(m, 256)
    assert m % tm == 0
    return pl.pallas_call(
        _rmsnorm_kernel,
        grid=(m // tm,),
        in_specs=[pl.BlockSpec((tm, d), lambda i: (i, 0)),
                  pl.BlockSpec((1, d), lambda i: (0, 0))],
        out_specs=pl.BlockSpec((tm, d), lambda i: (i, 0)),
        out_shape=jax.ShapeDtypeStruct((m, d), out_dtype),
        compiler_params=pltpu.CompilerParams(dimension_semantics=("parallel",),
                                             vmem_limit_bytes=V7X_VMEM_LIMIT_BYTES),
        name="rmsnorm",
    )(x2d, gain.reshape(1, d))


def _mm_kernel(*refs, nk, act, has_res):
    if has_res:
        a_ref, w_ref, r_ref, o_ref, acc_ref = refs
    else:
        a_ref, w_ref, o_ref, acc_ref = refs
        r_ref = None
    k = pl.program_id(2)

    @pl.when(k == 0)
    def _():
        acc_ref[...] = jnp.zeros_like(acc_ref)

    acc_ref[...] += jnp.dot(a_ref[...].astype(BF16), w_ref[...], preferred_element_type=F32)

    @pl.when(k == nk - 1)
    def _():
        acc = acc_ref[...]
        if act == "relu2":
            acc = jnp.square(jnp.maximum(acc, 0.0))
        if act == "glu":
            acc = r_ref[...].astype(F32) * jax.nn.sigmoid(acc)
        elif has_res:
            acc = acc + r_ref[...].astype(F32)
        o_ref[...] = acc.astype(o_ref.dtype)


def matmul(a, w, *, res=None, act=None, out_dtype=F32):
    m, kdim = a.shape
    n = w.shape[1]
    tm = min(m, 1024)
    tn = 512 if n % 512 == 0 else (256 if n % 256 == 0 else 128)
    tk = min(kdim, 4096)
    assert m % tm == 0 and n % tn == 0 and kdim % tk == 0
    nk = kdim // tk
    in_specs = [pl.BlockSpec((tm, tk), lambda i, j, k: (i, k)),
                pl.BlockSpec((tk, tn), lambda i, j, k: (k, j))]
    args = [a, w]
    if res is not None:
        in_specs.append(pl.BlockSpec((tm, tn), lambda i, j, k: (i, j)))
        args.append(res)
    return pl.pallas_call(
        functools.partial(_mm_kernel, nk=nk, act=act, has_res=res is not None),
        grid=(m // tm, n // tn, nk),
        in_specs=in_specs,
        out_specs=pl.BlockSpec((tm, tn), lambda i, j, k: (i, j)),
        out_shape=jax.ShapeDtypeStruct((m, n), out_dtype),
        scratch_shapes=[pltpu.VMEM((tm, tn), F32)],
        compiler_params=pltpu.CompilerParams(
            dimension_semantics=("parallel", "parallel", "arbitrary"),
            vmem_limit_bytes=V7X_VMEM_LIMIT_BYTES),
        name="matmul",
    )(*args)


def _gelu_tanh(x):
    return 0.5 * x * (1.0 + jnp.tanh(math.sqrt(2.0 / math.pi) * (x + 0.044715 * (x * x * x))))


def _dot_nt(a, b):
    return lax.dot_general(a, b, (((1,), (1,)), ((), ())), preferred_element_type=F32)


def _dot_tn(a, b):
    return lax.dot_general(a, b, (((0,), (0,)), ((), ())), preferred_element_type=F32)


def _compress_rows(x_ref, n_full, pe_ref, w1_ref, w2_ref):
    pieces = [x_ref[0, pl.ds(s, n_full, stride=CMP_STRIDE), :] for s in range(CMP_STRIDE)]
    x = jnp.concatenate(pieces, axis=1).astype(BF16)
    half = CMP_STRIDE * HEAD_DIM
    z0 = jnp.dot(x, w1_ref[:half, :], preferred_element_type=F32)
    z1 = jnp.dot(x, w1_ref[half:, :], preferred_element_type=F32)
    z1 = pltpu.roll(z1, n_full - 1, 0)
    pe = jnp.broadcast_to(pe_ref[...], (8, CMP_BLOCK * HEAD_DIM)).astype(BF16)
    h0 = jnp.dot(pe, w1_ref[...], preferred_element_type=F32)[0:1, :]
    h = _gelu_tanh(z0 + z1 + h0)
    return jnp.dot(h.astype(BF16), w2_ref[...], preferred_element_type=F32)


def _compress_prompt_kernel(xk_ref, xv_ref, pek_ref, w1k_ref, w2k_ref, pev_ref, w1v_ref, w2v_ref,
                            kc_ref, vc_ref, *, n_full):
    for x_ref, pe_ref, w1_ref, w2_ref, o_ref in ((xk_ref, pek_ref, w1k_ref, w2k_ref, kc_ref),
                                                  (xv_ref, pev_ref, w1v_ref, w2v_ref, vc_ref)):
        out = _compress_rows(x_ref, n_full, pe_ref, w1_ref, w2_ref).astype(o_ref.dtype)
        n_pad = o_ref.shape[2]
        o_ref[0, 0, :n_full, :] = out
        if n_pad > n_full:
            o_ref[0, 0, n_full:, :] = jnp.zeros((n_pad - n_full, HEAD_DIM), o_ref.dtype)


def compress_prompt(proj3, col_k, col_v, pe_k, w1_k, w2_k, pe_v, w1_v, w2_v):
    B, T, _ = proj3.shape
    n_full = T // CMP_STRIDE
    n_pad = _round_up(n_full, LANE)
    flat = CMP_BLOCK * HEAD_DIM
    wspec = [pl.BlockSpec((1, flat), lambda b, g: (0, 0)),
             pl.BlockSpec((flat, HEAD_DIM), lambda b, g: (0, 0)),
             pl.BlockSpec((HEAD_DIM, HEAD_DIM), lambda b, g: (0, 0))]
    out_sds = jax.ShapeDtypeStruct((B, NSA_KV_HEADS, n_pad, HEAD_DIM), BF16)
    ospec = pl.BlockSpec((1, 1, n_pad, HEAD_DIM), lambda b, g: (b, g, 0, 0))
    return pl.pallas_call(
        functools.partial(_compress_prompt_kernel, n_full=n_full),
        grid=(B, NSA_KV_HEADS),
        in_specs=[pl.BlockSpec((1, T, HEAD_DIM), lambda b, g: (b, 0, col_k // HEAD_DIM + g)),
                  pl.BlockSpec((1, T, HEAD_DIM), lambda b, g: (b, 0, col_v // HEAD_DIM + g))] + wspec + wspec,
        out_specs=(ospec, ospec),
        out_shape=(out_sds, out_sds),
        compiler_params=pltpu.CompilerParams(dimension_semantics=("parallel", "parallel"),
                                             vmem_limit_bytes=V7X_VMEM_LIMIT_BYTES),
        name="nsa_compress_prompt",
    )(proj3, proj3,
      pe_k.reshape(1, flat), w1_k.reshape(flat, HEAD_DIM).astype(BF16), w2_k.astype(BF16),
      pe_v.reshape(1, flat), w1_v.reshape(flat, HEAD_DIM).astype(BF16), w2_v.astype(BF16))


ATT_TILE = 128
N_BIAS_TILES = 4


def _bucket_table():
    d = np.arange(MAX_DISTANCE)
    max_exact = N_BUCKETS // 2
    large = max_exact + (np.log(np.maximum(d, 1).astype(np.float32) / np.float32(max_exact))
                         / np.float32(math.log(MAX_DISTANCE / max_exact))
                         * np.float32(N_BUCKETS - max_exact)).astype(np.int32)
    return np.where(d < max_exact, d, np.minimum(large, N_BUCKETS - 1)).astype(np.int32)


def _bias_of_distance(rel_bias, dist):
    bt = _bucket_table()
    buckets = bt[np.clip(dist, 0, MAX_DISTANCE - 1)]
    b = jnp.moveaxis(rel_bias.astype(F32)[buckets], -1, 0)
    return jnp.where(jnp.asarray(dist >= 0)[None], b, NEG)


def _bias_by_distance(rel_bias):
    return rel_bias.astype(F32)[_bucket_table()].T


def _toeplitz(v, n):
    h = v.shape[0]
    w = jnp.pad(v, ((0, 0), (0, 1)))
    m = jnp.tile(w, (1, n))[:, :n * (2 * n - 1)].reshape(h, n, 2 * n - 1)
    return m[:, :, n - 1:]


def _softmax_step(s, m_ref, l_ref, acc_ref, v):
    m_old = m_ref[...]
    m_new = jnp.maximum(m_old, jnp.max(s, axis=-1, keepdims=True))
    alpha = jnp.exp(m_old - m_new)
    p = jnp.exp(s - m_new)
    l_ref[...] = alpha * l_ref[...] + jnp.sum(p, axis=-1, keepdims=True)
    acc_ref[...] = alpha * acc_ref[...] + jnp.dot(p.astype(BF16), v, preferred_element_type=F32)
    m_ref[...] = m_new


def _nsa_prompt_kernel(q_ref, kc_ref, vc_ref, ks_ref, vs_ref, kw_ref, vw_ref, gate_ref,
                       bcmp_ref, btile_ref, ovl_ref, exp_ref, o_ref,
                       ksb, vsb, kwb, vwb, m_ref, l_ref, acc_ref, mask_ref, *, n_sel, n_top):
    tq = ATT_TILE
    qi = pl.program_id(2)
    n_kt = mask_ref.shape[0]
    n_sel_pad = exp_ref.shape[0]

    @pl.when(qi == 0)
    def _():
        ksb[...] = ks_ref[0].astype(BF16)
        vsb[...] = vs_ref[0].astype(BF16)
        kwb[...] = kw_ref[0].astype(BF16)
        vwb[...] = vw_ref[0].astype(BF16)

    row = lax.broadcasted_iota(jnp.int32, (tq, LANE), 0)
    lane = lax.broadcasted_iota(jnp.int32, (tq, LANE), 1)
    t_pos = qi * tq + row
    gates = jax.nn.sigmoid(gate_ref[0, 0])

    valid = t_pos >= lane * CMP_STRIDE + (CMP_BLOCK - 1)
    kc = kc_ref[0, 0]
    vc = vc_ref[0, 0]
    psum = jnp.zeros((tq, LANE), F32)
    qs = []
    for j in range(NSA_HPG):
        q = (q_ref[0, :, j * HEAD_DIM:(j + 1) * HEAD_DIM] * HEAD_DIM ** -0.5).astype(BF16)
        qs.append(q)
        s = jnp.where(valid, _dot_nt(q, kc) + bcmp_ref[j], NEG)
        e = jnp.where(valid, jnp.exp(s - jnp.max(s, axis=-1, keepdims=True)), 0.0)
        den = jnp.sum(e, axis=-1, keepdims=True)
        p = e / jnp.where(den > 0.0, den, 1.0)
        psum = psum + p
        o_ref[0, :, j * HEAD_DIM:(j + 1) * HEAD_DIM] = gates[:, 3 * j:3 * j + 1] * jnp.dot(
            p.astype(BF16), vc, preferred_element_type=F32)

    hi = psum.astype(BF16)
    r1 = psum - hi.astype(F32)
    mid = r1.astype(BF16)
    lo = (r1 - mid.astype(F32)).astype(BF16)
    ovl = ovl_ref[...]
    p_sel = _dot_nt(ovl, hi) + _dot_nt(ovl, mid) + _dot_nt(ovl, lo)
    blk = lax.broadcasted_iota(jnp.int32, (n_sel_pad, tq), 0)
    cur = (qi * tq + lax.broadcasted_iota(jnp.int32, (n_sel_pad, tq), 1)) // SEL_BLOCK
    forced = (blk == 0) | (blk == cur) | (blk == cur - 1)
    score = jnp.where(blk <= cur, p_sel + jnp.where(forced, FORCE_SCORE, 0.0), NEG)
    rank = jnp.zeros((n_sel_pad, tq), F32)
    for jb in range(n_sel):
        other = score[jb:jb + 1, :]
        beats = (other > score) | ((other == score) & (blk > jb))
        rank = rank + jnp.where(beats, 1.0, 0.0)
    sel = jnp.where((rank < n_top) & (blk <= cur) & (blk < n_sel), 1.0, 0.0).astype(BF16)
    for kt in range(n_kt):
        mask_ref[kt] = _dot_tn(sel, exp_ref[:, kt * tq:(kt + 1) * tq])

    def attend(j, kb, vb, kt, bias_idx, use_mask):
        off = pl.multiple_of(kt * tq, tq)
        s = _dot_nt(qs[j], kb[pl.ds(off, tq), :]) + btile_ref[j, bias_idx]
        if use_mask:
            s = jnp.where(mask_ref[kt] > 0.5, s, NEG)
        _softmax_step(s, m_ref, l_ref, acc_ref, vb[pl.ds(off, tq), :])

    def reset():
        m_ref[...] = jnp.full(m_ref.shape, NEG, F32)
        l_ref[...] = jnp.zeros(l_ref.shape, F32)
        acc_ref[...] = jnp.zeros(acc_ref.shape, F32)

    for j in range(NSA_HPG):
        cols = slice(j * HEAD_DIM, (j + 1) * HEAD_DIM)
        reset()

        def slc_body(kt, c, j=j):
            attend(j, ksb, vsb, kt, jnp.minimum(qi - kt, 2), True)
            return c

        lax.fori_loop(0, qi + 1, slc_body, 0)
        o_ref[0, :, cols] = o_ref[0, :, cols] + gates[:, 3 * j + 1:3 * j + 2] * (
            acc_ref[...] / l_ref[...])

        reset()

        def win_body(kt, c, j=j):
            dd = qi - kt
            attend(j, kwb, vwb, kt, jnp.where(dd == WINDOW // ATT_TILE, 3, jnp.minimum(dd, 2)), False)
            return c

        lax.fori_loop(jnp.maximum(qi - WINDOW // ATT_TILE, 0), qi + 1, win_body, 0)
        o_ref[0, :, cols] = o_ref[0, :, cols] + gates[:, 3 * j + 2:3 * j + 3] * (
            acc_ref[...] / l_ref[...])


def nsa_prompt(proj3, cols, kcmp, vcmp, rel_bias):
    B, T, _ = proj3.shape
    tq = ATT_TILE
    assert T % tq == 0 and T % SEL_BLOCK == 0
    G, J = NSA_KV_HEADS, NSA_HPG
    nq = T // tq
    n_sel = T // SEL_BLOCK
    n_top = min(SEL_TOP, n_sel)
    n_sel_pad = _round_up(n_sel, 16)
    n_pad = kcmp.shape[2]
    assert n_pad == LANE, "one lane tile of compressed blocks"

    t = np.arange(T)[:, None]
    n = np.arange(n_pad)[None, :]
    bcmp = _bias_of_distance(rel_bias, np.maximum(t - (n * CMP_STRIDE + CMP_BLOCK - 1), 0))
    r = np.arange(tq)[:, None]
    c = np.arange(tq)[None, :]
    far = np.full((tq, tq), MAX_DISTANCE)
    edge = np.where(r <= c, MAX_DISTANCE, -1)
    btile = _bias_of_distance(rel_bias, np.stack([r - c, tq + r - c, far, edge]))
    cmp_start = np.arange(n_pad) * CMP_STRIDE
    sel_start = np.arange(n_sel_pad) * SEL_BLOCK
    ovl = ((cmp_start[None, :] < sel_start[:, None] + SEL_BLOCK)
           & (cmp_start[None, :] + CMP_BLOCK > sel_start[:, None])
           & (np.arange(n_pad)[None, :] < T // CMP_STRIDE - 1))
    ovl = jnp.asarray(ovl, BF16)
    expand = jnp.asarray(np.arange(T)[None, :] // SEL_BLOCK == np.arange(n_sel_pad)[:, None], BF16)
    gates = proj3[:, :, cols['ng']:cols['ng'] + 3 * NSA_HEADS].reshape(B, T, G, 3 * J).transpose(0, 2, 1, 3)

    kv_spec = lambda name: pl.BlockSpec((1, T, HEAD_DIM),
                                        lambda b, g, i, o=cols[name] // HEAD_DIM: (b, 0, o + g))
    cmp_spec = pl.BlockSpec((1, 1, n_pad, HEAD_DIM), lambda b, g, i: (b, g, 0, 0))
    return pl.pallas_call(
        functools.partial(_nsa_prompt_kernel, n_sel=n_sel, n_top=n_top),
        grid=(B, G, nq),
        in_specs=[pl.BlockSpec((1, tq, J * HEAD_DIM),
                               lambda b, g, i, o=cols['nq'] // (J * HEAD_DIM): (b, i, o + g)),
                  cmp_spec, cmp_spec,
                  kv_spec('ks'), kv_spec('vs'), kv_spec('kw'), kv_spec('vw'),
                  pl.BlockSpec((1, 1, tq, 3 * J), lambda b, g, i: (b, g, i, 0)),
                  pl.BlockSpec((J, tq, n_pad), lambda b, g, i: (g, i, 0)),
                  pl.BlockSpec((J, N_BIAS_TILES, tq, tq), lambda b, g, i: (g, 0, 0, 0)),
                  pl.BlockSpec((n_sel_pad, n_pad), lambda b, g, i: (0, 0)),
                  pl.BlockSpec((n_sel_pad, T), lambda b, g, i: (0, 0))],
        out_specs=pl.BlockSpec((1, tq, J * HEAD_DIM), lambda b, g, i: (b, i, g)),
        out_shape=jax.ShapeDtypeStruct((B, T, NSA_WIDTH), F32),
        scratch_shapes=[pltpu.VMEM((T, HEAD_DIM), BF16)] * 4 + [
            pltpu.VMEM((tq, LANE), F32), pltpu.VMEM((tq, LANE), F32), pltpu.VMEM((tq, HEAD_DIM), F32),
            pltpu.VMEM((nq, tq, tq), F32)],
        compiler_params=pltpu.CompilerParams(
            dimension_semantics=("parallel", "parallel", "arbitrary"),
            vmem_limit_bytes=V7X_VMEM_LIMIT_BYTES),
        name="nsa_prompt",
    )(proj3, kcmp, vcmp, proj3, proj3, proj3, proj3, gates, bcmp, btile, ovl, expand)


SLC_CLASS_TILES = 4


def _rows_softmax_pv(s_tiles, v_tiles):
    m = s_tiles[0]
    for s in s_tiles[1:]:
        m = jnp.maximum(m, s)
    m = jnp.max(m, axis=-1, keepdims=True)
    l = None
    o = None
    for s, v in zip(s_tiles, v_tiles):
        p = jnp.exp(s - m)
        l = p if l is None else l + p
        pv = jnp.dot(p.astype(BF16), v, preferred_element_type=F32)
        o = pv if o is None else o + pv
    return o / jnp.sum(l, axis=-1, keepdims=True)


def _nsa_prompt2_kernel(q_ref, kc_ref, vc_ref, ks_ref, vs_ref, kw_ref, vw_ref, gate_ref,
                        bcmp_ref, btile_ref, ovl_ref, pick_ref, cvec_ref, kaug_s_ref, kaug_w_ref, o_ref,
                        ksb, vsb, kwb, vwb, s_ref, *, n_sel, n_top, T):
    tq = ATT_TILE
    J = NSA_HPG
    qi = pl.program_id(2)
    n_sel_pad = pick_ref.shape[0]
    nq = T // tq

    @pl.when(qi == 0)
    def _():
        ksb[:, :HEAD_DIM] = ks_ref[0].astype(BF16)
        ksb[:, HEAD_DIM:] = kaug_s_ref[...]
        vsb[...] = vs_ref[0].astype(BF16)
        kwb[:WINDOW, :HEAD_DIM] = jnp.zeros((WINDOW, HEAD_DIM), BF16)
        kwb[WINDOW:, :HEAD_DIM] = kw_ref[0].astype(BF16)
        kwb[:, HEAD_DIM:] = kaug_w_ref[...]
        vwb[:WINDOW, :] = jnp.zeros((WINDOW, HEAD_DIM), BF16)
        vwb[WINDOW:, :] = vw_ref[0].astype(BF16)

    def stack(f):
        return jnp.concatenate([f(j) for j in range(J)], axis=0)

    def put(vals, first):
        for j in range(J):
            cols = slice(j * HEAD_DIM, (j + 1) * HEAD_DIM)
            v = vals[j * tq:(j + 1) * tq]
            o_ref[0, :, cols] = v if first else o_ref[0, :, cols] + v

    gates = jax.nn.sigmoid(gate_ref[0, 0])
    gate = lambda c: stack(lambda j: gates[:, 3 * j + c:3 * j + c + 1])
    q_all = stack(lambda j: q_ref[0, :, j * HEAD_DIM:(j + 1) * HEAD_DIM] * HEAD_DIM ** -0.5).astype(BF16)

    row = lax.broadcasted_iota(jnp.int32, (tq, LANE), 0)
    lane = lax.broadcasted_iota(jnp.int32, (tq, LANE), 1)
    valid1 = qi * tq + row >= lane * CMP_STRIDE + (CMP_BLOCK - 1)
    valid = stack(lambda j: valid1)
    s = jnp.where(valid, _dot_nt(q_all, kc_ref[0, 0]) + stack(lambda j: bcmp_ref[j]), NEG)
    e = jnp.where(valid, jnp.exp(s - jnp.max(s, axis=-1, keepdims=True)), 0.0)
    den = jnp.sum(e, axis=-1, keepdims=True)
    p = e / jnp.where(den > 0.0, den, 1.0)
    put(gate(0) * jnp.dot(p.astype(BF16), vc_ref[0, 0], preferred_element_type=F32), True)
    psum = p[0:tq]
    for j in range(1, J):
        psum = psum + p[j * tq:(j + 1) * tq]

    hi = psum.astype(BF16)
    r1 = psum - hi.astype(F32)
    mid = r1.astype(BF16)
    lo = (r1 - mid.astype(F32)).astype(BF16)
    ovl = ovl_ref[...]
    p_sel = _dot_nt(ovl, hi) + _dot_nt(ovl, mid) + _dot_nt(ovl, lo)
    blk = lax.broadcasted_iota(jnp.int32, (n_sel_pad, tq), 0)
    cur = (qi * tq + lax.broadcasted_iota(jnp.int32, (n_sel_pad, tq), 1)) // SEL_BLOCK
    forced = (blk == 0) | (blk == cur) | (blk == cur - 1)
    score = jnp.where(blk <= cur, p_sel + jnp.where(forced, FORCE_SCORE, 0.0), NEG)
    rank = jnp.zeros((n_sel_pad, tq), F32)
    for jb in range(n_sel):
        other = score[jb:jb + 1, :]
        beats = (other > score) | ((other == score) & (blk > jb))
        rank = rank + jnp.where(beats, 1.0, 0.0)
    sel = jnp.where((rank < n_top) & (blk <= cur) & (blk < n_sel), 1.0, 0.0).astype(BF16)
    aug = (_dot_tn(sel, pick_ref[...]) - cvec_ref[...]).astype(BF16)
    q_aug = jnp.concatenate([q_all, stack(lambda j: aug)], axis=1)

    bt = lambda i: stack(lambda j: btile_ref[j, i])

    for cls in range(-(-nq // SLC_CLASS_TILES)):
        n_t = min((cls + 1) * SLC_CLASS_TILES, nq)

        @pl.when(qi // SLC_CLASS_TILES == cls)
        def _(n_t=n_t):
            for kt in range(n_t):
                s_ref[kt] = _dot_nt(q_aug, ksb[kt * tq:(kt + 1) * tq, :])
            s_ref[qi] = s_ref[qi] + bt(0)

            @pl.when(qi > 0)
            def _():
                s_ref[qi - 1] = s_ref[qi - 1] + bt(1)

            o = _rows_softmax_pv([s_ref[kt] for kt in range(n_t)],
                                 [vsb[kt * tq:(kt + 1) * tq, :] for kt in range(n_t)])
            put(gate(1) * o, False)

    n_w = WINDOW // tq + 1
    s_tiles, v_tiles = [], []
    for w in range(n_w):
        rows = pl.ds(pl.multiple_of((qi + w) * tq, tq), tq)
        s = _dot_nt(q_aug, kwb[rows, :])
        if w == 0:
            s = s + bt(3)
        elif w == n_w - 2:
            s = s + bt(1)
        elif w == n_w - 1:
            s = s + bt(0)
        s_tiles.append(s)
        v_tiles.append(vwb[rows, :])
    put(gate(2) * _rows_softmax_pv(s_tiles, v_tiles), False)


def nsa_prompt2(proj3, cols, kcmp, vcmp, rel_bias):
    B, T, _ = proj3.shape
    tq = ATT_TILE
    assert T % tq == 0 and T % SEL_BLOCK == 0 and WINDOW % tq == 0
    G, J = NSA_KV_HEADS, NSA_HPG
    nq = T // tq
    n_sel = T // SEL_BLOCK
    n_top = min(SEL_TOP, n_sel)
    n_sel_pad = _round_up(n_sel, 16)
    assert n_sel_pad < LANE
    n_pad = kcmp.shape[2]
    assert n_pad == LANE, "one lane tile of compressed blocks"

    bd = _bias_by_distance(rel_bias)
    by_dist = jnp.concatenate([bd, jnp.broadcast_to(bd[:, -1:], (NSA_HEADS, max(T - MAX_DISTANCE, 0)))], axis=1)
    shifts = [n * CMP_STRIDE + CMP_BLOCK - 1 for n in range(n_pad)]
    bcmp = jnp.stack([jnp.pad(by_dist[:, :T - s], ((0, 0), (s, 0))) if s < T else jnp.zeros((NSA_HEADS, T), F32)
                      for s in shifts], axis=-1)
    d_diag = (tq - 1) - np.arange(2 * tq - 1)
    rel = bd - bd[:, -1:]
    gen = lambda d: jnp.where(jnp.asarray(d >= 0)[None], rel[:, np.clip(d, 0, MAX_DISTANCE - 1)], NEG)
    edge = jnp.broadcast_to(jnp.where(jnp.asarray(d_diag <= 0), 0.0, NEG)[None], (NSA_HEADS, 2 * tq - 1))
    btile = jnp.stack([_toeplitz(gen(d_diag), tq), _toeplitz(gen(d_diag + tq), tq),
                       jnp.zeros((NSA_HEADS, tq, tq), F32), _toeplitz(edge, tq)], axis=1)
    cmp_start = np.arange(n_pad) * CMP_STRIDE
    sel_start = np.arange(n_sel_pad) * SEL_BLOCK
    ovl = ((cmp_start[None, :] < sel_start[:, None] + SEL_BLOCK)
           & (cmp_start[None, :] + CMP_BLOCK > sel_start[:, None])
           & (np.arange(n_pad)[None, :] < T // CMP_STRIDE - 1))
    ovl = jnp.asarray(ovl, BF16)
    pick = jnp.asarray(np.arange(LANE)[None, :] == np.arange(n_sel_pad)[:, None], BF16)
    lane_i = np.arange(LANE)
    cvec = jnp.asarray(((lane_i < n_sel) | (lane_i == n_sel_pad))[None, :], F32)
    big = -NEG
    kaug_s = jnp.asarray(np.where(np.arange(T)[:, None] // SEL_BLOCK == lane_i[None, :], big, 0.0), BF16)
    kaug_w = jnp.asarray(np.where((np.arange(T + WINDOW)[:, None] < WINDOW) & (lane_i[None, :] == n_sel_pad),
                                  big, 0.0), BF16)
    gates = proj3[:, :, cols['ng']:cols['ng'] + 3 * NSA_HEADS].reshape(B, T, G, 3 * J).transpose(0, 2, 1, 3)

    kv_spec = lambda name: pl.BlockSpec((1, T, HEAD_DIM),
                                        lambda b, g, i, o=cols[name] // HEAD_DIM: (b, 0, o + g))
    cmp_spec = pl.BlockSpec((1, 1, n_pad, HEAD_DIM), lambda b, g, i: (b, g, 0, 0))
    const2 = lambda shape: pl.BlockSpec(shape, lambda b, g, i: (0, 0))
    return pl.pallas_call(
        functools.partial(_nsa_prompt2_kernel, n_sel=n_sel, n_top=n_top, T=T),
        grid=(B, G, nq),
        in_specs=[pl.BlockSpec((1, tq, J * HEAD_DIM),
                               lambda b, g, i, o=cols['nq'] // (J * HEAD_DIM): (b, i, o + g)),
                  cmp_spec, cmp_spec,
                  kv_spec('ks'), kv_spec('vs'), kv_spec('kw'), kv_spec('vw'),
                  pl.BlockSpec((1, 1, tq, 3 * J), lambda b, g, i: (b, g, i, 0)),
                  pl.BlockSpec((J, tq, n_pad), lambda b, g, i: (g, i, 0)),
                  pl.BlockSpec((J, N_BIAS_TILES, tq, tq), lambda b, g, i: (g, 0, 0, 0)),
                  const2((n_sel_pad, n_pad)), const2((n_sel_pad, LANE)), const2((1, LANE)),
                  const2((T, LANE)), const2((T + WINDOW, LANE))],
        out_specs=pl.BlockSpec((1, tq, J * HEAD_DIM), lambda b, g, i: (b, i, g)),
        out_shape=jax.ShapeDtypeStruct((B, T, NSA_WIDTH), F32),
        scratch_shapes=[pltpu.VMEM((T, 2 * HEAD_DIM), BF16), pltpu.VMEM((T, HEAD_DIM), BF16),
                        pltpu.VMEM((T + WINDOW, 2 * HEAD_DIM), BF16), pltpu.VMEM((T + WINDOW, HEAD_DIM), BF16),
                        pltpu.VMEM((nq, J * tq, tq), F32)],
        compiler_params=pltpu.CompilerParams(
            dimension_semantics=("parallel", "parallel", "arbitrary"),
            vmem_limit_bytes=V7X_VMEM_LIMIT_BYTES),
        name="nsa_prompt",
    )(proj3, kcmp, vcmp, proj3, proj3, proj3, proj3, gates, bcmp, btile, ovl, pick, cvec, kaug_s, kaug_w)


PAGE_ROWS = PAGE_SIZE * NSA_KV_HEADS
STRIDES_PER_PAGE = PAGE_SIZE // CMP_STRIDE
PACK_ROWS = 16


def _cache_rows(cache):
    return cache.reshape(-1, HEAD_DIM)


PAGES_PER_STEP = 4


def _page_specs(layer):
    return [pl.BlockSpec((2 * PAGE_ROWS, HEAD_DIM),
                         lambda b, p, pt, k=k: (pt[b, p * PAGES_PER_STEP + k] * DEPTH + layer, 0))
            for k in range(PAGES_PER_STEP)]


def _cmp_partial_kernel(pt_ref, *refs):
    page_refs = refs[:PAGES_PER_STEP]
    w1k_ref, w1v_ref, z_ref = refs[PAGES_PER_STEP:]
    G = NSA_KV_HEADS
    half = CMP_STRIDE * HEAD_DIM
    n_rows = PAGES_PER_STEP * STRIDES_PER_PAGE
    for kv, w1_ref in ((0, w1k_ref), (1, w1v_ref)):
        xs = []
        for g in range(G):
            for page_ref in page_refs:
                pieces = [page_ref[pl.ds(kv * PAGE_ROWS + s * G + g, STRIDES_PER_PAGE, stride=CMP_STRIDE * G), :]
                          for s in range(CMP_STRIDE)]
                xs.append(jnp.concatenate(pieces, axis=1))
        x = jnp.concatenate(xs, axis=0).astype(BF16)
        for j in range(CMP_BLOCK // CMP_STRIDE):
            z = jnp.dot(x, w1_ref[j * half:(j + 1) * half, :], preferred_element_type=F32)
            for g in range(G):
                z_ref[0, kv, j, g] = z[g * n_rows:(g + 1) * n_rows]


def cmp_partial(cache_rows, page_table, layer, w1_k, w1_v):
    B, n_pages = page_table.shape
    flat = CMP_BLOCK * HEAD_DIM
    n_str = n_pages * STRIDES_PER_PAGE
    assert n_pages % PAGES_PER_STEP == 0
    wspec = pl.BlockSpec((flat, HEAD_DIM), lambda b, p, pt: (0, 0))
    return pl.pallas_call(
        _cmp_partial_kernel,
        grid_spec=pltpu.PrefetchScalarGridSpec(
            num_scalar_prefetch=1,
            grid=(B, n_pages // PAGES_PER_STEP),
            in_specs=_page_specs(layer) + [wspec, wspec],
            out_specs=pl.BlockSpec((1, 2, 2, NSA_KV_HEADS, PAGES_PER_STEP * STRIDES_PER_PAGE, HEAD_DIM),
                                   lambda b, p, pt: (b, 0, 0, 0, p, 0))),
        out_shape=jax.ShapeDtypeStruct((B, 2, 2, NSA_KV_HEADS, n_str, HEAD_DIM), F32),
        compiler_params=pltpu.CompilerParams(dimension_semantics=("parallel", "arbitrary"),
                                             vmem_limit_bytes=V7X_VMEM_LIMIT_BYTES),
        name="nsa_cmp_partial",
    )(page_table, *([cache_rows] * PAGES_PER_STEP),
      w1_k.reshape(flat, HEAD_DIM).astype(BF16), w1_v.reshape(flat, HEAD_DIM).astype(BF16))


def _pad_rows(x, rows):
    extra = rows - x.shape[0]
    return jnp.concatenate([x, jnp.zeros((extra, x.shape[1]), x.dtype)], axis=0) if extra else x


def _cmp_attn_sample_kernel(z_ref, pek_ref, w1k_ref, w2k_ref, pev_ref, w1v_ref, w2v_ref, q_ref, bias_ref,
                            ovl_ref, o_ref, psel_ref, *, q0):
    T = q_ref.shape[1]
    J = NSA_HPG
    n_str = z_ref.shape[4]

    def finish(kv, pe_ref, w1_ref, w2_ref):
        z1 = pltpu.roll(z_ref[0, kv, 1, 0], n_str - 1, 0)
        pe = jnp.broadcast_to(pe_ref[...], (PACK_ROWS, CMP_BLOCK * HEAD_DIM)).astype(BF16)
        h0 = jnp.dot(pe, w1_ref[...], preferred_element_type=F32)[0:1, :]
        h = _gelu_tanh(z_ref[0, kv, 0, 0] + z1 + h0)
        return jnp.dot(h.astype(BF16), w2_ref[...], preferred_element_type=F32).astype(BF16)

    kc = finish(0, pek_ref, w1k_ref, w2k_ref)
    vc = finish(1, pev_ref, w1v_ref, w2v_ref)
    stack = lambda f: jnp.concatenate([f(j) for j in range(J)], axis=0)
    q_all = stack(lambda j: q_ref[0, :, j * HEAD_DIM:(j + 1) * HEAD_DIM] * HEAD_DIM ** -0.5).astype(BF16)
    row = lax.broadcasted_iota(jnp.int32, (T, n_str), 0)
    lane = lax.broadcasted_iota(jnp.int32, (T, n_str), 1)
    valid1 = (q0 + row >= lane * CMP_STRIDE + (CMP_BLOCK - 1)) & (lane < n_str - 1)
    valid = stack(lambda j: valid1)
    s = jnp.where(valid, _dot_nt(q_all, kc) + stack(lambda j: bias_ref[j]), NEG)
    e = jnp.where(valid, jnp.exp(s - jnp.max(s, axis=-1, keepdims=True)), 0.0)
    den = jnp.sum(e, axis=-1, keepdims=True)
    p = e / jnp.where(den > 0.0, den, 1.0)
    o_ref[0, 0] = jnp.dot(p.astype(BF16), vc, preferred_element_type=F32)
    psum = p[0:T]
    for j in range(1, J):
        psum = psum + p[j * T:(j + 1) * T]
    psum = _pad_rows(psum, PACK_ROWS)
    hi = psum.astype(BF16)
    r1 = psum - hi.astype(F32)
    mid = r1.astype(BF16)
    lo = (r1 - mid.astype(F32)).astype(BF16)
    ovl = ovl_ref[...]
    p_sel = (jnp.dot(hi, ovl, preferred_element_type=F32) + jnp.dot(mid, ovl, preferred_element_type=F32)
             + jnp.dot(lo, ovl, preferred_element_type=F32))
    psel_ref[0, 0] = p_sel[:T]


def _rank_select_kernel(psel_ref, tpos_ref, out_ref, score_ref, rank_ref, *, n_sel, n_top):
    shape = psel_ref.shape
    blk = lax.broadcasted_iota(jnp.int32, shape, 0)
    cur = jnp.broadcast_to(tpos_ref[...], shape) // SEL_BLOCK
    forced = (blk == 0) | (blk == cur) | (blk == cur - 1)
    ok = (blk <= cur) & (blk < n_sel)
    score = jnp.where(ok, psel_ref[...] + jnp.where(forced, FORCE_SCORE, 0.0), NEG)
    score_ref[...] = score
    rank_ref[...] = jnp.zeros(shape, F32)

    def body(jb, c):
        other = jnp.broadcast_to(score_ref[pl.ds(jb, 1), :], shape)
        beats = (other > score) | ((other == score) & (blk > jb))
        rank_ref[...] = rank_ref[...] + jnp.where(beats, 1.0, 0.0)
        return c

    lax.fori_loop(0, n_sel, body, 0)
    out_ref[...] = jnp.where((rank_ref[...] < n_top) & ok, 0.0, NEG)


def _slc_win_sample_kernel(pt_ref, *refs, win_buf):
    page_refs = refs[:PAGES_PER_STEP]
    (qT_ref, seladd_ref, blast_ref, selnew_ref, ksn_ref, vsn_ref, kwn_ref, vwn_ref, bnew_ref, cw_ref, bwin_ref,
     ocmp_ref, gate_ref, o_ref, m_ref, l_ref, acc_ref) = refs[PAGES_PER_STEP:]
    G = NSA_KV_HEADS
    T = ksn_ref.shape[1]
    p = pl.program_id(1)
    last = pl.num_programs(1) - 1
    lane_group = lax.broadcasted_iota(jnp.int32, (1, LANE), 1) // (LANE // G)

    @pl.when(p == 0)
    def _():
        m_ref[...] = jnp.full(m_ref.shape, NEG, F32)
        l_ref[...] = jnp.zeros(l_ref.shape, F32)
        acc_ref[...] = jnp.zeros(acc_ref.shape, F32)

    def scores(k_of_g):
        s = None
        for g in range(G):
            sg = jnp.dot(k_of_g(g).astype(BF16), qT_ref[0, g], preferred_element_type=F32)
            s = sg if s is None else s + sg
        return s

    def weighted_values(pT, v_of_g):
        o = None
        for g in range(G):
            pg = jnp.where(lane_group == g, pT, 0.0).astype(BF16)
            og = _dot_tn(v_of_g(g).astype(BF16), pg)
            o = og if o is None else o + og
        return o

    def accumulate(segments):
        m_old = m_ref[0:1, :]
        m_new = m_old
        for sT, _ in segments:
            m_new = jnp.maximum(m_new, jnp.max(sT, axis=0, keepdims=True))
        alpha = jnp.exp(m_old - m_new)
        l_new = alpha * l_ref[0:1, :]
        acc = alpha * acc_ref[...]
        for sT, v_of_g in segments:
            pT = jnp.exp(sT - m_new)
            l_new = l_new + jnp.sum(pT, axis=0, keepdims=True)
            acc = acc + weighted_values(pT, v_of_g)
        l_ref[...] = jnp.broadcast_to(l_new, l_ref.shape)
        acc_ref[...] = acc
        m_ref[...] = jnp.broadcast_to(m_new, m_ref.shape)

    key = lax.broadcasted_iota(jnp.int32, (PAGE_SIZE, LANE), 0)
    blocks_per_page = PAGE_SIZE // SEL_BLOCK
    is_last = (p == last).astype(F32)
    segments = []
    for k, page_ref in enumerate(page_refs):
        k_page = lambda g, r=page_ref: r[pl.ds(g, PAGE_SIZE, stride=G), :]
        v_page = lambda g, r=page_ref: r[pl.ds(PAGE_ROWS + g, PAGE_SIZE, stride=G), :]
        mask = seladd_ref[0, k, blocks_per_page - 1:blocks_per_page, :]
        for i in range(blocks_per_page - 2, -1, -1):
            mask = jnp.where(key < (i + 1) * SEL_BLOCK, seladd_ref[0, k, i:i + 1, :], mask)
        sT = scores(k_page) + mask
        if k == PAGES_PER_STEP - 1:
            sT = sT + blast_ref[...] * is_last
        segments.append((sT, v_page))
    accumulate(segments)

    @pl.when(p == last)
    def _():
        new = lambda ref: (lambda g: _pad_rows(ref[0, :, g * HEAD_DIM:(g + 1) * HEAD_DIM], PACK_ROWS))
        accumulate([(scores(new(ksn_ref)) + bnew_ref[...] + selnew_ref[0], new(vsn_ref))])
        o_slc = (acc_ref[...] / l_ref[0:1, :]).T

        k_win = lambda g: cw_ref[pl.ds(g, win_buf, stride=G), :]
        v_win = lambda g: cw_ref[pl.ds(win_buf * G + g, win_buf, stride=G), :]
        s_w = scores(k_win) + bwin_ref[...]
        s_n = scores(new(kwn_ref)) + bnew_ref[...]
        m = jnp.maximum(jnp.max(s_w, axis=0, keepdims=True), jnp.max(s_n, axis=0, keepdims=True))
        p_w = jnp.exp(s_w - m)
        p_n = jnp.exp(s_n - m)
        den = jnp.sum(p_w, axis=0, keepdims=True) + jnp.sum(p_n, axis=0, keepdims=True)
        o_win = ((weighted_values(p_w, v_win) + weighted_values(p_n, new(vwn_ref))) / den).T

        gates = jax.nn.sigmoid(gate_ref[0])
        o_ref[0] = gates[:, 0:1] * ocmp_ref[0] + gates[:, 1:2] * o_slc + gates[:, 2:3] * o_win


def nsa_sample(proj3, cols, cache_cmp, cache_slc, cache_win, page_table, layer, rel_bias,
               pe_k, w1_k, w2_k, pe_v, w1_v, w2_v):
    B, T, _ = proj3.shape
    G, J, H = NSA_KV_HEADS, NSA_HPG, NSA_HEADS
    n_pages = page_table.shape[1]
    past = n_pages * PAGE_SIZE
    q0 = past
    win_buf = cache_win.shape[3]
    L = G * J * T
    assert L == LANE and T <= PACK_ROWS and T < CMP_STRIDE and past % SEL_BLOCK == 0 and T <= SEL_BLOCK
    assert win_buf == min(WINDOW, past)
    n_str = past // CMP_STRIDE
    n_sel = past // SEL_BLOCK + 1
    n_top = min(SEL_TOP, n_sel)
    n_sel_rows = _round_up(n_sel, 8)
    n_sel_lanes = _round_up(n_sel, LANE)
    flat = CMP_BLOCK * HEAD_DIM
    b_far = rel_bias.astype(F32)[_bucket_table()[MAX_DISTANCE - 1]]

    def lanes(x):
        x = jnp.moveaxis(x, 0, -1)
        return jnp.broadcast_to(x[..., None], x.shape + (T,)).reshape(x.shape[:-1] + (L,))

    def lane_bias(dist, ok):
        b = _bias_of_distance(rel_bias, np.maximum(dist, 0)) - b_far[:, None, None]
        b = jnp.where(jnp.asarray(ok)[None], b, NEG)
        return jnp.moveaxis(b.reshape(G, J, dist.shape[0], T), 2, 0).reshape(dist.shape[0], L)

    ti = np.arange(T)[None, :]
    z = cmp_partial(_cache_rows(cache_cmp), page_table, layer, w1_k, w1_v)
    n = np.arange(n_str)[None, :]
    dist_c = q0 + np.arange(T)[:, None] - (n * CMP_STRIDE + CMP_BLOCK - 1)
    n_far = int(np.argmax(dist_c.min(axis=0) < MAX_DISTANCE)) if (dist_c.min(axis=0) < MAX_DISTANCE).any() else n_str
    bcmp = jnp.concatenate([jnp.broadcast_to(b_far[:, None, None], (H, T, n_far)),
                            _bias_of_distance(rel_bias, np.maximum(dist_c[:, n_far:], 0))], axis=2)
    cmp_start = np.arange(n_str) * CMP_STRIDE
    sel_start = np.arange(n_sel_lanes) * SEL_BLOCK
    ovl = ((cmp_start[:, None] < sel_start[None, :] + SEL_BLOCK) & (cmp_start[:, None] + CMP_BLOCK > sel_start[None, :])
           & (np.arange(n_str)[:, None] < n_str - 1) & (np.arange(n_sel_lanes)[None, :] < n_sel))
    wspec = [pl.BlockSpec((1, flat), lambda b, g: (0, 0)),
             pl.BlockSpec((flat, HEAD_DIM), lambda b, g: (0, 0)),
             pl.BlockSpec((HEAD_DIM, HEAD_DIM), lambda b, g: (0, 0))]
    o_cmp, p_sel = pl.pallas_call(
        functools.partial(_cmp_attn_sample_kernel, q0=q0),
        grid=(B, G),
        in_specs=[pl.BlockSpec((1, 2, 2, 1, n_str, HEAD_DIM), lambda b, g: (b, 0, 0, g, 0, 0))] + wspec + wspec + [
            pl.BlockSpec((1, T, J * HEAD_DIM), lambda b, g, o=cols['nq'] // (J * HEAD_DIM): (b, 0, o + g)),
            pl.BlockSpec((J, T, n_str), lambda b, g: (g, 0, 0)),
            pl.BlockSpec((n_str, n_sel_lanes), lambda b, g: (0, 0))],
        out_specs=(pl.BlockSpec((1, 1, J * T, HEAD_DIM), lambda b, g: (b, g, 0, 0)),
                   pl.BlockSpec((1, 1, T, n_sel_lanes), lambda b, g: (b, g, 0, 0))),
        out_shape=(jax.ShapeDtypeStruct((B, G, J * T, HEAD_DIM), F32),
                   jax.ShapeDtypeStruct((B, G, T, n_sel_lanes), F32)),
        compiler_params=pltpu.CompilerParams(dimension_semantics=("parallel", "parallel"),
                                             vmem_limit_bytes=V7X_VMEM_LIMIT_BYTES),
        name="nsa_cmp_attn_sample",
    )(z, pe_k.reshape(1, flat), w1_k.reshape(flat, HEAD_DIM).astype(BF16), w2_k.astype(BF16),
      pe_v.reshape(1, flat), w1_v.reshape(flat, HEAD_DIM).astype(BF16), w2_v.astype(BF16),
      proj3, bcmp, jnp.asarray(ovl, BF16))

    n_bgt = B * G * T
    psel_t = p_sel.reshape(n_bgt, n_sel_lanes)[:, :n_sel_rows].T
    tpos = jnp.asarray(np.tile(q0 + np.arange(T), B * G)[None, :], jnp.int32)
    seladd = pl.pallas_call(
        functools.partial(_rank_select_kernel, n_sel=n_sel, n_top=n_top),
        out_shape=jax.ShapeDtypeStruct((n_sel_rows, n_bgt), F32),
        scratch_shapes=[pltpu.VMEM((n_sel_rows, n_bgt), F32), pltpu.VMEM((n_sel_rows, n_bgt), F32)],
        name="nsa_rank_select",
    )(psel_t, tpos)
    seladd = seladd.T.reshape(B, G, 1, T, n_sel_rows)
    seladd = jnp.broadcast_to(seladd, (B, G, J, T, n_sel_rows)).reshape(B, L, n_sel_rows)
    bpp = PAGE_SIZE // SEL_BLOCK
    sel_past = seladd[:, :, :n_sel - 1].reshape(B, L, n_pages, bpp).transpose(0, 2, 3, 1)
    sel_new = seladd[:, :, n_sel - 1].reshape(B, 1, L)

    q = proj3[:, :, cols['nq']:cols['nq'] + NSA_WIDTH].reshape(B, T, G, J, HEAD_DIM) * HEAD_DIM ** -0.5
    q_t = q.transpose(0, 2, 4, 3, 1).reshape(B, G, HEAD_DIM, J * T)
    place = jnp.asarray(np.arange(G)[:, None, None] == (np.arange(L) // (J * T))[None, None, :])
    q_pad = jnp.where(place[None], jnp.tile(q_t, (1, 1, 1, G)), 0.0).astype(BF16)
    ki = np.arange(PAGE_SIZE)[:, None]
    b_last = lane_bias(PAGE_SIZE + ti - ki, np.ones((PAGE_SIZE, T), bool))
    kn = np.arange(PACK_ROWS)[:, None]
    b_new = lane_bias(ti - kn, (ti - kn >= 0) & (kn < T))
    kw = np.arange(win_buf)[:, None]
    b_win = lane_bias(win_buf + ti - kw, win_buf + ti - kw <= WINDOW)
    gates = proj3[:, :, cols['ng']:cols['ng'] + 3 * H].reshape(B, T, G, J, 3).transpose(0, 2, 3, 1, 4).reshape(B, L, 3)
    win_rows = 2 * win_buf * G
    new_spec = lambda name: pl.BlockSpec((1, T, KV_WIDTH), lambda b, p, pt, o=cols[name] // KV_WIDTH: (b, 0, o))
    const = lambda shape: pl.BlockSpec(shape, lambda b, p, pt: (0, 0))
    per_b = lambda shape: pl.BlockSpec((1,) + shape, lambda b, p, pt: (b,) + (0,) * len(shape))
    out = pl.pallas_call(
        functools.partial(_slc_win_sample_kernel, win_buf=win_buf),
        grid_spec=pltpu.PrefetchScalarGridSpec(
            num_scalar_prefetch=1,
            grid=(B, n_pages // PAGES_PER_STEP),
            in_specs=_page_specs(layer) + [
                      per_b((G, HEAD_DIM, L)),
                      pl.BlockSpec((1, PAGES_PER_STEP, bpp, L), lambda b, p, pt: (b, p, 0, 0)),
                      const((PAGE_SIZE, L)), per_b((1, L)),
                      new_spec('ks'), new_spec('vs'), new_spec('kw'), new_spec('vw'),
                      const((PACK_ROWS, L)),
                      pl.BlockSpec((win_rows, HEAD_DIM), lambda b, p, pt: (layer * B + b, 0)),
                      const((win_buf, L)), per_b((L, HEAD_DIM)), per_b((L, 3))],
            out_specs=per_b((L, HEAD_DIM)),
            scratch_shapes=[pltpu.VMEM((8, L), F32), pltpu.VMEM((8, L), F32), pltpu.VMEM((HEAD_DIM, L), F32)]),
        out_shape=jax.ShapeDtypeStruct((B, L, HEAD_DIM), F32),
        compiler_params=pltpu.CompilerParams(dimension_semantics=("parallel", "arbitrary"),
                                             vmem_limit_bytes=V7X_VMEM_LIMIT_BYTES),
        name="nsa_slc_win_sample",
    )(page_table, *([_cache_rows(cache_slc)] * PAGES_PER_STEP), q_pad, sel_past, b_last, sel_new,
      proj3, proj3, proj3, proj3, b_new,
      cache_win.reshape(-1, HEAD_DIM), b_win, o_cmp.reshape(B, L, HEAD_DIM), gates)
    return out.reshape(B, G, J, T, HEAD_DIM).transpose(0, 3, 1, 2, 4).reshape(B, T, NSA_WIDTH)


def _retention_tables(T, q0):
    c = _largest_divisor(T, RET_CHUNK)
    cp = max(c, RET_CHUNK)
    lg = np.log1p(-(2.0 ** (-5.0 - np.arange(RET_HEADS, dtype=np.float32)))).astype(np.float32)
    i = np.arange(cp)
    rel = i[:, None] - i[None, :]
    inside = (i < c)[:, None] & (i < c)[None, :]
    decay = np.where((rel >= 0) & inside, np.exp(np.maximum(rel, 0)[None] * lg[:, None, None]), 0.0)
    q_dec = np.broadcast_to(np.exp((i + 1)[None, :, None] * lg[:, None, None]), (RET_HEADS, cp, HEAD_DIM))
    k_dec = np.where((i < c)[None, :, None], np.exp((c - 1 - i)[None, :, None] * lg[:, None, None]), 0.0)
    k_dec = np.broadcast_to(k_dec, (RET_HEADS, cp, HEAD_DIM))
    chunk_dec = np.broadcast_to(np.exp(c * lg)[:, None, None], (RET_HEADS, 8, HEAD_DIM))
    half = HEAD_DIM // 2
    inv = (1.0 / (10000.0 ** np.linspace(0.0, 1.0, half, dtype=np.float32))).astype(np.float32)
    ang = (q0 + np.arange(T)).astype(np.float32)[:, None] * inv[None]
    cos, sin = np.cos(ang), np.sin(ang)
    cosf = np.concatenate([cos, cos], axis=1)
    sinf = np.concatenate([-sin, sin], axis=1)
    f = lambda a: jnp.asarray(a, F32)
    return c, cp, f(decay), f(q_dec), f(k_dec), f(chunk_dec), f(cosf), f(sinf)


def _retention_kernel(q_ref, k_ref, v_ref, g_ref, s0_ref, cos_ref, sin_ref, dec_ref, qd_ref, kd_ref, cd_ref,
                      o_ref, s_ref, *, c, cp, n):
    decay = dec_ref[0]
    q_dec = qd_ref[0]
    k_dec = kd_ref[0]
    chunk_dec = cd_ref[0, 0:1, :]
    half = HEAD_DIM // 2

    def load(ref, rows):
        x = ref[rows, :]
        if cp > c:
            x = jnp.concatenate([x, jnp.zeros((cp - c, HEAD_DIM), x.dtype)], axis=0)
        return x

    def rot(x, cos, sin):
        return x * cos + pltpu.roll(x, half, 1) * sin

    def body(i, s):
        rows = pl.ds(pl.multiple_of(i * c, c), c)
        cos, sin = load(cos_ref, rows), load(sin_ref, rows)
        q = rot(load(q_ref.at[0], rows), cos, sin)
        k = rot(load(k_ref.at[0], rows), cos, sin) * HEAD_DIM ** -0.5
        v = load(v_ref.at[0], rows).astype(BF16)
        qb = q.astype(BF16)
        inner = _dot_nt(qb, k.astype(BF16)) * decay
        o = (jnp.dot(inner.astype(BF16), v, preferred_element_type=F32)
             + jnp.dot(qb, s.astype(BF16), preferred_element_type=F32) * q_dec)
        s = s * chunk_dec + _dot_tn((k * k_dec).astype(BF16), v)
        o = o * lax.rsqrt(jnp.mean(o * o, axis=-1, keepdims=True) + EPS)
        g = g_ref[0, rows, :]
        o_ref[0, rows, :] = g * jax.nn.sigmoid(g) * o[:c]
        return s

    s_ref[0, 0] = lax.fori_loop(0, n, body, s0_ref[0, 0])


def retention(proj3, cols, s0, q0):
    B, T, _ = proj3.shape
    c, cp, decay, q_dec, k_dec, chunk_dec, cosf, sinf = _retention_tables(T, q0)
    col = lambda name: pl.BlockSpec((1, T, HEAD_DIM), lambda b, h, o=cols[name] // HEAD_DIM: (b, 0, o + h))
    tab = lambda r: pl.BlockSpec((1, r, HEAD_DIM), lambda b, h: (h, 0, 0))
    full = pl.BlockSpec((T, HEAD_DIM), lambda b, h: (0, 0))
    state = pl.BlockSpec((1, 1, HEAD_DIM, HEAD_DIM), lambda b, h: (b, h, 0, 0))
    return pl.pallas_call(
        functools.partial(_retention_kernel, c=c, cp=cp, n=T // c),
        grid=(B, RET_HEADS),
        in_specs=[col('rq'), col('rk'), col('rv'), col('rg'), state, full, full,
                  tab(cp), tab(cp), tab(cp), tab(8)],
        out_specs=(pl.BlockSpec((1, T, HEAD_DIM), lambda b, h: (b, 0, h)), state),
        out_shape=(jax.ShapeDtypeStruct((B, T, RET_WIDTH), F32),
                   jax.ShapeDtypeStruct((B, RET_HEADS, HEAD_DIM, HEAD_DIM), F32)),
        compiler_params=pltpu.CompilerParams(dimension_semantics=("parallel", "parallel"),
                                             vmem_limit_bytes=V7X_VMEM_LIMIT_BYTES),
        name="retention",
    )(proj3, proj3, proj3, proj3, s0.astype(F32), cosf, sinf, decay, q_dec, k_dec, chunk_dec)


S5_BLK_GROUPS = 8
S5_BLK_STATE = S5_BLK_GROUPS * S5_STATE
S5_BLK_CH = S5_BLK_GROUPS * S5_GROUP
S5_SCAN_ROWS = 8


def _s5_params(lam_re, lam_im, log_step, b_re, b_im, c_re, c_im):
    nb = S5_GROUPS // S5_BLK_GROUPS
    lam = lax.complex(lam_re.astype(F32), lam_im.astype(F32))
    step = jnp.exp(log_step.astype(F32))[:, None]
    a_bar = jnp.exp(lam * step)
    b_bar = ((a_bar - 1.0) / lam)[..., None] * lax.complex(b_re.astype(F32), b_im.astype(F32))
    r = np.arange(S5_SCAN_ROWS)

    def powers(k, keep):
        p = jnp.exp(lam[None] * step[None] * jnp.asarray(k, F32)[:, None, None])
        return jnp.where(jnp.asarray(keep)[:, None, None], p, 0.0)

    tabs = [powers(np.full(S5_SCAN_ROWS, k), r >= k) for k in (1, 2, 4)]
    tabs.append(powers(r + 1, r >= 0))
    tab = jnp.stack(tabs)
    tab = tab.reshape(4, S5_SCAN_ROWS, nb, S5_BLK_STATE).transpose(2, 0, 1, 3)
    atab = jnp.concatenate([tab.real, tab.imag], axis=1)

    eye = jnp.eye(S5_BLK_GROUPS, dtype=F32)
    bb = b_bar.reshape(nb, S5_BLK_GROUPS, S5_STATE, S5_GROUP)

    def in_mat(x):
        return jnp.einsum('ngpc,gh->ngchp', x, eye).reshape(nb, S5_BLK_CH, S5_BLK_STATE)

    bmat = jnp.concatenate([in_mat(bb.real), in_mat(bb.imag)], axis=-1)
    bmat = bmat.astype(BF16)
    cr = c_re.astype(F32).reshape(nb, S5_BLK_GROUPS, S5_GROUP, S5_STATE)
    ci = c_im.astype(F32).reshape(nb, S5_BLK_GROUPS, S5_GROUP, S5_STATE)

    def out_mat(x):
        return jnp.einsum('ngcp,gh->ngphc', x, eye).reshape(nb, S5_BLK_STATE, S5_BLK_CH)

    cmat = jnp.concatenate([out_mat(cr), -out_mat(ci)], axis=1).astype(BF16)
    return atab, bmat, cmat


def _s5_scan_tile(xr, xi, cr, ci, atab_ref):
    for idx, k in enumerate((1, 2, 4)):
        pr, pi = atab_ref[0, idx], atab_ref[0, 4 + idx]
        sr, si = pltpu.roll(xr, k, 0), pltpu.roll(xi, k, 0)
        xr, xi = xr + pr * sr - pi * si, xi + pr * si + pi * sr
    pr, pi = atab_ref[0, 3], atab_ref[0, 7]
    xr, xi = xr + pr * cr - pi * ci, xi + pr * ci + pi * cr
    last = S5_SCAN_ROWS - 1
    cr = jnp.broadcast_to(xr[last:last + 1, :], xr.shape)
    ci = jnp.broadcast_to(xi[last:last + 1, :], xi.shape)
    return xr, xi, cr, ci


def _s5_prompt_kernel(u_ref, x0_ref, atab_ref, b_ref, c_ref, d_ref, y_ref, st_ref, xs_ref, *, T):
    u = u_ref[0]
    pad = (-T) % 16

    def padded(x):
        return jnp.concatenate([x, jnp.zeros((pad, x.shape[1]), x.dtype)], axis=0) if pad else x

    xs_ref[...] = jnp.dot(padded(u).astype(BF16), b_ref[0], preferred_element_type=F32)[:T]
    n = S5_BLK_STATE
    R = S5_SCAN_ROWS

    def body(i, carry):
        cr, ci = carry
        rows = pl.ds(pl.multiple_of(i * R, R), R)
        xr, xi, cr, ci = _s5_scan_tile(xs_ref[rows, :n], xs_ref[rows, n:], cr, ci, atab_ref)
        xs_ref[rows, :n] = xr
        xs_ref[rows, n:] = xi
        return cr, ci

    x0 = x0_ref[0, 0]
    cr0 = jnp.broadcast_to(x0[0:1, :], (R, n))
    ci0 = jnp.broadcast_to(x0[1:2, :], (R, n))
    cr, ci = lax.fori_loop(0, T // R, body, (cr0, ci0))
    st_ref[0, 0] = jnp.concatenate([cr[0:1], ci[0:1]], axis=0)
    y = jnp.dot(padded(xs_ref[...]).astype(BF16), c_ref[0], preferred_element_type=F32)[:T] + d_ref[...] * u
    y_ref[0] = _gelu_tanh(y)


def s5_prompt(proj3, cols, x0, lam_re, lam_im, log_step, b_re, b_im, c_re, c_im, d):
    B, T, _ = proj3.shape
    assert T % S5_SCAN_ROWS == 0
    nb = S5_GROUPS // S5_BLK_GROUPS
    atab, bmat, cmat = _s5_params(lam_re, lam_im, log_step, b_re, b_im, c_re, c_im)
    x0b = x0.astype(F32).reshape(B, nb, S5_BLK_STATE, 2).transpose(0, 1, 3, 2)
    blk3 = lambda shape: pl.BlockSpec((1,) + shape, lambda b, j: (j, 0, 0))
    y, st = pl.pallas_call(
        functools.partial(_s5_prompt_kernel, T=T),
        grid=(B, nb),
        in_specs=[pl.BlockSpec((1, T, S5_BLK_CH), lambda b, j, o=cols['su'] // S5_BLK_CH: (b, 0, o + j)),
                  pl.BlockSpec((1, 1, 2, S5_BLK_STATE), lambda b, j: (b, j, 0, 0)),
                  pl.BlockSpec((1, 8, S5_SCAN_ROWS, S5_BLK_STATE), lambda b, j: (j, 0, 0, 0)),
                  blk3((S5_BLK_CH, 2 * S5_BLK_STATE)), blk3((2 * S5_BLK_STATE, S5_BLK_CH)),
                  pl.BlockSpec((1, S5_BLK_CH), lambda b, j: (0, j))],
        out_specs=(pl.BlockSpec((1, T, S5_BLK_CH), lambda b, j: (b, 0, j)),
                   pl.BlockSpec((1, 1, 2, S5_BLK_STATE), lambda b, j: (b, j, 0, 0))),
        out_shape=(jax.ShapeDtypeStruct((B, T, S5_WIDTH), F32),
                   jax.ShapeDtypeStruct((B, nb, 2, S5_BLK_STATE), F32)),
        scratch_shapes=[pltpu.VMEM((T, 2 * S5_BLK_STATE), F32)],
        compiler_params=pltpu.CompilerParams(dimension_semantics=("parallel", "parallel"),
                                             vmem_limit_bytes=V7X_VMEM_LIMIT_BYTES),
        name="s5_prompt",
    )(proj3, x0b, atab, bmat, cmat, d.astype(F32).reshape(1, S5_WIDTH))
    st = st.transpose(0, 1, 3, 2).reshape(B, S5_GROUPS, S5_STATE, 2)
    return y, st


def _branch_norm_kernel(ro_ref, so_ref, no_ref, bn_ref, o_ref):
    off = 0
    for ref in (ro_ref, so_ref, no_ref):
        x = ref[...]
        w = x.shape[-1]
        y = x * lax.rsqrt(jnp.mean(x * x, axis=-1, keepdims=True) + EPS) * bn_ref[:, off:off + w]
        o_ref[:, off:off + w] = y.astype(o_ref.dtype)
        off += w


def branch_norm(ro, so, no, bn):
    m = ro.shape[0]
    tm = min(m, 256)
    assert m % tm == 0
    spec = lambda w: pl.BlockSpec((tm, w), lambda i: (i, 0))
    return pl.pallas_call(
        _branch_norm_kernel,
        grid=(m // tm,),
        in_specs=[spec(RET_WIDTH), spec(S5_WIDTH), spec(NSA_WIDTH), pl.BlockSpec((1, D_MODEL), lambda i: (0, 0))],
        out_specs=spec(D_MODEL),
        out_shape=jax.ShapeDtypeStruct((m, D_MODEL), BF16),
        compiler_params=pltpu.CompilerParams(dimension_semantics=("parallel",),
                                             vmem_limit_bytes=V7X_VMEM_LIMIT_BYTES),
        name="branch_norm",
    )(ro, so, no, bn.astype(F32).reshape(1, D_MODEL))


def _largest_divisor(n, cap):
    return max(d for d in range(1, min(n, cap) + 1) if n % d == 0)


def _t5_bucket(dist):
    max_exact = N_BUCKETS // 2
    d = jnp.maximum(dist, 0)
    large = max_exact + (jnp.log(jnp.maximum(d, 1).astype(F32) / max_exact)
                         / math.log(MAX_DISTANCE / max_exact) * (N_BUCKETS - max_exact)).astype(jnp.int32)
    return jnp.where(d < max_exact, d, jnp.minimum(large, N_BUCKETS - 1))


def _rotary(x, pos):
    half = HEAD_DIM // 2
    inv = 1.0 / (10000.0 ** jnp.linspace(0.0, 1.0, half, dtype=F32))
    ang = pos.astype(F32)[:, None] * inv[None]
    cos, sin = jnp.cos(ang)[None, :, None, :], jnp.sin(ang)[None, :, None, :]
    x1, x2 = x[..., :half], x[..., half:]
    return jnp.concatenate([x1 * cos - x2 * sin, x1 * sin + x2 * cos], axis=-1)


def _retention(q, k, v, s0, q0):
    B, T = q.shape[:2]
    pos = q0 + jnp.arange(T)
    q = _rotary(q, pos)
    k = _rotary(k, pos) * HEAD_DIM ** -0.5
    lg = jnp.log1p(-(2.0 ** (-5.0 - jnp.arange(RET_HEADS, dtype=F32))))
    c = _largest_divisor(T, RET_CHUNK)
    n = T // c
    i = jnp.arange(c)
    rel = i[:, None] - i[None, :]
    decay = jnp.where(rel[None] >= 0, jnp.exp(jnp.maximum(rel, 0)[None] * lg[:, None, None]), 0.0)
    q_dec = jnp.exp((i + 1)[:, None] * lg[None])[None, :, :, None]
    k_dec = jnp.exp((c - 1 - i)[:, None] * lg[None])[None, :, :, None]
    chunk_dec = jnp.exp(c * lg)[None, :, None, None]

    def to_chunks(a):
        return a.reshape(B, n, c, RET_HEADS, HEAD_DIM).swapaxes(0, 1)

    def step(s, qkv):
        qc, kc, vc = qkv
        inner = jnp.einsum('bihd,bjhd->bhij', qc, kc) * decay
        o = (jnp.einsum('bhij,bjhv->bihv', inner, vc)
             + jnp.einsum('bihd,bhdv->bihv', qc, s) * q_dec)
        s = s * chunk_dec + jnp.einsum('bjhd,bjhv->bhdv', kc * k_dec, vc)
        return s, o

    s, o = lax.scan(step, s0.astype(F32), (to_chunks(q), to_chunks(k), to_chunks(v)))
    return o.swapaxes(0, 1).reshape(B, T, RET_HEADS, HEAD_DIM), s


def _s5_scan(u, x0, lam_re, lam_im, log_step, b_re, b_im, c_re, c_im, d):
    B, T = u.shape[:2]
    ug = u.reshape(B, T, S5_GROUPS, S5_GROUP).astype(jnp.complex64)
    lam = lax.complex(lam_re.astype(F32), lam_im.astype(F32))
    a_bar = jnp.exp(lam * jnp.exp(log_step.astype(F32))[:, None])
    b_bar = ((a_bar - 1.0) / lam)[..., None] * lax.complex(b_re.astype(F32), b_im.astype(F32))
    bu = jnp.einsum('gpc,btgc->btgp', b_bar, ug)
    init = lax.complex(x0[..., 0].astype(F32), x0[..., 1].astype(F32))
    bu = bu.at[:, 0].add(a_bar[None] * init)
    a = jnp.broadcast_to(a_bar, bu.shape)

    def combine(e1, e2):
        a1, b1 = e1
        a2, b2 = e2
        return a1 * a2, a2 * b1 + b2

    _, xs = lax.associative_scan(combine, (a, bu), axis=1)
    cm = lax.complex(c_re.astype(F32), c_im.astype(F32))
    y = jnp.einsum('gcp,btgp->btgc', cm, xs).real.reshape(B, T, S5_WIDTH) + d.astype(F32) * u
    x_last = xs[:, -1]
    return y, jnp.stack([x_last.real, x_last.imag], axis=-1)


def _compress(x, pe, w1, w2):
    B, L = x.shape[:2]
    r = CMP_BLOCK // CMP_STRIDE
    n_full = L // CMP_STRIDE
    n_cmp = n_full - r + 1
    xs = x[:, :n_full * CMP_STRIDE].reshape(B, n_full, CMP_STRIDE, NSA_KV_HEADS, HEAD_DIM)
    w1 = w1.astype(F32)
    w1r = w1.reshape(r, CMP_STRIDE, HEAD_DIM, HEAD_DIM)
    h = jnp.einsum('ld,ldh->h', pe.astype(F32), w1)
    for j in range(r):
        h = h + jnp.einsum('bnsgd,sdh->bngh', xs[:, j:j + n_cmp], w1r[j])
    return jnp.einsum('bngh,he->bnge', jax.nn.gelu(h), w2.astype(F32))


def _nsa(q, k_cmp, v_cmp, k_slc, v_slc, k_win, v_win, gates, q0, rel_bias,
         pe_k, w1_k, w2_k, pe_v, w1_v, w2_v):
    B, T = q.shape[:2]
    G, J = NSA_KV_HEADS, NSA_HPG
    Lk = q0 + T
    qg = (q.astype(F32) * HEAD_DIM ** -0.5).reshape(B, T, G, J, HEAD_DIM)
    t_pos = q0 + jnp.arange(T)
    table = rel_bias.astype(F32).reshape(N_BUCKETS, G, J)

    kc = _compress(k_cmp.astype(F32), pe_k, w1_k, w2_k)
    vc = _compress(v_cmp.astype(F32), pe_v, w1_v, w2_v)
    n_cmp = kc.shape[1]
    c_start = jnp.arange(n_cmp) * CMP_STRIDE
    dist = t_pos[:, None] - (c_start + CMP_BLOCK - 1)[None]
    valid = (dist >= 0)[:, None, None, :]
    bias = jnp.transpose(table[_t5_bucket(dist)], (0, 2, 3, 1))
    s = jnp.einsum('btgjd,bngd->btgjn', qg, kc) + bias
    p_cmp = jax.nn.softmax(jnp.where(valid, s, NEG), axis=-1) * valid
    o_cmp = jnp.einsum('btgjn,bngd->btgjd', p_cmp, vc)

    n_sel = -(-Lk // SEL_BLOCK)
    s_start = jnp.arange(n_sel) * SEL_BLOCK
    overlap = ((c_start[:, None] < s_start[None] + SEL_BLOCK)
               & (c_start[:, None] + CMP_BLOCK > s_start[None])).astype(F32)
    p_sel = jnp.einsum('btgn,nm->btgm', p_cmp.sum(axis=3), overlap)
    cur = t_pos // SEL_BLOCK
    blk = jnp.arange(n_sel)
    forced = (blk[None] == 0) | (blk[None] == cur[:, None]) | (blk[None] == cur[:, None] - 1)
    blk_ok = (blk[None] <= cur[:, None])[None, :, None, :]
    score = jnp.where(blk_ok, p_sel + FORCE_SCORE * forced[None, :, None, :], NEG)
    n_top = min(SEL_TOP, n_sel)
    _, idx = lax.top_k(score, n_top)

    pad = n_sel * SEL_BLOCK - Lk

    def blocks(a):
        a = jnp.pad(a.astype(F32), ((0, 0), (0, pad), (0, 0), (0, 0)))
        return a.reshape(B, n_sel, SEL_BLOCK, G, HEAD_DIM).transpose(0, 3, 1, 2, 4)

    kb, vb = blocks(k_slc), blocks(v_slc)
    qb = _largest_divisor(T, SEL_Q_BLOCK)
    nq = T // qb
    b_ix = jnp.arange(B)[:, None, None, None]
    g_ix = jnp.arange(G)[None, None, :, None]
    g_ix5 = g_ix[..., None]
    off = jnp.arange(SEL_BLOCK)

    def sel_block(args):
        qc, ic, tc = args
        kg = kb[b_ix, g_ix, ic]
        vg = vb[b_ix, g_ix, ic]
        kpos = ic[..., None] * SEL_BLOCK + off
        d = tc[None, :, None, None, None] - kpos
        ok = (d >= 0) & (ic <= (tc // SEL_BLOCK)[None, :, None, None])[..., None]
        bias_s = jnp.moveaxis(table[_t5_bucket(d), g_ix5], -1, 3)
        sc = jnp.einsum('bqgjd,bqgkld->bqgjkl', qc, kg) + bias_s
        sc = jnp.where(ok[:, :, :, None], sc, NEG).reshape(B, qb, G, J, n_top * SEL_BLOCK)
        pr = jax.nn.softmax(sc, axis=-1).reshape(B, qb, G, J, n_top, SEL_BLOCK)
        return jnp.einsum('bqgjkl,bqgkld->bqgjd', pr, vg)

    o_slc = lax.map(sel_block, (qg.reshape(B, nq, qb, G, J, HEAD_DIM).swapaxes(0, 1),
                                idx.reshape(B, nq, qb, G, n_top).swapaxes(0, 1),
                                t_pos.reshape(nq, qb)))
    o_slc = o_slc.swapaxes(0, 1).reshape(B, T, G, J, HEAD_DIM)

    qw = _largest_divisor(T, WIN_Q_BLOCK)
    nw = T // qw
    rows = jnp.arange(nw)[:, None] * qw + jnp.arange(qw + WINDOW)[None]
    kwin = k_win.astype(F32)[:, rows]
    vwin = v_win.astype(F32)[:, rows]
    kpos = q0 - WINDOW + rows
    dw = t_pos.reshape(nw, qw)[:, :, None] - kpos[:, None, :]
    okw = (kpos[:, None, :] >= 0) & (dw >= 0) & (dw <= WINDOW)
    bias_w = jnp.transpose(table[_t5_bucket(dw)], (0, 3, 4, 1, 2))
    sw = jnp.einsum('bcqgjd,bckgd->bcgjqk', qg.reshape(B, nw, qw, G, J, HEAD_DIM), kwin) + bias_w
    pw = jax.nn.softmax(jnp.where(okw[:, None, None], sw, NEG), axis=-1)
    o_win = jnp.einsum('bcgjqk,bckgd->bcqgjd', pw, vwin).reshape(B, T, G, J, HEAD_DIM)

    g = jax.nn.sigmoid(gates.astype(F32)).reshape(B, T, G, J, 3)
    o = g[..., 0:1] * o_cmp + g[..., 1:2] * o_slc + g[..., 2:3] * o_win
    return o.reshape(B, T, NSA_WIDTH)


def _jnp_rms(x, gain):
    xf = x.astype(F32)
    xf = xf * lax.rsqrt(jnp.mean(xf * xf, axis=-1, keepdims=True) + EPS)
    return xf * gain.astype(F32)


def _block(x, q0, win_buf, ret_s0, s5_s0, win_k0, win_v0, past_cmp_k, past_cmp_v,
           past_slc_k, past_slc_v, lw, rel_bias):
    B, T, _ = x.shape
    G = NSA_KV_HEADS
    M = B * T
    x2 = x.reshape(M, D_MODEL)
    h = rmsnorm(x2, lw['norm_mix'], BF16)
    proj = matmul(h, lw['w_in'])[:, :IN_COLS].reshape(B, T, IN_COLS)
    (rq, rk, rv, rg, su, nq_, kc, vc, ks, vs, kw, vw, ng) = jnp.split(
        proj, np.cumsum(IN_SPLITS)[:-1].tolist(), axis=-1)

    def heads(a, n):
        return a.reshape(B, T, n, HEAD_DIM)

    ro, ret_s = _retention(heads(rq, RET_HEADS), heads(rk, RET_HEADS), heads(rv, RET_HEADS), ret_s0, q0)
    ro = ro * lax.rsqrt(jnp.mean(ro * ro, axis=-1, keepdims=True) + EPS)
    ro = jax.nn.silu(rg) * ro.reshape(B, T, RET_WIDTH)

    sy, s5_s = _s5_scan(su, s5_s0, lw['s5_lambda_re'], lw['s5_lambda_im'], lw['s5_log_step'],
                        lw['s5_b_re'], lw['s5_b_im'], lw['s5_c_re'], lw['s5_c_im'], lw['s5_d'])
    sy = jax.nn.gelu(sy)
    so = sy * jax.nn.sigmoid(jnp.einsum('btc,ce->bte', sy, lw['s5_w_glu']))

    kc, vc, ks, vs, kw, vw = (heads(a, G) for a in (kc, vc, ks, vs, kw, vw))
    k_cmp_all = jnp.concatenate([past_cmp_k, kc], axis=1)
    v_cmp_all = jnp.concatenate([past_cmp_v, vc], axis=1)
    k_slc_all = jnp.concatenate([past_slc_k, ks], axis=1)
    v_slc_all = jnp.concatenate([past_slc_v, vs], axis=1)
    k_win_all = jnp.concatenate([win_k0, kw], axis=1)
    v_win_all = jnp.concatenate([win_v0, vw], axis=1)
    no = _nsa(heads(nq_, NSA_HEADS), k_cmp_all, v_cmp_all, k_slc_all, v_slc_all, k_win_all, v_win_all,
              ng.reshape(B, T, NSA_HEADS, 3), q0, rel_bias,
              lw['cmp_pe_k'], lw['cmp_w1_k'], lw['cmp_w2_k'], lw['cmp_pe_v'], lw['cmp_w1_v'], lw['cmp_w2_v'])

    bn = lw['branch_norm']
    mix = jnp.concatenate([_jnp_rms(ro, bn[:RET_WIDTH]),
                           _jnp_rms(so, bn[RET_WIDTH:RET_WIDTH + S5_WIDTH]),
                           _jnp_rms(no, bn[RET_WIDTH + S5_WIDTH:])], axis=-1).astype(BF16)
    x2 = matmul(mix.reshape(M, D_MODEL), lw['w_out'], res=x2)

    h = rmsnorm(x2, lw['norm_ffn'], BF16)
    up = matmul(h, lw['w_up'], act="relu2", out_dtype=BF16)
    x2 = matmul(up, lw['w_down'], res=x2)

    cmp_rows = jnp.stack([kc, vc], axis=1)
    slc_rows = jnp.stack([ks, vs], axis=1)
    win_new = jnp.stack([k_win_all[:, -win_buf:], v_win_all[:, -win_buf:]], axis=1)
    return x2.reshape(B, T, D_MODEL), cmp_rows, slc_rows, win_new, ret_s, s5_s


_COL_NAMES = ('rq', 'rk', 'rv', 'rg', 'su', 'nq', 'kc', 'vc', 'ks', 'vs', 'kw', 'vw', 'ng')
COLS = {name: int(off) for name, off in zip(_COL_NAMES, np.concatenate([[0], np.cumsum(IN_SPLITS)]))}


def _block_prompt(x, win_buf, lw, rel_bias):
    B, T, _ = x.shape
    G = NSA_KV_HEADS
    M = B * T
    assert T >= win_buf
    x2 = x.reshape(M, D_MODEL)
    h = rmsnorm(x2, lw['norm_mix'], BF16)
    proj3 = matmul(h, lw['w_in']).reshape(B, T, -1)

    ro, ret_s = retention(proj3, COLS, jnp.zeros((B, RET_HEADS, HEAD_DIM, HEAD_DIM), F32), 0)
    zero_s5 = jnp.zeros((B, S5_GROUPS, S5_STATE, 2), F32)
    sy, s5_s = s5_prompt(proj3, COLS, zero_s5, lw['s5_lambda_re'], lw['s5_lambda_im'], lw['s5_log_step'],
                         lw['s5_b_re'], lw['s5_b_im'], lw['s5_c_re'], lw['s5_c_im'], lw['s5_d'])
    sy2 = sy.reshape(M, S5_WIDTH)
    so = matmul(sy2, lw['s5_w_glu'], res=sy2, act="glu")
    kcmp, vcmp = compress_prompt(proj3, COLS['kc'], COLS['vc'], lw['cmp_pe_k'], lw['cmp_w1_k'], lw['cmp_w2_k'],
                                 lw['cmp_pe_v'], lw['cmp_w1_v'], lw['cmp_w2_v'])
    no = nsa_prompt2(proj3, COLS, kcmp, vcmp, rel_bias)

    mix = branch_norm(ro.reshape(M, RET_WIDTH), so, no.reshape(M, NSA_WIDTH), lw['branch_norm'])
    x2 = matmul(mix, lw['w_out'], res=x2)
    h = rmsnorm(x2, lw['norm_ffn'], BF16)
    up = matmul(h, lw['w_up'], act="relu2", out_dtype=BF16)
    x2 = matmul(up, lw['w_down'], res=x2)

    rows = lambda name: proj3[:, :, COLS[name]:COLS[name] + KV_WIDTH].reshape(B, T, G, HEAD_DIM)
    cmp_rows = jnp.stack([rows('kc'), rows('vc')], axis=1)
    slc_rows = jnp.stack([rows('ks'), rows('vs')], axis=1)
    win_new = jnp.stack([rows('kw')[:, T - win_buf:], rows('vw')[:, T - win_buf:]], axis=1)
    return x2.reshape(B, T, D_MODEL), cmp_rows, slc_rows, win_new, ret_s, s5_s


def _mixer_tail(x2, ro, so_in, no, lw):
    so = matmul(so_in, lw['s5_w_glu'], res=so_in, act="glu")
    mix = branch_norm(ro, so, no, lw['branch_norm'])
    x2 = matmul(mix, lw['w_out'], res=x2)
    h = rmsnorm(x2, lw['norm_ffn'], BF16)
    up = matmul(h, lw['w_up'], act="relu2", out_dtype=BF16)
    return matmul(up, lw['w_down'], res=x2)


def _block_sample(x, layer, cache_cmp, cache_slc, cache_win, state_ret, state_s5, page_table, lw, rel_bias):
    B, T, _ = x.shape
    G = NSA_KV_HEADS
    M = B * T
    past_len = page_table.shape[1] * PAGE_SIZE
    win_buf = cache_win.shape[3]
    x2 = x.reshape(M, D_MODEL)
    h = rmsnorm(x2, lw['norm_mix'], BF16)
    proj3 = matmul(h, lw['w_in']).reshape(B, T, -1)

    ro, ret_s = retention(proj3, COLS, state_ret[layer], past_len)
    sy, s5_s = s5_prompt(proj3, COLS, state_s5[layer], lw['s5_lambda_re'], lw['s5_lambda_im'], lw['s5_log_step'],
                         lw['s5_b_re'], lw['s5_b_im'], lw['s5_c_re'], lw['s5_c_im'], lw['s5_d'])
    no = nsa_sample(proj3, COLS, cache_cmp, cache_slc, cache_win, page_table, layer, rel_bias,
                    lw['cmp_pe_k'], lw['cmp_w1_k'], lw['cmp_w2_k'], lw['cmp_pe_v'], lw['cmp_w1_v'], lw['cmp_w2_v'])
    x2 = _mixer_tail(x2, ro.reshape(M, RET_WIDTH), sy.reshape(M, S5_WIDTH), no.reshape(M, NSA_WIDTH), lw)

    rows = lambda name: proj3[:, :, COLS[name]:COLS[name] + KV_WIDTH].reshape(B, T, G, HEAD_DIM)
    cmp_rows = jnp.stack([rows('kc'), rows('vc')], axis=1)
    slc_rows = jnp.stack([rows('ks'), rows('vs')], axis=1)
    win_new = jnp.concatenate([cache_win[layer], jnp.stack([rows('kw'), rows('vw')], axis=1)],
                              axis=2)[:, :, -win_buf:]
    return x2.reshape(B, T, D_MODEL), cmp_rows, slc_rows, win_new, ret_s, s5_s


def _gather_pages(pool, page_table, layer):
    pages = pool[page_table, layer]
    db, npg = page_table.shape
    pages = pages.transpose(0, 2, 1, 3, 4, 5).reshape(db, 2, npg * PAGE_SIZE, NSA_KV_HEADS, HEAD_DIM)
    return pages[:, 0], pages[:, 1]


def kernel(x_prompt, x_sample, cache_cmp, cache_slc, cache_win, state_ret, state_s5, page_table,
           rel_bias, norm_mix, w_in, s5_lambda_re, s5_lambda_im, s5_log_step, s5_b_re, s5_b_im,
           s5_c_re, s5_c_im, s5_d, s5_w_glu, cmp_pe_k, cmp_w1_k, cmp_w2_k, cmp_pe_v, cmp_w1_v,
           cmp_w2_v, branch_norm, w_out, norm_ffn, w_up, w_down, norm_final):
    past_len = page_table.shape[1] * PAGE_SIZE
    win_buf = cache_win.shape[3]
    bp = x_prompt.shape[0]
    G, HD = NSA_KV_HEADS, HEAD_DIM
    xp, xs = x_prompt, x_sample
    empty = jnp.zeros((bp, 0, G, HD), x_prompt.dtype)
    zero_win = jnp.zeros((bp, WINDOW, G, HD), x_prompt.dtype)
    zero_ret = jnp.zeros((bp, RET_HEADS, HD, HD), F32)
    zero_s5 = jnp.zeros((bp, S5_GROUPS, S5_STATE, 2), F32)
    in_pad = _round_up(IN_COLS, 512) - IN_COLS
    cmp_p, cmp_s, slc_p, slc_s, win_p, win_s, ret_p, ret_s, s5_p, s5_s = ([] for _ in range(10))
    for l in range(DEPTH):
        lw = {
            'norm_mix': norm_mix[l],
            'w_in': jnp.pad(w_in[l].astype(BF16), ((0, 0), (0, in_pad))),
            's5_lambda_re': s5_lambda_re[l], 's5_lambda_im': s5_lambda_im[l], 's5_log_step': s5_log_step[l],
            's5_b_re': s5_b_re[l], 's5_b_im': s5_b_im[l], 's5_c_re': s5_c_re[l], 's5_c_im': s5_c_im[l],
            's5_d': s5_d[l], 's5_w_glu': s5_w_glu[l],
            'cmp_pe_k': cmp_pe_k[l], 'cmp_w1_k': cmp_w1_k[l], 'cmp_w2_k': cmp_w2_k[l],
            'cmp_pe_v': cmp_pe_v[l], 'cmp_w1_v': cmp_w1_v[l], 'cmp_w2_v': cmp_w2_v[l],
            'branch_norm': branch_norm[l], 'w_out': w_out[l].astype(BF16),
            'norm_ffn': norm_ffn[l], 'w_up': w_up[l].astype(BF16), 'w_down': w_down[l].astype(BF16),
        }
        xp, c_rows, s_rows, w_new, r_new, s_new = _block_prompt(xp, win_buf, lw, rel_bias)
        cmp_p.append(c_rows); slc_p.append(s_rows); win_p.append(w_new); ret_p.append(r_new); s5_p.append(s_new)
        xs, c_rows, s_rows, w_new, r_new, s_new = _block_sample(
            xs, l, cache_cmp, cache_slc, cache_win, state_ret, state_s5, page_table, lw, rel_bias)
        cmp_s.append(c_rows); slc_s.append(s_rows); win_s.append(w_new); ret_s.append(r_new); s5_s.append(s_new)
    y_prompt = rmsnorm(xp.reshape(-1, D_MODEL), norm_final, F32).reshape(xp.shape)
    y_sample = rmsnorm(xs.reshape(-1, D_MODEL), norm_final, F32).reshape(xs.shape)
    return (y_prompt, y_sample,
            jnp.stack(cmp_p, axis=1), jnp.stack(cmp_s, axis=1),
            jnp.stack(slc_p, axis=1), jnp.stack(slc_s, axis=1),
            jnp.stack(win_p, axis=0), jnp.stack(win_s, axis=0),
            jnp.stack(ret_p, axis=0), jnp.stack(ret_s, axis=0),
            jnp.stack(s5_p, axis=0), jnp.stack(s5_s, axis=0))
```

```python
import functools
import math

import jax
import jax.numpy as jnp
import numpy as np
from jax import lax
from jax.experimental import pallas as pl
from jax.experimental.pallas import tpu as pltpu

F32 = jnp.float32
BF16 = jnp.bfloat16

D_MODEL = 4096
DEPTH = 2
PAGE_SIZE = 128
HEAD_DIM = 128
RET_WIDTH = 1024
RET_HEADS = 8
RET_CHUNK = 128
S5_WIDTH = 1024
S5_GROUP = 16
S5_GROUPS = 64
S5_STATE = 64
NSA_WIDTH = 2048
NSA_HEADS = 16
NSA_KV_HEADS = 4
NSA_HPG = 4
KV_WIDTH = 512
CMP_BLOCK = 32
CMP_STRIDE = 16
SEL_BLOCK = 64
SEL_TOP = 16
WINDOW = 512
FORCE_SCORE = 1e4
NEG = -1e30
N_BUCKETS = 32
MAX_DISTANCE = 128
EPS = 1e-6
IN_SPLITS = (RET_WIDTH, RET_WIDTH, RET_WIDTH, RET_WIDTH, S5_WIDTH, NSA_WIDTH,
             KV_WIDTH, KV_WIDTH, KV_WIDTH, KV_WIDTH, KV_WIDTH, KV_WIDTH, 3 * NSA_HEADS)
IN_COLS = sum(IN_SPLITS)

V7X_VMEM_LIMIT_BYTES = 48 * 1024 * 1024
LANE = 128
PACK_ROWS = 16


def _round_up(n, m):
    return -(-n // m) * m


def _largest_divisor(n, cap):
    return max(d for d in range(1, min(n, cap) + 1) if n % d == 0)


def _pad_rows(x, rows):
    extra = rows - x.shape[0]
    return jnp.concatenate([x, jnp.zeros((extra, x.shape[1]), x.dtype)], axis=0) if extra else x


def _gelu_tanh(x):
    return 0.5 * x * (1.0 + jnp.tanh(math.sqrt(2.0 / math.pi) * (x + 0.044715 * (x * x * x))))


def _dot_nt(a, b):
    return lax.dot_general(a, b, (((1,), (1,)), ((), ())), preferred_element_type=F32)


def _dot_tn(a, b):
    return lax.dot_general(a, b, (((0,), (0,)), ((), ())), preferred_element_type=F32)


def _rmsnorm_kernel(x_ref, g_ref, o_ref):
    x = x_ref[...].astype(F32)
    ms = jnp.mean(x * x, axis=-1, keepdims=True)
    o_ref[...] = (x * lax.rsqrt(ms + EPS) * g_ref[...].astype(F32)).astype(o_ref.dtype)


def rmsnorm(x2d, gain, out_dtype):
    m, d = x2d.shape
    tm = min(m, 256)
    assert m % tm == 0
    return pl.pallas_call(
        _rmsnorm_kernel,
        grid=(m // tm,),
        in_specs=[pl.BlockSpec((tm, d), lambda i: (i, 0)),
                  pl.BlockSpec((1, d), lambda i: (0, 0))],
        out_specs=pl.BlockSpec((tm, d), lambda i: (i, 0)),
        out_shape=jax.ShapeDtypeStruct((m, d), out_dtype),
        compiler_params=pltpu.CompilerParams(dimension_semantics=("parallel",),
                                             vmem_limit_bytes=V7X_VMEM_LIMIT_BYTES),
        name="rmsnorm",
    )(x2d, gain.reshape(1, d))


def _mm_kernel(*refs, nk, act, has_res):
    if has_res:
        a_ref, w_ref, r_ref, o_ref, acc_ref = refs
    else:
        a_ref, w_ref, o_ref, acc_ref = refs
        r_ref = None
    k = pl.program_id(2)

    @pl.when(k == 0)
    def _():
        acc_ref[...] = jnp.zeros_like(acc_ref)

    acc_ref[...] += jnp.dot(a_ref[...].astype(BF16), w_ref[...], preferred_element_type=F32)

    @pl.when(k == nk - 1)
    def _():
        acc = acc_ref[...]
        if act == "relu2":
            acc = jnp.square(jnp.maximum(acc, 0.0))
        if act == "glu":
            acc = r_ref[...].astype(F32) * jax.nn.sigmoid(acc)
        elif has_res:
            acc = acc + r_ref[...].astype(F32)
        o_ref[...] = acc.astype(o_ref.dtype)


def matmul(a, w, layer, *, res=None, act=None, out_dtype=F32):
    m, kdim = a.shape
    n = w.shape[2]
    tm = min(m, 1024)
    tn = 512 if n % 512 == 0 else (256 if n % 256 == 0 else 128)
    tk = min(kdim, 4096)
    assert m % tm == 0 and n % tn == 0 and kdim % tk == 0
    nk = kdim // tk
    in_specs = [pl.BlockSpec((tm, tk), lambda i, j, k: (i, k)),
                pl.BlockSpec((None, tk, tn), lambda i, j, k: (layer, k, j))]
    args = [a, w]
    if res is not None:
        in_specs.append(pl.BlockSpec((tm, tn), lambda i, j, k: (i, j)))
        args.append(res)
    return pl.pallas_call(
        functools.partial(_mm_kernel, nk=nk, act=act, has_res=res is not None),
        grid=(m // tm, n // tn, nk),
        in_specs=in_specs,
        out_specs=pl.BlockSpec((tm, tn), lambda i, j, k: (i, j)),
        out_shape=jax.ShapeDtypeStruct((m, n), out_dtype),
        scratch_shapes=[pltpu.VMEM((tm, tn), F32)],
        compiler_params=pltpu.CompilerParams(
            dimension_semantics=("parallel", "parallel", "arbitrary"),
            vmem_limit_bytes=V7X_VMEM_LIMIT_BYTES),
        name="matmul",
    )(*args)


def _compress_rows(x_ref, n_full, pe_ref, w1_ref, w2_ref):
    pieces = [x_ref[0, pl.ds(s, n_full, stride=CMP_STRIDE), :] for s in range(CMP_STRIDE)]
    x = jnp.concatenate(pieces, axis=1).astype(BF16)
    half = CMP_STRIDE * HEAD_DIM
    z0 = jnp.dot(x, w1_ref[:half, :], preferred_element_type=F32)
    z1 = jnp.dot(x, w1_ref[half:, :], preferred_element_type=F32)
    z1 = pltpu.roll(z1, n_full - 1, 0)
    pe = jnp.broadcast_to(pe_ref[...], (8, CMP_BLOCK * HEAD_DIM)).astype(BF16)
    h0 = jnp.dot(pe, w1_ref[...], preferred_element_type=F32)[0:1, :]
    h = _gelu_tanh(z0 + z1 + h0)
    return jnp.dot(h.astype(BF16), w2_ref[...], preferred_element_type=F32)


def _compress_prompt_kernel(xk_ref, xv_ref, pek_ref, w1k_ref, w2k_ref, pev_ref, w1v_ref, w2v_ref,
                            kc_ref, vc_ref, *, n_full):
    for x_ref, pe_ref, w1_ref, w2_ref, o_ref in ((xk_ref, pek_ref, w1k_ref, w2k_ref, kc_ref),
                                                  (xv_ref, pev_ref, w1v_ref, w2v_ref, vc_ref)):
        out = _compress_rows(x_ref, n_full, pe_ref, w1_ref, w2_ref).astype(o_ref.dtype)
        n_pad = o_ref.shape[2]
        o_ref[0, 0, :n_full, :] = out
        if n_pad > n_full:
            o_ref[0, 0, n_full:, :] = jnp.zeros((n_pad - n_full, HEAD_DIM), o_ref.dtype)


def compress_prompt(proj3, col_k, col_v, pe_k, w1_k, w2_k, pe_v, w1_v, w2_v):
    B, T, _ = proj3.shape
    n_full = T // CMP_STRIDE
    n_pad = _round_up(n_full, LANE)
    flat = CMP_BLOCK * HEAD_DIM
    wspec = [pl.BlockSpec((1, flat), lambda b, g: (0, 0)),
             pl.BlockSpec((flat, HEAD_DIM), lambda b, g: (0, 0)),
             pl.BlockSpec((HEAD_DIM, HEAD_DIM), lambda b, g: (0, 0))]
    out_sds = jax.ShapeDtypeStruct((B, NSA_KV_HEADS, n_pad, HEAD_DIM), BF16)
    ospec = pl.BlockSpec((1, 1, n_pad, HEAD_DIM), lambda b, g: (b, g, 0, 0))
    return pl.pallas_call(
        functools.partial(_compress_prompt_kernel, n_full=n_full),
        grid=(B, NSA_KV_HEADS),
        in_specs=[pl.BlockSpec((1, T, HEAD_DIM), lambda b, g: (b, 0, col_k // HEAD_DIM + g)),
                  pl.BlockSpec((1, T, HEAD_DIM), lambda b, g: (b, 0, col_v // HEAD_DIM + g))] + wspec + wspec,
        out_specs=(ospec, ospec),
        out_shape=(out_sds, out_sds),
        compiler_params=pltpu.CompilerParams(dimension_semantics=("parallel", "parallel"),
                                             vmem_limit_bytes=V7X_VMEM_LIMIT_BYTES),
        name="nsa_compress_prompt",
    )(proj3, proj3,
      pe_k.reshape(1, flat), w1_k.reshape(flat, HEAD_DIM).astype(BF16), w2_k.astype(BF16),
      pe_v.reshape(1, flat), w1_v.reshape(flat, HEAD_DIM).astype(BF16), w2_v.astype(BF16))


def _bucket_table():
    d = np.arange(MAX_DISTANCE)
    max_exact = N_BUCKETS // 2
    large = max_exact + (np.log(np.maximum(d, 1).astype(np.float32) / np.float32(max_exact))
                         / np.float32(math.log(MAX_DISTANCE / max_exact))
                         * np.float32(N_BUCKETS - max_exact)).astype(np.int32)
    return np.where(d < max_exact, d, np.minimum(large, N_BUCKETS - 1)).astype(np.int32)


def _bias_of_distance(rel_bias, dist):
    bt = _bucket_table()
    buckets = bt[np.clip(dist, 0, MAX_DISTANCE - 1)]
    b = jnp.moveaxis(rel_bias.astype(F32)[buckets], -1, 0)
    return jnp.where(jnp.asarray(dist >= 0)[None], b, NEG)


def _bias_by_distance(rel_bias):
    return rel_bias.astype(F32)[_bucket_table()].T


def _toeplitz(v, n):
    h = v.shape[0]
    w = jnp.pad(v, ((0, 0), (0, 1)))
    m = jnp.tile(w, (1, n))[:, :n * (2 * n - 1)].reshape(h, n, 2 * n - 1)
    return m[:, :, n - 1:]


ATT_TILE = 128
N_BIAS_TILES = 4
SLC_CLASS_TILES = 4
MXU_TILES = 2


def _rows_softmax_pv(s_tiles, v_rows):
    m = s_tiles[0]
    for s in s_tiles[1:]:
        m = jnp.maximum(m, s)
    m = jnp.max(m, axis=-1, keepdims=True)
    l = None
    o = None
    for i in range(0, len(s_tiles), MXU_TILES):
        ps = [jnp.exp(s - m) for s in s_tiles[i:i + MXU_TILES]]
        for p in ps:
            l = p if l is None else l + p
        p_cat = ps[0] if len(ps) == 1 else jnp.concatenate(ps, axis=1)
        pv = jnp.dot(p_cat.astype(BF16), v_rows(i, len(ps)), preferred_element_type=F32)
        o = pv if o is None else o + pv
    return o / jnp.sum(l, axis=-1, keepdims=True)


def _score_tiles(q, k_rows, n_tiles):
    tiles = []
    for i in range(0, n_tiles, MXU_TILES):
        n = min(MXU_TILES, n_tiles - i)
        s = _dot_nt(q, k_rows(i, n))
        tiles += [s[:, j * LANE:(j + 1) * LANE] for j in range(n)]
    return tiles


def _nsa_prompt_kernel(q_ref, kc_ref, vc_ref, ks_ref, vs_ref, kw_ref, vw_ref, gate_ref,
                       bcmp_ref, btile_ref, ovl_ref, pick_ref, cvec_ref, kaug_s_ref, kaug_w_ref, o_ref,
                       ksb, vsb, kwb, vwb, s_ref, *, n_sel, n_top, T):
    tq = ATT_TILE
    J = NSA_HPG
    qi = pl.program_id(2)
    n_sel_pad = pick_ref.shape[0]
    nq = T // tq

    @pl.when(qi == 0)
    def _():
        ksb[:, :HEAD_DIM] = ks_ref[0].astype(BF16)
        ksb[:, HEAD_DIM:] = kaug_s_ref[...]
        vsb[...] = vs_ref[0].astype(BF16)
        kwb[:WINDOW, :HEAD_DIM] = jnp.zeros((WINDOW, HEAD_DIM), BF16)
        kwb[WINDOW:, :HEAD_DIM] = kw_ref[0].astype(BF16)
        kwb[:, HEAD_DIM:] = kaug_w_ref[...]
        vwb[:WINDOW, :] = jnp.zeros((WINDOW, HEAD_DIM), BF16)
        vwb[WINDOW:, :] = vw_ref[0].astype(BF16)

    def stack(f):
        return jnp.concatenate([f(j) for j in range(J)], axis=0)

    def put(vals, first):
        for j in range(J):
            cols = slice(j * HEAD_DIM, (j + 1) * HEAD_DIM)
            v = vals[j * tq:(j + 1) * tq]
            o_ref[0, :, cols] = v if first else o_ref[0, :, cols] + v

    gates = jax.nn.sigmoid(gate_ref[0, 0])
    gate = lambda c: stack(lambda j: gates[:, 3 * j + c:3 * j + c + 1])
    q_all = stack(lambda j: q_ref[0, :, j * HEAD_DIM:(j + 1) * HEAD_DIM] * HEAD_DIM ** -0.5).astype(BF16)

    row = lax.broadcasted_iota(jnp.int32, (tq, LANE), 0)
    lane = lax.broadcasted_iota(jnp.int32, (tq, LANE), 1)
    valid1 = qi * tq + row >= lane * CMP_STRIDE + (CMP_BLOCK - 1)
    valid = stack(lambda j: valid1)
    s = jnp.where(valid, _dot_nt(q_all, kc_ref[0, 0]) + stack(lambda j: bcmp_ref[j]), NEG)
    e = jnp.where(valid, jnp.exp(s - jnp.max(s, axis=-1, keepdims=True)), 0.0)
    den = jnp.sum(e, axis=-1, keepdims=True)
    p = e / jnp.where(den > 0.0, den, 1.0)
    put(gate(0) * jnp.dot(p.astype(BF16), vc_ref[0, 0], preferred_element_type=F32), True)
    psum = p[0:tq]
    for j in range(1, J):
        psum = psum + p[j * tq:(j + 1) * tq]

    bt = lambda i: stack(lambda j: btile_ref[j, i])

    pad_flag = jnp.broadcast_to(-cvec_ref[...], (tq, LANE)).astype(BF16)
    q_win = jnp.concatenate([q_all, stack(lambda j: pad_flag)], axis=1)
    n_w = WINDOW // tq + 1
    win_rows = lambda i, n: pl.ds(pl.multiple_of((qi + i) * tq, tq), n * tq)
    s_tiles = _score_tiles(q_win, lambda i, n: kwb[win_rows(i, n), :], n_w)
    s_tiles[0] = s_tiles[0] + bt(3)
    s_tiles[n_w - 2] = s_tiles[n_w - 2] + bt(1)
    s_tiles[n_w - 1] = s_tiles[n_w - 1] + bt(0)
    put(gate(2) * _rows_softmax_pv(s_tiles, lambda i, n: vwb[win_rows(i, n), :]), False)

    hi = psum.astype(BF16)
    r1 = psum - hi.astype(F32)
    mid = r1.astype(BF16)
    lo = (r1 - mid.astype(F32)).astype(BF16)
    ovl = ovl_ref[...]
    p_sel = _dot_nt(ovl, hi) + _dot_nt(ovl, mid) + _dot_nt(ovl, lo)
    blk = lax.broadcasted_iota(jnp.int32, (n_sel_pad, tq), 0)
    cur = (qi * tq + lax.broadcasted_iota(jnp.int32, (n_sel_pad, tq), 1)) // SEL_BLOCK
    forced = (blk == 0) | (blk == cur) | (blk == cur - 1)
    score = jnp.where(blk <= cur, p_sel + jnp.where(forced, FORCE_SCORE, 0.0), NEG)
    rank = jnp.zeros((n_sel_pad, tq), F32)
    for jb in range(n_sel):
        other = score[jb:jb + 1, :]
        beats = (other > score) | ((other == score) & (blk > jb))
        rank = rank + jnp.where(beats, 1.0, 0.0)
    sel = jnp.where((rank < n_top) & (blk <= cur) & (blk < n_sel), 1.0, 0.0).astype(BF16)
    aug = (_dot_tn(sel, pick_ref[...]) - cvec_ref[...]).astype(BF16)
    q_aug = jnp.concatenate([q_all, stack(lambda j: aug)], axis=1)

    for cls in range(-(-nq // SLC_CLASS_TILES)):
        n_t = min((cls + 1) * SLC_CLASS_TILES, nq)

        @pl.when(qi // SLC_CLASS_TILES == cls)
        def _(n_t=n_t):
            for kt, s in enumerate(_score_tiles(q_aug, lambda i, n: ksb[i * tq:(i + n) * tq, :], n_t)):
                s_ref[kt] = s
            s_ref[qi] = s_ref[qi] + bt(0)

            @pl.when(qi > 0)
            def _():
                s_ref[qi - 1] = s_ref[qi - 1] + bt(1)

            o = _rows_softmax_pv([s_ref[kt] for kt in range(n_t)], lambda i, n: vsb[i * tq:(i + n) * tq, :])
            put(gate(1) * o, False)


def nsa_prompt(proj3, cols, kcmp, vcmp, rel_bias):
    B, T, _ = proj3.shape
    tq = ATT_TILE
    assert T % tq == 0 and T % SEL_BLOCK == 0 and WINDOW % tq == 0
    G, J = NSA_KV_HEADS, NSA_HPG
    nq = T // tq
    n_sel = T // SEL_BLOCK
    n_top = min(SEL_TOP, n_sel)
    n_sel_pad = _round_up(n_sel, 16)
    assert n_sel_pad < LANE
    n_pad = kcmp.shape[2]
    assert n_pad == LANE, "one lane tile of compressed blocks"

    bd = _bias_by_distance(rel_bias)
    n_a = T // CMP_STRIDE
    assert n_a == n_pad
    d_cmp = (CMP_STRIDE * ((n_a - 1) - np.arange(2 * n_a - 1))[None, :]
             + np.arange(CMP_STRIDE)[:, None] - (CMP_BLOCK - 1))
    gen_cmp = jnp.where(jnp.asarray(d_cmp >= 0)[None], bd[:, np.clip(d_cmp, 0, MAX_DISTANCE - 1)], 0.0)
    bcmp = _toeplitz(gen_cmp.reshape(NSA_HEADS * CMP_STRIDE, 2 * n_a - 1), n_a)
    bcmp = bcmp.reshape(NSA_HEADS, CMP_STRIDE, n_a, n_a).transpose(0, 2, 1, 3).reshape(NSA_HEADS, T, n_pad)
    d_diag = (tq - 1) - np.arange(2 * tq - 1)
    rel = bd - bd[:, -1:]
    gen = lambda d: jnp.where(jnp.asarray(d >= 0)[None], rel[:, np.clip(d, 0, MAX_DISTANCE - 1)], NEG)
    edge = jnp.broadcast_to(jnp.where(jnp.asarray(d_diag <= 0), 0.0, NEG)[None], (NSA_HEADS, 2 * tq - 1))
    btile = jnp.stack([_toeplitz(gen(d_diag), tq), _toeplitz(gen(d_diag + tq), tq),
                       jnp.zeros((NSA_HEADS, tq, tq), F32), _toeplitz(edge, tq)], axis=1)
    cmp_start = np.arange(n_pad) * CMP_STRIDE
    sel_start = np.arange(n_sel_pad) * SEL_BLOCK
    ovl = ((cmp_start[None, :] < sel_start[:, None] + SEL_BLOCK)
           & (cmp_start[None, :] + CMP_BLOCK > sel_start[:, None])
           & (np.arange(n_pad)[None, :] < T // CMP_STRIDE - 1))
    ovl = jnp.asarray(ovl, BF16)
    pick = jnp.asarray(np.arange(LANE)[None, :] == np.arange(n_sel_pad)[:, None], BF16)
    lane_i = np.arange(LANE)
    cvec = jnp.asarray(((lane_i < n_sel) | (lane_i == n_sel_pad))[None, :], F32)
    big = -NEG
    kaug_s = jnp.asarray(np.where(np.arange(T)[:, None] // SEL_BLOCK == lane_i[None, :], big, 0.0), BF16)
    kaug_w = jnp.asarray(np.where((np.arange(T + WINDOW)[:, None] < WINDOW) & (lane_i[None, :] == n_sel_pad),
                                  big, 0.0), BF16)
    gates = proj3[:, :, cols['ng']:cols['ng'] + 3 * NSA_HEADS].reshape(B, T, G, 3 * J).transpose(0, 2, 1, 3)

    kv_spec = lambda name: pl.BlockSpec((1, T, HEAD_DIM),
                                        lambda b, g, i, o=cols[name] // HEAD_DIM: (b, 0, o + g))
    cmp_spec = pl.BlockSpec((1, 1, n_pad, HEAD_DIM), lambda b, g, i: (b, g, 0, 0))
    const2 = lambda shape: pl.BlockSpec(shape, lambda b, g, i: (0, 0))
    return pl.pallas_call(
        functools.partial(_nsa_prompt_kernel, n_sel=n_sel, n_top=n_top, T=T),
        grid=(B, G, nq),
        in_specs=[pl.BlockSpec((1, tq, J * HEAD_DIM),
                               lambda b, g, i, o=cols['nq'] // (J * HEAD_DIM): (b, i, o + g)),
                  cmp_spec, cmp_spec,
                  kv_spec('ks'), kv_spec('vs'), kv_spec('kw'), kv_spec('vw'),
                  pl.BlockSpec((1, 1, tq, 3 * J), lambda b, g, i: (b, g, i, 0)),
                  pl.BlockSpec((J, tq, n_pad), lambda b, g, i: (g, i, 0)),
                  pl.BlockSpec((J, N_BIAS_TILES, tq, tq), lambda b, g, i: (g, 0, 0, 0)),
                  const2((n_sel_pad, n_pad)), const2((n_sel_pad, LANE)), const2((1, LANE)),
                  const2((T, LANE)), const2((T + WINDOW, LANE))],
        out_specs=pl.BlockSpec((1, tq, J * HEAD_DIM), lambda b, g, i: (b, i, g)),
        out_shape=jax.ShapeDtypeStruct((B, T, NSA_WIDTH), F32),
        scratch_shapes=[pltpu.VMEM((T, 2 * HEAD_DIM), BF16), pltpu.VMEM((T, HEAD_DIM), BF16),
                        pltpu.VMEM((T + WINDOW, 2 * HEAD_DIM), BF16), pltpu.VMEM((T + WINDOW, HEAD_DIM), BF16),
                        pltpu.VMEM((nq, J * tq, tq), F32)],
        compiler_params=pltpu.CompilerParams(
            dimension_semantics=("parallel", "parallel", "arbitrary"),
            vmem_limit_bytes=V7X_VMEM_LIMIT_BYTES),
        name="nsa_prompt",
    )(proj3, kcmp, vcmp, proj3, proj3, proj3, proj3, gates, bcmp, btile, ovl, pick, cvec, kaug_s, kaug_w)


PAGE_ROWS = PAGE_SIZE * NSA_KV_HEADS
STRIDES_PER_PAGE = PAGE_SIZE // CMP_STRIDE
PAGES_PER_STEP = 4


def _cache_rows(cache):
    return cache.reshape(-1, HEAD_DIM)


def _page_specs(layer):
    return [pl.BlockSpec((2 * PAGE_ROWS, HEAD_DIM),
                         lambda b, p, pt, k=k: (pt[b, p * PAGES_PER_STEP + k] * DEPTH + layer, 0))
            for k in range(PAGES_PER_STEP)]


def _cmp_partial_kernel(pt_ref, *refs):
    page_refs = refs[:PAGES_PER_STEP]
    w1k_ref, w1v_ref, z_ref = refs[PAGES_PER_STEP:]
    G = NSA_KV_HEADS
    half = CMP_STRIDE * HEAD_DIM
    n_rows = PAGES_PER_STEP * STRIDES_PER_PAGE
    for kv, w1_ref in ((0, w1k_ref), (1, w1v_ref)):
        xs = []
        for g in range(G):
            for page_ref in page_refs:
                pieces = [page_ref[pl.ds(kv * PAGE_ROWS + s * G + g, STRIDES_PER_PAGE, stride=CMP_STRIDE * G), :]
                          for s in range(CMP_STRIDE)]
                xs.append(jnp.concatenate(pieces, axis=1))
        x = jnp.concatenate(xs, axis=0).astype(BF16)
        for j in range(CMP_BLOCK // CMP_STRIDE):
            z = jnp.dot(x, w1_ref[j * half:(j + 1) * half, :], preferred_element_type=F32)
            for g in range(G):
                z_ref[0, kv, j, g] = z[g * n_rows:(g + 1) * n_rows]


def cmp_partial(cache_rows, page_table, layer, w1_k, w1_v):
    B, n_pages = page_table.shape
    flat = CMP_BLOCK * HEAD_DIM
    n_str = n_pages * STRIDES_PER_PAGE
    assert n_pages % PAGES_PER_STEP == 0
    wspec = pl.BlockSpec((flat, HEAD_DIM), lambda b, p, pt: (0, 0))
    return pl.pallas_call(
        _cmp_partial_kernel,
        grid_spec=pltpu.PrefetchScalarGridSpec(
            num_scalar_prefetch=1,
            grid=(B, n_pages // PAGES_PER_STEP),
            in_specs=_page_specs(layer) + [wspec, wspec],
            out_specs=pl.BlockSpec((1, 2, 2, NSA_KV_HEADS, PAGES_PER_STEP * STRIDES_PER_PAGE, HEAD_DIM),
                                   lambda b, p, pt: (b, 0, 0, 0, p, 0))),
        out_shape=jax.ShapeDtypeStruct((B, 2, 2, NSA_KV_HEADS, n_str, HEAD_DIM), F32),
        compiler_params=pltpu.CompilerParams(dimension_semantics=("parallel", "arbitrary"),
                                             vmem_limit_bytes=V7X_VMEM_LIMIT_BYTES),
        name="nsa_cmp_partial",
    )(page_table, *([cache_rows] * PAGES_PER_STEP),
      w1_k.reshape(flat, HEAD_DIM).astype(BF16), w1_v.reshape(flat, HEAD_DIM).astype(BF16))


def _cmp_attn_sample_kernel(z_ref, pek_ref, w1k_ref, w2k_ref, pev_ref, w1v_ref, w2v_ref, q_ref, bias_ref,
                            ovl_ref, o_ref, psel_ref, *, q0):
    T = q_ref.shape[1]
    J = NSA_HPG
    n_str = z_ref.shape[4]

    def finish(kv, pe_ref, w1_ref, w2_ref):
        z1 = pltpu.roll(z_ref[0, kv, 1, 0], n_str - 1, 0)
        pe = jnp.broadcast_to(pe_ref[...], (PACK_ROWS, CMP_BLOCK * HEAD_DIM)).astype(BF16)
        h0 = jnp.dot(pe, w1_ref[...], preferred_element_type=F32)[0:1, :]
        h = _gelu_tanh(z_ref[0, kv, 0, 0] + z1 + h0)
        return jnp.dot(h.astype(BF16), w2_ref[...], preferred_element_type=F32).astype(BF16)

    kc = finish(0, pek_ref, w1k_ref, w2k_ref)
    vc = finish(1, pev_ref, w1v_ref, w2v_ref)
    stack = lambda f: jnp.concatenate([f(j) for j in range(J)], axis=0)
    q_all = stack(lambda j: q_ref[0, :, j * HEAD_DIM:(j + 1) * HEAD_DIM] * HEAD_DIM ** -0.5).astype(BF16)
    row = lax.broadcasted_iota(jnp.int32, (T, n_str), 0)
    lane = lax.broadcasted_iota(jnp.int32, (T, n_str), 1)
    valid1 = (q0 + row >= lane * CMP_STRIDE + (CMP_BLOCK - 1)) & (lane < n_str - 1)
    valid = stack(lambda j: valid1)
    s = jnp.where(valid, _dot_nt(q_all, kc) + stack(lambda j: bias_ref[j]), NEG)
    e = jnp.where(valid, jnp.exp(s - jnp.max(s, axis=-1, keepdims=True)), 0.0)
    den = jnp.sum(e, axis=-1, keepdims=True)
    p = e / jnp.where(den > 0.0, den, 1.0)
    o_ref[0, 0] = jnp.dot(p.astype(BF16), vc, preferred_element_type=F32)
    psum = p[0:T]
    for j in range(1, J):
        psum = psum + p[j * T:(j + 1) * T]
    psum = _pad_rows(psum, PACK_ROWS)
    hi = psum.astype(BF16)
    r1 = psum - hi.astype(F32)
    mid = r1.astype(BF16)
    lo = (r1 - mid.astype(F32)).astype(BF16)
    ovl = ovl_ref[...]
    p_sel = (jnp.dot(hi, ovl, preferred_element_type=F32) + jnp.dot(mid, ovl, preferred_element_type=F32)
             + jnp.dot(lo, ovl, preferred_element_type=F32))
    psel_ref[0, 0] = p_sel[:T]


def _rank_select_kernel(psel_ref, tpos_ref, out_ref, score_ref, rank_ref, *, n_sel, n_top):
    shape = psel_ref.shape
    blk = lax.broadcasted_iota(jnp.int32, shape, 0)
    cur = jnp.broadcast_to(tpos_ref[...], shape) // SEL_BLOCK
    forced = (blk == 0) | (blk == cur) | (blk == cur - 1)
    ok = (blk <= cur) & (blk < n_sel)
    score = jnp.where(ok, psel_ref[...] + jnp.where(forced, FORCE_SCORE, 0.0), NEG)
    score_ref[...] = score
    rank_ref[...] = jnp.zeros(shape, F32)

    def body(jb, c):
        other = jnp.broadcast_to(score_ref[pl.ds(jb, 1), :], shape)
        beats = (other > score) | ((other == score) & (blk > jb))
        rank_ref[...] = rank_ref[...] + jnp.where(beats, 1.0, 0.0)
        return c

    lax.fori_loop(0, n_sel, body, 0)
    out_ref[...] = jnp.where((rank_ref[...] < n_top) & ok, 0.0, NEG)


def _slc_win_sample_kernel(pt_ref, *refs, win_buf):
    page_refs = refs[:PAGES_PER_STEP]
    (qT_ref, seladd_ref, blast_ref, selnew_ref, ksn_ref, vsn_ref, kwn_ref, vwn_ref, bnew_ref, cw_ref, bwin_ref,
     ocmp_ref, gate_ref, o_ref, m_ref, l_ref, acc_ref) = refs[PAGES_PER_STEP:]
    G = NSA_KV_HEADS
    p = pl.program_id(1)
    last = pl.num_programs(1) - 1
    lane_group = lax.broadcasted_iota(jnp.int32, (1, LANE), 1) // (LANE // G)

    @pl.when(p == 0)
    def _():
        m_ref[...] = jnp.full(m_ref.shape, NEG, F32)
        l_ref[...] = jnp.zeros(l_ref.shape, F32)
        acc_ref[...] = jnp.zeros(acc_ref.shape, F32)

    def scores(k_of_g):
        s = None
        for g in range(G):
            sg = jnp.dot(k_of_g(g).astype(BF16), qT_ref[0, g], preferred_element_type=F32)
            s = sg if s is None else s + sg
        return s

    def weighted_values(pT, v_of_g):
        o = None
        for g in range(G):
            pg = jnp.where(lane_group == g, pT, 0.0).astype(BF16)
            og = _dot_tn(v_of_g(g).astype(BF16), pg)
            o = og if o is None else o + og
        return o

    def accumulate(segments):
        m_old = m_ref[0:1, :]
        m_new = m_old
        for sT, _ in segments:
            m_new = jnp.maximum(m_new, jnp.max(sT, axis=0, keepdims=True))
        alpha = jnp.exp(m_old - m_new)
        l_new = alpha * l_ref[0:1, :]
        acc = alpha * acc_ref[...]
        for sT, v_of_g in segments:
            pT = jnp.exp(sT - m_new)
            l_new = l_new + jnp.sum(pT, axis=0, keepdims=True)
            acc = acc + weighted_values(pT, v_of_g)
        l_ref[...] = jnp.broadcast_to(l_new, l_ref.shape)
        acc_ref[...] = acc
        m_ref[...] = jnp.broadcast_to(m_new, m_ref.shape)

    key = lax.broadcasted_iota(jnp.int32, (PAGE_SIZE, LANE), 0)
    blocks_per_page = PAGE_SIZE // SEL_BLOCK
    is_last = (p == last).astype(F32)
    segments = []
    for k, page_ref in enumerate(page_refs):
        k_page = lambda g, r=page_ref: r[pl.ds(g, PAGE_SIZE, stride=G), :]
        v_page = lambda g, r=page_ref: r[pl.ds(PAGE_ROWS + g, PAGE_SIZE, stride=G), :]
        mask = seladd_ref[0, k, blocks_per_page - 1:blocks_per_page, :]
        for i in range(blocks_per_page - 2, -1, -1):
            mask = jnp.where(key < (i + 1) * SEL_BLOCK, seladd_ref[0, k, i:i + 1, :], mask)
        sT = scores(k_page) + mask
        if k == PAGES_PER_STEP - 1:
            sT = sT + blast_ref[...] * is_last
        segments.append((sT, v_page))
    accumulate(segments)

    @pl.when(p == last)
    def _():
        new = lambda ref: (lambda g: _pad_rows(ref[0, :, g * HEAD_DIM:(g + 1) * HEAD_DIM], PACK_ROWS))
        accumulate([(scores(new(ksn_ref)) + bnew_ref[...] + selnew_ref[0], new(vsn_ref))])
        o_slc = (acc_ref[...] / l_ref[0:1, :]).T

        k_win = lambda g: cw_ref[pl.ds(g, win_buf, stride=G), :]
        v_win = lambda g: cw_ref[pl.ds(win_buf * G + g, win_buf, stride=G), :]
        s_w = scores(k_win) + bwin_ref[...]
        s_n = scores(new(kwn_ref)) + bnew_ref[...]
        m = jnp.maximum(jnp.max(s_w, axis=0, keepdims=True), jnp.max(s_n, axis=0, keepdims=True))
        p_w = jnp.exp(s_w - m)
        p_n = jnp.exp(s_n - m)
        den = jnp.sum(p_w, axis=0, keepdims=True) + jnp.sum(p_n, axis=0, keepdims=True)
        o_win = ((weighted_values(p_w, v_win) + weighted_values(p_n, new(vwn_ref))) / den).T

        gates = jax.nn.sigmoid(gate_ref[0])
        o_ref[0] = gates[:, 0:1] * ocmp_ref[0] + gates[:, 1:2] * o_slc + gates[:, 2:3] * o_win


def nsa_sample(proj3, cols, cache_cmp, cache_slc, cache_win, page_table, layer, rel_bias,
               pe_k, w1_k, w2_k, pe_v, w1_v, w2_v):
    B, T, _ = proj3.shape
    G, J, H = NSA_KV_HEADS, NSA_HPG, NSA_HEADS
    n_pages = page_table.shape[1]
    past = n_pages * PAGE_SIZE
    q0 = past
    win_buf = cache_win.shape[3]
    L = G * J * T
    assert L == LANE and T <= PACK_ROWS and T < CMP_STRIDE and past % SEL_BLOCK == 0 and T <= SEL_BLOCK
    assert win_buf == min(WINDOW, past)
    n_str = past // CMP_STRIDE
    n_sel = past // SEL_BLOCK + 1
    n_top = min(SEL_TOP, n_sel)
    n_sel_rows = _round_up(n_sel, 8)
    n_sel_lanes = _round_up(n_sel, LANE)
    flat = CMP_BLOCK * HEAD_DIM
    b_far = rel_bias.astype(F32)[_bucket_table()[MAX_DISTANCE - 1]]

    def lane_bias(dist, ok):
        b = _bias_of_distance(rel_bias, np.maximum(dist, 0)) - b_far[:, None, None]
        b = jnp.where(jnp.asarray(ok)[None], b, NEG)
        return jnp.moveaxis(b.reshape(G, J, dist.shape[0], T), 2, 0).reshape(dist.shape[0], L)

    ti = np.arange(T)[None, :]
    z = cmp_partial(_cache_rows(cache_cmp), page_table, layer, w1_k, w1_v)
    n = np.arange(n_str)[None, :]
    dist_c = q0 + np.arange(T)[:, None] - (n * CMP_STRIDE + CMP_BLOCK - 1)
    near = dist_c.min(axis=0) < MAX_DISTANCE
    n_far = int(np.argmax(near)) if near.any() else n_str
    bcmp = jnp.concatenate([jnp.broadcast_to(b_far[:, None, None], (H, T, n_far)),
                            _bias_of_distance(rel_bias, np.maximum(dist_c[:, n_far:], 0))], axis=2)
    cmp_start = np.arange(n_str) * CMP_STRIDE
    sel_start = np.arange(n_sel_lanes) * SEL_BLOCK
    ovl = ((cmp_start[:, None] < sel_start[None, :] + SEL_BLOCK) & (cmp_start[:, None] + CMP_BLOCK > sel_start[None, :])
           & (np.arange(n_str)[:, None] < n_str - 1) & (np.arange(n_sel_lanes)[None, :] < n_sel))
    wspec = [pl.BlockSpec((1, flat), lambda b, g: (0, 0)),
             pl.BlockSpec((flat, HEAD_DIM), lambda b, g: (0, 0)),
             pl.BlockSpec((HEAD_DIM, HEAD_DIM), lambda b, g: (0, 0))]
    o_cmp, p_sel = pl.pallas_call(
        functools.partial(_cmp_attn_sample_kernel, q0=q0),
        grid=(B, G),
        in_specs=[pl.BlockSpec((1, 2, 2, 1, n_str, HEAD_DIM), lambda b, g: (b, 0, 0, g, 0, 0))] + wspec + wspec + [
            pl.BlockSpec((1, T, J * HEAD_DIM), lambda b, g, o=cols['nq'] // (J * HEAD_DIM): (b, 0, o + g)),
            pl.BlockSpec((J, T, n_str), lambda b, g: (g, 0, 0)),
            pl.BlockSpec((n_str, n_sel_lanes), lambda b, g: (0, 0))],
        out_specs=(pl.BlockSpec((1, 1, J * T, HEAD_DIM), lambda b, g: (b, g, 0, 0)),
                   pl.BlockSpec((1, 1, T, n_sel_lanes), lambda b, g: (b, g, 0, 0))),
        out_shape=(jax.ShapeDtypeStruct((B, G, J * T, HEAD_DIM), F32),
                   jax.ShapeDtypeStruct((B, G, T, n_sel_lanes), F32)),
        compiler_params=pltpu.CompilerParams(dimension_semantics=("parallel", "parallel"),
                                             vmem_limit_bytes=V7X_VMEM_LIMIT_BYTES),
        name="nsa_cmp_attn_sample",
    )(z, pe_k.reshape(1, flat), w1_k.reshape(flat, HEAD_DIM).astype(BF16), w2_k.astype(BF16),
      pe_v.reshape(1, flat), w1_v.reshape(flat, HEAD_DIM).astype(BF16), w2_v.astype(BF16),
      proj3, bcmp, jnp.asarray(ovl, BF16))

    n_bgt = B * G * T
    psel_t = p_sel.reshape(n_bgt, n_sel_lanes)[:, :n_sel_rows].T
    tpos = jnp.asarray(np.tile(q0 + np.arange(T), B * G)[None, :], jnp.int32)
    seladd = pl.pallas_call(
        functools.partial(_rank_select_kernel, n_sel=n_sel, n_top=n_top),
        out_shape=jax.ShapeDtypeStruct((n_sel_rows, n_bgt), F32),
        scratch_shapes=[pltpu.VMEM((n_sel_rows, n_bgt), F32), pltpu.VMEM((n_sel_rows, n_bgt), F32)],
        name="nsa_rank_select",
    )(psel_t, tpos)
    seladd = seladd.T.reshape(B, G, 1, T, n_sel_rows)
    seladd = jnp.broadcast_to(seladd, (B, G, J, T, n_sel_rows)).reshape(B, L, n_sel_rows)
    bpp = PAGE_SIZE // SEL_BLOCK
    sel_past = seladd[:, :, :n_sel - 1].reshape(B, L, n_pages, bpp).transpose(0, 2, 3, 1)
    sel_new = seladd[:, :, n_sel - 1].reshape(B, 1, L)

    q = proj3[:, :, cols['nq']:cols['nq'] + NSA_WIDTH].reshape(B, T, G, J, HEAD_DIM) * HEAD_DIM ** -0.5
    q_t = q.transpose(0, 2, 4, 3, 1).reshape(B, G, HEAD_DIM, J * T)
    place = jnp.asarray(np.arange(G)[:, None, None] == (np.arange(L) // (J * T))[None, None, :])
    q_pad = jnp.where(place[None], jnp.tile(q_t, (1, 1, 1, G)), 0.0).astype(BF16)
    ki = np.arange(PAGE_SIZE)[:, None]
    b_last = lane_bias(PAGE_SIZE + ti - ki, np.ones((PAGE_SIZE, T), bool))
    kn = np.arange(PACK_ROWS)[:, None]
    b_new = lane_bias(ti - kn, (ti - kn >= 0) & (kn < T))
    kw = np.arange(win_buf)[:, None]
    b_win = lane_bias(win_buf + ti - kw, win_buf + ti - kw <= WINDOW)
    gates = proj3[:, :, cols['ng']:cols['ng'] + 3 * H].reshape(B, T, G, J, 3).transpose(0, 2, 3, 1, 4).reshape(B, L, 3)
    win_rows = 2 * win_buf * G
    new_spec = lambda name: pl.BlockSpec((1, T, KV_WIDTH), lambda b, p, pt, o=cols[name] // KV_WIDTH: (b, 0, o))
    const = lambda shape: pl.BlockSpec(shape, lambda b, p, pt: (0, 0))
    per_b = lambda shape: pl.BlockSpec((1,) + shape, lambda b, p, pt: (b,) + (0,) * len(shape))
    out = pl.pallas_call(
        functools.partial(_slc_win_sample_kernel, win_buf=win_buf),
        grid_spec=pltpu.PrefetchScalarGridSpec(
            num_scalar_prefetch=1,
            grid=(B, n_pages // PAGES_PER_STEP),
            in_specs=_page_specs(layer) + [
                per_b((G, HEAD_DIM, L)),
                pl.BlockSpec((1, PAGES_PER_STEP, bpp, L), lambda b, p, pt: (b, p, 0, 0)),
                const((PAGE_SIZE, L)), per_b((1, L)),
                new_spec('ks'), new_spec('vs'), new_spec('kw'), new_spec('vw'),
                const((PACK_ROWS, L)),
                pl.BlockSpec((win_rows, HEAD_DIM), lambda b, p, pt: (layer * B + b, 0)),
                const((win_buf, L)), per_b((L, HEAD_DIM)), per_b((L, 3))],
            out_specs=per_b((L, HEAD_DIM)),
            scratch_shapes=[pltpu.VMEM((8, L), F32), pltpu.VMEM((8, L), F32), pltpu.VMEM((HEAD_DIM, L), F32)]),
        out_shape=jax.ShapeDtypeStruct((B, L, HEAD_DIM), F32),
        compiler_params=pltpu.CompilerParams(dimension_semantics=("parallel", "arbitrary"),
                                             vmem_limit_bytes=V7X_VMEM_LIMIT_BYTES),
        name="nsa_slc_win_sample",
    )(page_table, *([_cache_rows(cache_slc)] * PAGES_PER_STEP), q_pad, sel_past, b_last, sel_new,
      proj3, proj3, proj3, proj3, b_new,
      cache_win.reshape(-1, HEAD_DIM), b_win, o_cmp.reshape(B, L, HEAD_DIM), gates)
    return out.reshape(B, G, J, T, HEAD_DIM).transpose(0, 3, 1, 2, 4).reshape(B, T, NSA_WIDTH)


def _retention_tables(T, q0):
    c = _largest_divisor(T, RET_CHUNK)
    cp = max(c, RET_CHUNK)
    lg = np.log1p(-(2.0 ** (-5.0 - np.arange(RET_HEADS, dtype=np.float32)))).astype(np.float32)
    i = np.arange(cp)
    rel = i[:, None] - i[None, :]
    inside = (i < c)[:, None] & (i < c)[None, :]
    decay = np.where((rel >= 0) & inside, np.exp(np.maximum(rel, 0)[None] * lg[:, None, None]), 0.0)
    q_dec = np.broadcast_to(np.exp((i + 1)[None, :, None] * lg[:, None, None]), (RET_HEADS, cp, HEAD_DIM))
    k_dec = np.where((i < c)[None, :, None], np.exp((c - 1 - i)[None, :, None] * lg[:, None, None]), 0.0)
    k_dec = np.broadcast_to(k_dec, (RET_HEADS, cp, HEAD_DIM))
    chunk_dec = np.broadcast_to(np.exp(c * lg)[:, None, None], (RET_HEADS, 8, HEAD_DIM))
    half = HEAD_DIM // 2
    inv = (1.0 / (10000.0 ** np.linspace(0.0, 1.0, half, dtype=np.float32))).astype(np.float32)
    ang = (q0 + np.arange(T)).astype(np.float32)[:, None] * inv[None]
    cos, sin = np.cos(ang), np.sin(ang)
    cosf = np.concatenate([cos, cos], axis=1)
    sinf = np.concatenate([-sin, sin], axis=1)
    f = lambda a: jnp.asarray(a, F32)
    return c, cp, f(decay), f(q_dec), f(k_dec), f(chunk_dec), f(cosf), f(sinf)


def _retention_kernel(q_ref, k_ref, v_ref, g_ref, s0_ref, cos_ref, sin_ref, dec_ref, qd_ref, kd_ref, cd_ref,
                      o_ref, s_ref, *, c, cp, n):
    decay = dec_ref[0]
    q_dec = qd_ref[0]
    k_dec = kd_ref[0]
    chunk_dec = cd_ref[0, 0:1, :]
    half = HEAD_DIM // 2

    def load(ref, rows):
        return _pad_rows(ref[rows, :], cp)

    def rot(x, cos, sin):
        return x * cos + pltpu.roll(x, half, 1) * sin

    def body(i, s):
        rows = pl.ds(pl.multiple_of(i * c, c), c)
        cos, sin = load(cos_ref, rows), load(sin_ref, rows)
        q = rot(load(q_ref.at[0], rows), cos, sin)
        k = rot(load(k_ref.at[0], rows), cos, sin) * HEAD_DIM ** -0.5
        v = load(v_ref.at[0], rows).astype(BF16)
        qb = q.astype(BF16)
        inner = _dot_nt(qb, k.astype(BF16)) * decay
        o = (jnp.dot(inner.astype(BF16), v, preferred_element_type=F32)
             + jnp.dot(qb, s.astype(BF16), preferred_element_type=F32) * q_dec)
        s = s * chunk_dec + _dot_tn((k * k_dec).astype(BF16), v)
        o = o * lax.rsqrt(jnp.mean(o * o, axis=-1, keepdims=True) + EPS)
        g = g_ref[0, rows, :]
        o_ref[0, rows, :] = g * jax.nn.sigmoid(g) * o[:c]
        return s

    s_ref[0, 0] = lax.fori_loop(0, n, body, s0_ref[0, 0])


def retention(proj3, cols, s0, q0):
    B, T, _ = proj3.shape
    c, cp, decay, q_dec, k_dec, chunk_dec, cosf, sinf = _retention_tables(T, q0)
    col = lambda name: pl.BlockSpec((1, T, HEAD_DIM), lambda b, h, o=cols[name] // HEAD_DIM: (b, 0, o + h))
    tab = lambda r: pl.BlockSpec((1, r, HEAD_DIM), lambda b, h: (h, 0, 0))
    full = pl.BlockSpec((T, HEAD_DIM), lambda b, h: (0, 0))
    state = pl.BlockSpec((1, 1, HEAD_DIM, HEAD_DIM), lambda b, h: (b, h, 0, 0))
    return pl.pallas_call(
        functools.partial(_retention_kernel, c=c, cp=cp, n=T // c),
        grid=(B, RET_HEADS),
        in_specs=[col('rq'), col('rk'), col('rv'), col('rg'), state, full, full,
                  tab(cp), tab(cp), tab(cp), tab(8)],
        out_specs=(pl.BlockSpec((1, T, HEAD_DIM), lambda b, h: (b, 0, h)), state),
        out_shape=(jax.ShapeDtypeStruct((B, T, RET_WIDTH), F32),
                   jax.ShapeDtypeStruct((B, RET_HEADS, HEAD_DIM, HEAD_DIM), F32)),
        compiler_params=pltpu.CompilerParams(dimension_semantics=("parallel", "parallel"),
                                             vmem_limit_bytes=V7X_VMEM_LIMIT_BYTES),
        name="retention",
    )(proj3, proj3, proj3, proj3, s0.astype(F32), cosf, sinf, decay, q_dec, k_dec, chunk_dec)


S5_BLK_GROUPS = 8
S5_BLK_STATE = S5_BLK_GROUPS * S5_STATE
S5_BLK_CH = S5_BLK_GROUPS * S5_GROUP
S5_SCAN_ROWS = 8


def _s5_params(lam_re, lam_im, log_step, b_re, b_im, c_re, c_im):
    nb = S5_GROUPS // S5_BLK_GROUPS
    lam = lax.complex(lam_re.astype(F32), lam_im.astype(F32))
    step = jnp.exp(log_step.astype(F32))[:, None]
    a_bar = jnp.exp(lam * step)
    b_bar = ((a_bar - 1.0) / lam)[..., None] * lax.complex(b_re.astype(F32), b_im.astype(F32))
    r = np.arange(S5_SCAN_ROWS)

    def powers(k, keep):
        p = jnp.exp(lam[None] * step[None] * jnp.asarray(k, F32)[:, None, None])
        return jnp.where(jnp.asarray(keep)[:, None, None], p, 0.0)

    tabs = [powers(np.full(S5_SCAN_ROWS, k), r >= k) for k in (1, 2, 4)]
    tabs.append(powers(r + 1, r >= 0))
    tab = jnp.stack(tabs)
    tab = tab.reshape(4, S5_SCAN_ROWS, nb, S5_BLK_STATE).transpose(2, 0, 1, 3)
    atab = jnp.concatenate([tab.real, tab.imag], axis=1)

    eye = jnp.eye(S5_BLK_GROUPS, dtype=F32)
    bb = b_bar.reshape(nb, S5_BLK_GROUPS, S5_STATE, S5_GROUP)

    def in_mat(x):
        return jnp.einsum('ngpc,gh->ngchp', x, eye).reshape(nb, S5_BLK_CH, S5_BLK_STATE)

    bmat = jnp.concatenate([in_mat(bb.real), in_mat(bb.imag)], axis=-1).astype(BF16)
    cr = c_re.astype(F32).reshape(nb, S5_BLK_GROUPS, S5_GROUP, S5_STATE)
    ci = c_im.astype(F32).reshape(nb, S5_BLK_GROUPS, S5_GROUP, S5_STATE)

    def out_mat(x):
        return jnp.einsum('ngcp,gh->ngphc', x, eye).reshape(nb, S5_BLK_STATE, S5_BLK_CH)

    cmat = jnp.concatenate([out_mat(cr), -out_mat(ci)], axis=1).astype(BF16)
    return atab, bmat, cmat


def _s5_scan_tile(xr, xi, cr, ci, atab_ref):
    for idx, k in enumerate((1, 2, 4)):
        pr, pi = atab_ref[0, idx], atab_ref[0, 4 + idx]
        sr, si = pltpu.roll(xr, k, 0), pltpu.roll(xi, k, 0)
        xr, xi = xr + pr * sr - pi * si, xi + pr * si + pi * sr
    pr, pi = atab_ref[0, 3], atab_ref[0, 7]
    xr, xi = xr + pr * cr - pi * ci, xi + pr * ci + pi * cr
    last = S5_SCAN_ROWS - 1
    cr = jnp.broadcast_to(xr[last:last + 1, :], xr.shape)
    ci = jnp.broadcast_to(xi[last:last + 1, :], xi.shape)
    return xr, xi, cr, ci


def _s5_kernel(u_ref, x0_ref, atab_ref, b_ref, c_ref, d_ref, y_ref, st_ref, xs_ref, *, T):
    u = u_ref[0]
    t_pad = _round_up(T, PACK_ROWS)
    xs_ref[...] = jnp.dot(_pad_rows(u, t_pad).astype(BF16), b_ref[0], preferred_element_type=F32)[:T]
    n = S5_BLK_STATE
    R = S5_SCAN_ROWS

    def body(i, carry):
        cr, ci = carry
        rows = pl.ds(pl.multiple_of(i * R, R), R)
        xr, xi, cr, ci = _s5_scan_tile(xs_ref[rows, :n], xs_ref[rows, n:], cr, ci, atab_ref)
        xs_ref[rows, :n] = xr
        xs_ref[rows, n:] = xi
        return cr, ci

    x0 = x0_ref[0, 0]
    cr0 = jnp.broadcast_to(x0[0:1, :], (R, n))
    ci0 = jnp.broadcast_to(x0[1:2, :], (R, n))
    cr, ci = lax.fori_loop(0, T // R, body, (cr0, ci0))
    st_ref[0, 0] = jnp.concatenate([cr[0:1], ci[0:1]], axis=0)
    y = jnp.dot(_pad_rows(xs_ref[...], t_pad).astype(BF16), c_ref[0], preferred_element_type=F32)[:T]
    y_ref[0] = _gelu_tanh(y + d_ref[...] * u)


def s5_scan(proj3, cols, x0, lam_re, lam_im, log_step, b_re, b_im, c_re, c_im, d):
    B, T, _ = proj3.shape
    assert T % S5_SCAN_ROWS == 0
    nb = S5_GROUPS // S5_BLK_GROUPS
    atab, bmat, cmat = _s5_params(lam_re, lam_im, log_step, b_re, b_im, c_re, c_im)
    x0b = x0.astype(F32).reshape(B, nb, S5_BLK_STATE, 2).transpose(0, 1, 3, 2)
    blk3 = lambda shape: pl.BlockSpec((1,) + shape, lambda b, j: (j, 0, 0))
    y, st = pl.pallas_call(
        functools.partial(_s5_kernel, T=T),
        grid=(B, nb),
        in_specs=[pl.BlockSpec((1, T, S5_BLK_CH), lambda b, j, o=cols['su'] // S5_BLK_CH: (b, 0, o + j)),
                  pl.BlockSpec((1, 1, 2, S5_BLK_STATE), lambda b, j: (b, j, 0, 0)),
                  pl.BlockSpec((1, 8, S5_SCAN_ROWS, S5_BLK_STATE), lambda b, j: (j, 0, 0, 0)),
                  blk3((S5_BLK_CH, 2 * S5_BLK_STATE)), blk3((2 * S5_BLK_STATE, S5_BLK_CH)),
                  pl.BlockSpec((1, S5_BLK_CH), lambda b, j: (0, j))],
        out_specs=(pl.BlockSpec((1, T, S5_BLK_CH), lambda b, j: (b, 0, j)),
                   pl.BlockSpec((1, 1, 2, S5_BLK_STATE), lambda b, j: (b, j, 0, 0))),
        out_shape=(jax.ShapeDtypeStruct((B, T, S5_WIDTH), F32),
                   jax.ShapeDtypeStruct((B, nb, 2, S5_BLK_STATE), F32)),
        scratch_shapes=[pltpu.VMEM((T, 2 * S5_BLK_STATE), F32)],
        compiler_params=pltpu.CompilerParams(dimension_semantics=("parallel", "parallel"),
                                             vmem_limit_bytes=V7X_VMEM_LIMIT_BYTES),
        name="s5_scan",
    )(proj3, x0b, atab, bmat, cmat, d.astype(F32).reshape(1, S5_WIDTH))
    st = st.transpose(0, 1, 3, 2).reshape(B, S5_GROUPS, S5_STATE, 2)
    return y, st


def _branch_norm_kernel(ro_ref, so_ref, no_ref, bn_ref, o_ref):
    off = 0
    for ref in (ro_ref, so_ref, no_ref):
        x = ref[...]
        w = x.shape[-1]
        y = x * lax.rsqrt(jnp.mean(x * x, axis=-1, keepdims=True) + EPS) * bn_ref[:, off:off + w]
        o_ref[:, off:off + w] = y.astype(o_ref.dtype)
        off += w


def branch_norm(ro, so, no, bn):
    m = ro.shape[0]
    tm = min(m, 256)
    assert m % tm == 0
    spec = lambda w: pl.BlockSpec((tm, w), lambda i: (i, 0))
    return pl.pallas_call(
        _branch_norm_kernel,
        grid=(m // tm,),
        in_specs=[spec(RET_WIDTH), spec(S5_WIDTH), spec(NSA_WIDTH), pl.BlockSpec((1, D_MODEL), lambda i: (0, 0))],
        out_specs=spec(D_MODEL),
        out_shape=jax.ShapeDtypeStruct((m, D_MODEL), BF16),
        compiler_params=pltpu.CompilerParams(dimension_semantics=("parallel",),
                                             vmem_limit_bytes=V7X_VMEM_LIMIT_BYTES),
        name="branch_norm",
    )(ro, so, no, bn.astype(F32).reshape(1, D_MODEL))


_COL_NAMES = ('rq', 'rk', 'rv', 'rg', 'su', 'nq', 'kc', 'vc', 'ks', 'vs', 'kw', 'vw', 'ng')
COLS = {name: int(off) for name, off in zip(_COL_NAMES, np.concatenate([[0], np.cumsum(IN_SPLITS)]))}


def _block(x, layer, w, rel_bias, past, win_buf):
    B, T, _ = x.shape
    G = NSA_KV_HEADS
    M = B * T
    x2 = x.reshape(M, D_MODEL)
    h = rmsnorm(x2, w['norm_mix'][layer], BF16)
    proj3 = matmul(h, w['w_in'], layer).reshape(B, T, -1)
    s5_w = [w[k][layer] for k in ('s5_lambda_re', 's5_lambda_im', 's5_log_step', 's5_b_re', 's5_b_im',
                                  's5_c_re', 's5_c_im', 's5_d')]
    cmp_w = [w[k][layer] for k in ('cmp_pe_k', 'cmp_w1_k', 'cmp_w2_k', 'cmp_pe_v', 'cmp_w1_v', 'cmp_w2_v')]
    if past is None:
        q0 = 0
        ret_s0 = jnp.zeros((B, RET_HEADS, HEAD_DIM, HEAD_DIM), F32)
        s5_s0 = jnp.zeros((B, S5_GROUPS, S5_STATE, 2), F32)
        win_prev = jnp.zeros((B, 2, WINDOW, G, HEAD_DIM), x.dtype)
    else:
        cache_cmp, cache_slc, cache_win, state_ret, state_s5, page_table = past
        q0 = page_table.shape[1] * PAGE_SIZE
        ret_s0, s5_s0, win_prev = state_ret[layer], state_s5[layer], cache_win[layer]

    ro, ret_s = retention(proj3, COLS, ret_s0, q0)
    sy, s5_s = s5_scan(proj3, COLS, s5_s0, *s5_w)
    if past is None:
        kcmp, vcmp = compress_prompt(proj3, COLS['kc'], COLS['vc'], *cmp_w)
        no = nsa_prompt(proj3, COLS, kcmp, vcmp, rel_bias)
    else:
        no = nsa_sample(proj3, COLS, cache_cmp, cache_slc, cache_win, page_table, layer, rel_bias, *cmp_w)

    sy2 = sy.reshape(M, S5_WIDTH)
    so = matmul(sy2, w['s5_w_glu'], layer, res=sy2, act="glu")
    mix = branch_norm(ro.reshape(M, RET_WIDTH), so, no.reshape(M, NSA_WIDTH), w['branch_norm'][layer])
    x2 = matmul(mix, w['w_out'], layer, res=x2)
    h = rmsnorm(x2, w['norm_ffn'][layer], BF16)
    up = matmul(h, w['w_up'], layer, act="relu2", out_dtype=BF16)
    x2 = matmul(up, w['w_down'], layer, res=x2)

    rows = lambda name: proj3[:, :, COLS[name]:COLS[name] + KV_WIDTH].reshape(B, T, G, HEAD_DIM)
    cmp_rows = jnp.stack([rows('kc'), rows('vc')], axis=1)
    slc_rows = jnp.stack([rows('ks'), rows('vs')], axis=1)
    win_new = jnp.concatenate([win_prev, jnp.stack([rows('kw'), rows('vw')], axis=1)], axis=2)[:, :, -win_buf:]
    return x2.reshape(B, T, D_MODEL), cmp_rows, slc_rows, win_new, ret_s, s5_s


def kernel(x_prompt, x_sample, cache_cmp, cache_slc, cache_win, state_ret, state_s5, page_table,
           rel_bias, norm_mix, w_in, s5_lambda_re, s5_lambda_im, s5_log_step, s5_b_re, s5_b_im,
           s5_c_re, s5_c_im, s5_d, s5_w_glu, cmp_pe_k, cmp_w1_k, cmp_w2_k, cmp_pe_v, cmp_w1_v,
           cmp_w2_v, branch_norm, w_out, norm_ffn, w_up, w_down, norm_final):
    win_buf = cache_win.shape[3]
    in_pad = _round_up(IN_COLS, 512) - IN_COLS
    w = dict(
        norm_mix=norm_mix, norm_ffn=norm_ffn, branch_norm=branch_norm,
        w_in=jnp.pad(w_in.astype(BF16), ((0, 0), (0, 0), (0, in_pad))),
        w_out=w_out.astype(BF16), w_up=w_up.astype(BF16), w_down=w_down.astype(BF16),
        s5_w_glu=s5_w_glu.astype(BF16),
        s5_lambda_re=s5_lambda_re, s5_lambda_im=s5_lambda_im, s5_log_step=s5_log_step, s5_b_re=s5_b_re,
        s5_b_im=s5_b_im, s5_c_re=s5_c_re, s5_c_im=s5_c_im, s5_d=s5_d,
        cmp_pe_k=cmp_pe_k, cmp_w1_k=cmp_w1_k, cmp_w2_k=cmp_w2_k,
        cmp_pe_v=cmp_pe_v, cmp_w1_v=cmp_w1_v, cmp_w2_v=cmp_w2_v)
    past = (cache_cmp, cache_slc, cache_win, state_ret, state_s5, page_table)
    xp, xs = x_prompt, x_sample
    written_p, written_s = [], []
    for layer in range(DEPTH):
        xp, *entries = _block(xp, layer, w, rel_bias, None, win_buf)
        written_p.append(entries)
        xs, *entries = _block(xs, layer, w, rel_bias, past, win_buf)
        written_s.append(entries)
    y_prompt = rmsnorm(xp.reshape(-1, D_MODEL), norm_final, F32).reshape(xp.shape)
    y_sample = rmsnorm(xs.reshape(-1, D_MODEL), norm_final, F32).reshape(xs.shape)
    stacked = lambda written, i, axis: jnp.stack([entries[i] for entries in written], axis=axis)
    return (y_prompt, y_sample,
            stacked(written_p, 0, 1), stacked(written_s, 0, 1),
            stacked(written_p, 1, 1), stacked(written_s, 1, 1),
            stacked(written_p, 2, 0), stacked(written_s, 2, 0),
            stacked(written_p, 3, 0), stacked(written_s, 3, 0),
            stacked(written_p, 4, 0), stacked(written_s, 4, 0))
```

```python
import functools
import math

import jax
import jax.numpy as jnp
import numpy as np
from jax import lax
from jax.experimental import pallas as pl
from jax.experimental.pallas import tpu as pltpu

F32 = jnp.float32
BF16 = jnp.bfloat16

D_MODEL = 4096
DEPTH = 2
PAGE_SIZE = 128
HEAD_DIM = 128
RET_WIDTH = 1024
RET_HEADS = 8
RET_CHUNK = 128
S5_WIDTH = 1024
S5_GROUP = 16
S5_GROUPS = 64
S5_STATE = 64
NSA_WIDTH = 2048
NSA_HEADS = 16
NSA_KV_HEADS = 4
NSA_HPG = 4
KV_WIDTH = 512
CMP_BLOCK = 32
CMP_STRIDE = 16
SEL_BLOCK = 64
SEL_TOP = 16
WINDOW = 512
FORCE_SCORE = 1e4
NEG = -1e30
N_BUCKETS = 32
MAX_DISTANCE = 128
EPS = 1e-6
IN_SPLITS = (RET_WIDTH, RET_WIDTH, RET_WIDTH, RET_WIDTH, S5_WIDTH, NSA_WIDTH,
             KV_WIDTH, KV_WIDTH, KV_WIDTH, KV_WIDTH, KV_WIDTH, KV_WIDTH, 3 * NSA_HEADS)
IN_COLS = sum(IN_SPLITS)

V7X_VMEM_LIMIT_BYTES = 48 * 1024 * 1024
LANE = 128
PACK_ROWS = 16


def _round_up(n, m):
    return -(-n // m) * m


def _largest_divisor(n, cap):
    return max(d for d in range(1, min(n, cap) + 1) if n % d == 0)


def _pad_rows(x, rows):
    extra = rows - x.shape[0]
    return jnp.concatenate([x, jnp.zeros((extra, x.shape[1]), x.dtype)], axis=0) if extra else x


def _gelu_tanh(x):
    return 0.5 * x * (1.0 + jnp.tanh(math.sqrt(2.0 / math.pi) * (x + 0.044715 * (x * x * x))))


def _dot_nt(a, b):
    return lax.dot_general(a, b, (((1,), (1,)), ((), ())), preferred_element_type=F32)


def _dot_tn(a, b):
    return lax.dot_general(a, b, (((0,), (0,)), ((), ())), preferred_element_type=F32)


def _rmsnorm_kernel(x_ref, g_ref, o_ref):
    x = x_ref[...].astype(F32)
    ms = jnp.mean(x * x, axis=-1, keepdims=True)
    o_ref[...] = (x * lax.rsqrt(ms + EPS) * g_ref[...].astype(F32)).astype(o_ref.dtype)


def rmsnorm(x2d, gain, out_dtype):
    m, d = x2d.shape
    tm = min(m, 256)
    assert m % tm == 0
    return pl.pallas_call(
        _rmsnorm_kernel,
        grid=(m // tm,),
        in_specs=[pl.BlockSpec((tm, d), lambda i: (i, 0)),
                  pl.BlockSpec((1, d), lambda i: (0, 0))],
        out_specs=pl.BlockSpec((tm, d), lambda i: (i, 0)),
        out_shape=jax.ShapeDtypeStruct((m, d), out_dtype),
        compiler_params=pltpu.CompilerParams(dimension_semantics=("parallel",),
                                             vmem_limit_bytes=V7X_VMEM_LIMIT_BYTES),
        name="rmsnorm",
    )(x2d, gain.reshape(1, d))


def _mm_kernel(*refs, nk, act, has_res):
    if has_res:
        a_ref, w_ref, r_ref, o_ref, acc_ref = refs
    else:
        a_ref, w_ref, o_ref, acc_ref = refs
        r_ref = None
    k = pl.program_id(2)

    @pl.when(k == 0)
    def _():
        acc_ref[...] = jnp.zeros_like(acc_ref)

    acc_ref[...] += jnp.dot(a_ref[...].astype(BF16), w_ref[...], preferred_element_type=F32)

    @pl.when(k == nk - 1)
    def _():
        acc = acc_ref[...]
        if act == "relu2":
            acc = jnp.square(jnp.maximum(acc, 0.0))
        if act == "glu":
            acc = r_ref[...].astype(F32) * jax.nn.sigmoid(acc)
        elif has_res:
            acc = acc + r_ref[...].astype(F32)
        o_ref[...] = acc.astype(o_ref.dtype)


def matmul(a, w, layer, *, res=None, act=None, out_dtype=F32):
    m, kdim = a.shape
    n = w.shape[2]
    tm = min(m, 1024)
    tn = 512 if n % 512 == 0 else (256 if n % 256 == 0 else 128)
    tk = min(kdim, 4096)
    assert m % tm == 0 and n % tn == 0 and kdim % tk == 0
    nk = kdim // tk
    in_specs = [pl.BlockSpec((tm, tk), lambda i, j, k: (i, k)),
                pl.BlockSpec((None, tk, tn), lambda i, j, k: (layer, k, j))]
    args = [a, w]
    if res is not None:
        in_specs.append(pl.BlockSpec((tm, tn), lambda i, j, k: (i, j)))
        args.append(res)
    return pl.pallas_call(
        functools.partial(_mm_kernel, nk=nk, act=act, has_res=res is not None),
        grid=(m // tm, n // tn, nk),
        in_specs=in_specs,
        out_specs=pl.BlockSpec((tm, tn), lambda i, j, k: (i, j)),
        out_shape=jax.ShapeDtypeStruct((m, n), out_dtype),
        scratch_shapes=[pltpu.VMEM((tm, tn), F32)],
        compiler_params=pltpu.CompilerParams(
            dimension_semantics=("parallel", "parallel", "arbitrary"),
            vmem_limit_bytes=V7X_VMEM_LIMIT_BYTES),
        name="matmul",
    )(*args)


def _compress_rows(x_ref, n_full, pe_ref, w1_ref, w2_ref):
    pieces = [x_ref[0, pl.ds(s, n_full, stride=CMP_STRIDE), :] for s in range(CMP_STRIDE)]
    x = jnp.concatenate(pieces, axis=1).astype(BF16)
    half = CMP_STRIDE * HEAD_DIM
    z0 = jnp.dot(x, w1_ref[:half, :], preferred_element_type=F32)
    z1 = jnp.dot(x, w1_ref[half:, :], preferred_element_type=F32)
    z1 = pltpu.roll(z1, n_full - 1, 0)
    pe = jnp.broadcast_to(pe_ref[...], (8, CMP_BLOCK * HEAD_DIM)).astype(BF16)
    h0 = jnp.dot(pe, w1_ref[...], preferred_element_type=F32)[0:1, :]
    h = _gelu_tanh(z0 + z1 + h0)
    return jnp.dot(h.astype(BF16), w2_ref[...], preferred_element_type=F32)


def _compress_prompt_kernel(xk_ref, xv_ref, pek_ref, w1k_ref, w2k_ref, pev_ref, w1v_ref, w2v_ref,
                            kc_ref, vc_ref, *, n_full):
    for x_ref, pe_ref, w1_ref, w2_ref, o_ref in ((xk_ref, pek_ref, w1k_ref, w2k_ref, kc_ref),
                                                  (xv_ref, pev_ref, w1v_ref, w2v_ref, vc_ref)):
        out = _compress_rows(x_ref, n_full, pe_ref, w1_ref, w2_ref).astype(o_ref.dtype)
        n_pad = o_ref.shape[2]
        o_ref[0, 0, :n_full, :] = out
        if n_pad > n_full:
            o_ref[0, 0, n_full:, :] = jnp.zeros((n_pad - n_full, HEAD_DIM), o_ref.dtype)


def compress_prompt(proj3, col_k, col_v, pe_k, w1_k, w2_k, pe_v, w1_v, w2_v):
    B, T, _ = proj3.shape
    n_full = T // CMP_STRIDE
    n_pad = _round_up(n_full, LANE)
    flat = CMP_BLOCK * HEAD_DIM
    wspec = [pl.BlockSpec((1, flat), lambda b, g: (0, 0)),
             pl.BlockSpec((flat, HEAD_DIM), lambda b, g: (0, 0)),
             pl.BlockSpec((HEAD_DIM, HEAD_DIM), lambda b, g: (0, 0))]
    out_sds = jax.ShapeDtypeStruct((B, NSA_KV_HEADS, n_pad, HEAD_DIM), BF16)
    ospec = pl.BlockSpec((1, 1, n_pad, HEAD_DIM), lambda b, g: (b, g, 0, 0))
    return pl.pallas_call(
        functools.partial(_compress_prompt_kernel, n_full=n_full),
        grid=(B, NSA_KV_HEADS),
        in_specs=[pl.BlockSpec((1, T, HEAD_DIM), lambda b, g: (b, 0, col_k // HEAD_DIM + g)),
                  pl.BlockSpec((1, T, HEAD_DIM), lambda b, g: (b, 0, col_v // HEAD_DIM + g))] + wspec + wspec,
        out_specs=(ospec, ospec),
        out_shape=(out_sds, out_sds),
        compiler_params=pltpu.CompilerParams(dimension_semantics=("parallel", "parallel"),
                                             vmem_limit_bytes=V7X_VMEM_LIMIT_BYTES),
        name="nsa_compress_prompt",
    )(proj3, proj3,
      pe_k.reshape(1, flat), w1_k.reshape(flat, HEAD_DIM).astype(BF16), w2_k.astype(BF16),
      pe_v.reshape(1, flat), w1_v.reshape(flat, HEAD_DIM).astype(BF16), w2_v.astype(BF16))


def _bucket_table():
    d = np.arange(MAX_DISTANCE)
    max_exact = N_BUCKETS // 2
    large = max_exact + (np.log(np.maximum(d, 1).astype(np.float32) / np.float32(max_exact))
                         / np.float32(math.log(MAX_DISTANCE / max_exact))
                         * np.float32(N_BUCKETS - max_exact)).astype(np.int32)
    return np.where(d < max_exact, d, np.minimum(large, N_BUCKETS - 1)).astype(np.int32)


def _bias_of_distance(rel_bias, dist):
    bt = _bucket_table()
    buckets = bt[np.clip(dist, 0, MAX_DISTANCE - 1)]
    b = jnp.moveaxis(rel_bias.astype(F32)[buckets], -1, 0)
    return jnp.where(jnp.asarray(dist >= 0)[None], b, NEG)


def _bias_by_distance(rel_bias):
    return rel_bias.astype(F32)[_bucket_table()].T


def _toeplitz(v, n):
    h = v.shape[0]
    w = jnp.pad(v, ((0, 0), (0, 1)))
    m = jnp.tile(w, (1, n))[:, :n * (2 * n - 1)].reshape(h, n, 2 * n - 1)
    return m[:, :, n - 1:]


ATT_TILE = 128
N_BIAS_TILES = 4
SLC_CLASS_TILES = 4
MXU_TILES = 2


def _rows_softmax_pv(s_tiles, v_rows):
    m = s_tiles[0]
    for s in s_tiles[1:]:
        m = jnp.maximum(m, s)
    m = jnp.max(m, axis=-1, keepdims=True)
    l = None
    o = None
    for i in range(0, len(s_tiles), MXU_TILES):
        ps = [jnp.exp(s - m) for s in s_tiles[i:i + MXU_TILES]]
        for p in ps:
            l = p if l is None else l + p
        p_cat = ps[0] if len(ps) == 1 else jnp.concatenate(ps, axis=1)
        pv = jnp.dot(p_cat.astype(BF16), v_rows(i, len(ps)), preferred_element_type=F32)
        o = pv if o is None else o + pv
    return o / jnp.sum(l, axis=-1, keepdims=True)


def _score_tiles(q, k_rows, n_tiles):
    tiles = []
    for i in range(0, n_tiles, MXU_TILES):
        n = min(MXU_TILES, n_tiles - i)
        s = _dot_nt(q, k_rows(i, n))
        tiles += [s[:, j * LANE:(j + 1) * LANE] for j in range(n)]
    return tiles


def _nsa_prompt_kernel(q_ref, kc_ref, vc_ref, ks_ref, vs_ref, kw_ref, vw_ref, gate_ref,
                       bcmp_ref, btile_ref, ovl_ref, pick_ref, cvec_ref, kaug_s_ref, kaug_w_ref, o_ref,
                       ksb, vsb, kwb, vwb, s_ref, *, n_sel, n_top, T):
    tq = ATT_TILE
    J = NSA_HPG
    qi = pl.program_id(2)
    n_sel_pad = pick_ref.shape[0]
    nq = T // tq

    @pl.when(qi == 0)
    def _():
        ksb[:, :HEAD_DIM] = ks_ref[0].astype(BF16)
        ksb[:, HEAD_DIM:] = kaug_s_ref[...]
        vsb[...] = vs_ref[0].astype(BF16)
        kwb[:WINDOW, :HEAD_DIM] = jnp.zeros((WINDOW, HEAD_DIM), BF16)
        kwb[WINDOW:, :HEAD_DIM] = kw_ref[0].astype(BF16)
        kwb[:, HEAD_DIM:] = kaug_w_ref[...]
        vwb[:WINDOW, :] = jnp.zeros((WINDOW, HEAD_DIM), BF16)
        vwb[WINDOW:, :] = vw_ref[0].astype(BF16)

    def stack(f):
        return jnp.concatenate([f(j) for j in range(J)], axis=0)

    def put(vals, first):
        for j in range(J):
            cols = slice(j * HEAD_DIM, (j + 1) * HEAD_DIM)
            v = vals[j * tq:(j + 1) * tq]
            o_ref[0, :, cols] = v if first else o_ref[0, :, cols] + v

    gates = jax.nn.sigmoid(gate_ref[0, 0])
    gate = lambda c: stack(lambda j: gates[:, 3 * j + c:3 * j + c + 1])
    q_all = stack(lambda j: q_ref[0, :, j * HEAD_DIM:(j + 1) * HEAD_DIM] * HEAD_DIM ** -0.5).astype(BF16)

    row = lax.broadcasted_iota(jnp.int32, (tq, LANE), 0)
    lane = lax.broadcasted_iota(jnp.int32, (tq, LANE), 1)
    valid1 = qi * tq + row >= lane * CMP_STRIDE + (CMP_BLOCK - 1)
    valid = stack(lambda j: valid1)
    s = jnp.where(valid, _dot_nt(q_all, kc_ref[0, 0]) + stack(lambda j: bcmp_ref[j]), NEG)
    e = jnp.where(valid, jnp.exp(s - jnp.max(s, axis=-1, keepdims=True)), 0.0)
    den = jnp.sum(e, axis=-1, keepdims=True)
    p = e / jnp.where(den > 0.0, den, 1.0)
    put(gate(0) * jnp.dot(p.astype(BF16), vc_ref[0, 0], preferred_element_type=F32), True)
    psum = p[0:tq]
    for j in range(1, J):
        psum = psum + p[j * tq:(j + 1) * tq]

    bt = lambda i: stack(lambda j: btile_ref[j, i])

    pad_flag = jnp.broadcast_to(-cvec_ref[...], (tq, LANE)).astype(BF16)
    q_win = jnp.concatenate([q_all, stack(lambda j: pad_flag)], axis=1)
    n_w = WINDOW // tq + 1
    win_rows = lambda i, n: pl.ds(pl.multiple_of((qi + i) * tq, tq), n * tq)
    s_tiles = _score_tiles(q_win, lambda i, n: kwb[win_rows(i, n), :], n_w)
    s_tiles[0] = s_tiles[0] + bt(3)
    s_tiles[n_w - 2] = s_tiles[n_w - 2] + bt(1)
    s_tiles[n_w - 1] = s_tiles[n_w - 1] + bt(0)
    put(gate(2) * _rows_softmax_pv(s_tiles, lambda i, n: vwb[win_rows(i, n), :]), False)

    hi = psum.astype(BF16)
    r1 = psum - hi.astype(F32)
    mid = r1.astype(BF16)
    lo = (r1 - mid.astype(F32)).astype(BF16)
    ovl = ovl_ref[...]
    p_sel = _dot_nt(ovl, hi) + _dot_nt(ovl, mid) + _dot_nt(ovl, lo)
    blk = lax.broadcasted_iota(jnp.int32, (n_sel_pad, tq), 0)
    cur = (qi * tq + lax.broadcasted_iota(jnp.int32, (n_sel_pad, tq), 1)) // SEL_BLOCK
    forced = (blk == 0) | (blk == cur) | (blk == cur - 1)
    score = jnp.where(blk <= cur, p_sel + jnp.where(forced, FORCE_SCORE, 0.0), NEG)
    rank = jnp.zeros((n_sel_pad, tq), F32)
    for jb in range(n_sel):
        other = score[jb:jb + 1, :]
        beats = (other > score) | ((other == score) & (blk > jb))
        rank = rank + jnp.where(beats, 1.0, 0.0)
    sel = jnp.where((rank < n_top) & (blk <= cur) & (blk < n_sel), 1.0, 0.0).astype(BF16)
    aug = (_dot_tn(sel, pick_ref[...]) - cvec_ref[...]).astype(BF16)
    q_aug = jnp.concatenate([q_all, stack(lambda j: aug)], axis=1)

    for cls in range(-(-nq // SLC_CLASS_TILES)):
        n_t = min((cls + 1) * SLC_CLASS_TILES, nq)

        @pl.when(qi // SLC_CLASS_TILES == cls)
        def _(n_t=n_t):
            for kt, s in enumerate(_score_tiles(q_aug, lambda i, n: ksb[i * tq:(i + n) * tq, :], n_t)):
                s_ref[kt] = s
            s_ref[qi] = s_ref[qi] + bt(0)

            @pl.when(qi > 0)
            def _():
                s_ref[qi - 1] = s_ref[qi - 1] + bt(1)

            o = _rows_softmax_pv([s_ref[kt] for kt in range(n_t)], lambda i, n: vsb[i * tq:(i + n) * tq, :])
            put(gate(1) * o, False)


def nsa_prompt(proj3, cols, kcmp, vcmp, rel_bias):
    B, T, _ = proj3.shape
    tq = ATT_TILE
    assert T % tq == 0 and T % SEL_BLOCK == 0 and WINDOW % tq == 0
    G, J = NSA_KV_HEADS, NSA_HPG
    nq = T // tq
    n_sel = T // SEL_BLOCK
    n_top = min(SEL_TOP, n_sel)
    n_sel_pad = _round_up(n_sel, 16)
    assert n_sel_pad < LANE
    n_pad = kcmp.shape[2]
    assert n_pad == LANE, "one lane tile of compressed blocks"

    bd = _bias_by_distance(rel_bias)
    n_a = T // CMP_STRIDE
    assert n_a == n_pad
    d_cmp = (CMP_STRIDE * ((n_a - 1) - np.arange(2 * n_a - 1))[None, :]
             + np.arange(CMP_STRIDE)[:, None] - (CMP_BLOCK - 1))
    gen_cmp = jnp.where(jnp.asarray(d_cmp >= 0)[None], bd[:, np.clip(d_cmp, 0, MAX_DISTANCE - 1)], 0.0)
    bcmp = _toeplitz(gen_cmp.reshape(NSA_HEADS * CMP_STRIDE, 2 * n_a - 1), n_a)
    bcmp = bcmp.reshape(NSA_HEADS, CMP_STRIDE, n_a, n_a).transpose(0, 2, 1, 3).reshape(NSA_HEADS, T, n_pad)
    d_diag = (tq - 1) - np.arange(2 * tq - 1)
    rel = bd - bd[:, -1:]
    gen = lambda d: jnp.where(jnp.asarray(d >= 0)[None], rel[:, np.clip(d, 0, MAX_DISTANCE - 1)], NEG)
    edge = jnp.broadcast_to(jnp.where(jnp.asarray(d_diag <= 0), 0.0, NEG)[None], (NSA_HEADS, 2 * tq - 1))
    btile = jnp.stack([_toeplitz(gen(d_diag), tq), _toeplitz(gen(d_diag + tq), tq),
                       jnp.zeros((NSA_HEADS, tq, tq), F32), _toeplitz(edge, tq)], axis=1)
    cmp_start = np.arange(n_pad) * CMP_STRIDE
    sel_start = np.arange(n_sel_pad) * SEL_BLOCK
    ovl = ((cmp_start[None, :] < sel_start[:, None] + SEL_BLOCK)
           & (cmp_start[None, :] + CMP_BLOCK > sel_start[:, None])
           & (np.arange(n_pad)[None, :] < T // CMP_STRIDE - 1))
    ovl = jnp.asarray(ovl, BF16)
    pick = jnp.asarray(np.arange(LANE)[None, :] == np.arange(n_sel_pad)[:, None], BF16)
    lane_i = np.arange(LANE)
    cvec = jnp.asarray(((lane_i < n_sel) | (lane_i == n_sel_pad))[None, :], F32)
    big = -NEG
    kaug_s = jnp.asarray(np.where(np.arange(T)[:, None] // SEL_BLOCK == lane_i[None, :], big, 0.0), BF16)
    kaug_w = jnp.asarray(np.where((np.arange(T + WINDOW)[:, None] < WINDOW) & (lane_i[None, :] == n_sel_pad),
                                  big, 0.0), BF16)
    gates = proj3[:, :, cols['ng']:cols['ng'] + 3 * NSA_HEADS].reshape(B, T, G, 3 * J).transpose(0, 2, 1, 3)

    kv_spec = lambda name: pl.BlockSpec((1, T, HEAD_DIM),
                                        lambda b, g, i, o=cols[name] // HEAD_DIM: (b, 0, o + g))
    cmp_spec = pl.BlockSpec((1, 1, n_pad, HEAD_DIM), lambda b, g, i: (b, g, 0, 0))
    const2 = lambda shape: pl.BlockSpec(shape, lambda b, g, i: (0, 0))
    return pl.pallas_call(
        functools.partial(_nsa_prompt_kernel, n_sel=n_sel, n_top=n_top, T=T),
        grid=(B, G, nq),
        in_specs=[pl.BlockSpec((1, tq, J * HEAD_DIM),
                               lambda b, g, i, o=cols['nq'] // (J * HEAD_DIM): (b, i, o + g)),
                  cmp_spec, cmp_spec,
                  kv_spec('ks'), kv_spec('vs'), kv_spec('kw'), kv_spec('vw'),
                  pl.BlockSpec((1, 1, tq, 3 * J), lambda b, g, i: (b, g, i, 0)),
                  pl.BlockSpec((J, tq, n_pad), lambda b, g, i: (g, i, 0)),
                  pl.BlockSpec((J, N_BIAS_TILES, tq, tq), lambda b, g, i: (g, 0, 0, 0)),
                  const2((n_sel_pad, n_pad)), const2((n_sel_pad, LANE)), const2((1, LANE)),
                  const2((T, LANE)), const2((T + WINDOW, LANE))],
        out_specs=pl.BlockSpec((1, tq, J * HEAD_DIM), lambda b, g, i: (b, i, g)),
        out_shape=jax.ShapeDtypeStruct((B, T, NSA_WIDTH), F32),
        scratch_shapes=[pltpu.VMEM((T, 2 * HEAD_DIM), BF16), pltpu.VMEM((T, HEAD_DIM), BF16),
                        pltpu.VMEM((T + WINDOW, 2 * HEAD_DIM), BF16), pltpu.VMEM((T + WINDOW, HEAD_DIM), BF16),
                        pltpu.VMEM((nq, J * tq, tq), F32)],
        compiler_params=pltpu.CompilerParams(
            dimension_semantics=("parallel", "parallel", "arbitrary"),
            vmem_limit_bytes=V7X_VMEM_LIMIT_BYTES),
        name="nsa_prompt",
    )(proj3, kcmp, vcmp, proj3, proj3, proj3, proj3, gates, bcmp, btile, ovl, pick, cvec, kaug_s, kaug_w)


PAGE_ROWS = PAGE_SIZE * NSA_KV_HEADS
STRIDES_PER_PAGE = PAGE_SIZE // CMP_STRIDE
PAGES_PER_STEP = 8


def _cache_rows(cache):
    return cache.reshape(-1, HEAD_DIM)


def _page_specs(layer):
    return [pl.BlockSpec((2 * PAGE_ROWS, HEAD_DIM),
                         lambda b, p, pt, k=k: (pt[b, p * PAGES_PER_STEP + k] * DEPTH + layer, 0))
            for k in range(PAGES_PER_STEP)]


def _cmp_partial_kernel(pt_ref, *refs):
    page_refs = refs[:PAGES_PER_STEP]
    w1k_ref, w1v_ref, z_ref = refs[PAGES_PER_STEP:]
    G = NSA_KV_HEADS
    half = CMP_STRIDE * HEAD_DIM
    n_rows = PAGES_PER_STEP * STRIDES_PER_PAGE
    for kv, w1_ref in ((0, w1k_ref), (1, w1v_ref)):
        xs = []
        for g in range(G):
            for page_ref in page_refs:
                pieces = [page_ref[pl.ds(kv * PAGE_ROWS + s * G + g, STRIDES_PER_PAGE, stride=CMP_STRIDE * G), :]
                          for s in range(CMP_STRIDE)]
                xs.append(jnp.concatenate(pieces, axis=1))
        x = jnp.concatenate(xs, axis=0).astype(BF16)
        for j in range(CMP_BLOCK // CMP_STRIDE):
            z = jnp.dot(x, w1_ref[j * half:(j + 1) * half, :], preferred_element_type=F32)
            for g in range(G):
                z_ref[0, kv, j, g] = z[g * n_rows:(g + 1) * n_rows]


def cmp_partial(cache_rows, page_table, layer, w1_k, w1_v):
    B, n_pages = page_table.shape
    flat = CMP_BLOCK * HEAD_DIM
    n_str = n_pages * STRIDES_PER_PAGE
    assert n_pages % PAGES_PER_STEP == 0
    wspec = pl.BlockSpec((flat, HEAD_DIM), lambda b, p, pt: (0, 0))
    return pl.pallas_call(
        _cmp_partial_kernel,
        grid_spec=pltpu.PrefetchScalarGridSpec(
            num_scalar_prefetch=1,
            grid=(B, n_pages // PAGES_PER_STEP),
            in_specs=_page_specs(layer) + [wspec, wspec],
            out_specs=pl.BlockSpec((1, 2, 2, NSA_KV_HEADS, PAGES_PER_STEP * STRIDES_PER_PAGE, HEAD_DIM),
                                   lambda b, p, pt: (b, 0, 0, 0, p, 0))),
        out_shape=jax.ShapeDtypeStruct((B, 2, 2, NSA_KV_HEADS, n_str, HEAD_DIM), F32),
        compiler_params=pltpu.CompilerParams(dimension_semantics=("parallel", "arbitrary"),
                                             vmem_limit_bytes=V7X_VMEM_LIMIT_BYTES),
        name="nsa_cmp_partial",
    )(page_table, *([cache_rows] * PAGES_PER_STEP),
      w1_k.reshape(flat, HEAD_DIM).astype(BF16), w1_v.reshape(flat, HEAD_DIM).astype(BF16))


def _cmp_attn_sample_kernel(z_ref, pek_ref, w1k_ref, w2k_ref, pev_ref, w1v_ref, w2v_ref, q_ref, bias_ref,
                            ovl_ref, o_ref, psel_ref, *, q0):
    T = q_ref.shape[1]
    J = NSA_HPG
    n_str = z_ref.shape[4]

    def finish(kv, pe_ref, w1_ref, w2_ref):
        z1 = pltpu.roll(z_ref[0, kv, 1, 0], n_str - 1, 0)
        pe = jnp.broadcast_to(pe_ref[...], (PACK_ROWS, CMP_BLOCK * HEAD_DIM)).astype(BF16)
        h0 = jnp.dot(pe, w1_ref[...], preferred_element_type=F32)[0:1, :]
        h = _gelu_tanh(z_ref[0, kv, 0, 0] + z1 + h0)
        return jnp.dot(h.astype(BF16), w2_ref[...], preferred_element_type=F32).astype(BF16)

    kc = finish(0, pek_ref, w1k_ref, w2k_ref)
    vc = finish(1, pev_ref, w1v_ref, w2v_ref)
    stack = lambda f: jnp.concatenate([f(j) for j in range(J)], axis=0)
    q_all = stack(lambda j: q_ref[0, :, j * HEAD_DIM:(j + 1) * HEAD_DIM] * HEAD_DIM ** -0.5).astype(BF16)
    row = lax.broadcasted_iota(jnp.int32, (T, n_str), 0)
    lane = lax.broadcasted_iota(jnp.int32, (T, n_str), 1)
    valid1 = (q0 + row >= lane * CMP_STRIDE + (CMP_BLOCK - 1)) & (lane < n_str - 1)
    valid = stack(lambda j: valid1)
    s = jnp.where(valid, _dot_nt(q_all, kc) + stack(lambda j: bias_ref[j]), NEG)
    e = jnp.where(valid, jnp.exp(s - jnp.max(s, axis=-1, keepdims=True)), 0.0)
    den = jnp.sum(e, axis=-1, keepdims=True)
    p = e / jnp.where(den > 0.0, den, 1.0)
    o_ref[0, 0] = jnp.dot(p.astype(BF16), vc, preferred_element_type=F32)
    psum = p[0:T]
    for j in range(1, J):
        psum = psum + p[j * T:(j + 1) * T]
    psum = _pad_rows(psum, PACK_ROWS)
    hi = psum.astype(BF16)
    r1 = psum - hi.astype(F32)
    mid = r1.astype(BF16)
    lo = (r1 - mid.astype(F32)).astype(BF16)
    ovl = ovl_ref[...]
    p_sel = (jnp.dot(hi, ovl, preferred_element_type=F32) + jnp.dot(mid, ovl, preferred_element_type=F32)
             + jnp.dot(lo, ovl, preferred_element_type=F32))
    psel_ref[0, 0] = p_sel[:T]


def _rank_select_kernel(psel_ref, tpos_ref, out_ref, score_ref, rank_ref, *, n_sel, n_top):
    shape = psel_ref.shape
    blk = lax.broadcasted_iota(jnp.int32, shape, 0)
    cur = jnp.broadcast_to(tpos_ref[...], shape) // SEL_BLOCK
    forced = (blk == 0) | (blk == cur) | (blk == cur - 1)
    ok = (blk <= cur) & (blk < n_sel)
    score = jnp.where(ok, psel_ref[...] + jnp.where(forced, FORCE_SCORE, 0.0), NEG)
    score_ref[...] = score
    rank_ref[...] = jnp.zeros(shape, F32)

    def body(jb, c):
        other = jnp.broadcast_to(score_ref[pl.ds(jb, 1), :], shape)
        beats = (other > score) | ((other == score) & (blk > jb))
        rank_ref[...] = rank_ref[...] + jnp.where(beats, 1.0, 0.0)
        return c

    lax.fori_loop(0, n_sel, body, 0)
    out_ref[...] = jnp.where((rank_ref[...] < n_top) & ok, 0.0, NEG)


def _slc_win_sample_kernel(pt_ref, *refs, win_buf):
    page_refs = refs[:PAGES_PER_STEP]
    (qT_ref, seladd_ref, blast_ref, selnew_ref, ksn_ref, vsn_ref, kwn_ref, vwn_ref, bnew_ref, cw_ref, bwin_ref,
     ocmp_ref, gate_ref, o_ref, m_ref, l_ref, acc_ref) = refs[PAGES_PER_STEP:]
    G = NSA_KV_HEADS
    p = pl.program_id(1)
    last = pl.num_programs(1) - 1
    lane_group = lax.broadcasted_iota(jnp.int32, (1, LANE), 1) // (LANE // G)

    @pl.when(p == 0)
    def _():
        m_ref[...] = jnp.full(m_ref.shape, NEG, F32)
        l_ref[...] = jnp.zeros(l_ref.shape, F32)
        acc_ref[...] = jnp.zeros(acc_ref.shape, F32)

    def scores(k_of_g):
        s = None
        for g in range(G):
            sg = jnp.dot(k_of_g(g).astype(BF16), qT_ref[0, g], preferred_element_type=F32)
            s = sg if s is None else s + sg
        return s

    def weighted_values(pT, v_of_g):
        o = None
        for g in range(G):
            pg = jnp.where(lane_group == g, pT, 0.0).astype(BF16)
            og = _dot_tn(v_of_g(g).astype(BF16), pg)
            o = og if o is None else o + og
        return o

    def accumulate(segments):
        m_old = m_ref[0:1, :]
        m_new = m_old
        for sT, _ in segments:
            m_new = jnp.maximum(m_new, jnp.max(sT, axis=0, keepdims=True))
        alpha = jnp.exp(m_old - m_new)
        l_new = alpha * l_ref[0:1, :]
        acc = alpha * acc_ref[...]
        for sT, v_of_g in segments:
            pT = jnp.exp(sT - m_new)
            l_new = l_new + jnp.sum(pT, axis=0, keepdims=True)
            acc = acc + weighted_values(pT, v_of_g)
        l_ref[...] = jnp.broadcast_to(l_new, l_ref.shape)
        acc_ref[...] = acc
        m_ref[...] = jnp.broadcast_to(m_new, m_ref.shape)

    key = lax.broadcasted_iota(jnp.int32, (PAGE_SIZE, LANE), 0)
    blocks_per_page = PAGE_SIZE // SEL_BLOCK
    is_last = (p == last).astype(F32)
    segments = []
    for k, page_ref in enumerate(page_refs):
        k_page = lambda g, r=page_ref: r[pl.ds(g, PAGE_SIZE, stride=G), :]
        v_page = lambda g, r=page_ref: r[pl.ds(PAGE_ROWS + g, PAGE_SIZE, stride=G), :]
        mask = seladd_ref[0, k, blocks_per_page - 1:blocks_per_page, :]
        for i in range(blocks_per_page - 2, -1, -1):
            mask = jnp.where(key < (i + 1) * SEL_BLOCK, seladd_ref[0, k, i:i + 1, :], mask)
        sT = scores(k_page) + mask
        if k == PAGES_PER_STEP - 1:
            sT = sT + blast_ref[...] * is_last
        segments.append((sT, v_page))
    accumulate(segments)

    @pl.when(p == last)
    def _():
        new = lambda ref: (lambda g: _pad_rows(ref[0, :, g * HEAD_DIM:(g + 1) * HEAD_DIM], PACK_ROWS))
        accumulate([(scores(new(ksn_ref)) + bnew_ref[...] + selnew_ref[0], new(vsn_ref))])
        o_slc = (acc_ref[...] / l_ref[0:1, :]).T

        k_win = lambda g: cw_ref[pl.ds(g, win_buf, stride=G), :]
        v_win = lambda g: cw_ref[pl.ds(win_buf * G + g, win_buf, stride=G), :]
        s_w = scores(k_win) + bwin_ref[...]
        s_n = scores(new(kwn_ref)) + bnew_ref[...]
        m = jnp.maximum(jnp.max(s_w, axis=0, keepdims=True), jnp.max(s_n, axis=0, keepdims=True))
        p_w = jnp.exp(s_w - m)
        p_n = jnp.exp(s_n - m)
        den = jnp.sum(p_w, axis=0, keepdims=True) + jnp.sum(p_n, axis=0, keepdims=True)
        o_win = ((weighted_values(p_w, v_win) + weighted_values(p_n, new(vwn_ref))) / den).T

        gates = jax.nn.sigmoid(gate_ref[0])
        o_ref[0] = gates[:, 0:1] * ocmp_ref[0] + gates[:, 1:2] * o_slc + gates[:, 2:3] * o_win


def nsa_sample(proj3, cols, cache_cmp, cache_slc, cache_win, page_table, layer, rel_bias,
               pe_k, w1_k, w2_k, pe_v, w1_v, w2_v):
    B, T, _ = proj3.shape
    G, J, H = NSA_KV_HEADS, NSA_HPG, NSA_HEADS
    n_pages = page_table.shape[1]
    past = n_pages * PAGE_SIZE
    q0 = past
    win_buf = cache_win.shape[3]
    L = G * J * T
    assert L == LANE and T <= PACK_ROWS and T < CMP_STRIDE and past % SEL_BLOCK == 0 and T <= SEL_BLOCK
    assert win_buf == min(WINDOW, past)
    n_str = past // CMP_STRIDE
    n_sel = past // SEL_BLOCK + 1
    n_top = min(SEL_TOP, n_sel)
    n_sel_rows = _round_up(n_sel, 8)
    n_sel_lanes = _round_up(n_sel, LANE)
    flat = CMP_BLOCK * HEAD_DIM
    b_far = rel_bias.astype(F32)[_bucket_table()[MAX_DISTANCE - 1]]

    def lane_bias(dist, ok):
        b = _bias_of_distance(rel_bias, np.maximum(dist, 0)) - b_far[:, None, None]
        b = jnp.where(jnp.asarray(ok)[None], b, NEG)
        return jnp.moveaxis(b.reshape(G, J, dist.shape[0], T), 2, 0).reshape(dist.shape[0], L)

    ti = np.arange(T)[None, :]
    z = cmp_partial(_cache_rows(cache_cmp), page_table, layer, w1_k, w1_v)
    n = np.arange(n_str)[None, :]
    dist_c = q0 + np.arange(T)[:, None] - (n * CMP_STRIDE + CMP_BLOCK - 1)
    near = dist_c.min(axis=0) < MAX_DISTANCE
    n_far = int(np.argmax(near)) if near.any() else n_str
    bcmp = jnp.concatenate([jnp.broadcast_to(b_far[:, None, None], (H, T, n_far)),
                            _bias_of_distance(rel_bias, np.maximum(dist_c[:, n_far:], 0))], axis=2)
    cmp_start = np.arange(n_str) * CMP_STRIDE
    sel_start = np.arange(n_sel_lanes) * SEL_BLOCK
    ovl = ((cmp_start[:, None] < sel_start[None, :] + SEL_BLOCK) & (cmp_start[:, None] + CMP_BLOCK > sel_start[None, :])
           & (np.arange(n_str)[:, None] < n_str - 1) & (np.arange(n_sel_lanes)[None, :] < n_sel))
    wspec = [pl.BlockSpec((1, flat), lambda b, g: (0, 0)),
             pl.BlockSpec((flat, HEAD_DIM), lambda b, g: (0, 0)),
             pl.BlockSpec((HEAD_DIM, HEAD_DIM), lambda b, g: (0, 0))]
    o_cmp, p_sel = pl.pallas_call(
        functools.partial(_cmp_attn_sample_kernel, q0=q0),
        grid=(B, G),
        in_specs=[pl.BlockSpec((1, 2, 2, 1, n_str, HEAD_DIM), lambda b, g: (b, 0, 0, g, 0, 0))] + wspec + wspec + [
            pl.BlockSpec((1, T, J * HEAD_DIM), lambda b, g, o=cols['nq'] // (J * HEAD_DIM): (b, 0, o + g)),
            pl.BlockSpec((J, T, n_str), lambda b, g: (g, 0, 0)),
            pl.BlockSpec((n_str, n_sel_lanes), lambda b, g: (0, 0))],
        out_specs=(pl.BlockSpec((1, 1, J * T, HEAD_DIM), lambda b, g: (b, g, 0, 0)),
                   pl.BlockSpec((1, 1, T, n_sel_lanes), lambda b, g: (b, g, 0, 0))),
        out_shape=(jax.ShapeDtypeStruct((B, G, J * T, HEAD_DIM), F32),
                   jax.ShapeDtypeStruct((B, G, T, n_sel_lanes), F32)),
        compiler_params=pltpu.CompilerParams(dimension_semantics=("parallel", "parallel"),
                                             vmem_limit_bytes=V7X_VMEM_LIMIT_BYTES),
        name="nsa_cmp_attn_sample",
    )(z, pe_k.reshape(1, flat), w1_k.reshape(flat, HEAD_DIM).astype(BF16), w2_k.astype(BF16),
      pe_v.reshape(1, flat), w1_v.reshape(flat, HEAD_DIM).astype(BF16), w2_v.astype(BF16),
      proj3, bcmp, jnp.asarray(ovl, BF16))

    n_bgt = B * G * T
    psel_t = p_sel.reshape(n_bgt, n_sel_lanes)[:, :n_sel_rows].T
    tpos = jnp.asarray(np.tile(q0 + np.arange(T), B * G)[None, :], jnp.int32)
    seladd = pl.pallas_call(
        functools.partial(_rank_select_kernel, n_sel=n_sel, n_top=n_top),
        out_shape=jax.ShapeDtypeStruct((n_sel_rows, n_bgt), F32),
        scratch_shapes=[pltpu.VMEM((n_sel_rows, n_bgt), F32), pltpu.VMEM((n_sel_rows, n_bgt), F32)],
        name="nsa_rank_select",
    )(psel_t, tpos)
    seladd = seladd.T.reshape(B, G, 1, T, n_sel_rows)
    seladd = jnp.broadcast_to(seladd, (B, G, J, T, n_sel_rows)).reshape(B, L, n_sel_rows)
    bpp = PAGE_SIZE // SEL_BLOCK
    sel_past = seladd[:, :, :n_sel - 1].reshape(B, L, n_pages, bpp).transpose(0, 2, 3, 1)
    sel_new = seladd[:, :, n_sel - 1].reshape(B, 1, L)

    q = proj3[:, :, cols['nq']:cols['nq'] + NSA_WIDTH].reshape(B, T, G, J, HEAD_DIM) * HEAD_DIM ** -0.5
    q_t = q.transpose(0, 2, 4, 3, 1).reshape(B, G, HEAD_DIM, J * T)
    place = jnp.asarray(np.arange(G)[:, None, None] == (np.arange(L) // (J * T))[None, None, :])
    q_pad = jnp.where(place[None], jnp.tile(q_t, (1, 1, 1, G)), 0.0).astype(BF16)
    ki = np.arange(PAGE_SIZE)[:, None]
    b_last = lane_bias(PAGE_SIZE + ti - ki, np.ones((PAGE_SIZE, T), bool))
    kn = np.arange(PACK_ROWS)[:, None]
    b_new = lane_bias(ti - kn, (ti - kn >= 0) & (kn < T))
    kw = np.arange(win_buf)[:, None]
    b_win = lane_bias(win_buf + ti - kw, win_buf + ti - kw <= WINDOW)
    gates = proj3[:, :, cols['ng']:cols['ng'] + 3 * H].reshape(B, T, G, J, 3).transpose(0, 2, 3, 1, 4).reshape(B, L, 3)
    win_rows = 2 * win_buf * G
    new_spec = lambda name: pl.BlockSpec((1, T, KV_WIDTH), lambda b, p, pt, o=cols[name] // KV_WIDTH: (b, 0, o))
    const = lambda shape: pl.BlockSpec(shape, lambda b, p, pt: (0, 0))
    per_b = lambda shape: pl.BlockSpec((1,) + shape, lambda b, p, pt: (b,) + (0,) * len(shape))
    out = pl.pallas_call(
        functools.partial(_slc_win_sample_kernel, win_buf=win_buf),
        grid_spec=pltpu.PrefetchScalarGridSpec(
            num_scalar_prefetch=1,
            grid=(B, n_pages // PAGES_PER_STEP),
            in_specs=_page_specs(layer) + [
                per_b((G, HEAD_DIM, L)),
                pl.BlockSpec((1, PAGES_PER_STEP, bpp, L), lambda b, p, pt: (b, p, 0, 0)),
                const((PAGE_SIZE, L)), per_b((1, L)),
                new_spec('ks'), new_spec('vs'), new_spec('kw'), new_spec('vw'),
                const((PACK_ROWS, L)),
                pl.BlockSpec((win_rows, HEAD_DIM), lambda b, p, pt: (layer * B + b, 0)),
                const((win_buf, L)), per_b((L, HEAD_DIM)), per_b((L, 3))],
            out_specs=per_b((L, HEAD_DIM)),
            scratch_shapes=[pltpu.VMEM((8, L), F32), pltpu.VMEM((8, L), F32), pltpu.VMEM((HEAD_DIM, L), F32)]),
        out_shape=jax.ShapeDtypeStruct((B, L, HEAD_DIM), F32),
        compiler_params=pltpu.CompilerParams(dimension_semantics=("parallel", "arbitrary"),
                                             vmem_limit_bytes=V7X_VMEM_LIMIT_BYTES),
        name="nsa_slc_win_sample",
    )(page_table, *([_cache_rows(cache_slc)] * PAGES_PER_STEP), q_pad, sel_past, b_last, sel_new,
      proj3, proj3, proj3, proj3, b_new,
      cache_win.reshape(-1, HEAD_DIM), b_win, o_cmp.reshape(B, L, HEAD_DIM), gates)
    return out.reshape(B, G, J, T, HEAD_DIM).transpose(0, 3, 1, 2, 4).reshape(B, T, NSA_WIDTH)


def _retention_tables(T, q0):
    c = _largest_divisor(T, RET_CHUNK)
    cp = max(c, RET_CHUNK)
    lg = np.log1p(-(2.0 ** (-5.0 - np.arange(RET_HEADS, dtype=np.float32)))).astype(np.float32)
    i = np.arange(cp)
    rel = i[:, None] - i[None, :]
    inside = (i < c)[:, None] & (i < c)[None, :]
    decay = np.where((rel >= 0) & inside, np.exp(np.maximum(rel, 0)[None] * lg[:, None, None]), 0.0)
    q_dec = np.broadcast_to(np.exp((i + 1)[None, :, None] * lg[:, None, None]), (RET_HEADS, cp, HEAD_DIM))
    k_dec = np.where((i < c)[None, :, None], np.exp((c - 1 - i)[None, :, None] * lg[:, None, None]), 0.0)
    k_dec = np.broadcast_to(k_dec, (RET_HEADS, cp, HEAD_DIM))
    chunk_dec = np.broadcast_to(np.exp(c * lg)[:, None, None], (RET_HEADS, 8, HEAD_DIM))
    half = HEAD_DIM // 2
    inv = (1.0 / (10000.0 ** np.linspace(0.0, 1.0, half, dtype=np.float32))).astype(np.float32)
    ang = (q0 + np.arange(T)).astype(np.float32)[:, None] * inv[None]
    cos, sin = np.cos(ang), np.sin(ang)
    cosf = np.concatenate([cos, cos], axis=1)
    sinf = np.concatenate([-sin, sin], axis=1)
    f = lambda a: jnp.asarray(a, F32)
    return c, cp, f(decay), f(q_dec), f(k_dec), f(chunk_dec), f(cosf), f(sinf)


RET_HEADS_PER_STEP = 2


def _retention_kernel(q_ref, k_ref, v_ref, g_ref, s0_ref, cos_ref, sin_ref, dec_ref, qd_ref, kd_ref, cd_ref,
                      o_ref, s_ref, *, c, cp, n):
    half = HEAD_DIM // 2
    heads = range(RET_HEADS_PER_STEP)

    def load(ref, rows):
        return _pad_rows(ref[rows, :], cp)

    def rot(x, cos, sin):
        return x * cos + pltpu.roll(x, half, 1) * sin

    def body(i, states):
        rows = pl.ds(pl.multiple_of(i * c, c), c)
        cos, sin = load(cos_ref, rows), load(sin_ref, rows)
        q_all, k_all, v_all = load(q_ref.at[0], rows), load(k_ref.at[0], rows), load(v_ref.at[0], rows)
        new_states = []
        for hh in heads:
            cols = slice(hh * HEAD_DIM, (hh + 1) * HEAD_DIM)
            s = states[hh]
            q = rot(q_all[:, cols], cos, sin)
            k = rot(k_all[:, cols], cos, sin) * HEAD_DIM ** -0.5
            v = v_all[:, cols].astype(BF16)
            qb = q.astype(BF16)
            inner = _dot_nt(qb, k.astype(BF16)) * dec_ref[hh]
            o = (jnp.dot(inner.astype(BF16), v, preferred_element_type=F32)
                 + jnp.dot(qb, s.astype(BF16), preferred_element_type=F32) * qd_ref[hh])
            new_states.append(s * cd_ref[hh, 0:1, :] + _dot_tn((k * kd_ref[hh]).astype(BF16), v))
            o = o * lax.rsqrt(jnp.mean(o * o, axis=-1, keepdims=True) + EPS)
            g = g_ref[0, rows, cols]
            o_ref[0, rows, cols] = g * jax.nn.sigmoid(g) * o[:c]
        return tuple(new_states)

    final = lax.fori_loop(0, n, body, tuple(s0_ref[0, hh] for hh in heads))
    for hh in heads:
        s_ref[0, hh] = final[hh]


def retention(proj3, cols, s0, q0):
    B, T, _ = proj3.shape
    c, cp, decay, q_dec, k_dec, chunk_dec, cosf, sinf = _retention_tables(T, q0)
    hp = RET_HEADS_PER_STEP
    wide = hp * HEAD_DIM
    col = lambda name: pl.BlockSpec((1, T, wide), lambda b, h, o=cols[name] // wide: (b, 0, o + h))
    tab = lambda r: pl.BlockSpec((hp, r, HEAD_DIM), lambda b, h: (h, 0, 0))
    full = pl.BlockSpec((T, HEAD_DIM), lambda b, h: (0, 0))
    state = pl.BlockSpec((1, hp, HEAD_DIM, HEAD_DIM), lambda b, h: (b, h, 0, 0))
    return pl.pallas_call(
        functools.partial(_retention_kernel, c=c, cp=cp, n=T // c),
        grid=(B, RET_HEADS // hp),
        in_specs=[col('rq'), col('rk'), col('rv'), col('rg'), state, full, full,
                  tab(cp), tab(cp), tab(cp), tab(8)],
        out_specs=(pl.BlockSpec((1, T, wide), lambda b, h: (b, 0, h)), state),
        out_shape=(jax.ShapeDtypeStruct((B, T, RET_WIDTH), F32),
                   jax.ShapeDtypeStruct((B, RET_HEADS, HEAD_DIM, HEAD_DIM), F32)),
        compiler_params=pltpu.CompilerParams(dimension_semantics=("parallel", "parallel"),
                                             vmem_limit_bytes=V7X_VMEM_LIMIT_BYTES),
        name="retention",
    )(proj3, proj3, proj3, proj3, s0.astype(F32), cosf, sinf, decay, q_dec, k_dec, chunk_dec)


S5_BLK_GROUPS = 8
S5_BLK_STATE = S5_BLK_GROUPS * S5_STATE
S5_BLK_CH = S5_BLK_GROUPS * S5_GROUP
S5_SCAN_ROWS = 8


def _s5_params(lam_re, lam_im, log_step, b_re, b_im, c_re, c_im):
    nb = S5_GROUPS // S5_BLK_GROUPS
    lam = lax.complex(lam_re.astype(F32), lam_im.astype(F32))
    step = jnp.exp(log_step.astype(F32))[:, None]
    a_bar = jnp.exp(lam * step)
    b_bar = ((a_bar - 1.0) / lam)[..., None] * lax.complex(b_re.astype(F32), b_im.astype(F32))
    r = np.arange(S5_SCAN_ROWS)

    def powers(k, keep):
        p = jnp.exp(lam[None] * step[None] * jnp.asarray(k, F32)[:, None, None])
        return jnp.where(jnp.asarray(keep)[:, None, None], p, 0.0)

    tabs = [powers(np.full(S5_SCAN_ROWS, k), r >= k) for k in (1, 2, 4)]
    tabs.append(powers(r + 1, r >= 0))
    tab = jnp.stack(tabs)
    tab = tab.reshape(4, S5_SCAN_ROWS, nb, S5_BLK_STATE).transpose(2, 0, 1, 3)
    atab = jnp.concatenate([tab.real, tab.imag], axis=1)

    eye = jnp.eye(S5_BLK_GROUPS, dtype=F32)
    bb = b_bar.reshape(nb, S5_BLK_GROUPS, S5_STATE, S5_GROUP)

    def in_mat(x):
        return jnp.einsum('ngpc,gh->ngchp', x, eye).reshape(nb, S5_BLK_CH, S5_BLK_STATE)

    bmat = jnp.concatenate([in_mat(bb.real), in_mat(bb.imag)], axis=-1).astype(BF16)
    cr = c_re.astype(F32).reshape(nb, S5_BLK_GROUPS, S5_GROUP, S5_STATE)
    ci = c_im.astype(F32).reshape(nb, S5_BLK_GROUPS, S5_GROUP, S5_STATE)

    def out_mat(x):
        return jnp.einsum('ngcp,gh->ngphc', x, eye).reshape(nb, S5_BLK_STATE, S5_BLK_CH)

    cmat = jnp.concatenate([out_mat(cr), -out_mat(ci)], axis=1).astype(BF16)
    return atab, bmat, cmat


def _s5_scan_tile(xr, xi, cr, ci, atab_ref):
    for idx, k in enumerate((1, 2, 4)):
        pr, pi = atab_ref[0, idx], atab_ref[0, 4 + idx]
        sr, si = pltpu.roll(xr, k, 0), pltpu.roll(xi, k, 0)
        xr, xi = xr + pr * sr - pi * si, xi + pr * si + pi * sr
    pr, pi = atab_ref[0, 3], atab_ref[0, 7]
    xr, xi = xr + pr * cr - pi * ci, xi + pr * ci + pi * cr
    last = S5_SCAN_ROWS - 1
    cr = jnp.broadcast_to(xr[last:last + 1, :], xr.shape)
    ci = jnp.broadcast_to(xi[last:last + 1, :], xi.shape)
    return xr, xi, cr, ci


def _s5_kernel(u_ref, x0_ref, atab_ref, b_ref, c_ref, d_ref, y_ref, st_ref, xs_ref, *, T):
    u = u_ref[0]
    t_pad = _round_up(T, PACK_ROWS)
    xs_ref[...] = jnp.dot(_pad_rows(u, t_pad).astype(BF16), b_ref[0], preferred_element_type=F32)[:T]
    n = S5_BLK_STATE
    R = S5_SCAN_ROWS

    def body(i, carry):
        cr, ci = carry
        rows = pl.ds(pl.multiple_of(i * R, R), R)
        xr, xi, cr, ci = _s5_scan_tile(xs_ref[rows, :n], xs_ref[rows, n:], cr, ci, atab_ref)
        xs_ref[rows, :n] = xr
        xs_ref[rows, n:] = xi
        return cr, ci

    x0 = x0_ref[0, 0]
    cr0 = jnp.broadcast_to(x0[0:1, :], (R, n))
    ci0 = jnp.broadcast_to(x0[1:2, :], (R, n))
    cr, ci = lax.fori_loop(0, T // R, body, (cr0, ci0))
    st_ref[0, 0] = jnp.concatenate([cr[0:1], ci[0:1]], axis=0)
    y = jnp.dot(_pad_rows(xs_ref[...], t_pad).astype(BF16), c_ref[0], preferred_element_type=F32)[:T]
    y_ref[0] = _gelu_tanh(y + d_ref[...] * u)


def s5_scan(proj3, cols, x0, lam_re, lam_im, log_step, b_re, b_im, c_re, c_im, d):
    B, T, _ = proj3.shape
    assert T % S5_SCAN_ROWS == 0
    nb = S5_GROUPS // S5_BLK_GROUPS
    atab, bmat, cmat = _s5_params(lam_re, lam_im, log_step, b_re, b_im, c_re, c_im)
    x0b = x0.astype(F32).reshape(B, nb, S5_BLK_STATE, 2).transpose(0, 1, 3, 2)
    blk3 = lambda shape: pl.BlockSpec((1,) + shape, lambda b, j: (j, 0, 0))
    y, st = pl.pallas_call(
        functools.partial(_s5_kernel, T=T),
        grid=(B, nb),
        in_specs=[pl.BlockSpec((1, T, S5_BLK_CH), lambda b, j, o=cols['su'] // S5_BLK_CH: (b, 0, o + j)),
                  pl.BlockSpec((1, 1, 2, S5_BLK_STATE), lambda b, j: (b, j, 0, 0)),
                  pl.BlockSpec((1, 8, S5_SCAN_ROWS, S5_BLK_STATE), lambda b, j: (j, 0, 0, 0)),
                  blk3((S5_BLK_CH, 2 * S5_BLK_STATE)), blk3((2 * S5_BLK_STATE, S5_BLK_CH)),
                  pl.BlockSpec((1, S5_BLK_CH), lambda b, j: (0, j))],
        out_specs=(pl.BlockSpec((1, T, S5_BLK_CH), lambda b, j: (b, 0, j)),
                   pl.BlockSpec((1, 1, 2, S5_BLK_STATE), lambda b, j: (b, j, 0, 0))),
        out_shape=(jax.ShapeDtypeStruct((B, T, S5_WIDTH), F32),
                   jax.ShapeDtypeStruct((B, nb, 2, S5_BLK_STATE), F32)),
        scratch_shapes=[pltpu.VMEM((T, 2 * S5_BLK_STATE), F32)],
        compiler_params=pltpu.CompilerParams(dimension_semantics=("parallel", "parallel"),
                                             vmem_limit_bytes=V7X_VMEM_LIMIT_BYTES),
        name="s5_scan",
    )(proj3, x0b, atab, bmat, cmat, d.astype(F32).reshape(1, S5_WIDTH))
    st = st.transpose(0, 1, 3, 2).reshape(B, S5_GROUPS, S5_STATE, 2)
    return y, st


def _branch_norm_kernel(ro_ref, so_ref, no_ref, bn_ref, o_ref):
    off = 0
    for ref in (ro_ref, so_ref, no_ref):
        x = ref[...]
        w = x.shape[-1]
        y = x * lax.rsqrt(jnp.mean(x * x, axis=-1, keepdims=True) + EPS) * bn_ref[:, off:off + w]
        o_ref[:, off:off + w] = y.astype(o_ref.dtype)
        off += w


def branch_norm(ro, so, no, bn):
    m = ro.shape[0]
    tm = min(m, 256)
    assert m % tm == 0
    spec = lambda w: pl.BlockSpec((tm, w), lambda i: (i, 0))
    return pl.pallas_call(
        _branch_norm_kernel,
        grid=(m // tm,),
        in_specs=[spec(RET_WIDTH), spec(S5_WIDTH), spec(NSA_WIDTH), pl.BlockSpec((1, D_MODEL), lambda i: (0, 0))],
        out_specs=spec(D_MODEL),
        out_shape=jax.ShapeDtypeStruct((m, D_MODEL), BF16),
        compiler_params=pltpu.CompilerParams(dimension_semantics=("parallel",),
                                             vmem_limit_bytes=V7X_VMEM_LIMIT_BYTES),
        name="branch_norm",
    )(ro, so, no, bn.astype(F32).reshape(1, D_MODEL))


def _kv_rows_kernel(x_ref, o_ref):
    x = x_ref[0]
    for g in range(NSA_KV_HEADS):
        o_ref[pl.ds(g, x.shape[0], stride=NSA_KV_HEADS), :] = x[:, g * HEAD_DIM:(g + 1) * HEAD_DIM]


def kv_rows(proj3, col_k, t_start, t_len):
    B = proj3.shape[0]
    tT = min(t_len, 256)
    assert t_len % tT == 0 and t_start % tT == 0 and col_k % KV_WIDTH == 0
    n = t_len // tT
    rows = pl.pallas_call(
        _kv_rows_kernel,
        grid=(B, 2, n),
        in_specs=[pl.BlockSpec((1, tT, KV_WIDTH), lambda b, kv, i: (b, t_start // tT + i, col_k // KV_WIDTH + kv))],
        out_specs=pl.BlockSpec((tT * NSA_KV_HEADS, HEAD_DIM), lambda b, kv, i: ((b * 2 + kv) * n + i, 0)),
        out_shape=jax.ShapeDtypeStruct((B * 2 * t_len * NSA_KV_HEADS, HEAD_DIM), F32),
        compiler_params=pltpu.CompilerParams(dimension_semantics=("parallel", "parallel", "parallel"),
                                             vmem_limit_bytes=V7X_VMEM_LIMIT_BYTES),
        name="kv_rows",
    )(proj3)
    return rows.reshape(B, 2, t_len, NSA_KV_HEADS, HEAD_DIM)


_COL_NAMES = ('rq', 'rk', 'rv', 'rg', 'su', 'nq', 'kc', 'vc', 'ks', 'vs', 'kw', 'vw', 'ng')
COLS = {name: int(off) for name, off in zip(_COL_NAMES, np.concatenate([[0], np.cumsum(IN_SPLITS)]))}


def _block(x, layer, w, rel_bias, past, win_buf):
    B, T, _ = x.shape
    G = NSA_KV_HEADS
    M = B * T
    x2 = x.reshape(M, D_MODEL)
    h = rmsnorm(x2, w['norm_mix'][layer], BF16)
    proj3 = matmul(h, w['w_in'], layer).reshape(B, T, -1)
    s5_w = [w[k][layer] for k in ('s5_lambda_re', 's5_lambda_im', 's5_log_step', 's5_b_re', 's5_b_im',
                                  's5_c_re', 's5_c_im', 's5_d')]
    cmp_w = [w[k][layer] for k in ('cmp_pe_k', 'cmp_w1_k', 'cmp_w2_k', 'cmp_pe_v', 'cmp_w1_v', 'cmp_w2_v')]
    if past is None:
        q0 = 0
        ret_s0 = jnp.zeros((B, RET_HEADS, HEAD_DIM, HEAD_DIM), F32)
        s5_s0 = jnp.zeros((B, S5_GROUPS, S5_STATE, 2), F32)
        win_prev = jnp.zeros((B, 2, WINDOW, G, HEAD_DIM), x.dtype)
    else:
        cache_cmp, cache_slc, cache_win, state_ret, state_s5, page_table = past
        q0 = page_table.shape[1] * PAGE_SIZE
        ret_s0, s5_s0, win_prev = state_ret[layer], state_s5[layer], cache_win[layer]

    ro, ret_s = retention(proj3, COLS, ret_s0, q0)
    sy, s5_s = s5_scan(proj3, COLS, s5_s0, *s5_w)
    if past is None:
        kcmp, vcmp = compress_prompt(proj3, COLS['kc'], COLS['vc'], *cmp_w)
        no = nsa_prompt(proj3, COLS, kcmp, vcmp, rel_bias)
    else:
        no = nsa_sample(proj3, COLS, cache_cmp, cache_slc, cache_win, page_table, layer, rel_bias, *cmp_w)

    sy2 = sy.reshape(M, S5_WIDTH)
    so = matmul(sy2, w['s5_w_glu'], layer, res=sy2, act="glu")
    mix = branch_norm(ro.reshape(M, RET_WIDTH), so, no.reshape(M, NSA_WIDTH), w['branch_norm'][layer])
    x2 = matmul(mix, w['w_out'], layer, res=x2)
    h = rmsnorm(x2, w['norm_ffn'][layer], BF16)
    up = matmul(h, w['w_up'], layer, act="relu2", out_dtype=BF16)
    x2 = matmul(up, w['w_down'], layer, res=x2)

    cmp_rows = kv_rows(proj3, COLS['kc'], 0, T)
    slc_rows = kv_rows(proj3, COLS['ks'], 0, T)
    if T >= win_buf:
        win_new = kv_rows(proj3, COLS['kw'], T - win_buf, win_buf)
    else:
        win_new = jnp.concatenate([win_prev, kv_rows(proj3, COLS['kw'], 0, T)], axis=2)[:, :, -win_buf:]
    return x2.reshape(B, T, D_MODEL), cmp_rows, slc_rows, win_new, ret_s, s5_s


def kernel(x_prompt, x_sample, cache_cmp, cache_slc, cache_win, state_ret, state_s5, page_table,
           rel_bias, norm_mix, w_in, s5_lambda_re, s5_lambda_im, s5_log_step, s5_b_re, s5_b_im,
           s5_c_re, s5_c_im, s5_d, s5_w_glu, cmp_pe_k, cmp_w1_k, cmp_w2_k, cmp_pe_v, cmp_w1_v,
           cmp_w2_v, branch_norm, w_out, norm_ffn, w_up, w_down, norm_final):
    win_buf = cache_win.shape[3]
    in_pad = _round_up(IN_COLS, 512) - IN_COLS
    w = dict(
        norm_mix=norm_mix, norm_ffn=norm_ffn, branch_norm=branch_norm,
        w_in=jnp.pad(w_in.astype(BF16), ((0, 0), (0, 0), (0, in_pad))),
        w_out=w_out.astype(BF16), w_up=w_up.astype(BF16), w_down=w_down.astype(BF16),
        s5_w_glu=s5_w_glu.astype(BF16),
        s5_lambda_re=s5_lambda_re, s5_lambda_im=s5_lambda_im, s5_log_step=s5_log_step, s5_b_re=s5_b_re,
        s5_b_im=s5_b_im, s5_c_re=s5_c_re, s5_c_im=s5_c_im, s5_d=s5_d,
        cmp_pe_k=cmp_pe_k, cmp_w1_k=cmp_w1_k, cmp_w2_k=cmp_w2_k,
        cmp_pe_v=cmp_pe_v, cmp_w1_v=cmp_w1_v, cmp_w2_v=cmp_w2_v)
    past = (cache_cmp, cache_slc, cache_win, state_ret, state_s5, page_table)
    xp, xs = x_prompt, x_sample
    written_p, written_s = [], []
    for layer in range(DEPTH):
        xp, *entries = _block(xp, layer, w, rel_bias, None, win_buf)
        written_p.append(entries)
        xs, *entries = _block(xs, layer, w, rel_bias, past, win_buf)
        written_s.append(entries)
    y_prompt = rmsnorm(xp.reshape(-1, D_MODEL), norm_final, F32).reshape(xp.shape)
    y_sample = rmsnorm(xs.reshape(-1, D_MODEL), norm_final, F32).reshape(xs.shape)
    stacked = lambda written, i, axis: jnp.stack([entries[i] for entries in written], axis=axis)
    return (y_prompt, y_sample,
            stacked(written_p, 0, 1), stacked(written_s, 0, 1),
            stacked(written_p, 1, 1), stacked(written_s, 1, 1),
            stacked(written_p, 2, 0), stacked(written_s, 2, 0),
            stacked(written_p, 3, 0), stacked(written_s, 3, 0),
            stacked(written_p, 4, 0), stacked(written_s, 4, 0))
```

```python
import functools
import math

import jax
import jax.numpy as jnp
import numpy as np
from jax import lax
from jax.experimental import pallas as pl
from jax.experimental.pallas import tpu as pltpu

F32 = jnp.float32
BF16 = jnp.bfloat16

D_MODEL = 4096
DEPTH = 2
PAGE_SIZE = 128
HEAD_DIM = 128
RET_WIDTH = 1024
RET_HEADS = 8
RET_CHUNK = 128
S5_WIDTH = 1024
S5_GROUP = 16
S5_GROUPS = 64
S5_STATE = 64
NSA_WIDTH = 2048
NSA_HEADS = 16
NSA_KV_HEADS = 4
NSA_HPG = 4
KV_WIDTH = 512
CMP_BLOCK = 32
CMP_STRIDE = 16
SEL_BLOCK = 64
SEL_TOP = 16
WINDOW = 512
FORCE_SCORE = 1e4
NEG = -1e30
N_BUCKETS = 32
MAX_DISTANCE = 128
EPS = 1e-6
IN_SPLITS = (RET_WIDTH, RET_WIDTH, RET_WIDTH, RET_WIDTH, S5_WIDTH, NSA_WIDTH,
             KV_WIDTH, KV_WIDTH, KV_WIDTH, KV_WIDTH, KV_WIDTH, KV_WIDTH, 3 * NSA_HEADS)
IN_COLS = sum(IN_SPLITS)

V7X_VMEM_LIMIT_BYTES = 48 * 1024 * 1024
LANE = 128
PACK_ROWS = 16


def _round_up(n, m):
    return -(-n // m) * m


def _largest_divisor(n, cap):
    return max(d for d in range(1, min(n, cap) + 1) if n % d == 0)


def _pad_rows(x, rows):
    extra = rows - x.shape[0]
    return jnp.concatenate([x, jnp.zeros((extra, x.shape[1]), x.dtype)], axis=0) if extra else x


def _gelu_tanh(x):
    return 0.5 * x * (1.0 + jnp.tanh(math.sqrt(2.0 / math.pi) * (x + 0.044715 * (x * x * x))))


def _dot_nt(a, b):
    return lax.dot_general(a, b, (((1,), (1,)), ((), ())), preferred_element_type=F32)


def _dot_tn(a, b):
    return lax.dot_general(a, b, (((0,), (0,)), ((), ())), preferred_element_type=F32)


def _rmsnorm_kernel(x_ref, g_ref, o_ref):
    x = x_ref[...].astype(F32)
    ms = jnp.mean(x * x, axis=-1, keepdims=True)
    o_ref[...] = (x * lax.rsqrt(ms + EPS) * g_ref[...].astype(F32)).astype(o_ref.dtype)


def rmsnorm(x2d, gain, out_dtype):
    m, d = x2d.shape
    tm = min(m, 256)
    assert m % tm == 0
    return pl.pallas_call(
        _rmsnorm_kernel,
        grid=(m // tm,),
        in_specs=[pl.BlockSpec((tm, d), lambda i: (i, 0)),
                  pl.BlockSpec((1, d), lambda i: (0, 0))],
        out_specs=pl.BlockSpec((tm, d), lambda i: (i, 0)),
        out_shape=jax.ShapeDtypeStruct((m, d), out_dtype),
        compiler_params=pltpu.CompilerParams(dimension_semantics=("parallel",),
                                             vmem_limit_bytes=V7X_VMEM_LIMIT_BYTES),
        name="rmsnorm",
    )(x2d, gain.reshape(1, d))


def _mm_kernel(*refs, nk, act, has_res):
    if has_res:
        a_ref, w_ref, r_ref, o_ref, acc_ref = refs
    else:
        a_ref, w_ref, o_ref, acc_ref = refs
        r_ref = None
    k = pl.program_id(2)

    @pl.when(k == 0)
    def _():
        acc_ref[...] = jnp.zeros_like(acc_ref)

    acc_ref[...] += jnp.dot(a_ref[...].astype(BF16), w_ref[...], preferred_element_type=F32)

    @pl.when(k == nk - 1)
    def _():
        acc = acc_ref[...]
        if act == "relu2":
            acc = jnp.square(jnp.maximum(acc, 0.0))
        if act == "glu":
            acc = r_ref[...].astype(F32) * jax.nn.sigmoid(acc)
        elif has_res:
            acc = acc + r_ref[...].astype(F32)
        o_ref[...] = acc.astype(o_ref.dtype)


def matmul(a, w, layer, *, res=None, act=None, out_dtype=F32):
    m, kdim = a.shape
    n = w.shape[2]
    tm = min(m, 1024)
    tn = 512 if n % 512 == 0 else (256 if n % 256 == 0 else 128)
    tk = min(kdim, 4096)
    assert m % tm == 0 and n % tn == 0 and kdim % tk == 0
    nk = kdim // tk
    in_specs = [pl.BlockSpec((tm, tk), lambda i, j, k: (i, k)),
                pl.BlockSpec((None, tk, tn), lambda i, j, k: (layer, k, j))]
    args = [a, w]
    if res is not None:
        in_specs.append(pl.BlockSpec((tm, tn), lambda i, j, k: (i, j)))
        args.append(res)
    return pl.pallas_call(
        functools.partial(_mm_kernel, nk=nk, act=act, has_res=res is not None),
        grid=(m // tm, n // tn, nk),
        in_specs=in_specs,
        out_specs=pl.BlockSpec((tm, tn), lambda i, j, k: (i, j)),
        out_shape=jax.ShapeDtypeStruct((m, n), out_dtype),
        scratch_shapes=[pltpu.VMEM((tm, tn), F32)],
        compiler_params=pltpu.CompilerParams(
            dimension_semantics=("parallel", "parallel", "arbitrary"),
            vmem_limit_bytes=V7X_VMEM_LIMIT_BYTES),
        name="matmul",
    )(*args)


def _compress_rows(x_ref, n_full, pe_ref, w1_ref, w2_ref):
    pieces = [x_ref[0, pl.ds(s, n_full, stride=CMP_STRIDE), :] for s in range(CMP_STRIDE)]
    x = jnp.concatenate(pieces, axis=1).astype(BF16)
    half = CMP_STRIDE * HEAD_DIM
    z0 = jnp.dot(x, w1_ref[:half, :], preferred_element_type=F32)
    z1 = jnp.dot(x, w1_ref[half:, :], preferred_element_type=F32)
    z1 = pltpu.roll(z1, n_full - 1, 0)
    pe = jnp.broadcast_to(pe_ref[...], (8, CMP_BLOCK * HEAD_DIM)).astype(BF16)
    h0 = jnp.dot(pe, w1_ref[...], preferred_element_type=F32)[0:1, :]
    h = _gelu_tanh(z0 + z1 + h0)
    return jnp.dot(h.astype(BF16), w2_ref[...], preferred_element_type=F32)


def _compress_prompt_kernel(xk_ref, xv_ref, pek_ref, w1k_ref, w2k_ref, pev_ref, w1v_ref, w2v_ref,
                            kc_ref, vc_ref, *, n_full):
    for x_ref, pe_ref, w1_ref, w2_ref, o_ref in ((xk_ref, pek_ref, w1k_ref, w2k_ref, kc_ref),
                                                  (xv_ref, pev_ref, w1v_ref, w2v_ref, vc_ref)):
        out = _compress_rows(x_ref, n_full, pe_ref, w1_ref, w2_ref).astype(o_ref.dtype)
        n_pad = o_ref.shape[2]
        o_ref[0, 0, :n_full, :] = out
        if n_pad > n_full:
            o_ref[0, 0, n_full:, :] = jnp.zeros((n_pad - n_full, HEAD_DIM), o_ref.dtype)


def compress_prompt(proj3, col_k, col_v, pe_k, w1_k, w2_k, pe_v, w1_v, w2_v):
    B, T, _ = proj3.shape
    n_full = T // CMP_STRIDE
    n_pad = _round_up(n_full, LANE)
    flat = CMP_BLOCK * HEAD_DIM
    wspec = [pl.BlockSpec((1, flat), lambda b, g: (0, 0)),
             pl.BlockSpec((flat, HEAD_DIM), lambda b, g: (0, 0)),
             pl.BlockSpec((HEAD_DIM, HEAD_DIM), lambda b, g: (0, 0))]
    out_sds = jax.ShapeDtypeStruct((B, NSA_KV_HEADS, n_pad, HEAD_DIM), BF16)
    ospec = pl.BlockSpec((1, 1, n_pad, HEAD_DIM), lambda b, g: (b, g, 0, 0))
    return pl.pallas_call(
        functools.partial(_compress_prompt_kernel, n_full=n_full),
        grid=(B, NSA_KV_HEADS),
        in_specs=[pl.BlockSpec((1, T, HEAD_DIM), lambda b, g: (b, 0, col_k // HEAD_DIM + g)),
                  pl.BlockSpec((1, T, HEAD_DIM), lambda b, g: (b, 0, col_v // HEAD_DIM + g))] + wspec + wspec,
        out_specs=(ospec, ospec),
        out_shape=(out_sds, out_sds),
        compiler_params=pltpu.CompilerParams(dimension_semantics=("parallel", "parallel"),
                                             vmem_limit_bytes=V7X_VMEM_LIMIT_BYTES),
        name="nsa_compress_prompt",
    )(proj3, proj3,
      pe_k.reshape(1, flat), w1_k.reshape(flat, HEAD_DIM).astype(BF16), w2_k.astype(BF16),
      pe_v.reshape(1, flat), w1_v.reshape(flat, HEAD_DIM).astype(BF16), w2_v.astype(BF16))


def _bucket_table():
    d = np.arange(MAX_DISTANCE)
    max_exact = N_BUCKETS // 2
    large = max_exact + (np.log(np.maximum(d, 1).astype(np.float32) / np.float32(max_exact))
                         / np.float32(math.log(MAX_DISTANCE / max_exact))
                         * np.float32(N_BUCKETS - max_exact)).astype(np.int32)
    return np.where(d < max_exact, d, np.minimum(large, N_BUCKETS - 1)).astype(np.int32)


def _bias_of_distance(rel_bias, dist):
    bt = _bucket_table()
    buckets = bt[np.clip(dist, 0, MAX_DISTANCE - 1)]
    b = jnp.moveaxis(rel_bias.astype(F32)[buckets], -1, 0)
    return jnp.where(jnp.asarray(dist >= 0)[None], b, NEG)


def _bias_by_distance(rel_bias):
    return rel_bias.astype(F32)[_bucket_table()].T


def _toeplitz(v, n):
    h = v.shape[0]
    w = jnp.pad(v, ((0, 0), (0, 1)))
    m = jnp.tile(w, (1, n))[:, :n * (2 * n - 1)].reshape(h, n, 2 * n - 1)
    return m[:, :, n - 1:]


ATT_TILE = 128
N_BIAS_TILES = 4
SLC_CLASS_TILES = 2
MXU_TILES = 2


def _rows_softmax_pv(s_tiles, v_rows):
    m = s_tiles[0]
    for s in s_tiles[1:]:
        m = jnp.maximum(m, s)
    m = jnp.max(m, axis=-1, keepdims=True)
    l = None
    o = None
    for i in range(0, len(s_tiles), MXU_TILES):
        ps = [jnp.exp(s - m) for s in s_tiles[i:i + MXU_TILES]]
        for p in ps:
            l = p if l is None else l + p
        p_cat = ps[0] if len(ps) == 1 else jnp.concatenate(ps, axis=1)
        pv = jnp.dot(p_cat.astype(BF16), v_rows(i, len(ps)), preferred_element_type=F32)
        o = pv if o is None else o + pv
    return o / jnp.sum(l, axis=-1, keepdims=True)


def _score_tiles(q, k_rows, n_tiles):
    tiles = []
    for i in range(0, n_tiles, MXU_TILES):
        n = min(MXU_TILES, n_tiles - i)
        s = _dot_nt(q, k_rows(i, n))
        tiles += [s[:, j * LANE:(j + 1) * LANE] for j in range(n)]
    return tiles


def _nsa_prompt_kernel(q_ref, kc_ref, vc_ref, ks_ref, vs_ref, kw_ref, vw_ref, gate_ref,
                       bcmp_ref, btile_ref, ovl_ref, pick_ref, cvec_ref, kaug_s_ref, kaug_w_ref, o_ref,
                       ksb, vsb, kwb, vwb, s_ref, *, n_sel, n_top, T):
    tq = ATT_TILE
    J = NSA_HPG
    qi = pl.program_id(2)
    n_sel_pad = pick_ref.shape[0]
    nq = T // tq

    @pl.when(qi == 0)
    def _():
        ksb[:, :HEAD_DIM] = ks_ref[0].astype(BF16)
        ksb[:, HEAD_DIM:] = kaug_s_ref[...]
        vsb[...] = vs_ref[0].astype(BF16)
        kwb[:WINDOW, :HEAD_DIM] = jnp.zeros((WINDOW, HEAD_DIM), BF16)
        kwb[WINDOW:, :HEAD_DIM] = kw_ref[0].astype(BF16)
        kwb[:, HEAD_DIM:] = kaug_w_ref[...]
        vwb[:WINDOW, :] = jnp.zeros((WINDOW, HEAD_DIM), BF16)
        vwb[WINDOW:, :] = vw_ref[0].astype(BF16)

    def stack(f):
        return jnp.concatenate([f(j) for j in range(J)], axis=0)

    def put(vals, first):
        for j in range(J):
            cols = slice(j * HEAD_DIM, (j + 1) * HEAD_DIM)
            v = vals[j * tq:(j + 1) * tq]
            o_ref[0, :, cols] = v if first else o_ref[0, :, cols] + v

    gates = jax.nn.sigmoid(gate_ref[0, 0])
    gate = lambda c: stack(lambda j: gates[:, 3 * j + c:3 * j + c + 1])
    q_all = stack(lambda j: q_ref[0, :, j * HEAD_DIM:(j + 1) * HEAD_DIM] * HEAD_DIM ** -0.5).astype(BF16)

    row = lax.broadcasted_iota(jnp.int32, (tq, LANE), 0)
    lane = lax.broadcasted_iota(jnp.int32, (tq, LANE), 1)
    valid1 = qi * tq + row >= lane * CMP_STRIDE + (CMP_BLOCK - 1)
    valid = stack(lambda j: valid1)
    s = jnp.where(valid, _dot_nt(q_all, kc_ref[0, 0]) + stack(lambda j: bcmp_ref[j]), NEG)
    e = jnp.where(valid, jnp.exp(s - jnp.max(s, axis=-1, keepdims=True)), 0.0)
    den = jnp.sum(e, axis=-1, keepdims=True)
    p = e / jnp.where(den > 0.0, den, 1.0)
    put(gate(0) * jnp.dot(p.astype(BF16), vc_ref[0, 0], preferred_element_type=F32), True)
    psum = p[0:tq]
    for j in range(1, J):
        psum = psum + p[j * tq:(j + 1) * tq]

    bt = lambda i: stack(lambda j: btile_ref[j, i])

    pad_flag = jnp.broadcast_to(-cvec_ref[...], (tq, LANE)).astype(BF16)
    q_win = jnp.concatenate([q_all, stack(lambda j: pad_flag)], axis=1)
    n_w = WINDOW // tq + 1
    win_rows = lambda i, n: pl.ds(pl.multiple_of((qi + i) * tq, tq), n * tq)
    s_tiles = _score_tiles(q_win, lambda i, n: kwb[win_rows(i, n), :], n_w)
    s_tiles[0] = s_tiles[0] + bt(3)
    s_tiles[n_w - 2] = s_tiles[n_w - 2] + bt(1)
    s_tiles[n_w - 1] = s_tiles[n_w - 1] + bt(0)
    put(gate(2) * _rows_softmax_pv(s_tiles, lambda i, n: vwb[win_rows(i, n), :]), False)

    hi = psum.astype(BF16)
    r1 = psum - hi.astype(F32)
    mid = r1.astype(BF16)
    lo = (r1 - mid.astype(F32)).astype(BF16)
    ovl = ovl_ref[...]
    p_sel = _dot_nt(ovl, hi) + _dot_nt(ovl, mid) + _dot_nt(ovl, lo)
    blk = lax.broadcasted_iota(jnp.int32, (n_sel_pad, tq), 0)
    cur = (qi * tq + lax.broadcasted_iota(jnp.int32, (n_sel_pad, tq), 1)) // SEL_BLOCK
    forced = (blk == 0) | (blk == cur) | (blk == cur - 1)
    score = jnp.where(blk <= cur, p_sel + jnp.where(forced, FORCE_SCORE, 0.0), NEG)
    rank = jnp.zeros((n_sel_pad, tq), F32)
    for jb in range(n_sel):
        other = score[jb:jb + 1, :]
        beats = (other > score) | ((other == score) & (blk > jb))
        rank = rank + jnp.where(beats, 1.0, 0.0)
    sel = jnp.where((rank < n_top) & (blk <= cur) & (blk < n_sel), 1.0, 0.0).astype(BF16)
    aug = (_dot_tn(sel, pick_ref[...]) - cvec_ref[...]).astype(BF16)
    q_aug = jnp.concatenate([q_all, stack(lambda j: aug)], axis=1)

    for cls in range(-(-nq // SLC_CLASS_TILES)):
        n_t = min((cls + 1) * SLC_CLASS_TILES, nq)

        @pl.when(qi // SLC_CLASS_TILES == cls)
        def _(n_t=n_t):
            for kt, s in enumerate(_score_tiles(q_aug, lambda i, n: ksb[i * tq:(i + n) * tq, :], n_t)):
                s_ref[kt] = s
            s_ref[qi] = s_ref[qi] + bt(0)

            @pl.when(qi > 0)
            def _():
                s_ref[qi - 1] = s_ref[qi - 1] + bt(1)

            o = _rows_softmax_pv([s_ref[kt] for kt in range(n_t)], lambda i, n: vsb[i * tq:(i + n) * tq, :])
            put(gate(1) * o, False)


def nsa_prompt(proj3, cols, ng, kcmp, vcmp, rel_bias):
    B, T, _ = proj3.shape
    tq = ATT_TILE
    assert T % tq == 0 and T % SEL_BLOCK == 0 and WINDOW % tq == 0
    G, J = NSA_KV_HEADS, NSA_HPG
    nq = T // tq
    n_sel = T // SEL_BLOCK
    n_top = min(SEL_TOP, n_sel)
    n_sel_pad = _round_up(n_sel, 16)
    assert n_sel_pad < LANE
    n_pad = kcmp.shape[2]
    assert n_pad == LANE, "one lane tile of compressed blocks"

    bd = _bias_by_distance(rel_bias)
    n_a = T // CMP_STRIDE
    assert n_a == n_pad
    d_cmp = (CMP_STRIDE * ((n_a - 1) - np.arange(2 * n_a - 1))[None, :]
             + np.arange(CMP_STRIDE)[:, None] - (CMP_BLOCK - 1))
    gen_cmp = jnp.where(jnp.asarray(d_cmp >= 0)[None], bd[:, np.clip(d_cmp, 0, MAX_DISTANCE - 1)], 0.0)
    bcmp = _toeplitz(gen_cmp.reshape(NSA_HEADS * CMP_STRIDE, 2 * n_a - 1), n_a)
    bcmp = bcmp.reshape(NSA_HEADS, CMP_STRIDE, n_a, n_a).transpose(0, 2, 1, 3).reshape(NSA_HEADS, T, n_pad)
    d_diag = (tq - 1) - np.arange(2 * tq - 1)
    rel = bd - bd[:, -1:]
    gen = lambda d: jnp.where(jnp.asarray(d >= 0)[None], rel[:, np.clip(d, 0, MAX_DISTANCE - 1)], NEG)
    edge = jnp.broadcast_to(jnp.where(jnp.asarray(d_diag <= 0), 0.0, NEG)[None], (NSA_HEADS, 2 * tq - 1))
    btile = jnp.stack([_toeplitz(gen(d_diag), tq), _toeplitz(gen(d_diag + tq), tq),
                       jnp.zeros((NSA_HEADS, tq, tq), F32), _toeplitz(edge, tq)], axis=1)
    cmp_start = np.arange(n_pad) * CMP_STRIDE
    sel_start = np.arange(n_sel_pad) * SEL_BLOCK
    ovl = ((cmp_start[None, :] < sel_start[:, None] + SEL_BLOCK)
           & (cmp_start[None, :] + CMP_BLOCK > sel_start[:, None])
           & (np.arange(n_pad)[None, :] < T // CMP_STRIDE - 1))
    ovl = jnp.asarray(ovl, BF16)
    pick = jnp.asarray(np.arange(LANE)[None, :] == np.arange(n_sel_pad)[:, None], BF16)
    lane_i = np.arange(LANE)
    cvec = jnp.asarray(((lane_i < n_sel) | (lane_i == n_sel_pad))[None, :], F32)
    big = -NEG
    kaug_s = jnp.asarray(np.where(np.arange(T)[:, None] // SEL_BLOCK == lane_i[None, :], big, 0.0), BF16)
    kaug_w = jnp.asarray(np.where((np.arange(T + WINDOW)[:, None] < WINDOW) & (lane_i[None, :] == n_sel_pad),
                                  big, 0.0), BF16)
    gates = ng.reshape(B, T, G, 3 * J).transpose(0, 2, 1, 3)

    kv_spec = lambda name: pl.BlockSpec((1, T, HEAD_DIM),
                                        lambda b, g, i, o=cols[name] // HEAD_DIM: (b, 0, o + g))
    cmp_spec = pl.BlockSpec((1, 1, n_pad, HEAD_DIM), lambda b, g, i: (b, g, 0, 0))
    const2 = lambda shape: pl.BlockSpec(shape, lambda b, g, i: (0, 0))
    return pl.pallas_call(
        functools.partial(_nsa_prompt_kernel, n_sel=n_sel, n_top=n_top, T=T),
        grid=(B, G, nq),
        in_specs=[pl.BlockSpec((1, tq, J * HEAD_DIM),
                               lambda b, g, i, o=cols['nq'] // (J * HEAD_DIM): (b, i, o + g)),
                  cmp_spec, cmp_spec,
                  kv_spec('ks'), kv_spec('vs'), kv_spec('kw'), kv_spec('vw'),
                  pl.BlockSpec((1, 1, tq, 3 * J), lambda b, g, i: (b, g, i, 0)),
                  pl.BlockSpec((J, tq, n_pad), lambda b, g, i: (g, i, 0)),
                  pl.BlockSpec((J, N_BIAS_TILES, tq, tq), lambda b, g, i: (g, 0, 0, 0)),
                  const2((n_sel_pad, n_pad)), const2((n_sel_pad, LANE)), const2((1, LANE)),
                  const2((T, LANE)), const2((T + WINDOW, LANE))],
        out_specs=pl.BlockSpec((1, tq, J * HEAD_DIM), lambda b, g, i: (b, i, g)),
        out_shape=jax.ShapeDtypeStruct((B, T, NSA_WIDTH), F32),
        scratch_shapes=[pltpu.VMEM((T, 2 * HEAD_DIM), BF16), pltpu.VMEM((T, HEAD_DIM), BF16),
                        pltpu.VMEM((T + WINDOW, 2 * HEAD_DIM), BF16), pltpu.VMEM((T + WINDOW, HEAD_DIM), BF16),
                        pltpu.VMEM((nq, J * tq, tq), F32)],
        compiler_params=pltpu.CompilerParams(
            dimension_semantics=("parallel", "parallel", "arbitrary"),
            vmem_limit_bytes=V7X_VMEM_LIMIT_BYTES),
        name="nsa_prompt",
    )(proj3, kcmp, vcmp, proj3, proj3, proj3, proj3, gates, bcmp, btile, ovl, pick, cvec, kaug_s, kaug_w)


PAGE_ROWS = PAGE_SIZE * NSA_KV_HEADS
STRIDES_PER_PAGE = PAGE_SIZE // CMP_STRIDE
PAGES_PER_STEP = 8


def _cache_rows(cache):
    return cache.reshape(-1, HEAD_DIM)


def _page_specs(layer):
    return [pl.BlockSpec((2 * PAGE_ROWS, HEAD_DIM),
                         lambda b, p, pt, k=k: (pt[b, p * PAGES_PER_STEP + k] * DEPTH + layer, 0))
            for k in range(PAGES_PER_STEP)]


def _cmp_partial_kernel(pt_ref, *refs):
    page_refs = refs[:PAGES_PER_STEP]
    w1k_ref, w1v_ref, z_ref = refs[PAGES_PER_STEP:]
    G = NSA_KV_HEADS
    half = CMP_STRIDE * HEAD_DIM
    n_rows = PAGES_PER_STEP * STRIDES_PER_PAGE
    for kv, w1_ref in ((0, w1k_ref), (1, w1v_ref)):
        xs = []
        for g in range(G):
            for page_ref in page_refs:
                pieces = [page_ref[pl.ds(kv * PAGE_ROWS + s * G + g, STRIDES_PER_PAGE, stride=CMP_STRIDE * G), :]
                          for s in range(CMP_STRIDE)]
                xs.append(jnp.concatenate(pieces, axis=1))
        x = jnp.concatenate(xs, axis=0).astype(BF16)
        for j in range(CMP_BLOCK // CMP_STRIDE):
            z = jnp.dot(x, w1_ref[j * half:(j + 1) * half, :], preferred_element_type=F32)
            for g in range(G):
                z_ref[0, kv, j, g] = z[g * n_rows:(g + 1) * n_rows]


def cmp_partial(cache_rows, page_table, layer, w1_k, w1_v):
    B, n_pages = page_table.shape
    flat = CMP_BLOCK * HEAD_DIM
    n_str = n_pages * STRIDES_PER_PAGE
    assert n_pages % PAGES_PER_STEP == 0
    wspec = pl.BlockSpec((flat, HEAD_DIM), lambda b, p, pt: (0, 0))
    return pl.pallas_call(
        _cmp_partial_kernel,
        grid_spec=pltpu.PrefetchScalarGridSpec(
            num_scalar_prefetch=1,
            grid=(B, n_pages // PAGES_PER_STEP),
            in_specs=_page_specs(layer) + [wspec, wspec],
            out_specs=pl.BlockSpec((1, 2, 2, NSA_KV_HEADS, PAGES_PER_STEP * STRIDES_PER_PAGE, HEAD_DIM),
                                   lambda b, p, pt: (b, 0, 0, 0, p, 0))),
        out_shape=jax.ShapeDtypeStruct((B, 2, 2, NSA_KV_HEADS, n_str, HEAD_DIM), F32),
        compiler_params=pltpu.CompilerParams(dimension_semantics=("parallel", "arbitrary"),
                                             vmem_limit_bytes=V7X_VMEM_LIMIT_BYTES),
        name="nsa_cmp_partial",
    )(page_table, *([cache_rows] * PAGES_PER_STEP),
      w1_k.reshape(flat, HEAD_DIM).astype(BF16), w1_v.reshape(flat, HEAD_DIM).astype(BF16))


def _cmp_attn_sample_kernel(z_ref, pek_ref, w1k_ref, w2k_ref, pev_ref, w1v_ref, w2v_ref, q_ref, bias_ref,
                            ovl_ref, o_ref, psel_ref, *, q0):
    T = q_ref.shape[1]
    J = NSA_HPG
    n_str = z_ref.shape[4]

    def finish(kv, pe_ref, w1_ref, w2_ref):
        z1 = pltpu.roll(z_ref[0, kv, 1, 0], n_str - 1, 0)
        pe = jnp.broadcast_to(pe_ref[...], (PACK_ROWS, CMP_BLOCK * HEAD_DIM)).astype(BF16)
        h0 = jnp.dot(pe, w1_ref[...], preferred_element_type=F32)[0:1, :]
        h = _gelu_tanh(z_ref[0, kv, 0, 0] + z1 + h0)
        return jnp.dot(h.astype(BF16), w2_ref[...], preferred_element_type=F32).astype(BF16)

    kc = finish(0, pek_ref, w1k_ref, w2k_ref)
    vc = finish(1, pev_ref, w1v_ref, w2v_ref)
    stack = lambda f: jnp.concatenate([f(j) for j in range(J)], axis=0)
    q_all = stack(lambda j: q_ref[0, :, j * HEAD_DIM:(j + 1) * HEAD_DIM] * HEAD_DIM ** -0.5).astype(BF16)
    row = lax.broadcasted_iota(jnp.int32, (T, n_str), 0)
    lane = lax.broadcasted_iota(jnp.int32, (T, n_str), 1)
    valid1 = (q0 + row >= lane * CMP_STRIDE + (CMP_BLOCK - 1)) & (lane < n_str - 1)
    valid = stack(lambda j: valid1)
    s = jnp.where(valid, _dot_nt(q_all, kc) + stack(lambda j: bias_ref[j]), NEG)
    e = jnp.where(valid, jnp.exp(s - jnp.max(s, axis=-1, keepdims=True)), 0.0)
    den = jnp.sum(e, axis=-1, keepdims=True)
    p = e / jnp.where(den > 0.0, den, 1.0)
    o_ref[0, 0] = jnp.dot(p.astype(BF16), vc, preferred_element_type=F32)
    psum = p[0:T]
    for j in range(1, J):
        psum = psum + p[j * T:(j + 1) * T]
    psum = _pad_rows(psum, PACK_ROWS)
    hi = psum.astype(BF16)
    r1 = psum - hi.astype(F32)
    mid = r1.astype(BF16)
    lo = (r1 - mid.astype(F32)).astype(BF16)
    ovl = ovl_ref[...]
    p_sel = (jnp.dot(hi, ovl, preferred_element_type=F32) + jnp.dot(mid, ovl, preferred_element_type=F32)
             + jnp.dot(lo, ovl, preferred_element_type=F32))
    psel_ref[0, 0] = p_sel[:T]


def _rank_select_kernel(psel_ref, tpos_ref, out_ref, score_ref, rank_ref, *, n_sel, n_top):
    shape = psel_ref.shape
    blk = lax.broadcasted_iota(jnp.int32, shape, 0)
    cur = jnp.broadcast_to(tpos_ref[...], shape) // SEL_BLOCK
    forced = (blk == 0) | (blk == cur) | (blk == cur - 1)
    ok = (blk <= cur) & (blk < n_sel)
    score = jnp.where(ok, psel_ref[...] + jnp.where(forced, FORCE_SCORE, 0.0), NEG)
    score_ref[...] = score
    rank_ref[...] = jnp.zeros(shape, F32)

    def body(jb, c):
        other = jnp.broadcast_to(score_ref[pl.ds(jb, 1), :], shape)
        beats = (other > score) | ((other == score) & (blk > jb))
        rank_ref[...] = rank_ref[...] + jnp.where(beats, 1.0, 0.0)
        return c

    lax.fori_loop(0, n_sel, body, 0)
    out_ref[...] = jnp.where((rank_ref[...] < n_top) & ok, 0.0, NEG)


def _slc_win_sample_kernel(pt_ref, *refs, win_buf):
    page_refs = refs[:PAGES_PER_STEP]
    (qT_ref, seladd_ref, blast_ref, selnew_ref, ksn_ref, vsn_ref, kwn_ref, vwn_ref, bnew_ref, cw_ref, bwin_ref,
     ocmp_ref, gate_ref, o_ref, m_ref, l_ref, acc_ref) = refs[PAGES_PER_STEP:]
    G = NSA_KV_HEADS
    p = pl.program_id(1)
    last = pl.num_programs(1) - 1
    lane_group = lax.broadcasted_iota(jnp.int32, (1, LANE), 1) // (LANE // G)

    @pl.when(p == 0)
    def _():
        m_ref[...] = jnp.full(m_ref.shape, NEG, F32)
        l_ref[...] = jnp.zeros(l_ref.shape, F32)
        acc_ref[...] = jnp.zeros(acc_ref.shape, F32)

    def scores(k_of_g):
        s = None
        for g in range(G):
            sg = jnp.dot(k_of_g(g).astype(BF16), qT_ref[0, g], preferred_element_type=F32)
            s = sg if s is None else s + sg
        return s

    def weighted_values(pT, v_of_g):
        o = None
        for g in range(G):
            pg = jnp.where(lane_group == g, pT, 0.0).astype(BF16)
            og = _dot_tn(v_of_g(g).astype(BF16), pg)
            o = og if o is None else o + og
        return o

    def accumulate(segments):
        m_old = m_ref[0:1, :]
        m_new = m_old
        for sT, _ in segments:
            m_new = jnp.maximum(m_new, jnp.max(sT, axis=0, keepdims=True))
        alpha = jnp.exp(m_old - m_new)
        l_new = alpha * l_ref[0:1, :]
        acc = alpha * acc_ref[...]
        for sT, v_of_g in segments:
            pT = jnp.exp(sT - m_new)
            l_new = l_new + jnp.sum(pT, axis=0, keepdims=True)
            acc = acc + weighted_values(pT, v_of_g)
        l_ref[...] = jnp.broadcast_to(l_new, l_ref.shape)
        acc_ref[...] = acc
        m_ref[...] = jnp.broadcast_to(m_new, m_ref.shape)

    key = lax.broadcasted_iota(jnp.int32, (PAGE_SIZE, LANE), 0)
    blocks_per_page = PAGE_SIZE // SEL_BLOCK
    is_last = (p == last).astype(F32)
    segments = []
    for k, page_ref in enumerate(page_refs):
        k_page = lambda g, r=page_ref: r[pl.ds(g, PAGE_SIZE, stride=G), :]
        v_page = lambda g, r=page_ref: r[pl.ds(PAGE_ROWS + g, PAGE_SIZE, stride=G), :]
        mask = seladd_ref[0, k, blocks_per_page - 1:blocks_per_page, :]
        for i in range(blocks_per_page - 2, -1, -1):
            mask = jnp.where(key < (i + 1) * SEL_BLOCK, seladd_ref[0, k, i:i + 1, :], mask)
        sT = scores(k_page) + mask
        if k == PAGES_PER_STEP - 1:
            sT = sT + blast_ref[...] * is_last
        segments.append((sT, v_page))
    accumulate(segments)

    @pl.when(p == last)
    def _():
        new = lambda ref: (lambda g: _pad_rows(ref[0, :, g * HEAD_DIM:(g + 1) * HEAD_DIM], PACK_ROWS))
        accumulate([(scores(new(ksn_ref)) + bnew_ref[...] + selnew_ref[0], new(vsn_ref))])
        o_slc = (acc_ref[...] / l_ref[0:1, :]).T

        k_win = lambda g: cw_ref[pl.ds(g, win_buf, stride=G), :]
        v_win = lambda g: cw_ref[pl.ds(win_buf * G + g, win_buf, stride=G), :]
        s_w = scores(k_win) + bwin_ref[...]
        s_n = scores(new(kwn_ref)) + bnew_ref[...]
        m = jnp.maximum(jnp.max(s_w, axis=0, keepdims=True), jnp.max(s_n, axis=0, keepdims=True))
        p_w = jnp.exp(s_w - m)
        p_n = jnp.exp(s_n - m)
        den = jnp.sum(p_w, axis=0, keepdims=True) + jnp.sum(p_n, axis=0, keepdims=True)
        o_win = ((weighted_values(p_w, v_win) + weighted_values(p_n, new(vwn_ref))) / den).T

        gates = jax.nn.sigmoid(gate_ref[0])
        o_ref[0] = gates[:, 0:1] * ocmp_ref[0] + gates[:, 1:2] * o_slc + gates[:, 2:3] * o_win


def nsa_sample(proj3, cols, ng, cache_cmp, cache_slc, cache_win, page_table, layer, rel_bias,
               pe_k, w1_k, w2_k, pe_v, w1_v, w2_v):
    B, T, _ = proj3.shape
    G, J, H = NSA_KV_HEADS, NSA_HPG, NSA_HEADS
    n_pages = page_table.shape[1]
    past = n_pages * PAGE_SIZE
    q0 = past
    win_buf = cache_win.shape[3]
    L = G * J * T
    assert L == LANE and T <= PACK_ROWS and T < CMP_STRIDE and past % SEL_BLOCK == 0 and T <= SEL_BLOCK
    assert win_buf == min(WINDOW, past)
    n_str = past // CMP_STRIDE
    n_sel = past // SEL_BLOCK + 1
    n_top = min(SEL_TOP, n_sel)
    n_sel_rows = _round_up(n_sel, 8)
    n_sel_lanes = _round_up(n_sel, LANE)
    flat = CMP_BLOCK * HEAD_DIM
    b_far = rel_bias.astype(F32)[_bucket_table()[MAX_DISTANCE - 1]]

    def lane_bias(dist, ok):
        b = _bias_of_distance(rel_bias, np.maximum(dist, 0)) - b_far[:, None, None]
        b = jnp.where(jnp.asarray(ok)[None], b, NEG)
        return jnp.moveaxis(b.reshape(G, J, dist.shape[0], T), 2, 0).reshape(dist.shape[0], L)

    ti = np.arange(T)[None, :]
    z = cmp_partial(_cache_rows(cache_cmp), page_table, layer, w1_k, w1_v)
    n = np.arange(n_str)[None, :]
    dist_c = q0 + np.arange(T)[:, None] - (n * CMP_STRIDE + CMP_BLOCK - 1)
    near = dist_c.min(axis=0) < MAX_DISTANCE
    n_far = int(np.argmax(near)) if near.any() else n_str
    bcmp = jnp.concatenate([jnp.broadcast_to(b_far[:, None, None], (H, T, n_far)),
                            _bias_of_distance(rel_bias, np.maximum(dist_c[:, n_far:], 0))], axis=2)
    cmp_start = np.arange(n_str) * CMP_STRIDE
    sel_start = np.arange(n_sel_lanes) * SEL_BLOCK
    ovl = ((cmp_start[:, None] < sel_start[None, :] + SEL_BLOCK) & (cmp_start[:, None] + CMP_BLOCK > sel_start[None, :])
           & (np.arange(n_str)[:, None] < n_str - 1) & (np.arange(n_sel_lanes)[None, :] < n_sel))
    wspec = [pl.BlockSpec((1, flat), lambda b, g: (0, 0)),
             pl.BlockSpec((flat, HEAD_DIM), lambda b, g: (0, 0)),
             pl.BlockSpec((HEAD_DIM, HEAD_DIM), lambda b, g: (0, 0))]
    o_cmp, p_sel = pl.pallas_call(
        functools.partial(_cmp_attn_sample_kernel, q0=q0),
        grid=(B, G),
        in_specs=[pl.BlockSpec((1, 2, 2, 1, n_str, HEAD_DIM), lambda b, g: (b, 0, 0, g, 0, 0))] + wspec + wspec + [
            pl.BlockSpec((1, T, J * HEAD_DIM), lambda b, g, o=cols['nq'] // (J * HEAD_DIM): (b, 0, o + g)),
            pl.BlockSpec((J, T, n_str), lambda b, g: (g, 0, 0)),
            pl.BlockSpec((n_str, n_sel_lanes), lambda b, g: (0, 0))],
        out_specs=(pl.BlockSpec((1, 1, J * T, HEAD_DIM), lambda b, g: (b, g, 0, 0)),
                   pl.BlockSpec((1, 1, T, n_sel_lanes), lambda b, g: (b, g, 0, 0))),
        out_shape=(jax.ShapeDtypeStruct((B, G, J * T, HEAD_DIM), F32),
                   jax.ShapeDtypeStruct((B, G, T, n_sel_lanes), F32)),
        compiler_params=pltpu.CompilerParams(dimension_semantics=("parallel", "parallel"),
                                             vmem_limit_bytes=V7X_VMEM_LIMIT_BYTES),
        name="nsa_cmp_attn_sample",
    )(z, pe_k.reshape(1, flat), w1_k.reshape(flat, HEAD_DIM).astype(BF16), w2_k.astype(BF16),
      pe_v.reshape(1, flat), w1_v.reshape(flat, HEAD_DIM).astype(BF16), w2_v.astype(BF16),
      proj3, bcmp, jnp.asarray(ovl, BF16))

    n_bgt = B * G * T
    psel_t = p_sel.reshape(n_bgt, n_sel_lanes)[:, :n_sel_rows].T
    tpos = jnp.asarray(np.tile(q0 + np.arange(T), B * G)[None, :], jnp.int32)
    seladd = pl.pallas_call(
        functools.partial(_rank_select_kernel, n_sel=n_sel, n_top=n_top),
        out_shape=jax.ShapeDtypeStruct((n_sel_rows, n_bgt), F32),
        scratch_shapes=[pltpu.VMEM((n_sel_rows, n_bgt), F32), pltpu.VMEM((n_sel_rows, n_bgt), F32)],
        name="nsa_rank_select",
    )(psel_t, tpos)
    seladd = seladd.T.reshape(B, G, 1, T, n_sel_rows)
    seladd = jnp.broadcast_to(seladd, (B, G, J, T, n_sel_rows)).reshape(B, L, n_sel_rows)
    bpp = PAGE_SIZE // SEL_BLOCK
    sel_past = seladd[:, :, :n_sel - 1].reshape(B, L, n_pages, bpp).transpose(0, 2, 3, 1)
    sel_new = seladd[:, :, n_sel - 1].reshape(B, 1, L)

    q = proj3[:, :, cols['nq']:cols['nq'] + NSA_WIDTH].reshape(B, T, G, J, HEAD_DIM) * HEAD_DIM ** -0.5
    q_t = q.transpose(0, 2, 4, 3, 1).reshape(B, G, HEAD_DIM, J * T)
    place = jnp.asarray(np.arange(G)[:, None, None] == (np.arange(L) // (J * T))[None, None, :])
    q_pad = jnp.where(place[None], jnp.tile(q_t, (1, 1, 1, G)), 0.0).astype(BF16)
    ki = np.arange(PAGE_SIZE)[:, None]
    b_last = lane_bias(PAGE_SIZE + ti - ki, np.ones((PAGE_SIZE, T), bool))
    kn = np.arange(PACK_ROWS)[:, None]
    b_new = lane_bias(ti - kn, (ti - kn >= 0) & (kn < T))
    kw = np.arange(win_buf)[:, None]
    b_win = lane_bias(win_buf + ti - kw, win_buf + ti - kw <= WINDOW)
    gates = ng.reshape(B, T, G, J, 3).transpose(0, 2, 3, 1, 4).reshape(B, L, 3)
    win_rows = 2 * win_buf * G
    new_spec = lambda name: pl.BlockSpec((1, T, KV_WIDTH), lambda b, p, pt, o=cols[name] // KV_WIDTH: (b, 0, o))
    const = lambda shape: pl.BlockSpec(shape, lambda b, p, pt: (0, 0))
    per_b = lambda shape: pl.BlockSpec((1,) + shape, lambda b, p, pt: (b,) + (0,) * len(shape))
    out = pl.pallas_call(
        functools.partial(_slc_win_sample_kernel, win_buf=win_buf),
        grid_spec=pltpu.PrefetchScalarGridSpec(
            num_scalar_prefetch=1,
            grid=(B, n_pages // PAGES_PER_STEP),
            in_specs=_page_specs(layer) + [
                per_b((G, HEAD_DIM, L)),
                pl.BlockSpec((1, PAGES_PER_STEP, bpp, L), lambda b, p, pt: (b, p, 0, 0)),
                const((PAGE_SIZE, L)), per_b((1, L)),
                new_spec('ks'), new_spec('vs'), new_spec('kw'), new_spec('vw'),
                const((PACK_ROWS, L)),
                pl.BlockSpec((win_rows, HEAD_DIM), lambda b, p, pt: (layer * B + b, 0)),
                const((win_buf, L)), per_b((L, HEAD_DIM)), per_b((L, 3))],
            out_specs=per_b((L, HEAD_DIM)),
            scratch_shapes=[pltpu.VMEM((8, L), F32), pltpu.VMEM((8, L), F32), pltpu.VMEM((HEAD_DIM, L), F32)]),
        out_shape=jax.ShapeDtypeStruct((B, L, HEAD_DIM), F32),
        compiler_params=pltpu.CompilerParams(dimension_semantics=("parallel", "arbitrary"),
                                             vmem_limit_bytes=V7X_VMEM_LIMIT_BYTES),
        name="nsa_slc_win_sample",
    )(page_table, *([_cache_rows(cache_slc)] * PAGES_PER_STEP), q_pad, sel_past, b_last, sel_new,
      proj3, proj3, proj3, proj3, b_new,
      cache_win.reshape(-1, HEAD_DIM), b_win, o_cmp.reshape(B, L, HEAD_DIM), gates)
    return out.reshape(B, G, J, T, HEAD_DIM).transpose(0, 3, 1, 2, 4).reshape(B, T, NSA_WIDTH)


def _retention_tables(T, q0):
    c = _largest_divisor(T, RET_CHUNK)
    cp = max(c, RET_CHUNK)
    lg = np.log1p(-(2.0 ** (-5.0 - np.arange(RET_HEADS, dtype=np.float32)))).astype(np.float32)
    i = np.arange(cp)
    rel = i[:, None] - i[None, :]
    inside = (i < c)[:, None] & (i < c)[None, :]
    decay = np.where((rel >= 0) & inside, np.exp(np.maximum(rel, 0)[None] * lg[:, None, None]), 0.0)
    q_dec = np.broadcast_to(np.exp((i + 1)[None, :, None] * lg[:, None, None]), (RET_HEADS, cp, HEAD_DIM))
    k_dec = np.where((i < c)[None, :, None], np.exp((c - 1 - i)[None, :, None] * lg[:, None, None]), 0.0)
    k_dec = np.broadcast_to(k_dec, (RET_HEADS, cp, HEAD_DIM))
    chunk_dec = np.broadcast_to(np.exp(c * lg)[:, None, None], (RET_HEADS, 8, HEAD_DIM))
    half = HEAD_DIM // 2
    inv = (1.0 / (10000.0 ** np.linspace(0.0, 1.0, half, dtype=np.float32))).astype(np.float32)
    ang = (q0 + np.arange(T)).astype(np.float32)[:, None] * inv[None]
    cos, sin = np.cos(ang), np.sin(ang)
    cosf = np.concatenate([cos, cos], axis=1)
    sinf = np.concatenate([-sin, sin], axis=1)
    f = lambda a: jnp.asarray(a, F32)
    return c, cp, f(decay), f(q_dec), f(k_dec), f(chunk_dec), f(cosf), f(sinf)


RET_HEADS_PER_STEP = 2


def _retention_kernel(q_ref, k_ref, v_ref, g_ref, s0_ref, cos_ref, sin_ref, dec_ref, qd_ref, kd_ref, cd_ref,
                      o_ref, s_ref, *, c, cp, n):
    half = HEAD_DIM // 2
    heads = range(RET_HEADS_PER_STEP)

    def load(ref, rows):
        return _pad_rows(ref[rows, :], cp)

    def rot(x, cos, sin):
        return x * cos + pltpu.roll(x, half, 1) * sin

    def body(i, states):
        rows = pl.ds(pl.multiple_of(i * c, c), c)
        cos, sin = load(cos_ref, rows), load(sin_ref, rows)
        q_all, k_all, v_all = load(q_ref.at[0], rows), load(k_ref.at[0], rows), load(v_ref.at[0], rows)
        new_states = []
        for hh in heads:
            cols = slice(hh * HEAD_DIM, (hh + 1) * HEAD_DIM)
            s = states[hh]
            q = rot(q_all[:, cols], cos, sin)
            k = rot(k_all[:, cols], cos, sin) * HEAD_DIM ** -0.5
            v = v_all[:, cols].astype(BF16)
            qb = q.astype(BF16)
            inner = _dot_nt(qb, k.astype(BF16)) * dec_ref[hh]
            o = (jnp.dot(inner.astype(BF16), v, preferred_element_type=F32)
                 + jnp.dot(qb, s.astype(BF16), preferred_element_type=F32) * qd_ref[hh])
            new_states.append(s * cd_ref[hh, 0:1, :] + _dot_tn((k * kd_ref[hh]).astype(BF16), v))
            o = o * lax.rsqrt(jnp.mean(o * o, axis=-1, keepdims=True) + EPS)
            g = g_ref[0, rows, cols]
            o_ref[0, rows, cols] = g * jax.nn.sigmoid(g) * o[:c]
        return tuple(new_states)

    final = lax.fori_loop(0, n, body, tuple(s0_ref[0, hh] for hh in heads))
    for hh in heads:
        s_ref[0, hh] = final[hh]


def retention(proj3, cols, s0, q0):
    B, T, _ = proj3.shape
    c, cp, decay, q_dec, k_dec, chunk_dec, cosf, sinf = _retention_tables(T, q0)
    hp = RET_HEADS_PER_STEP
    wide = hp * HEAD_DIM
    col = lambda name: pl.BlockSpec((1, T, wide), lambda b, h, o=cols[name] // wide: (b, 0, o + h))
    tab = lambda r: pl.BlockSpec((hp, r, HEAD_DIM), lambda b, h: (h, 0, 0))
    full = pl.BlockSpec((T, HEAD_DIM), lambda b, h: (0, 0))
    state = pl.BlockSpec((1, hp, HEAD_DIM, HEAD_DIM), lambda b, h: (b, h, 0, 0))
    return pl.pallas_call(
        functools.partial(_retention_kernel, c=c, cp=cp, n=T // c),
        grid=(B, RET_HEADS // hp),
        in_specs=[col('rq'), col('rk'), col('rv'), col('rg'), state, full, full,
                  tab(cp), tab(cp), tab(cp), tab(8)],
        out_specs=(pl.BlockSpec((1, T, wide), lambda b, h: (b, 0, h)), state),
        out_shape=(jax.ShapeDtypeStruct((B, T, RET_WIDTH), F32),
                   jax.ShapeDtypeStruct((B, RET_HEADS, HEAD_DIM, HEAD_DIM), F32)),
        compiler_params=pltpu.CompilerParams(dimension_semantics=("parallel", "parallel"),
                                             vmem_limit_bytes=V7X_VMEM_LIMIT_BYTES),
        name="retention",
    )(proj3, proj3, proj3, proj3, s0.astype(F32), cosf, sinf, decay, q_dec, k_dec, chunk_dec)


S5_BLK_GROUPS = 8
S5_BLK_STATE = S5_BLK_GROUPS * S5_STATE
S5_BLK_CH = S5_BLK_GROUPS * S5_GROUP
S5_SCAN_ROWS = 8


def _s5_params(lam_re, lam_im, log_step, b_re, b_im, c_re, c_im):
    nb = S5_GROUPS // S5_BLK_GROUPS
    lam = lax.complex(lam_re.astype(F32), lam_im.astype(F32))
    step = jnp.exp(log_step.astype(F32))[:, None]
    a_bar = jnp.exp(lam * step)
    b_bar = ((a_bar - 1.0) / lam)[..., None] * lax.complex(b_re.astype(F32), b_im.astype(F32))
    r = np.arange(S5_SCAN_ROWS)

    def powers(k, keep):
        p = jnp.exp(lam[None] * step[None] * jnp.asarray(k, F32)[:, None, None])
        return jnp.where(jnp.asarray(keep)[:, None, None], p, 0.0)

    tabs = [powers(np.full(S5_SCAN_ROWS, k), r >= k) for k in (1, 2, 4)]
    tabs.append(powers(r + 1, r >= 0))
    tab = jnp.stack(tabs)
    tab = tab.reshape(4, S5_SCAN_ROWS, nb, S5_BLK_STATE).transpose(2, 0, 1, 3)
    atab = jnp.concatenate([tab.real, tab.imag], axis=1)

    eye = jnp.eye(S5_BLK_GROUPS, dtype=F32)
    bb = b_bar.reshape(nb, S5_BLK_GROUPS, S5_STATE, S5_GROUP)

    def in_mat(x):
        return jnp.einsum('ngpc,gh->ngchp', x, eye).reshape(nb, S5_BLK_CH, S5_BLK_STATE)

    bmat = jnp.concatenate([in_mat(bb.real), in_mat(bb.imag)], axis=-1).astype(BF16)
    cr = c_re.astype(F32).reshape(nb, S5_BLK_GROUPS, S5_GROUP, S5_STATE)
    ci = c_im.astype(F32).reshape(nb, S5_BLK_GROUPS, S5_GROUP, S5_STATE)

    def out_mat(x):
        return jnp.einsum('ngcp,gh->ngphc', x, eye).reshape(nb, S5_BLK_STATE, S5_BLK_CH)

    cmat = jnp.concatenate([out_mat(cr), -out_mat(ci)], axis=1).astype(BF16)
    return atab, bmat, cmat


def _s5_scan_tile(xr, xi, cr, ci, atab_ref):
    for idx, k in enumerate((1, 2, 4)):
        pr, pi = atab_ref[0, idx], atab_ref[0, 4 + idx]
        sr, si = pltpu.roll(xr, k, 0), pltpu.roll(xi, k, 0)
        xr, xi = xr + pr * sr - pi * si, xi + pr * si + pi * sr
    pr, pi = atab_ref[0, 3], atab_ref[0, 7]
    xr, xi = xr + pr * cr - pi * ci, xi + pr * ci + pi * cr
    last = S5_SCAN_ROWS - 1
    cr = jnp.broadcast_to(xr[last:last + 1, :], xr.shape)
    ci = jnp.broadcast_to(xi[last:last + 1, :], xi.shape)
    return xr, xi, cr, ci


def _s5_kernel(u_ref, x0_ref, atab_ref, b_ref, c_ref, d_ref, y_ref, st_ref, xs_ref, *, T):
    u = u_ref[0]
    t_pad = _round_up(T, PACK_ROWS)
    xs_ref[...] = jnp.dot(_pad_rows(u, t_pad).astype(BF16), b_ref[0], preferred_element_type=F32)[:T]
    n = S5_BLK_STATE
    R = S5_SCAN_ROWS

    def body(i, carry):
        cr, ci = carry
        rows = pl.ds(pl.multiple_of(i * R, R), R)
        xr, xi, cr, ci = _s5_scan_tile(xs_ref[rows, :n], xs_ref[rows, n:], cr, ci, atab_ref)
        xs_ref[rows, :n] = xr
        xs_ref[rows, n:] = xi
        return cr, ci

    x0 = x0_ref[0, 0]
    cr0 = jnp.broadcast_to(x0[0:1, :], (R, n))
    ci0 = jnp.broadcast_to(x0[1:2, :], (R, n))
    cr, ci = lax.fori_loop(0, T // R, body, (cr0, ci0))
    st_ref[0, 0] = jnp.concatenate([cr[0:1], ci[0:1]], axis=0)
    y = jnp.dot(_pad_rows(xs_ref[...], t_pad).astype(BF16), c_ref[0], preferred_element_type=F32)[:T]
    y_ref[0] = _gelu_tanh(y + d_ref[...] * u)


def s5_scan(proj3, cols, x0, lam_re, lam_im, log_step, b_re, b_im, c_re, c_im, d):
    B, T, _ = proj3.shape
    assert T % S5_SCAN_ROWS == 0
    nb = S5_GROUPS // S5_BLK_GROUPS
    atab, bmat, cmat = _s5_params(lam_re, lam_im, log_step, b_re, b_im, c_re, c_im)
    x0b = x0.astype(F32).reshape(B, nb, S5_BLK_STATE, 2).transpose(0, 1, 3, 2)
    blk3 = lambda shape: pl.BlockSpec((1,) + shape, lambda b, j: (j, 0, 0))
    y, st = pl.pallas_call(
        functools.partial(_s5_kernel, T=T),
        grid=(B, nb),
        in_specs=[pl.BlockSpec((1, T, S5_BLK_CH), lambda b, j, o=cols['su'] // S5_BLK_CH: (b, 0, o + j)),
                  pl.BlockSpec((1, 1, 2, S5_BLK_STATE), lambda b, j: (b, j, 0, 0)),
                  pl.BlockSpec((1, 8, S5_SCAN_ROWS, S5_BLK_STATE), lambda b, j: (j, 0, 0, 0)),
                  blk3((S5_BLK_CH, 2 * S5_BLK_STATE)), blk3((2 * S5_BLK_STATE, S5_BLK_CH)),
                  pl.BlockSpec((1, S5_BLK_CH), lambda b, j: (0, j))],
        out_specs=(pl.BlockSpec((1, T, S5_BLK_CH), lambda b, j: (b, 0, j)),
                   pl.BlockSpec((1, 1, 2, S5_BLK_STATE), lambda b, j: (b, j, 0, 0))),
        out_shape=(jax.ShapeDtypeStruct((B, T, S5_WIDTH), F32),
                   jax.ShapeDtypeStruct((B, nb, 2, S5_BLK_STATE), F32)),
        scratch_shapes=[pltpu.VMEM((T, 2 * S5_BLK_STATE), F32)],
        compiler_params=pltpu.CompilerParams(dimension_semantics=("parallel", "parallel"),
                                             vmem_limit_bytes=V7X_VMEM_LIMIT_BYTES),
        name="s5_scan",
    )(proj3, x0b, atab, bmat, cmat, d.astype(F32).reshape(1, S5_WIDTH))
    st = st.transpose(0, 1, 3, 2).reshape(B, S5_GROUPS, S5_STATE, 2)
    return y, st


def _branch_norm_kernel(ro_ref, so_ref, no_ref, bn_ref, o_ref):
    off = 0
    for ref in (ro_ref, so_ref, no_ref):
        x = ref[...]
        w = x.shape[-1]
        y = x * lax.rsqrt(jnp.mean(x * x, axis=-1, keepdims=True) + EPS) * bn_ref[:, off:off + w]
        o_ref[:, off:off + w] = y.astype(o_ref.dtype)
        off += w


def branch_norm(ro, so, no, bn):
    m = ro.shape[0]
    tm = min(m, 256)
    assert m % tm == 0
    spec = lambda w: pl.BlockSpec((tm, w), lambda i: (i, 0))
    return pl.pallas_call(
        _branch_norm_kernel,
        grid=(m // tm,),
        in_specs=[spec(RET_WIDTH), spec(S5_WIDTH), spec(NSA_WIDTH), pl.BlockSpec((1, D_MODEL), lambda i: (0, 0))],
        out_specs=spec(D_MODEL),
        out_shape=jax.ShapeDtypeStruct((m, D_MODEL), BF16),
        compiler_params=pltpu.CompilerParams(dimension_semantics=("parallel",),
                                             vmem_limit_bytes=V7X_VMEM_LIMIT_BYTES),
        name="branch_norm",
    )(ro, so, no, bn.astype(F32).reshape(1, D_MODEL))


def _kv_rows_kernel(x_ref, o_ref):
    x = x_ref[0]
    for g in range(NSA_KV_HEADS):
        o_ref[pl.ds(g, x.shape[0], stride=NSA_KV_HEADS), :] = x[:, g * HEAD_DIM:(g + 1) * HEAD_DIM]


def kv_rows(proj3, col_k, t_start, t_len):
    B = proj3.shape[0]
    tT = min(t_len, 256)
    assert t_len % tT == 0 and t_start % tT == 0 and col_k % KV_WIDTH == 0
    n = t_len // tT
    rows = pl.pallas_call(
        _kv_rows_kernel,
        grid=(B, 2, n),
        in_specs=[pl.BlockSpec((1, tT, KV_WIDTH), lambda b, kv, i: (b, t_start // tT + i, col_k // KV_WIDTH + kv))],
        out_specs=pl.BlockSpec((tT * NSA_KV_HEADS, HEAD_DIM), lambda b, kv, i: ((b * 2 + kv) * n + i, 0)),
        out_shape=jax.ShapeDtypeStruct((B * 2 * t_len * NSA_KV_HEADS, HEAD_DIM), F32),
        compiler_params=pltpu.CompilerParams(dimension_semantics=("parallel", "parallel", "parallel"),
                                             vmem_limit_bytes=V7X_VMEM_LIMIT_BYTES),
        name="kv_rows",
    )(proj3)
    return rows.reshape(B, 2, t_len, NSA_KV_HEADS, HEAD_DIM)


_COL_NAMES = ('rq', 'rk', 'rv', 'rg', 'su', 'nq', 'kc', 'vc', 'ks', 'vs', 'kw', 'vw', 'ng')
COLS = {name: int(off) for name, off in zip(_COL_NAMES, np.concatenate([[0], np.cumsum(IN_SPLITS)]))}


def _block(x, layer, w, rel_bias, past, win_buf):
    B, T, _ = x.shape
    G = NSA_KV_HEADS
    M = B * T
    x2 = x.reshape(M, D_MODEL)
    h = rmsnorm(x2, w['norm_mix'][layer], BF16)
    proj3 = matmul(h, w['w_in'], layer).reshape(B, T, -1)
    ng = matmul(h, w['w_gate'], layer)[:, :3 * NSA_HEADS].reshape(B, T, 3 * NSA_HEADS)
    s5_w =[w[k][layer] for k in ('s5_lambda_re', 's5_lambda_im', 's5_log_step', 's5_b_re', 's5_b_im',
                                  's5_c_re', 's5_c_im', 's5_d')]
    cmp_w = [w[k][layer] for k in ('cmp_pe_k', 'cmp_w1_k', 'cmp_w2_k', 'cmp_pe_v', 'cmp_w1_v', 'cmp_w2_v')]
    if past is None:
        q0 = 0
        ret_s0 = jnp.zeros((B, RET_HEADS, HEAD_DIM, HEAD_DIM), F32)
        s5_s0 = jnp.zeros((B, S5_GROUPS, S5_STATE, 2), F32)
        win_prev = jnp.zeros((B, 2, WINDOW, G, HEAD_DIM), x.dtype)
    else:
        cache_cmp, cache_slc, cache_win, state_ret, state_s5, page_table = past
        q0 = page_table.shape[1] * PAGE_SIZE
        ret_s0, s5_s0, win_prev = state_ret[layer], state_s5[layer], cache_win[layer]

    ro, ret_s = retention(proj3, COLS, ret_s0, q0)
    sy, s5_s = s5_scan(proj3, COLS, s5_s0, *s5_w)
    if past is None:
        kcmp, vcmp = compress_prompt(proj3, COLS['kc'], COLS['vc'], *cmp_w)
        no = nsa_prompt(proj3, COLS, ng, kcmp, vcmp, rel_bias)
    else:
        no = nsa_sample(proj3, COLS, ng, cache_cmp, cache_slc, cache_win, page_table, layer, rel_bias, *cmp_w)

    sy2 = sy.reshape(M, S5_WIDTH)
    so = matmul(sy2, w['s5_w_glu'], layer, res=sy2, act="glu")
    mix = branch_norm(ro.reshape(M, RET_WIDTH), so, no.reshape(M, NSA_WIDTH), w['branch_norm'][layer])
    x2 = matmul(mix, w['w_out'], layer, res=x2)
    h = rmsnorm(x2, w['norm_ffn'][layer], BF16)
    up = matmul(h, w['w_up'], layer, act="relu2", out_dtype=BF16)
    x2 = matmul(up, w['w_down'], layer, res=x2)

    cmp_rows = kv_rows(proj3, COLS['kc'], 0, T)
    slc_rows = kv_rows(proj3, COLS['ks'], 0, T)
    if T >= win_buf:
        win_new = kv_rows(proj3, COLS['kw'], T - win_buf, win_buf)
    else:
        win_new = jnp.concatenate([win_prev, kv_rows(proj3, COLS['kw'], 0, T)], axis=2)[:, :, -win_buf:]
    return x2.reshape(B, T, D_MODEL), cmp_rows, slc_rows, win_new, ret_s, s5_s


def kernel(x_prompt, x_sample, cache_cmp, cache_slc, cache_win, state_ret, state_s5, page_table,
           rel_bias, norm_mix, w_in, s5_lambda_re, s5_lambda_im, s5_log_step, s5_b_re, s5_b_im,
           s5_c_re, s5_c_im, s5_d, s5_w_glu, cmp_pe_k, cmp_w1_k, cmp_w2_k, cmp_pe_v, cmp_w1_v,
           cmp_w2_v, branch_norm, w_out, norm_ffn, w_up, w_down, norm_final):
    win_buf = cache_win.shape[3]
    n_main = COLS['ng']
    assert n_main % LANE == 0
    w = dict(
        norm_mix=norm_mix, norm_ffn=norm_ffn, branch_norm=branch_norm,
        w_in=w_in[:, :, :n_main].astype(BF16),
        w_gate=jnp.pad(w_in[:, :, n_main:].astype(BF16), ((0, 0), (0, 0), (0, LANE - (IN_COLS - n_main)))),
        w_out=w_out.astype(BF16), w_up=w_up.astype(BF16), w_down=w_down.astype(BF16),
        s5_w_glu=s5_w_glu.astype(BF16),
        s5_lambda_re=s5_lambda_re, s5_lambda_im=s5_lambda_im, s5_log_step=s5_log_step, s5_b_re=s5_b_re,
        s5_b_im=s5_b_im, s5_c_re=s5_c_re, s5_c_im=s5_c_im, s5_d=s5_d,
        cmp_pe_k=cmp_pe_k, cmp_w1_k=cmp_w1_k, cmp_w2_k=cmp_w2_k,
        cmp_pe_v=cmp_pe_v, cmp_w1_v=cmp_w1_v, cmp_w2_v=cmp_w2_v)
    past = (cache_cmp, cache_slc, cache_win, state_ret, state_s5, page_table)
    xp, xs = x_prompt, x_sample
    written_p, written_s = [], []
    for layer in range(DEPTH):
        xp, *entries = _block(xp, layer, w, rel_bias, None, win_buf)
        written_p.append(entries)
        xs, *entries = _block(xs, layer, w, rel_bias, past, win_buf)
        written_s.append(entries)
    y_prompt = rmsnorm(xp.reshape(-1, D_MODEL), norm_final, F32).reshape(xp.shape)
    y_sample = rmsnorm(xs.reshape(-1, D_MODEL), norm_final, F32).reshape(xs.shape)
    stacked = lambda written, i, axis: jnp.stack([entries[i] for entries in written], axis=axis)
    return (y_prompt, y_sample,
            stacked(written_p, 0, 1), stacked(written_s, 0, 1),
            stacked(written_p, 1, 1), stacked(written_s, 1, 1),
            stacked(written_p, 2, 0), stacked(written_s, 2, 0),
            stacked(written_p, 3, 0), stacked(written_s, 3, 0),
            stacked(written_p, 4, 0), stacked(written_s, 4, 0))
```

```python
import functools
import math

import jax
import jax.numpy as jnp
import numpy as np
from jax import lax
from jax.experimental import pallas as pl
from jax.experimental.pallas import tpu as pltpu

F32 = jnp.float32
BF16 = jnp.bfloat16

D_MODEL = 4096
DEPTH = 2
PAGE_SIZE = 128
HEAD_DIM = 128
RET_WIDTH = 1024
RET_HEADS = 8
RET_CHUNK = 128
S5_WIDTH = 1024
S5_GROUP = 16
S5_GROUPS = 64
S5_STATE = 64
NSA_WIDTH = 2048
NSA_HEADS = 16
NSA_KV_HEADS = 4
NSA_HPG = 4
KV_WIDTH = 512
CMP_BLOCK = 32
CMP_STRIDE = 16
SEL_BLOCK = 64
SEL_TOP = 16
WINDOW = 512
FORCE_SCORE = 1e4
NEG = -1e30
N_BUCKETS = 32
MAX_DISTANCE = 128
EPS = 1e-6
IN_SPLITS = (RET_WIDTH, RET_WIDTH, RET_WIDTH, RET_WIDTH, S5_WIDTH, NSA_WIDTH,
             KV_WIDTH, KV_WIDTH, KV_WIDTH, KV_WIDTH, KV_WIDTH, KV_WIDTH, 3 * NSA_HEADS)
IN_COLS = sum(IN_SPLITS)

V7X_VMEM_LIMIT_BYTES = 48 * 1024 * 1024
LANE = 128
PACK_ROWS = 16


def _round_up(n, m):
    return -(-n // m) * m


def _largest_divisor(n, cap):
    return max(d for d in range(1, min(n, cap) + 1) if n % d == 0)


def _pad_rows(x, rows):
    extra = rows - x.shape[0]
    return jnp.concatenate([x, jnp.zeros((extra, x.shape[1]), x.dtype)], axis=0) if extra else x


def _gelu_tanh(x):
    return 0.5 * x * (1.0 + jnp.tanh(math.sqrt(2.0 / math.pi) * (x + 0.044715 * (x * x * x))))


def _dot_nt(a, b):
    return lax.dot_general(a, b, (((1,), (1,)), ((), ())), preferred_element_type=F32)


def _dot_tn(a, b):
    return lax.dot_general(a, b, (((0,), (0,)), ((), ())), preferred_element_type=F32)


def _rmsnorm_kernel(x_ref, g_ref, o_ref):
    x = x_ref[...].astype(F32)
    ms = jnp.mean(x * x, axis=-1, keepdims=True)
    o_ref[...] = (x * lax.rsqrt(ms + EPS) * g_ref[...].astype(F32)).astype(o_ref.dtype)


def rmsnorm(x2d, gain, out_dtype):
    m, d = x2d.shape
    tm = min(m, 256)
    assert m % tm == 0
    return pl.pallas_call(
        _rmsnorm_kernel,
        grid=(m // tm,),
        in_specs=[pl.BlockSpec((tm, d), lambda i: (i, 0)),
                  pl.BlockSpec((1, d), lambda i: (0, 0))],
        out_specs=pl.BlockSpec((tm, d), lambda i: (i, 0)),
        out_shape=jax.ShapeDtypeStruct((m, d), out_dtype),
        compiler_params=pltpu.CompilerParams(dimension_semantics=("parallel",),
                                             vmem_limit_bytes=V7X_VMEM_LIMIT_BYTES),
        name="rmsnorm",
    )(x2d, gain.reshape(1, d))


def _mm_kernel(*refs, nk, act, has_res):
    if has_res:
        a_ref, w_ref, r_ref, o_ref, acc_ref = refs
    else:
        a_ref, w_ref, o_ref, acc_ref = refs
        r_ref = None
    k = pl.program_id(2)

    @pl.when(k == 0)
    def _():
        acc_ref[...] = jnp.zeros_like(acc_ref)

    acc_ref[...] += jnp.dot(a_ref[...].astype(BF16), w_ref[...], preferred_element_type=F32)

    @pl.when(k == nk - 1)
    def _():
        acc = acc_ref[...]
        if act == "relu2":
            acc = jnp.square(jnp.maximum(acc, 0.0))
        if act == "glu":
            acc = r_ref[...].astype(F32) * jax.nn.sigmoid(acc)
        elif has_res:
            acc = acc + r_ref[...].astype(F32)
        o_ref[...] = acc.astype(o_ref.dtype)


def matmul(a, w, layer, *, n=None, res=None, act=None, out_dtype=F32):
    m, kdim = a.shape
    n = w.shape[2] if n is None else n
    tm = min(m, 1024)
    tn = 512 if n % 512 == 0 else (256 if n % 256 == 0 else 128)
    tk = min(kdim, 4096)
    assert m % tm == 0 and n % tn == 0 and kdim % tk == 0
    nk = kdim // tk
    in_specs = [pl.BlockSpec((tm, tk), lambda i, j, k: (i, k)),
                pl.BlockSpec((None, tk, tn), lambda i, j, k: (layer, k, j))]
    args = [a, w]
    if res is not None:
        in_specs.append(pl.BlockSpec((tm, tn), lambda i, j, k: (i, j)))
        args.append(res)
    return pl.pallas_call(
        functools.partial(_mm_kernel, nk=nk, act=act, has_res=res is not None),
        grid=(m // tm, n // tn, nk),
        in_specs=in_specs,
        out_specs=pl.BlockSpec((tm, tn), lambda i, j, k: (i, j)),
        out_shape=jax.ShapeDtypeStruct((m, n), out_dtype),
        scratch_shapes=[pltpu.VMEM((tm, tn), F32)],
        compiler_params=pltpu.CompilerParams(
            dimension_semantics=("parallel", "parallel", "arbitrary"),
            vmem_limit_bytes=V7X_VMEM_LIMIT_BYTES),
        name="matmul",
    )(*args)


def _compress_rows(x_ref, n_full, pe_ref, w1_ref, w2_ref):
    pieces = [x_ref[0, pl.ds(s, n_full, stride=CMP_STRIDE), :] for s in range(CMP_STRIDE)]
    x = jnp.concatenate(pieces, axis=1).astype(BF16)
    half = CMP_STRIDE * HEAD_DIM
    z0 = jnp.dot(x, w1_ref[:half, :], preferred_element_type=F32)
    z1 = jnp.dot(x, w1_ref[half:, :], preferred_element_type=F32)
    z1 = pltpu.roll(z1, n_full - 1, 0)
    pe = jnp.broadcast_to(pe_ref[...], (8, CMP_BLOCK * HEAD_DIM)).astype(BF16)
    h0 = jnp.dot(pe, w1_ref[...], preferred_element_type=F32)[0:1, :]
    h = _gelu_tanh(z0 + z1 + h0)
    return jnp.dot(h.astype(BF16), w2_ref[...], preferred_element_type=F32)


def _compress_prompt_kernel(xk_ref, xv_ref, pek_ref, w1k_ref, w2k_ref, pev_ref, w1v_ref, w2v_ref,
                            kc_ref, vc_ref, *, n_full):
    for x_ref, pe_ref, w1_ref, w2_ref, o_ref in ((xk_ref, pek_ref, w1k_ref, w2k_ref, kc_ref),
                                                  (xv_ref, pev_ref, w1v_ref, w2v_ref, vc_ref)):
        out = _compress_rows(x_ref, n_full, pe_ref, w1_ref, w2_ref).astype(o_ref.dtype)
        n_pad = o_ref.shape[2]
        o_ref[0, 0, :n_full, :] = out
        if n_pad > n_full:
            o_ref[0, 0, n_full:, :] = jnp.zeros((n_pad - n_full, HEAD_DIM), o_ref.dtype)


def compress_prompt(proj3, col_k, col_v, pe_k, w1_k, w2_k, pe_v, w1_v, w2_v):
    B, T, _ = proj3.shape
    n_full = T // CMP_STRIDE
    n_pad = _round_up(n_full, LANE)
    flat = CMP_BLOCK * HEAD_DIM
    wspec = [pl.BlockSpec((1, flat), lambda b, g: (0, 0)),
             pl.BlockSpec((flat, HEAD_DIM), lambda b, g: (0, 0)),
             pl.BlockSpec((HEAD_DIM, HEAD_DIM), lambda b, g: (0, 0))]
    out_sds = jax.ShapeDtypeStruct((B, NSA_KV_HEADS, n_pad, HEAD_DIM), BF16)
    ospec = pl.BlockSpec((1, 1, n_pad, HEAD_DIM), lambda b, g: (b, g, 0, 0))
    return pl.pallas_call(
        functools.partial(_compress_prompt_kernel, n_full=n_full),
        grid=(B, NSA_KV_HEADS),
        in_specs=[pl.BlockSpec((1, T, HEAD_DIM), lambda b, g: (b, 0, col_k // HEAD_DIM + g)),
                  pl.BlockSpec((1, T, HEAD_DIM), lambda b, g: (b, 0, col_v // HEAD_DIM + g))] + wspec + wspec,
        out_specs=(ospec, ospec),
        out_shape=(out_sds, out_sds),
        compiler_params=pltpu.CompilerParams(dimension_semantics=("parallel", "parallel"),
                                             vmem_limit_bytes=V7X_VMEM_LIMIT_BYTES),
        name="nsa_compress_prompt",
    )(proj3, proj3,
      pe_k.reshape(1, flat), w1_k.reshape(flat, HEAD_DIM).astype(BF16), w2_k.astype(BF16),
      pe_v.reshape(1, flat), w1_v.reshape(flat, HEAD_DIM).astype(BF16), w2_v.astype(BF16))


def _bucket_table():
    d = np.arange(MAX_DISTANCE)
    max_exact = N_BUCKETS // 2
    large = max_exact + (np.log(np.maximum(d, 1).astype(np.float32) / np.float32(max_exact))
                         / np.float32(math.log(MAX_DISTANCE / max_exact))
                         * np.float32(N_BUCKETS - max_exact)).astype(np.int32)
    return np.where(d < max_exact, d, np.minimum(large, N_BUCKETS - 1)).astype(np.int32)


def _bias_of_distance(rel_bias, dist):
    bt = _bucket_table()
    buckets = bt[np.clip(dist, 0, MAX_DISTANCE - 1)]
    b = jnp.moveaxis(rel_bias.astype(F32)[buckets], -1, 0)
    return jnp.where(jnp.asarray(dist >= 0)[None], b, NEG)


def _bias_by_distance(rel_bias):
    return rel_bias.astype(F32)[_bucket_table()].T


def _toeplitz(v, n):
    h = v.shape[0]
    w = jnp.pad(v, ((0, 0), (0, 1)))
    m = jnp.tile(w, (1, n))[:, :n * (2 * n - 1)].reshape(h, n, 2 * n - 1)
    return m[:, :, n - 1:]


ATT_TILE = 128
N_BIAS_TILES = 4
SLC_CLASS_TILES = 2
MXU_TILES = 2


def _rows_softmax_pv(s_tiles, v_rows):
    m = s_tiles[0]
    for s in s_tiles[1:]:
        m = jnp.maximum(m, s)
    m = jnp.max(m, axis=-1, keepdims=True)
    l = None
    o = None
    for i in range(0, len(s_tiles), MXU_TILES):
        ps = [jnp.exp(s - m) for s in s_tiles[i:i + MXU_TILES]]
        for p in ps:
            l = p if l is None else l + p
        p_cat = ps[0] if len(ps) == 1 else jnp.concatenate(ps, axis=1)
        pv = jnp.dot(p_cat.astype(BF16), v_rows(i, len(ps)), preferred_element_type=F32)
        o = pv if o is None else o + pv
    return o / jnp.sum(l, axis=-1, keepdims=True)


def _score_tiles(q, k_rows, n_tiles):
    tiles = []
    for i in range(0, n_tiles, MXU_TILES):
        n = min(MXU_TILES, n_tiles - i)
        s = _dot_nt(q, k_rows(i, n))
        tiles += [s[:, j * LANE:(j + 1) * LANE] for j in range(n)]
    return tiles


def _nsa_prompt_kernel(q_ref, kc_ref, vc_ref, ks_ref, vs_ref, kw_ref, vw_ref, gate_ref,
                       bcmp_ref, btile_ref, ovl_ref, pick_ref, cvec_ref, kaug_s_ref, kaug_w_ref, o_ref,
                       ksb, vsb, kwb, vwb, s_ref, *, n_sel, n_top, T):
    tq = ATT_TILE
    J = NSA_HPG
    qi = pl.program_id(2)
    n_sel_pad = pick_ref.shape[0]
    nq = T // tq

    @pl.when(qi == 0)
    def _():
        ksb[:, :HEAD_DIM] = ks_ref[0].astype(BF16)
        ksb[:, HEAD_DIM:] = kaug_s_ref[...]
        vsb[...] = vs_ref[0].astype(BF16)
        kwb[:WINDOW, :HEAD_DIM] = jnp.zeros((WINDOW, HEAD_DIM), BF16)
        kwb[WINDOW:, :HEAD_DIM] = kw_ref[0].astype(BF16)
        kwb[:, HEAD_DIM:] = kaug_w_ref[...]
        vwb[:WINDOW, :] = jnp.zeros((WINDOW, HEAD_DIM), BF16)
        vwb[WINDOW:, :] = vw_ref[0].astype(BF16)

    def stack(f):
        return jnp.concatenate([f(j) for j in range(J)], axis=0)

    def put(vals, first):
        for j in range(J):
            cols = slice(j * HEAD_DIM, (j + 1) * HEAD_DIM)
            v = vals[j * tq:(j + 1) * tq]
            o_ref[0, :, cols] = v if first else o_ref[0, :, cols] + v

    gates = jax.nn.sigmoid(gate_ref[0, 0])
    gate = lambda c: stack(lambda j: gates[:, 3 * j + c:3 * j + c + 1])
    q_all = stack(lambda j: q_ref[0, :, j * HEAD_DIM:(j + 1) * HEAD_DIM] * HEAD_DIM ** -0.5).astype(BF16)

    row = lax.broadcasted_iota(jnp.int32, (tq, LANE), 0)
    lane = lax.broadcasted_iota(jnp.int32, (tq, LANE), 1)
    valid1 = qi * tq + row >= lane * CMP_STRIDE + (CMP_BLOCK - 1)
    valid = stack(lambda j: valid1)
    s = jnp.where(valid, _dot_nt(q_all, kc_ref[0, 0]) + stack(lambda j: bcmp_ref[j]), NEG)
    e = jnp.where(valid, jnp.exp(s - jnp.max(s, axis=-1, keepdims=True)), 0.0)
    den = jnp.sum(e, axis=-1, keepdims=True)
    p = e / jnp.where(den > 0.0, den, 1.0)
    put(gate(0) * jnp.dot(p.astype(BF16), vc_ref[0, 0], preferred_element_type=F32), True)
    psum = p[0:tq]
    for j in range(1, J):
        psum = psum + p[j * tq:(j + 1) * tq]

    bt = lambda i: stack(lambda j: btile_ref[j, i])

    pad_flag = jnp.broadcast_to(-cvec_ref[...], (tq, LANE)).astype(BF16)
    q_win = jnp.concatenate([q_all, stack(lambda j: pad_flag)], axis=1)
    n_w = WINDOW // tq + 1
    win_rows = lambda i, n: pl.ds(pl.multiple_of((qi + i) * tq, tq), n * tq)
    s_tiles = _score_tiles(q_win, lambda i, n: kwb[win_rows(i, n), :], n_w)
    s_tiles[0] = s_tiles[0] + bt(3)
    s_tiles[n_w - 2] = s_tiles[n_w - 2] + bt(1)
    s_tiles[n_w - 1] = s_tiles[n_w - 1] + bt(0)
    put(gate(2) * _rows_softmax_pv(s_tiles, lambda i, n: vwb[win_rows(i, n), :]), False)

    hi = psum.astype(BF16)
    r1 = psum - hi.astype(F32)
    mid = r1.astype(BF16)
    lo = (r1 - mid.astype(F32)).astype(BF16)
    ovl = ovl_ref[...]
    p_sel = _dot_nt(ovl, hi) + _dot_nt(ovl, mid) + _dot_nt(ovl, lo)
    blk = lax.broadcasted_iota(jnp.int32, (n_sel_pad, tq), 0)
    cur = (qi * tq + lax.broadcasted_iota(jnp.int32, (n_sel_pad, tq), 1)) // SEL_BLOCK
    forced = (blk == 0) | (blk == cur) | (blk == cur - 1)
    score = jnp.where(blk <= cur, p_sel + jnp.where(forced, FORCE_SCORE, 0.0), NEG)
    rank = jnp.zeros((n_sel_pad, tq), F32)
    for jb in range(n_sel):
        other = score[jb:jb + 1, :]
        beats = (other > score) | ((other == score) & (blk > jb))
        rank = rank + jnp.where(beats, 1.0, 0.0)
    sel = jnp.where((rank < n_top) & (blk <= cur) & (blk < n_sel), 1.0, 0.0).astype(BF16)
    aug = (_dot_tn(sel, pick_ref[...]) - cvec_ref[...]).astype(BF16)
    q_aug = jnp.concatenate([q_all, stack(lambda j: aug)], axis=1)

    for cls in range(-(-nq // SLC_CLASS_TILES)):
        n_t = min((cls + 1) * SLC_CLASS_TILES, nq)

        @pl.when(qi // SLC_CLASS_TILES == cls)
        def _(n_t=n_t):
            for kt, s in enumerate(_score_tiles(q_aug, lambda i, n: ksb[i * tq:(i + n) * tq, :], n_t)):
                s_ref[kt] = s
            s_ref[qi] = s_ref[qi] + bt(0)

            @pl.when(qi > 0)
            def _():
                s_ref[qi - 1] = s_ref[qi - 1] + bt(1)

            o = _rows_softmax_pv([s_ref[kt] for kt in range(n_t)], lambda i, n: vsb[i * tq:(i + n) * tq, :])
            put(gate(1) * o, False)


def nsa_prompt(proj3, cols, ng, kcmp, vcmp, rel_bias):
    B, T, _ = proj3.shape
    tq = ATT_TILE
    assert T % tq == 0 and T % SEL_BLOCK == 0 and WINDOW % tq == 0
    G, J = NSA_KV_HEADS, NSA_HPG
    nq = T // tq
    n_sel = T // SEL_BLOCK
    n_top = min(SEL_TOP, n_sel)
    n_sel_pad = _round_up(n_sel, 16)
    assert n_sel_pad < LANE
    n_pad = kcmp.shape[2]
    assert n_pad == LANE, "one lane tile of compressed blocks"

    bd = _bias_by_distance(rel_bias)
    n_a = T // CMP_STRIDE
    assert n_a == n_pad
    d_cmp = (CMP_STRIDE * ((n_a - 1) - np.arange(2 * n_a - 1))[None, :]
             + np.arange(CMP_STRIDE)[:, None] - (CMP_BLOCK - 1))
    gen_cmp = jnp.where(jnp.asarray(d_cmp >= 0)[None], bd[:, np.clip(d_cmp, 0, MAX_DISTANCE - 1)], 0.0)
    bcmp = _toeplitz(gen_cmp.reshape(NSA_HEADS * CMP_STRIDE, 2 * n_a - 1), n_a)
    bcmp = bcmp.reshape(NSA_HEADS, CMP_STRIDE, n_a, n_a).transpose(0, 2, 1, 3).reshape(NSA_HEADS, T, n_pad)
    d_diag = (tq - 1) - np.arange(2 * tq - 1)
    rel = bd - bd[:, -1:]
    gen = lambda d: jnp.where(jnp.asarray(d >= 0)[None], rel[:, np.clip(d, 0, MAX_DISTANCE - 1)], NEG)
    edge = jnp.broadcast_to(jnp.where(jnp.asarray(d_diag <= 0), 0.0, NEG)[None], (NSA_HEADS, 2 * tq - 1))
    btile = jnp.stack([_toeplitz(gen(d_diag), tq), _toeplitz(gen(d_diag + tq), tq),
                       jnp.zeros((NSA_HEADS, tq, tq), F32), _toeplitz(edge, tq)], axis=1)
    cmp_start = np.arange(n_pad) * CMP_STRIDE
    sel_start = np.arange(n_sel_pad) * SEL_BLOCK
    ovl = ((cmp_start[None, :] < sel_start[:, None] + SEL_BLOCK)
           & (cmp_start[None, :] + CMP_BLOCK > sel_start[:, None])
           & (np.arange(n_pad)[None, :] < T // CMP_STRIDE - 1))
    ovl = jnp.asarray(ovl, BF16)
    pick = jnp.asarray(np.arange(LANE)[None, :] == np.arange(n_sel_pad)[:, None], BF16)
    lane_i = np.arange(LANE)
    cvec = jnp.asarray(((lane_i < n_sel) | (lane_i == n_sel_pad))[None, :], F32)
    big = -NEG
    kaug_s = jnp.asarray(np.where(np.arange(T)[:, None] // SEL_BLOCK == lane_i[None, :], big, 0.0), BF16)
    kaug_w = jnp.asarray(np.where((np.arange(T + WINDOW)[:, None] < WINDOW) & (lane_i[None, :] == n_sel_pad),
                                  big, 0.0), BF16)
    gates = ng.reshape(B, T, G, 3 * J).transpose(0, 2, 1, 3)

    kv_spec = lambda name: pl.BlockSpec((1, T, HEAD_DIM),
                                        lambda b, g, i, o=cols[name] // HEAD_DIM: (b, 0, o + g))
    cmp_spec = pl.BlockSpec((1, 1, n_pad, HEAD_DIM), lambda b, g, i: (b, g, 0, 0))
    const2 = lambda shape: pl.BlockSpec(shape, lambda b, g, i: (0, 0))
    return pl.pallas_call(
        functools.partial(_nsa_prompt_kernel, n_sel=n_sel, n_top=n_top, T=T),
        grid=(B, G, nq),
        in_specs=[pl.BlockSpec((1, tq, J * HEAD_DIM),
                               lambda b, g, i, o=cols['nq'] // (J * HEAD_DIM): (b, i, o + g)),
                  cmp_spec, cmp_spec,
                  kv_spec('ks'), kv_spec('vs'), kv_spec('kw'), kv_spec('vw'),
                  pl.BlockSpec((1, 1, tq, 3 * J), lambda b, g, i: (b, g, i, 0)),
                  pl.BlockSpec((J, tq, n_pad), lambda b, g, i: (g, i, 0)),
                  pl.BlockSpec((J, N_BIAS_TILES, tq, tq), lambda b, g, i: (g, 0, 0, 0)),
                  const2((n_sel_pad, n_pad)), const2((n_sel_pad, LANE)), const2((1, LANE)),
                  const2((T, LANE)), const2((T + WINDOW, LANE))],
        out_specs=pl.BlockSpec((1, tq, J * HEAD_DIM), lambda b, g, i: (b, i, g)),
        out_shape=jax.ShapeDtypeStruct((B, T, NSA_WIDTH), F32),
        scratch_shapes=[pltpu.VMEM((T, 2 * HEAD_DIM), BF16), pltpu.VMEM((T, HEAD_DIM), BF16),
                        pltpu.VMEM((T + WINDOW, 2 * HEAD_DIM), BF16), pltpu.VMEM((T + WINDOW, HEAD_DIM), BF16),
                        pltpu.VMEM((nq, J * tq, tq), F32)],
        compiler_params=pltpu.CompilerParams(
            dimension_semantics=("parallel", "parallel", "arbitrary"),
            vmem_limit_bytes=V7X_VMEM_LIMIT_BYTES),
        name="nsa_prompt",
    )(proj3, kcmp, vcmp, proj3, proj3, proj3, proj3, gates, bcmp, btile, ovl, pick, cvec, kaug_s, kaug_w)


PAGE_ROWS = PAGE_SIZE * NSA_KV_HEADS
STRIDES_PER_PAGE = PAGE_SIZE // CMP_STRIDE
PAGES_PER_STEP = 8


def _cache_rows(cache):
    return cache.reshape(-1, HEAD_DIM)


def _page_specs(layer):
    return [pl.BlockSpec((2 * PAGE_ROWS, HEAD_DIM),
                         lambda b, p, pt, k=k: (pt[b, p * PAGES_PER_STEP + k] * DEPTH + layer, 0))
            for k in range(PAGES_PER_STEP)]


def _cmp_partial_kernel(pt_ref, *refs):
    page_refs = refs[:PAGES_PER_STEP]
    w1k_ref, w1v_ref, z_ref = refs[PAGES_PER_STEP:]
    G = NSA_KV_HEADS
    half = CMP_STRIDE * HEAD_DIM
    n_rows = PAGES_PER_STEP * STRIDES_PER_PAGE
    for kv, w1_ref in ((0, w1k_ref), (1, w1v_ref)):
        slabs = [jnp.transpose(r[kv * PAGE_ROWS:(kv + 1) * PAGE_ROWS, :].reshape(
            STRIDES_PER_PAGE, CMP_STRIDE * G, HEAD_DIM), (1, 0, 2)) for r in page_refs]
        xs = []
        for g in range(G):
            for slab in slabs:
                xs.append(jnp.concatenate([slab[s * G + g] for s in range(CMP_STRIDE)], axis=1))
        x = jnp.concatenate(xs, axis=0).astype(BF16)
        for j in range(CMP_BLOCK // CMP_STRIDE):
            z = jnp.dot(x, w1_ref[j * half:(j + 1) * half, :], preferred_element_type=F32)
            for g in range(G):
                z_ref[0, kv, j, g] = z[g * n_rows:(g + 1) * n_rows]


def cmp_partial(cache_rows, page_table, layer, w1_k, w1_v):
    B, n_pages = page_table.shape
    flat = CMP_BLOCK * HEAD_DIM
    n_str = n_pages * STRIDES_PER_PAGE
    assert n_pages % PAGES_PER_STEP == 0
    wspec = pl.BlockSpec((flat, HEAD_DIM), lambda b, p, pt: (0, 0))
    return pl.pallas_call(
        _cmp_partial_kernel,
        grid_spec=pltpu.PrefetchScalarGridSpec(
            num_scalar_prefetch=1,
            grid=(B, n_pages // PAGES_PER_STEP),
            in_specs=_page_specs(layer) + [wspec, wspec],
            out_specs=pl.BlockSpec((1, 2, 2, NSA_KV_HEADS, PAGES_PER_STEP * STRIDES_PER_PAGE, HEAD_DIM),
                                   lambda b, p, pt: (b, 0, 0, 0, p, 0))),
        out_shape=jax.ShapeDtypeStruct((B, 2, 2, NSA_KV_HEADS, n_str, HEAD_DIM), F32),
        compiler_params=pltpu.CompilerParams(dimension_semantics=("parallel", "arbitrary"),
                                             vmem_limit_bytes=V7X_VMEM_LIMIT_BYTES),
        name="nsa_cmp_partial",
    )(page_table, *([cache_rows] * PAGES_PER_STEP),
      w1_k.reshape(flat, HEAD_DIM).astype(BF16), w1_v.reshape(flat, HEAD_DIM).astype(BF16))


def _cmp_attn_sample_kernel(z_ref, pek_ref, w1k_ref, w2k_ref, pev_ref, w1v_ref, w2v_ref, q_ref, bias_ref,
                            ovl_ref, o_ref, psel_ref, *, q0):
    T = q_ref.shape[1]
    J = NSA_HPG
    n_str = z_ref.shape[4]

    def finish(kv, pe_ref, w1_ref, w2_ref):
        z1 = pltpu.roll(z_ref[0, kv, 1, 0], n_str - 1, 0)
        pe = jnp.broadcast_to(pe_ref[...], (PACK_ROWS, CMP_BLOCK * HEAD_DIM)).astype(BF16)
        h0 = jnp.dot(pe, w1_ref[...], preferred_element_type=F32)[0:1, :]
        h = _gelu_tanh(z_ref[0, kv, 0, 0] + z1 + h0)
        return jnp.dot(h.astype(BF16), w2_ref[...], preferred_element_type=F32).astype(BF16)

    kc = finish(0, pek_ref, w1k_ref, w2k_ref)
    vc = finish(1, pev_ref, w1v_ref, w2v_ref)
    stack = lambda f: jnp.concatenate([f(j) for j in range(J)], axis=0)
    q_all = stack(lambda j: q_ref[0, :, j * HEAD_DIM:(j + 1) * HEAD_DIM] * HEAD_DIM ** -0.5).astype(BF16)
    row = lax.broadcasted_iota(jnp.int32, (T, n_str), 0)
    lane = lax.broadcasted_iota(jnp.int32, (T, n_str), 1)
    valid1 = (q0 + row >= lane * CMP_STRIDE + (CMP_BLOCK - 1)) & (lane < n_str - 1)
    valid = stack(lambda j: valid1)
    s = jnp.where(valid, _dot_nt(q_all, kc) + stack(lambda j: bias_ref[j]), NEG)
    e = jnp.where(valid, jnp.exp(s - jnp.max(s, axis=-1, keepdims=True)), 0.0)
    den = jnp.sum(e, axis=-1, keepdims=True)
    p = e / jnp.where(den > 0.0, den, 1.0)
    o_ref[0, 0] = jnp.dot(p.astype(BF16), vc, preferred_element_type=F32)
    psum = p[0:T]
    for j in range(1, J):
        psum = psum + p[j * T:(j + 1) * T]
    psum = _pad_rows(psum, PACK_ROWS)
    hi = psum.astype(BF16)
    r1 = psum - hi.astype(F32)
    mid = r1.astype(BF16)
    lo = (r1 - mid.astype(F32)).astype(BF16)
    ovl = ovl_ref[...]
    p_sel = (jnp.dot(hi, ovl, preferred_element_type=F32) + jnp.dot(mid, ovl, preferred_element_type=F32)
             + jnp.dot(lo, ovl, preferred_element_type=F32))
    psel_ref[0, 0] = p_sel[:T]


def _rank_select_kernel(psel_ref, tpos_ref, out_ref, score_ref, rank_ref, *, n_sel, n_top):
    shape = psel_ref.shape
    blk = lax.broadcasted_iota(jnp.int32, shape, 0)
    cur = jnp.broadcast_to(tpos_ref[...], shape) // SEL_BLOCK
    forced = (blk == 0) | (blk == cur) | (blk == cur - 1)
    ok = (blk <= cur) & (blk < n_sel)
    score = jnp.where(ok, psel_ref[...] + jnp.where(forced, FORCE_SCORE, 0.0), NEG)
    score_ref[...] = score
    rank_ref[...] = jnp.zeros(shape, F32)

    def body(jb, c):
        other = jnp.broadcast_to(score_ref[pl.ds(jb, 1), :], shape)
        beats = (other > score) | ((other == score) & (blk > jb))
        rank_ref[...] = rank_ref[...] + jnp.where(beats, 1.0, 0.0)
        return c

    lax.fori_loop(0, n_sel, body, 0)
    out_ref[...] = jnp.where((rank_ref[...] < n_top) & ok, 0.0, NEG)


def _slc_win_sample_kernel(pt_ref, *refs, win_buf):
    page_refs = refs[:PAGES_PER_STEP]
    (qT_ref, seladd_ref, blast_ref, selnew_ref, ksn_ref, vsn_ref, kwn_ref, vwn_ref, bnew_ref, cw_ref, bwin_ref,
     ocmp_ref, gate_ref, o_ref, m_ref, l_ref, acc_ref) = refs[PAGES_PER_STEP:]
    G = NSA_KV_HEADS
    p = pl.program_id(1)
    last = pl.num_programs(1) - 1
    lane_group = lax.broadcasted_iota(jnp.int32, (1, LANE), 1) // (LANE // G)

    @pl.when(p == 0)
    def _():
        m_ref[...] = jnp.full(m_ref.shape, NEG, F32)
        l_ref[...] = jnp.zeros(l_ref.shape, F32)
        acc_ref[...] = jnp.zeros(acc_ref.shape, F32)

    def scores(k_of_g):
        s = None
        for g in range(G):
            sg = jnp.dot(k_of_g(g).astype(BF16), qT_ref[0, g], preferred_element_type=F32)
            s = sg if s is None else s + sg
        return s

    def weighted_values(pT, v_of_g):
        o = None
        for g in range(G):
            pg = jnp.where(lane_group == g, pT, 0.0).astype(BF16)
            og = _dot_tn(v_of_g(g).astype(BF16), pg)
            o = og if o is None else o + og
        return o

    def accumulate(segments):
        m_old = m_ref[0:1, :]
        m_new = m_old
        for sT, _ in segments:
            m_new = jnp.maximum(m_new, jnp.max(sT, axis=0, keepdims=True))
        alpha = jnp.exp(m_old - m_new)
        l_new = alpha * l_ref[0:1, :]
        acc = alpha * acc_ref[...]
        for sT, v_of_g in segments:
            pT = jnp.exp(sT - m_new)
            l_new = l_new + jnp.sum(pT, axis=0, keepdims=True)
            acc = acc + weighted_values(pT, v_of_g)
        l_ref[...] = jnp.broadcast_to(l_new, l_ref.shape)
        acc_ref[...] = acc
        m_ref[...] = jnp.broadcast_to(m_new, m_ref.shape)

    key = lax.broadcasted_iota(jnp.int32, (PAGE_SIZE, LANE), 0)
    blocks_per_page = PAGE_SIZE // SEL_BLOCK
    is_last = (p == last).astype(F32)
    segments = []
    for k, page_ref in enumerate(page_refs):
        k_page = lambda g, r=page_ref: r[pl.ds(g, PAGE_SIZE, stride=G), :]
        v_page = lambda g, r=page_ref: r[pl.ds(PAGE_ROWS + g, PAGE_SIZE, stride=G), :]
        mask = seladd_ref[0, k, blocks_per_page - 1:blocks_per_page, :]
        for i in range(blocks_per_page - 2, -1, -1):
            mask = jnp.where(key < (i + 1) * SEL_BLOCK, seladd_ref[0, k, i:i + 1, :], mask)
        sT = scores(k_page) + mask
        if k == PAGES_PER_STEP - 1:
            sT = sT + blast_ref[...] * is_last
        segments.append((sT, v_page))
    accumulate(segments)

    @pl.when(p == last)
    def _():
        new = lambda ref: (lambda g: _pad_rows(ref[0, :, g * HEAD_DIM:(g + 1) * HEAD_DIM], PACK_ROWS))
        accumulate([(scores(new(ksn_ref)) + bnew_ref[...] + selnew_ref[0], new(vsn_ref))])
        o_slc = (acc_ref[...] / l_ref[0:1, :]).T

        k_win = lambda g: cw_ref[pl.ds(g, win_buf, stride=G), :]
        v_win = lambda g: cw_ref[pl.ds(win_buf * G + g, win_buf, stride=G), :]
        s_w = scores(k_win) + bwin_ref[...]
        s_n = scores(new(kwn_ref)) + bnew_ref[...]
        m = jnp.maximum(jnp.max(s_w, axis=0, keepdims=True), jnp.max(s_n, axis=0, keepdims=True))
        p_w = jnp.exp(s_w - m)
        p_n = jnp.exp(s_n - m)
        den = jnp.sum(p_w, axis=0, keepdims=True) + jnp.sum(p_n, axis=0, keepdims=True)
        o_win = ((weighted_values(p_w, v_win) + weighted_values(p_n, new(vwn_ref))) / den).T

        gates = jax.nn.sigmoid(gate_ref[0])
        o_ref[0] = gates[:, 0:1] * ocmp_ref[0] + gates[:, 1:2] * o_slc + gates[:, 2:3] * o_win


def nsa_sample(proj3, cols, ng, cache_cmp, cache_slc, cache_win, page_table, layer, rel_bias,
               pe_k, w1_k, w2_k, pe_v, w1_v, w2_v):
    B, T, _ = proj3.shape
    G, J, H = NSA_KV_HEADS, NSA_HPG, NSA_HEADS
    n_pages = page_table.shape[1]
    past = n_pages * PAGE_SIZE
    q0 = past
    win_buf = cache_win.shape[3]
    L = G * J * T
    assert L == LANE and T <= PACK_ROWS and T < CMP_STRIDE and past % SEL_BLOCK == 0 and T <= SEL_BLOCK
    assert win_buf == min(WINDOW, past)
    n_str = past // CMP_STRIDE
    n_sel = past // SEL_BLOCK + 1
    n_top = min(SEL_TOP, n_sel)
    n_sel_rows = _round_up(n_sel, 8)
    n_sel_lanes = _round_up(n_sel, LANE)
    flat = CMP_BLOCK * HEAD_DIM
    b_far = rel_bias.astype(F32)[_bucket_table()[MAX_DISTANCE - 1]]

    def lane_bias(dist, ok):
        b = _bias_of_distance(rel_bias, np.maximum(dist, 0)) - b_far[:, None, None]
        b = jnp.where(jnp.asarray(ok)[None], b, NEG)
        return jnp.moveaxis(b.reshape(G, J, dist.shape[0], T), 2, 0).reshape(dist.shape[0], L)

    ti = np.arange(T)[None, :]
    z = cmp_partial(_cache_rows(cache_cmp), page_table, layer, w1_k, w1_v)
    n = np.arange(n_str)[None, :]
    dist_c = q0 + np.arange(T)[:, None] - (n * CMP_STRIDE + CMP_BLOCK - 1)
    near = dist_c.min(axis=0) < MAX_DISTANCE
    n_far = int(np.argmax(near)) if near.any() else n_str
    bcmp = jnp.concatenate([jnp.broadcast_to(b_far[:, None, None], (H, T, n_far)),
                            _bias_of_distance(rel_bias, np.maximum(dist_c[:, n_far:], 0))], axis=2)
    cmp_start = np.arange(n_str) * CMP_STRIDE
    sel_start = np.arange(n_sel_lanes) * SEL_BLOCK
    ovl = ((cmp_start[:, None] < sel_start[None, :] + SEL_BLOCK) & (cmp_start[:, None] + CMP_BLOCK > sel_start[None, :])
           & (np.arange(n_str)[:, None] < n_str - 1) & (np.arange(n_sel_lanes)[None, :] < n_sel))
    wspec = [pl.BlockSpec((1, flat), lambda b, g: (0, 0)),
             pl.BlockSpec((flat, HEAD_DIM), lambda b, g: (0, 0)),
             pl.BlockSpec((HEAD_DIM, HEAD_DIM), lambda b, g: (0, 0))]
    o_cmp, p_sel = pl.pallas_call(
        functools.partial(_cmp_attn_sample_kernel, q0=q0),
        grid=(B, G),
        in_specs=[pl.BlockSpec((1, 2, 2, 1, n_str, HEAD_DIM), lambda b, g: (b, 0, 0, g, 0, 0))] + wspec + wspec + [
            pl.BlockSpec((1, T, J * HEAD_DIM), lambda b, g, o=cols['nq'] // (J * HEAD_DIM): (b, 0, o + g)),
            pl.BlockSpec((J, T, n_str), lambda b, g: (g, 0, 0)),
            pl.BlockSpec((n_str, n_sel_lanes), lambda b, g: (0, 0))],
        out_specs=(pl.BlockSpec((1, 1, J * T, HEAD_DIM), lambda b, g: (b, g, 0, 0)),
                   pl.BlockSpec((1, 1, T, n_sel_lanes), lambda b, g: (b, g, 0, 0))),
        out_shape=(jax.ShapeDtypeStruct((B, G, J * T, HEAD_DIM), F32),
                   jax.ShapeDtypeStruct((B, G, T, n_sel_lanes), F32)),
        compiler_params=pltpu.CompilerParams(dimension_semantics=("parallel", "parallel"),
                                             vmem_limit_bytes=V7X_VMEM_LIMIT_BYTES),
        name="nsa_cmp_attn_sample",
    )(z, pe_k.reshape(1, flat), w1_k.reshape(flat, HEAD_DIM).astype(BF16), w2_k.astype(BF16),
      pe_v.reshape(1, flat), w1_v.reshape(flat, HEAD_DIM).astype(BF16), w2_v.astype(BF16),
      proj3, bcmp, jnp.asarray(ovl, BF16))

    n_bgt = B * G * T
    psel_t = p_sel.reshape(n_bgt, n_sel_lanes)[:, :n_sel_rows].T
    tpos = jnp.asarray(np.tile(q0 + np.arange(T), B * G)[None, :], jnp.int32)
    seladd = pl.pallas_call(
        functools.partial(_rank_select_kernel, n_sel=n_sel, n_top=n_top),
        out_shape=jax.ShapeDtypeStruct((n_sel_rows, n_bgt), F32),
        scratch_shapes=[pltpu.VMEM((n_sel_rows, n_bgt), F32), pltpu.VMEM((n_sel_rows, n_bgt), F32)],
        name="nsa_rank_select",
    )(psel_t, tpos)
    seladd = seladd.T.reshape(B, G, 1, T, n_sel_rows)
    seladd = jnp.broadcast_to(seladd, (B, G, J, T, n_sel_rows)).reshape(B, L, n_sel_rows)
    bpp = PAGE_SIZE // SEL_BLOCK
    sel_past = seladd[:, :, :n_sel - 1].reshape(B, L, n_pages, bpp).transpose(0, 2, 3, 1)
    sel_new = seladd[:, :, n_sel - 1].reshape(B, 1, L)

    q = proj3[:, :, cols['nq']:cols['nq'] + NSA_WIDTH].reshape(B, T, G, J, HEAD_DIM) * HEAD_DIM ** -0.5
    q_t = q.transpose(0, 2, 4, 3, 1).reshape(B, G, HEAD_DIM, J * T)
    place = jnp.asarray(np.arange(G)[:, None, None] == (np.arange(L) // (J * T))[None, None, :])
    q_pad = jnp.where(place[None], jnp.tile(q_t, (1, 1, 1, G)), 0.0).astype(BF16)
    ki = np.arange(PAGE_SIZE)[:, None]
    b_last = lane_bias(PAGE_SIZE + ti - ki, np.ones((PAGE_SIZE, T), bool))
    kn = np.arange(PACK_ROWS)[:, None]
    b_new = lane_bias(ti - kn, (ti - kn >= 0) & (kn < T))
    kw = np.arange(win_buf)[:, None]
    b_win = lane_bias(win_buf + ti - kw, win_buf + ti - kw <= WINDOW)
    gates = ng.reshape(B, T, G, J, 3).transpose(0, 2, 3, 1, 4).reshape(B, L, 3)
    win_rows = 2 * win_buf * G
    new_spec = lambda name: pl.BlockSpec((1, T, KV_WIDTH), lambda b, p, pt, o=cols[name] // KV_WIDTH: (b, 0, o))
    const = lambda shape: pl.BlockSpec(shape, lambda b, p, pt: (0, 0))
    per_b = lambda shape: pl.BlockSpec((1,) + shape, lambda b, p, pt: (b,) + (0,) * len(shape))
    out = pl.pallas_call(
        functools.partial(_slc_win_sample_kernel, win_buf=win_buf),
        grid_spec=pltpu.PrefetchScalarGridSpec(
            num_scalar_prefetch=1,
            grid=(B, n_pages // PAGES_PER_STEP),
            in_specs=_page_specs(layer) + [
                per_b((G, HEAD_DIM, L)),
                pl.BlockSpec((1, PAGES_PER_STEP, bpp, L), lambda b, p, pt: (b, p, 0, 0)),
                const((PAGE_SIZE, L)), per_b((1, L)),
                new_spec('ks'), new_spec('vs'), new_spec('kw'), new_spec('vw'),
                const((PACK_ROWS, L)),
                pl.BlockSpec((win_rows, HEAD_DIM), lambda b, p, pt: (layer * B + b, 0)),
                const((win_buf, L)), per_b((L, HEAD_DIM)), per_b((L, 3))],
            out_specs=per_b((L, HEAD_DIM)),
            scratch_shapes=[pltpu.VMEM((8, L), F32), pltpu.VMEM((8, L), F32), pltpu.VMEM((HEAD_DIM, L), F32)]),
        out_shape=jax.ShapeDtypeStruct((B, L, HEAD_DIM), F32),
        compiler_params=pltpu.CompilerParams(dimension_semantics=("parallel", "arbitrary"),
                                             vmem_limit_bytes=V7X_VMEM_LIMIT_BYTES),
        name="nsa_slc_win_sample",
    )(page_table, *([_cache_rows(cache_slc)] * PAGES_PER_STEP), q_pad, sel_past, b_last, sel_new,
      proj3, proj3, proj3, proj3, b_new,
      cache_win.reshape(-1, HEAD_DIM), b_win, o_cmp.reshape(B, L, HEAD_DIM), gates)
    return out.reshape(B, G, J, T, HEAD_DIM).transpose(0, 3, 1, 2, 4).reshape(B, T, NSA_WIDTH)


def _retention_tables(T, q0):
    c = _largest_divisor(T, RET_CHUNK)
    cp = max(c, RET_CHUNK)
    lg = np.log1p(-(2.0 ** (-5.0 - np.arange(RET_HEADS, dtype=np.float32)))).astype(np.float32)
    i = np.arange(cp)
    rel = i[:, None] - i[None, :]
    inside = (i < c)[:, None] & (i < c)[None, :]
    decay = np.where((rel >= 0) & inside, np.exp(np.maximum(rel, 0)[None] * lg[:, None, None]), 0.0)
    q_dec = np.broadcast_to(np.exp((i + 1)[None, :, None] * lg[:, None, None]), (RET_HEADS, cp, HEAD_DIM))
    k_dec = np.where((i < c)[None, :, None], np.exp((c - 1 - i)[None, :, None] * lg[:, None, None]), 0.0)
    k_dec = np.broadcast_to(k_dec, (RET_HEADS, cp, HEAD_DIM))
    chunk_dec = np.broadcast_to(np.exp(c * lg)[:, None, None], (RET_HEADS, 8, HEAD_DIM))
    half = HEAD_DIM // 2
    inv = (1.0 / (10000.0 ** np.linspace(0.0, 1.0, half, dtype=np.float32))).astype(np.float32)
    ang = (q0 + np.arange(T)).astype(np.float32)[:, None] * inv[None]
    cos, sin = np.cos(ang), np.sin(ang)
    cosf = np.concatenate([cos, cos], axis=1)
    sinf = np.concatenate([-sin, sin], axis=1)
    f = lambda a: jnp.asarray(a, F32)
    return c, cp, f(decay), f(q_dec), f(k_dec), f(chunk_dec), f(cosf), f(sinf)


RET_HEADS_PER_STEP = 2


def _retention_kernel(q_ref, k_ref, v_ref, g_ref, s0_ref, cos_ref, sin_ref, dec_ref, qd_ref, kd_ref, cd_ref,
                      o_ref, s_ref, *, c, cp, n):
    half = HEAD_DIM // 2
    heads = range(RET_HEADS_PER_STEP)

    def load(ref, rows):
        return _pad_rows(ref[rows, :], cp)

    def rot(x, cos, sin):
        return x * cos + pltpu.roll(x, half, 1) * sin

    def body(i, states):
        rows = pl.ds(pl.multiple_of(i * c, c), c)
        cos, sin = load(cos_ref, rows), load(sin_ref, rows)
        q_all, k_all, v_all = load(q_ref.at[0], rows), load(k_ref.at[0], rows), load(v_ref.at[0], rows)
        new_states = []
        for hh in heads:
            cols = slice(hh * HEAD_DIM, (hh + 1) * HEAD_DIM)
            s = states[hh]
            q = rot(q_all[:, cols], cos, sin)
            k = rot(k_all[:, cols], cos, sin) * HEAD_DIM ** -0.5
            v = v_all[:, cols].astype(BF16)
            qb = q.astype(BF16)
            inner = _dot_nt(qb, k.astype(BF16)) * dec_ref[hh]
            o = (jnp.dot(inner.astype(BF16), v, preferred_element_type=F32)
                 + jnp.dot(qb, s.astype(BF16), preferred_element_type=F32) * qd_ref[hh])
            new_states.append(s * cd_ref[hh, 0:1, :] + _dot_tn((k * kd_ref[hh]).astype(BF16), v))
            o = o * lax.rsqrt(jnp.mean(o * o, axis=-1, keepdims=True) + EPS)
            g = g_ref[0, rows, cols]
            o_ref[0, rows, cols] = g * jax.nn.sigmoid(g) * o[:c]
        return tuple(new_states)

    final = lax.fori_loop(0, n, body, tuple(s0_ref[0, hh] for hh in heads))
    for hh in heads:
        s_ref[0, hh] = final[hh]


def retention(proj3, cols, s0, q0):
    B, T, _ = proj3.shape
    c, cp, decay, q_dec, k_dec, chunk_dec, cosf, sinf = _retention_tables(T, q0)
    hp = RET_HEADS_PER_STEP
    wide = hp * HEAD_DIM
    col = lambda name: pl.BlockSpec((1, T, wide), lambda b, h, o=cols[name] // wide: (b, 0, o + h))
    tab = lambda r: pl.BlockSpec((hp, r, HEAD_DIM), lambda b, h: (h, 0, 0))
    full = pl.BlockSpec((T, HEAD_DIM), lambda b, h: (0, 0))
    state = pl.BlockSpec((1, hp, HEAD_DIM, HEAD_DIM), lambda b, h: (b, h, 0, 0))
    return pl.pallas_call(
        functools.partial(_retention_kernel, c=c, cp=cp, n=T // c),
        grid=(B, RET_HEADS // hp),
        in_specs=[col('rq'), col('rk'), col('rv'), col('rg'), state, full, full,
                  tab(cp), tab(cp), tab(cp), tab(8)],
        out_specs=(pl.BlockSpec((1, T, wide), lambda b, h: (b, 0, h)), state),
        out_shape=(jax.ShapeDtypeStruct((B, T, RET_WIDTH), F32),
                   jax.ShapeDtypeStruct((B, RET_HEADS, HEAD_DIM, HEAD_DIM), F32)),
        compiler_params=pltpu.CompilerParams(dimension_semantics=("parallel", "parallel"),
                                             vmem_limit_bytes=V7X_VMEM_LIMIT_BYTES),
        name="retention",
    )(proj3, proj3, proj3, proj3, s0.astype(F32), cosf, sinf, decay, q_dec, k_dec, chunk_dec)


S5_BLK_GROUPS = 8
S5_BLK_STATE = S5_BLK_GROUPS * S5_STATE
S5_BLK_CH = S5_BLK_GROUPS * S5_GROUP
S5_SCAN_ROWS = 8


def _s5_params(lam_re, lam_im, log_step, b_re, b_im, c_re, c_im):
    nb = S5_GROUPS // S5_BLK_GROUPS
    lam = lax.complex(lam_re.astype(F32), lam_im.astype(F32))
    step = jnp.exp(log_step.astype(F32))[:, None]
    a_bar = jnp.exp(lam * step)
    b_bar = ((a_bar - 1.0) / lam)[..., None] * lax.complex(b_re.astype(F32), b_im.astype(F32))
    r = np.arange(S5_SCAN_ROWS)

    def powers(k, keep):
        p = jnp.exp(lam[None] * step[None] * jnp.asarray(k, F32)[:, None, None])
        return jnp.where(jnp.asarray(keep)[:, None, None], p, 0.0)

    tabs = [powers(np.full(S5_SCAN_ROWS, k), r >= k) for k in (1, 2, 4)]
    tabs.append(powers(r + 1, r >= 0))
    tab = jnp.stack(tabs)
    tab = tab.reshape(4, S5_SCAN_ROWS, nb, S5_BLK_STATE).transpose(2, 0, 1, 3)
    atab = jnp.concatenate([tab.real, tab.imag], axis=1)

    eye = jnp.eye(S5_BLK_GROUPS, dtype=F32)
    bb = b_bar.reshape(nb, S5_BLK_GROUPS, S5_STATE, S5_GROUP)

    def in_mat(x):
        return jnp.einsum('ngpc,gh->ngchp', x, eye).reshape(nb, S5_BLK_CH, S5_BLK_STATE)

    bmat = jnp.concatenate([in_mat(bb.real), in_mat(bb.imag)], axis=-1).astype(BF16)
    cr = c_re.astype(F32).reshape(nb, S5_BLK_GROUPS, S5_GROUP, S5_STATE)
    ci = c_im.astype(F32).reshape(nb, S5_BLK_GROUPS, S5_GROUP, S5_STATE)

    def out_mat(x):
        return jnp.einsum('ngcp,gh->ngphc', x, eye).reshape(nb, S5_BLK_STATE, S5_BLK_CH)

    cmat = jnp.concatenate([out_mat(cr), -out_mat(ci)], axis=1).astype(BF16)
    return atab, bmat, cmat


def _s5_scan_tile(xr, xi, cr, ci, atab_ref):
    for idx, k in enumerate((1, 2, 4)):
        pr, pi = atab_ref[0, idx], atab_ref[0, 4 + idx]
        sr, si = pltpu.roll(xr, k, 0), pltpu.roll(xi, k, 0)
        xr, xi = xr + pr * sr - pi * si, xi + pr * si + pi * sr
    pr, pi = atab_ref[0, 3], atab_ref[0, 7]
    xr, xi = xr + pr * cr - pi * ci, xi + pr * ci + pi * cr
    last = S5_SCAN_ROWS - 1
    cr = jnp.broadcast_to(xr[last:last + 1, :], xr.shape)
    ci = jnp.broadcast_to(xi[last:last + 1, :], xi.shape)
    return xr, xi, cr, ci


def _s5_kernel(u_ref, x0_ref, atab_ref, b_ref, c_ref, d_ref, y_ref, st_ref, xs_ref, *, T):
    u = u_ref[0]
    t_pad = _round_up(T, PACK_ROWS)
    xs_ref[...] = jnp.dot(_pad_rows(u, t_pad).astype(BF16), b_ref[0], preferred_element_type=F32)[:T]
    n = S5_BLK_STATE
    R = S5_SCAN_ROWS

    def body(i, carry):
        cr, ci = carry
        rows = pl.ds(pl.multiple_of(i * R, R), R)
        xr, xi, cr, ci = _s5_scan_tile(xs_ref[rows, :n], xs_ref[rows, n:], cr, ci, atab_ref)
        xs_ref[rows, :n] = xr
        xs_ref[rows, n:] = xi
        return cr, ci

    x0 = x0_ref[0, 0]
    cr0 = jnp.broadcast_to(x0[0:1, :], (R, n))
    ci0 = jnp.broadcast_to(x0[1:2, :], (R, n))
    cr, ci = lax.fori_loop(0, T // R, body, (cr0, ci0))
    st_ref[0, 0] = jnp.concatenate([cr[0:1], ci[0:1]], axis=0)
    y = jnp.dot(_pad_rows(xs_ref[...], t_pad).astype(BF16), c_ref[0], preferred_element_type=F32)[:T]
    y_ref[0] = _gelu_tanh(y + d_ref[...] * u)


def s5_scan(proj3, cols, x0, lam_re, lam_im, log_step, b_re, b_im, c_re, c_im, d):
    B, T, _ = proj3.shape
    assert T % S5_SCAN_ROWS == 0
    nb = S5_GROUPS // S5_BLK_GROUPS
    atab, bmat, cmat = _s5_params(lam_re, lam_im, log_step, b_re, b_im, c_re, c_im)
    x0b = x0.astype(F32).reshape(B, nb, S5_BLK_STATE, 2).transpose(0, 1, 3, 2)
    blk3 = lambda shape: pl.BlockSpec((1,) + shape, lambda b, j: (j, 0, 0))
    y, st = pl.pallas_call(
        functools.partial(_s5_kernel, T=T),
        grid=(B, nb),
        in_specs=[pl.BlockSpec((1, T, S5_BLK_CH), lambda b, j, o=cols['su'] // S5_BLK_CH: (b, 0, o + j)),
                  pl.BlockSpec((1, 1, 2, S5_BLK_STATE), lambda b, j: (b, j, 0, 0)),
                  pl.BlockSpec((1, 8, S5_SCAN_ROWS, S5_BLK_STATE), lambda b, j: (j, 0, 0, 0)),
                  blk3((S5_BLK_CH, 2 * S5_BLK_STATE)), blk3((2 * S5_BLK_STATE, S5_BLK_CH)),
                  pl.BlockSpec((1, S5_BLK_CH), lambda b, j: (0, j))],
        out_specs=(pl.BlockSpec((1, T, S5_BLK_CH), lambda b, j: (b, 0, j)),
                   pl.BlockSpec((1, 1, 2, S5_BLK_STATE), lambda b, j: (b, j, 0, 0))),
        out_shape=(jax.ShapeDtypeStruct((B, T, S5_WIDTH), F32),
                   jax.ShapeDtypeStruct((B, nb, 2, S5_BLK_STATE), F32)),
        scratch_shapes=[pltpu.VMEM((T, 2 * S5_BLK_STATE), F32)],
        compiler_params=pltpu.CompilerParams(dimension_semantics=("parallel", "parallel"),
                                             vmem_limit_bytes=V7X_VMEM_LIMIT_BYTES),
        name="s5_scan",
    )(proj3, x0b, atab, bmat, cmat, d.astype(F32).reshape(1, S5_WIDTH))
    st = st.transpose(0, 1, 3, 2).reshape(B, S5_GROUPS, S5_STATE, 2)
    return y, st


def _branch_norm_kernel(ro_ref, so_ref, no_ref, bn_ref, o_ref):
    off = 0
    for ref in (ro_ref, so_ref, no_ref):
        x = ref[...]
        w = x.shape[-1]
        y = x * lax.rsqrt(jnp.mean(x * x, axis=-1, keepdims=True) + EPS) * bn_ref[:, off:off + w]
        o_ref[:, off:off + w] = y.astype(o_ref.dtype)
        off += w


def branch_norm(ro, so, no, bn):
    m = ro.shape[0]
    tm = min(m, 256)
    assert m % tm == 0
    spec = lambda w: pl.BlockSpec((tm, w), lambda i: (i, 0))
    return pl.pallas_call(
        _branch_norm_kernel,
        grid=(m // tm,),
        in_specs=[spec(RET_WIDTH), spec(S5_WIDTH), spec(NSA_WIDTH), pl.BlockSpec((1, D_MODEL), lambda i: (0, 0))],
        out_specs=spec(D_MODEL),
        out_shape=jax.ShapeDtypeStruct((m, D_MODEL), BF16),
        compiler_params=pltpu.CompilerParams(dimension_semantics=("parallel",),
                                             vmem_limit_bytes=V7X_VMEM_LIMIT_BYTES),
        name="branch_norm",
    )(ro, so, no, bn.astype(F32).reshape(1, D_MODEL))


def _kv_rows_kernel(x_ref, o_ref):
    x = x_ref[0]
    for g in range(NSA_KV_HEADS):
        o_ref[pl.ds(g, x.shape[0], stride=NSA_KV_HEADS), :] = x[:, g * HEAD_DIM:(g + 1) * HEAD_DIM]


def kv_rows(proj3, col_k, t_start, t_len):
    B = proj3.shape[0]
    tT = min(t_len, 256)
    assert t_len % tT == 0 and t_start % tT == 0 and col_k % KV_WIDTH == 0
    n = t_len // tT
    rows = pl.pallas_call(
        _kv_rows_kernel,
        grid=(B, 2, n),
        in_specs=[pl.BlockSpec((1, tT, KV_WIDTH), lambda b, kv, i: (b, t_start // tT + i, col_k // KV_WIDTH + kv))],
        out_specs=pl.BlockSpec((tT * NSA_KV_HEADS, HEAD_DIM), lambda b, kv, i: ((b * 2 + kv) * n + i, 0)),
        out_shape=jax.ShapeDtypeStruct((B * 2 * t_len * NSA_KV_HEADS, HEAD_DIM), F32),
        compiler_params=pltpu.CompilerParams(dimension_semantics=("parallel", "parallel", "parallel"),
                                             vmem_limit_bytes=V7X_VMEM_LIMIT_BYTES),
        name="kv_rows",
    )(proj3)
    return rows.reshape(B, 2, t_len, NSA_KV_HEADS, HEAD_DIM)


_COL_NAMES = ('rq', 'rk', 'rv', 'rg', 'su', 'nq', 'kc', 'vc', 'ks', 'vs', 'kw', 'vw', 'ng')
COLS = {name: int(off) for name, off in zip(_COL_NAMES, np.concatenate([[0], np.cumsum(IN_SPLITS)]))}


def _block(x, layer, w, rel_bias, past, win_buf):
    B, T, _ = x.shape
    G = NSA_KV_HEADS
    M = B * T
    x2 = x.reshape(M, D_MODEL)
    h = rmsnorm(x2, w['norm_mix'][layer], BF16)
    proj3 = matmul(h, w['w_in'], layer, n=COLS['ng']).reshape(B, T, -1)
    ng = matmul(h, w['w_gate'], layer)[:, :3 * NSA_HEADS].reshape(B, T, 3 * NSA_HEADS)
    s5_w =[w[k][layer] for k in ('s5_lambda_re', 's5_lambda_im', 's5_log_step', 's5_b_re', 's5_b_im',
                                  's5_c_re', 's5_c_im', 's5_d')]
    cmp_w = [w[k][layer] for k in ('cmp_pe_k', 'cmp_w1_k', 'cmp_w2_k', 'cmp_pe_v', 'cmp_w1_v', 'cmp_w2_v')]
    if past is None:
        q0 = 0
        ret_s0 = jnp.zeros((B, RET_HEADS, HEAD_DIM, HEAD_DIM), F32)
        s5_s0 = jnp.zeros((B, S5_GROUPS, S5_STATE, 2), F32)
        win_prev = jnp.zeros((B, 2, WINDOW, G, HEAD_DIM), x.dtype)
    else:
        cache_cmp, cache_slc, cache_win, state_ret, state_s5, page_table = past
        q0 = page_table.shape[1] * PAGE_SIZE
        ret_s0, s5_s0, win_prev = state_ret[layer], state_s5[layer], cache_win[layer]

    ro, ret_s = retention(proj3, COLS, ret_s0, q0)
    sy, s5_s = s5_scan(proj3, COLS, s5_s0, *s5_w)
    if past is None:
        kcmp, vcmp = compress_prompt(proj3, COLS['kc'], COLS['vc'], *cmp_w)
        no = nsa_prompt(proj3, COLS, ng, kcmp, vcmp, rel_bias)
    else:
        no = nsa_sample(proj3, COLS, ng, cache_cmp, cache_slc, cache_win, page_table, layer, rel_bias, *cmp_w)

    sy2 = sy.reshape(M, S5_WIDTH)
    so = matmul(sy2, w['s5_w_glu'], layer, res=sy2, act="glu")
    mix = branch_norm(ro.reshape(M, RET_WIDTH), so, no.reshape(M, NSA_WIDTH), w['branch_norm'][layer])
    x2 = matmul(mix, w['w_out'], layer, res=x2)
    h = rmsnorm(x2, w['norm_ffn'][layer], BF16)
    up = matmul(h, w['w_up'], layer, act="relu2", out_dtype=BF16)
    x2 = matmul(up, w['w_down'], layer, res=x2)

    cmp_rows = kv_rows(proj3, COLS['kc'], 0, T)
    slc_rows = kv_rows(proj3, COLS['ks'], 0, T)
    if T >= win_buf:
        win_new = kv_rows(proj3, COLS['kw'], T - win_buf, win_buf)
    else:
        win_new = jnp.concatenate([win_prev, kv_rows(proj3, COLS['kw'], 0, T)], axis=2)[:, :, -win_buf:]
    return x2.reshape(B, T, D_MODEL), cmp_rows, slc_rows, win_new, ret_s, s5_s


def kernel(x_prompt, x_sample, cache_cmp, cache_slc, cache_win, state_ret, state_s5, page_table,
           rel_bias, norm_mix, w_in, s5_lambda_re, s5_lambda_im, s5_log_step, s5_b_re, s5_b_im,
           s5_c_re, s5_c_im, s5_d, s5_w_glu, cmp_pe_k, cmp_w1_k, cmp_w2_k, cmp_pe_v, cmp_w1_v,
           cmp_w2_v, branch_norm, w_out, norm_ffn, w_up, w_down, norm_final):
    win_buf = cache_win.shape[3]
    n_main = COLS['ng']
    assert n_main % LANE == 0
    w_in_b = w_in.astype(BF16)
    w = dict(
        norm_mix=norm_mix, norm_ffn=norm_ffn, branch_norm=branch_norm,
        w_in=w_in_b,
        w_gate=jnp.pad(w_in_b[:, :, n_main:], ((0, 0), (0, 0), (0, LANE - (IN_COLS - n_main)))),
        w_out=w_out.astype(BF16), w_up=w_up.astype(BF16), w_down=w_down.astype(BF16),
        s5_w_glu=s5_w_glu.astype(BF16),
        s5_lambda_re=s5_lambda_re, s5_lambda_im=s5_lambda_im, s5_log_step=s5_log_step, s5_b_re=s5_b_re,
        s5_b_im=s5_b_im, s5_c_re=s5_c_re, s5_c_im=s5_c_im, s5_d=s5_d,
        cmp_pe_k=cmp_pe_k, cmp_w1_k=cmp_w1_k, cmp_w2_k=cmp_w2_k,
        cmp_pe_v=cmp_pe_v, cmp_w1_v=cmp_w1_v, cmp_w2_v=cmp_w2_v)
    past = (cache_cmp, cache_slc, cache_win, state_ret, state_s5, page_table)
    xp, xs = x_prompt, x_sample
    written_p, written_s = [], []
    for layer in range(DEPTH):
        xp, *entries = _block(xp, layer, w, rel_bias, None, win_buf)
        written_p.append(entries)
        xs, *entries = _block(xs, layer, w, rel_bias, past, win_buf)
        written_s.append(entries)
    y_prompt = rmsnorm(xp.reshape(-1, D_MODEL), norm_final, F32).reshape(xp.shape)
    y_sample = rmsnorm(xs.reshape(-1, D_MODEL), norm_final, F32).reshape(xs.shape)
    stacked = lambda written, i, axis: jnp.stack([entries[i] for entries in written], axis=axis)
    return (y_prompt, y_sample,
            stacked(written_p, 0, 1), stacked(written_s, 0, 1),
            stacked(written_p, 1, 1), stacked(written_s, 1, 1),
            stacked(written_p, 2, 0), stacked(written_s, 2, 0),
            stacked(written_p, 3, 0), stacked(written_s, 3, 0),
            stacked(written_p, 4, 0), stacked(written_s, 4, 0))
```

```python
import functools
import math

import jax
import jax.numpy as jnp
import numpy as np
from jax import lax
from jax.experimental import pallas as pl
from jax.experimental.pallas import tpu as pltpu

F32 = jnp.float32
BF16 = jnp.bfloat16

D_MODEL = 4096
DEPTH = 2
PAGE_SIZE = 128
HEAD_DIM = 128
RET_WIDTH = 1024
RET_HEADS = 8
RET_CHUNK = 128
S5_WIDTH = 1024
S5_GROUP = 16
S5_GROUPS = 64
S5_STATE = 64
NSA_WIDTH = 2048
NSA_HEADS = 16
NSA_KV_HEADS = 4
NSA_HPG = 4
KV_WIDTH = 512
CMP_BLOCK = 32
CMP_STRIDE = 16
SEL_BLOCK = 64
SEL_TOP = 16
WINDOW = 512
FORCE_SCORE = 1e4
NEG = -1e30
N_BUCKETS = 32
MAX_DISTANCE = 128
EPS = 1e-6
IN_SPLITS = (RET_WIDTH, RET_WIDTH, RET_WIDTH, RET_WIDTH, S5_WIDTH, NSA_WIDTH,
             KV_WIDTH, KV_WIDTH, KV_WIDTH, KV_WIDTH, KV_WIDTH, KV_WIDTH, 3 * NSA_HEADS)
IN_COLS = sum(IN_SPLITS)

V7X_VMEM_LIMIT_BYTES = 48 * 1024 * 1024
LANE = 128
PACK_ROWS = 16


def _round_up(n, m):
    return -(-n // m) * m


def _largest_divisor(n, cap):
    return max(d for d in range(1, min(n, cap) + 1) if n % d == 0)


def _pad_rows(x, rows):
    extra = rows - x.shape[0]
    return jnp.concatenate([x, jnp.zeros((extra, x.shape[1]), x.dtype)], axis=0) if extra else x


def _gelu_tanh(x):
    return 0.5 * x * (1.0 + jnp.tanh(math.sqrt(2.0 / math.pi) * (x + 0.044715 * (x * x * x))))


def _dot_nt(a, b):
    return lax.dot_general(a, b, (((1,), (1,)), ((), ())), preferred_element_type=F32)


def _dot_tn(a, b):
    return lax.dot_general(a, b, (((0,), (0,)), ((), ())), preferred_element_type=F32)


def _rmsnorm_kernel(x_ref, g_ref, o_ref):
    x = x_ref[...].astype(F32)
    ms = jnp.mean(x * x, axis=-1, keepdims=True)
    o_ref[...] = (x * lax.rsqrt(ms + EPS) * g_ref[...].astype(F32)).astype(o_ref.dtype)


def rmsnorm(x2d, gain, out_dtype):
    m, d = x2d.shape
    tm = min(m, 256)
    assert m % tm == 0
    return pl.pallas_call(
        _rmsnorm_kernel,
        grid=(m // tm,),
        in_specs=[pl.BlockSpec((tm, d), lambda i: (i, 0)),
                  pl.BlockSpec((1, d), lambda i: (0, 0))],
        out_specs=pl.BlockSpec((tm, d), lambda i: (i, 0)),
        out_shape=jax.ShapeDtypeStruct((m, d), out_dtype),
        compiler_params=pltpu.CompilerParams(dimension_semantics=("parallel",),
                                             vmem_limit_bytes=V7X_VMEM_LIMIT_BYTES),
        name="rmsnorm",
    )(x2d, gain.reshape(1, d))


def _mm_kernel(*refs, nk, act, has_res, emit_w):
    refs = list(refs)
    a_ref, w_ref = refs[:2]
    r_ref = refs[2] if has_res else None
    o_ref = refs[2 + has_res]
    wb_ref = refs[3 + has_res] if emit_w else None
    acc_ref = refs[-1]
    k = pl.program_id(2)

    @pl.when(k == 0)
    def _():
        acc_ref[...] = jnp.zeros_like(acc_ref)

    w = w_ref[...].astype(BF16)
    if emit_w:
        wb_ref[...] = w
    acc_ref[...] += jnp.dot(a_ref[...].astype(BF16), w, preferred_element_type=F32)

    @pl.when(k == nk - 1)
    def _():
        acc = acc_ref[...]
        if act == "relu2":
            acc = jnp.square(jnp.maximum(acc, 0.0))
        if act == "glu":
            acc = r_ref[...].astype(F32) * jax.nn.sigmoid(acc)
        elif has_res:
            acc = acc + r_ref[...].astype(F32)
        o_ref[...] = acc.astype(o_ref.dtype)


def matmul(a, w, layer=None, *, n=None, res=None, act=None, out_dtype=F32, emit_bf16=False):
    m, kdim = a.shape
    n = w.shape[-1] if n is None else n
    tm = min(m, 1024)
    tn = 512 if n % 512 == 0 else (256 if n % 256 == 0 else 128)
    tk = min(kdim, 4096)
    assert m % tm == 0 and n % tn == 0 and kdim % tk == 0
    assert not emit_bf16 or m == tm
    nk = kdim // tk
    if layer is None:
        w_spec = pl.BlockSpec((tk, tn), lambda i, j, k: (k, j))
    else:
        w_spec = pl.BlockSpec((None, tk, tn), lambda i, j, k: (layer, k, j))
    in_specs = [pl.BlockSpec((tm, tk), lambda i, j, k: (i, k)), w_spec]
    args = [a, w]
    if res is not None:
        in_specs.append(pl.BlockSpec((tm, tn), lambda i, j, k: (i, j)))
        args.append(res)
    out_specs = pl.BlockSpec((tm, tn), lambda i, j, k: (i, j))
    out_shape = jax.ShapeDtypeStruct((m, n), out_dtype)
    if emit_bf16:
        out_specs = (out_specs, pl.BlockSpec((tk, tn), lambda i, j, k: (k, j)))
        out_shape = (out_shape, jax.ShapeDtypeStruct((kdim, n), BF16))
    return pl.pallas_call(
        functools.partial(_mm_kernel, nk=nk, act=act, has_res=res is not None, emit_w=emit_bf16),
        grid=(m // tm, n // tn, nk),
        in_specs=in_specs,
        out_specs=out_specs,
        out_shape=out_shape,
        scratch_shapes=[pltpu.VMEM((tm, tn), F32)],
        compiler_params=pltpu.CompilerParams(
            dimension_semantics=("parallel", "parallel", "arbitrary"),
            vmem_limit_bytes=V7X_VMEM_LIMIT_BYTES),
        name="matmul",
    )(*args)


def _compress_rows(x_ref, n_full, pe_ref, w1_ref, w2_ref):
    pieces = [x_ref[0, pl.ds(s, n_full, stride=CMP_STRIDE), :] for s in range(CMP_STRIDE)]
    x = jnp.concatenate(pieces, axis=1).astype(BF16)
    half = CMP_STRIDE * HEAD_DIM
    z0 = jnp.dot(x, w1_ref[:half, :], preferred_element_type=F32)
    z1 = jnp.dot(x, w1_ref[half:, :], preferred_element_type=F32)
    z1 = pltpu.roll(z1, n_full - 1, 0)
    pe = jnp.broadcast_to(pe_ref[...], (8, CMP_BLOCK * HEAD_DIM)).astype(BF16)
    h0 = jnp.dot(pe, w1_ref[...], preferred_element_type=F32)[0:1, :]
    h = _gelu_tanh(z0 + z1 + h0)
    return jnp.dot(h.astype(BF16), w2_ref[...], preferred_element_type=F32)


def _compress_prompt_kernel(xk_ref, xv_ref, pek_ref, w1k_ref, w2k_ref, pev_ref, w1v_ref, w2v_ref,
                            kc_ref, vc_ref, *, n_full):
    for x_ref, pe_ref, w1_ref, w2_ref, o_ref in ((xk_ref, pek_ref, w1k_ref, w2k_ref, kc_ref),
                                                  (xv_ref, pev_ref, w1v_ref, w2v_ref, vc_ref)):
        out = _compress_rows(x_ref, n_full, pe_ref, w1_ref, w2_ref).astype(o_ref.dtype)
        n_pad = o_ref.shape[2]
        o_ref[0, 0, :n_full, :] = out
        if n_pad > n_full:
            o_ref[0, 0, n_full:, :] = jnp.zeros((n_pad - n_full, HEAD_DIM), o_ref.dtype)


def compress_prompt(proj3, col_k, col_v, pe_k, w1_k, w2_k, pe_v, w1_v, w2_v):
    B, T, _ = proj3.shape
    n_full = T // CMP_STRIDE
    n_pad = _round_up(n_full, LANE)
    flat = CMP_BLOCK * HEAD_DIM
    wspec = [pl.BlockSpec((1, flat), lambda b, g: (0, 0)),
             pl.BlockSpec((flat, HEAD_DIM), lambda b, g: (0, 0)),
             pl.BlockSpec((HEAD_DIM, HEAD_DIM), lambda b, g: (0, 0))]
    out_sds = jax.ShapeDtypeStruct((B, NSA_KV_HEADS, n_pad, HEAD_DIM), BF16)
    ospec = pl.BlockSpec((1, 1, n_pad, HEAD_DIM), lambda b, g: (b, g, 0, 0))
    return pl.pallas_call(
        functools.partial(_compress_prompt_kernel, n_full=n_full),
        grid=(B, NSA_KV_HEADS),
        in_specs=[pl.BlockSpec((1, T, HEAD_DIM), lambda b, g: (b, 0, col_k // HEAD_DIM + g)),
                  pl.BlockSpec((1, T, HEAD_DIM), lambda b, g: (b, 0, col_v // HEAD_DIM + g))] + wspec + wspec,
        out_specs=(ospec, ospec),
        out_shape=(out_sds, out_sds),
        compiler_params=pltpu.CompilerParams(dimension_semantics=("parallel", "parallel"),
                                             vmem_limit_bytes=V7X_VMEM_LIMIT_BYTES),
        name="nsa_compress_prompt",
    )(proj3, proj3,
      pe_k.reshape(1, flat), w1_k.reshape(flat, HEAD_DIM).astype(BF16), w2_k.astype(BF16),
      pe_v.reshape(1, flat), w1_v.reshape(flat, HEAD_DIM).astype(BF16), w2_v.astype(BF16))


def _bucket_table():
    d = np.arange(MAX_DISTANCE)
    max_exact = N_BUCKETS // 2
    large = max_exact + (np.log(np.maximum(d, 1).astype(np.float32) / np.float32(max_exact))
                         / np.float32(math.log(MAX_DISTANCE / max_exact))
                         * np.float32(N_BUCKETS - max_exact)).astype(np.int32)
    return np.where(d < max_exact, d, np.minimum(large, N_BUCKETS - 1)).astype(np.int32)


def _bias_of_distance(rel_bias, dist):
    bt = _bucket_table()
    buckets = bt[np.clip(dist, 0, MAX_DISTANCE - 1)]
    b = jnp.moveaxis(rel_bias.astype(F32)[buckets], -1, 0)
    return jnp.where(jnp.asarray(dist >= 0)[None], b, NEG)


def _bias_by_distance(rel_bias):
    return rel_bias.astype(F32)[_bucket_table()].T


def _toeplitz(v, n):
    h = v.shape[0]
    w = jnp.pad(v, ((0, 0), (0, 1)))
    m = jnp.tile(w, (1, n))[:, :n * (2 * n - 1)].reshape(h, n, 2 * n - 1)
    return m[:, :, n - 1:]


ATT_TILE = 128
N_BIAS_TILES = 4
SLC_CLASS_TILES = 2
MXU_TILES = 2


def _rows_softmax_pv(s_tiles, v_rows):
    m = s_tiles[0]
    for s in s_tiles[1:]:
        m = jnp.maximum(m, s)
    m = jnp.max(m, axis=-1, keepdims=True)
    l = None
    o = None
    for i in range(0, len(s_tiles), MXU_TILES):
        ps = [jnp.exp(s - m) for s in s_tiles[i:i + MXU_TILES]]
        for p in ps:
            l = p if l is None else l + p
        p_cat = ps[0] if len(ps) == 1 else jnp.concatenate(ps, axis=1)
        pv = jnp.dot(p_cat.astype(BF16), v_rows(i, len(ps)), preferred_element_type=F32)
        o = pv if o is None else o + pv
    return o / jnp.sum(l, axis=-1, keepdims=True)


def _score_tiles(q, k_rows, n_tiles):
    tiles = []
    for i in range(0, n_tiles, MXU_TILES):
        n = min(MXU_TILES, n_tiles - i)
        s = _dot_nt(q, k_rows(i, n))
        tiles += [s[:, j * LANE:(j + 1) * LANE] for j in range(n)]
    return tiles


def _nsa_prompt_kernel(q_ref, kc_ref, vc_ref, ks_ref, vs_ref, kw_ref, vw_ref, gate_ref,
                       bcmp_ref, btile_ref, ovl_ref, pick_ref, cvec_ref, kaug_s_ref, kaug_w_ref, o_ref,
                       ksb, vsb, kwb, vwb, s_ref, *, n_sel, n_top, T):
    tq = ATT_TILE
    J = NSA_HPG
    qi = pl.program_id(2)
    n_sel_pad = pick_ref.shape[0]
    nq = T // tq

    @pl.when(qi == 0)
    def _():
        ksb[:, :HEAD_DIM] = ks_ref[0].astype(BF16)
        ksb[:, HEAD_DIM:] = kaug_s_ref[...]
        vsb[...] = vs_ref[0].astype(BF16)
        kwb[:WINDOW, :HEAD_DIM] = jnp.zeros((WINDOW, HEAD_DIM), BF16)
        kwb[WINDOW:, :HEAD_DIM] = kw_ref[0].astype(BF16)
        kwb[:, HEAD_DIM:] = kaug_w_ref[...]
        vwb[:WINDOW, :] = jnp.zeros((WINDOW, HEAD_DIM), BF16)
        vwb[WINDOW:, :] = vw_ref[0].astype(BF16)

    def stack(f):
        return jnp.concatenate([f(j) for j in range(J)], axis=0)

    def put(vals, first):
        for j in range(J):
            cols = slice(j * HEAD_DIM, (j + 1) * HEAD_DIM)
            v = vals[j * tq:(j + 1) * tq]
            o_ref[0, :, cols] = v if first else o_ref[0, :, cols] + v

    gates = jax.nn.sigmoid(gate_ref[0, 0])
    gate = lambda c: stack(lambda j: gates[:, 3 * j + c:3 * j + c + 1])
    q_all = stack(lambda j: q_ref[0, :, j * HEAD_DIM:(j + 1) * HEAD_DIM] * HEAD_DIM ** -0.5).astype(BF16)

    row = lax.broadcasted_iota(jnp.int32, (tq, LANE), 0)
    lane = lax.broadcasted_iota(jnp.int32, (tq, LANE), 1)
    valid1 = qi * tq + row >= lane * CMP_STRIDE + (CMP_BLOCK - 1)
    valid = stack(lambda j: valid1)
    s = jnp.where(valid, _dot_nt(q_all, kc_ref[0, 0]) + stack(lambda j: bcmp_ref[j]), NEG)
    e = jnp.where(valid, jnp.exp(s - jnp.max(s, axis=-1, keepdims=True)), 0.0)
    den = jnp.sum(e, axis=-1, keepdims=True)
    p = e / jnp.where(den > 0.0, den, 1.0)
    put(gate(0) * jnp.dot(p.astype(BF16), vc_ref[0, 0], preferred_element_type=F32), True)
    psum = p[0:tq]
    for j in range(1, J):
        psum = psum + p[j * tq:(j + 1) * tq]

    bt = lambda i: stack(lambda j: btile_ref[j, i])

    pad_flag = jnp.broadcast_to(-cvec_ref[...], (tq, LANE)).astype(BF16)
    q_win = jnp.concatenate([q_all, stack(lambda j: pad_flag)], axis=1)
    n_w = WINDOW // tq + 1
    win_rows = lambda i, n: pl.ds(pl.multiple_of((qi + i) * tq, tq), n * tq)
    s_tiles = _score_tiles(q_win, lambda i, n: kwb[win_rows(i, n), :], n_w)
    s_tiles[0] = s_tiles[0] + bt(3)
    s_tiles[n_w - 2] = s_tiles[n_w - 2] + bt(1)
    s_tiles[n_w - 1] = s_tiles[n_w - 1] + bt(0)
    put(gate(2) * _rows_softmax_pv(s_tiles, lambda i, n: vwb[win_rows(i, n), :]), False)

    hi = psum.astype(BF16)
    r1 = psum - hi.astype(F32)
    mid = r1.astype(BF16)
    lo = (r1 - mid.astype(F32)).astype(BF16)
    ovl = ovl_ref[...]
    p_sel = _dot_nt(ovl, hi) + _dot_nt(ovl, mid) + _dot_nt(ovl, lo)
    blk = lax.broadcasted_iota(jnp.int32, (n_sel_pad, tq), 0)
    cur = (qi * tq + lax.broadcasted_iota(jnp.int32, (n_sel_pad, tq), 1)) // SEL_BLOCK
    forced = (blk == 0) | (blk == cur) | (blk == cur - 1)
    score = jnp.where(blk <= cur, p_sel + jnp.where(forced, FORCE_SCORE, 0.0), NEG)
    rank = jnp.zeros((n_sel_pad, tq), F32)
    for jb in range(n_sel):
        other = score[jb:jb + 1, :]
        beats = (other > score) | ((other == score) & (blk > jb))
        rank = rank + jnp.where(beats, 1.0, 0.0)
    sel = jnp.where((rank < n_top) & (blk <= cur) & (blk < n_sel), 1.0, 0.0).astype(BF16)
    aug = (_dot_tn(sel, pick_ref[...]) - cvec_ref[...]).astype(BF16)
    q_aug = jnp.concatenate([q_all, stack(lambda j: aug)], axis=1)

    for cls in range(-(-nq // SLC_CLASS_TILES)):
        n_t = min((cls + 1) * SLC_CLASS_TILES, nq)

        @pl.when(qi // SLC_CLASS_TILES == cls)
        def _(n_t=n_t):
            for kt, s in enumerate(_score_tiles(q_aug, lambda i, n: ksb[i * tq:(i + n) * tq, :], n_t)):
                s_ref[kt] = s
            s_ref[qi] = s_ref[qi] + bt(0)

            @pl.when(qi > 0)
            def _():
                s_ref[qi - 1] = s_ref[qi - 1] + bt(1)

            o = _rows_softmax_pv([s_ref[kt] for kt in range(n_t)], lambda i, n: vsb[i * tq:(i + n) * tq, :])
            put(gate(1) * o, False)


def nsa_prompt(proj3, cols, ng, kcmp, vcmp, rel_bias):
    B, T, _ = proj3.shape
    tq = ATT_TILE
    assert T % tq == 0 and T % SEL_BLOCK == 0 and WINDOW % tq == 0
    G, J = NSA_KV_HEADS, NSA_HPG
    nq = T // tq
    n_sel = T // SEL_BLOCK
    n_top = min(SEL_TOP, n_sel)
    n_sel_pad = _round_up(n_sel, 16)
    assert n_sel_pad < LANE
    n_pad = kcmp.shape[2]
    assert n_pad == LANE, "one lane tile of compressed blocks"

    bd = _bias_by_distance(rel_bias)
    n_a = T // CMP_STRIDE
    assert n_a == n_pad
    d_cmp = (CMP_STRIDE * ((n_a - 1) - np.arange(2 * n_a - 1))[None, :]
             + np.arange(CMP_STRIDE)[:, None] - (CMP_BLOCK - 1))
    gen_cmp = jnp.where(jnp.asarray(d_cmp >= 0)[None], bd[:, np.clip(d_cmp, 0, MAX_DISTANCE - 1)], 0.0)
    bcmp = _toeplitz(gen_cmp.reshape(NSA_HEADS * CMP_STRIDE, 2 * n_a - 1), n_a)
    bcmp = bcmp.reshape(NSA_HEADS, CMP_STRIDE, n_a, n_a).transpose(0, 2, 1, 3).reshape(NSA_HEADS, T, n_pad)
    d_diag = (tq - 1) - np.arange(2 * tq - 1)
    rel = bd - bd[:, -1:]
    gen = lambda d: jnp.where(jnp.asarray(d >= 0)[None], rel[:, np.clip(d, 0, MAX_DISTANCE - 1)], NEG)
    edge = jnp.broadcast_to(jnp.where(jnp.asarray(d_diag <= 0), 0.0, NEG)[None], (NSA_HEADS, 2 * tq - 1))
    btile = jnp.stack([_toeplitz(gen(d_diag), tq), _toeplitz(gen(d_diag + tq), tq),
                       jnp.zeros((NSA_HEADS, tq, tq), F32), _toeplitz(edge, tq)], axis=1)
    cmp_start = np.arange(n_pad) * CMP_STRIDE
    sel_start = np.arange(n_sel_pad) * SEL_BLOCK
    ovl = ((cmp_start[None, :] < sel_start[:, None] + SEL_BLOCK)
           & (cmp_start[None, :] + CMP_BLOCK > sel_start[:, None])
           & (np.arange(n_pad)[None, :] < T // CMP_STRIDE - 1))
    ovl = jnp.asarray(ovl, BF16)
    pick = jnp.asarray(np.arange(LANE)[None, :] == np.arange(n_sel_pad)[:, None], BF16)
    lane_i = np.arange(LANE)
    cvec = jnp.asarray(((lane_i < n_sel) | (lane_i == n_sel_pad))[None, :], F32)
    big = -NEG
    kaug_s = jnp.asarray(np.where(np.arange(T)[:, None] // SEL_BLOCK == lane_i[None, :], big, 0.0), BF16)
    kaug_w = jnp.asarray(np.where((np.arange(T + WINDOW)[:, None] < WINDOW) & (lane_i[None, :] == n_sel_pad),
                                  big, 0.0), BF16)
    gates = ng.reshape(B, T, G, 3 * J).transpose(0, 2, 1, 3)

    kv_spec = lambda name: pl.BlockSpec((1, T, HEAD_DIM),
                                        lambda b, g, i, o=cols[name] // HEAD_DIM: (b, 0, o + g))
    cmp_spec = pl.BlockSpec((1, 1, n_pad, HEAD_DIM), lambda b, g, i: (b, g, 0, 0))
    const2 = lambda shape: pl.BlockSpec(shape, lambda b, g, i: (0, 0))
    return pl.pallas_call(
        functools.partial(_nsa_prompt_kernel, n_sel=n_sel, n_top=n_top, T=T),
        grid=(B, G, nq),
        in_specs=[pl.BlockSpec((1, tq, J * HEAD_DIM),
                               lambda b, g, i, o=cols['nq'] // (J * HEAD_DIM): (b, i, o + g)),
                  cmp_spec, cmp_spec,
                  kv_spec('ks'), kv_spec('vs'), kv_spec('kw'), kv_spec('vw'),
                  pl.BlockSpec((1, 1, tq, 3 * J), lambda b, g, i: (b, g, i, 0)),
                  pl.BlockSpec((J, tq, n_pad), lambda b, g, i: (g, i, 0)),
                  pl.BlockSpec((J, N_BIAS_TILES, tq, tq), lambda b, g, i: (g, 0, 0, 0)),
                  const2((n_sel_pad, n_pad)), const2((n_sel_pad, LANE)), const2((1, LANE)),
                  const2((T, LANE)), const2((T + WINDOW, LANE))],
        out_specs=pl.BlockSpec((1, tq, J * HEAD_DIM), lambda b, g, i: (b, i, g)),
        out_shape=jax.ShapeDtypeStruct((B, T, NSA_WIDTH), F32),
        scratch_shapes=[pltpu.VMEM((T, 2 * HEAD_DIM), BF16), pltpu.VMEM((T, HEAD_DIM), BF16),
                        pltpu.VMEM((T + WINDOW, 2 * HEAD_DIM), BF16), pltpu.VMEM((T + WINDOW, HEAD_DIM), BF16),
                        pltpu.VMEM((nq, J * tq, tq), F32)],
        compiler_params=pltpu.CompilerParams(
            dimension_semantics=("parallel", "parallel", "arbitrary"),
            vmem_limit_bytes=V7X_VMEM_LIMIT_BYTES),
        name="nsa_prompt",
    )(proj3, kcmp, vcmp, proj3, proj3, proj3, proj3, gates, bcmp, btile, ovl, pick, cvec, kaug_s, kaug_w)


PAGE_ROWS = PAGE_SIZE * NSA_KV_HEADS
STRIDES_PER_PAGE = PAGE_SIZE // CMP_STRIDE
PAGES_PER_STEP = 8


def _cache_rows(cache):
    return cache.reshape(-1, HEAD_DIM)


def _page_specs(layer):
    return [pl.BlockSpec((2 * PAGE_ROWS, HEAD_DIM),
                         lambda b, p, pt, k=k: (pt[b, p * PAGES_PER_STEP + k] * DEPTH + layer, 0))
            for k in range(PAGES_PER_STEP)]


def _cmp_partial_kernel(pt_ref, *refs):
    page_refs = refs[:PAGES_PER_STEP]
    w1k_ref, w1v_ref, z_ref = refs[PAGES_PER_STEP:]
    G = NSA_KV_HEADS
    half = CMP_STRIDE * HEAD_DIM
    n_rows = PAGES_PER_STEP * STRIDES_PER_PAGE
    for kv, w1_ref in ((0, w1k_ref), (1, w1v_ref)):
        slabs = [jnp.transpose(r[kv * PAGE_ROWS:(kv + 1) * PAGE_ROWS, :].reshape(
            STRIDES_PER_PAGE, CMP_STRIDE * G, HEAD_DIM), (1, 0, 2)) for r in page_refs]
        xs = []
        for g in range(G):
            for slab in slabs:
                xs.append(jnp.concatenate([slab[s * G + g] for s in range(CMP_STRIDE)], axis=1))
        x = jnp.concatenate(xs, axis=0).astype(BF16)
        for j in range(CMP_BLOCK // CMP_STRIDE):
            z = jnp.dot(x, w1_ref[j * half:(j + 1) * half, :], preferred_element_type=F32)
            for g in range(G):
                z_ref[0, kv, j, g] = z[g * n_rows:(g + 1) * n_rows]


def cmp_partial(cache_rows, page_table, layer, w1_k, w1_v):
    B, n_pages = page_table.shape
    flat = CMP_BLOCK * HEAD_DIM
    n_str = n_pages * STRIDES_PER_PAGE
    assert n_pages % PAGES_PER_STEP == 0
    wspec = pl.BlockSpec((flat, HEAD_DIM), lambda b, p, pt: (0, 0))
    return pl.pallas_call(
        _cmp_partial_kernel,
        grid_spec=pltpu.PrefetchScalarGridSpec(
            num_scalar_prefetch=1,
            grid=(B, n_pages // PAGES_PER_STEP),
            in_specs=_page_specs(layer) + [wspec, wspec],
            out_specs=pl.BlockSpec((1, 2, 2, NSA_KV_HEADS, PAGES_PER_STEP * STRIDES_PER_PAGE, HEAD_DIM),
                                   lambda b, p, pt: (b, 0, 0, 0, p, 0))),
        out_shape=jax.ShapeDtypeStruct((B, 2, 2, NSA_KV_HEADS, n_str, HEAD_DIM), F32),
        compiler_params=pltpu.CompilerParams(dimension_semantics=("parallel", "arbitrary"),
                                             vmem_limit_bytes=V7X_VMEM_LIMIT_BYTES),
        name="nsa_cmp_partial",
    )(page_table, *([cache_rows] * PAGES_PER_STEP),
      w1_k.reshape(flat, HEAD_DIM).astype(BF16), w1_v.reshape(flat, HEAD_DIM).astype(BF16))


def _cmp_attn_sample_kernel(z_ref, pek_ref, w1k_ref, w2k_ref, pev_ref, w1v_ref, w2v_ref, q_ref, bias_ref,
                            ovl_ref, o_ref, psel_ref, *, q0):
    T = q_ref.shape[1]
    J = NSA_HPG
    n_str = z_ref.shape[4]

    def finish(kv, pe_ref, w1_ref, w2_ref):
        z1 = pltpu.roll(z_ref[0, kv, 1, 0], n_str - 1, 0)
        pe = jnp.broadcast_to(pe_ref[...], (PACK_ROWS, CMP_BLOCK * HEAD_DIM)).astype(BF16)
        h0 = jnp.dot(pe, w1_ref[...], preferred_element_type=F32)[0:1, :]
        h = _gelu_tanh(z_ref[0, kv, 0, 0] + z1 + h0)
        return jnp.dot(h.astype(BF16), w2_ref[...], preferred_element_type=F32).astype(BF16)

    kc = finish(0, pek_ref, w1k_ref, w2k_ref)
    vc = finish(1, pev_ref, w1v_ref, w2v_ref)
    stack = lambda f: jnp.concatenate([f(j) for j in range(J)], axis=0)
    q_all = stack(lambda j: q_ref[0, :, j * HEAD_DIM:(j + 1) * HEAD_DIM] * HEAD_DIM ** -0.5).astype(BF16)
    row = lax.broadcasted_iota(jnp.int32, (T, n_str), 0)
    lane = lax.broadcasted_iota(jnp.int32, (T, n_str), 1)
    valid1 = (q0 + row >= lane * CMP_STRIDE + (CMP_BLOCK - 1)) & (lane < n_str - 1)
    valid = stack(lambda j: valid1)
    s = jnp.where(valid, _dot_nt(q_all, kc) + stack(lambda j: bias_ref[j]), NEG)
    e = jnp.where(valid, jnp.exp(s - jnp.max(s, axis=-1, keepdims=True)), 0.0)
    den = jnp.sum(e, axis=-1, keepdims=True)
    p = e / jnp.where(den > 0.0, den, 1.0)
    o_ref[0, 0] = jnp.dot(p.astype(BF16), vc, preferred_element_type=F32)
    psum = p[0:T]
    for j in range(1, J):
        psum = psum + p[j * T:(j + 1) * T]
    psum = _pad_rows(psum, PACK_ROWS)
    hi = psum.astype(BF16)
    r1 = psum - hi.astype(F32)
    mid = r1.astype(BF16)
    lo = (r1 - mid.astype(F32)).astype(BF16)
    ovl = ovl_ref[...]
    p_sel = (jnp.dot(hi, ovl, preferred_element_type=F32) + jnp.dot(mid, ovl, preferred_element_type=F32)
             + jnp.dot(lo, ovl, preferred_element_type=F32))
    psel_ref[0, 0] = p_sel[:T]


def _rank_select_kernel(psel_ref, tpos_ref, out_ref, score_ref, rank_ref, *, n_sel, n_top):
    shape = psel_ref.shape
    blk = lax.broadcasted_iota(jnp.int32, shape, 0)
    cur = jnp.broadcast_to(tpos_ref[...], shape) // SEL_BLOCK
    forced = (blk == 0) | (blk == cur) | (blk == cur - 1)
    ok = (blk <= cur) & (blk < n_sel)
    score = jnp.where(ok, psel_ref[...] + jnp.where(forced, FORCE_SCORE, 0.0), NEG)
    score_ref[...] = score
    rank_ref[...] = jnp.zeros(shape, F32)

    def body(jb, c):
        other = jnp.broadcast_to(score_ref[pl.ds(jb, 1), :], shape)
        beats = (other > score) | ((other == score) & (blk > jb))
        rank_ref[...] = rank_ref[...] + jnp.where(beats, 1.0, 0.0)
        return c

    lax.fori_loop(0, n_sel, body, 0)
    out_ref[...] = jnp.where((rank_ref[...] < n_top) & ok, 0.0, NEG)


def _slc_win_sample_kernel(pt_ref, *refs, win_buf):
    page_refs = refs[:PAGES_PER_STEP]
    (qT_ref, seladd_ref, blast_ref, selnew_ref, ksn_ref, vsn_ref, kwn_ref, vwn_ref, bnew_ref, cw_ref, bwin_ref,
     ocmp_ref, gate_ref, o_ref, m_ref, l_ref, acc_ref) = refs[PAGES_PER_STEP:]
    G = NSA_KV_HEADS
    p = pl.program_id(1)
    last = pl.num_programs(1) - 1
    lane_group = lax.broadcasted_iota(jnp.int32, (1, LANE), 1) // (LANE // G)

    @pl.when(p == 0)
    def _():
        m_ref[...] = jnp.full(m_ref.shape, NEG, F32)
        l_ref[...] = jnp.zeros(l_ref.shape, F32)
        acc_ref[...] = jnp.zeros(acc_ref.shape, F32)

    def scores(k_of_g):
        s = None
        for g in range(G):
            sg = jnp.dot(k_of_g(g).astype(BF16), qT_ref[0, g], preferred_element_type=F32)
            s = sg if s is None else s + sg
        return s

    def weighted_values(pT, v_of_g):
        o = None
        for g in range(G):
            pg = jnp.where(lane_group == g, pT, 0.0).astype(BF16)
            og = _dot_tn(v_of_g(g).astype(BF16), pg)
            o = og if o is None else o + og
        return o

    def accumulate(segments):
        m_old = m_ref[0:1, :]
        m_new = m_old
        for sT, _ in segments:
            m_new = jnp.maximum(m_new, jnp.max(sT, axis=0, keepdims=True))
        alpha = jnp.exp(m_old - m_new)
        l_new = alpha * l_ref[0:1, :]
        acc = alpha * acc_ref[...]
        for sT, v_of_g in segments:
            pT = jnp.exp(sT - m_new)
            l_new = l_new + jnp.sum(pT, axis=0, keepdims=True)
            acc = acc + weighted_values(pT, v_of_g)
        l_ref[...] = jnp.broadcast_to(l_new, l_ref.shape)
        acc_ref[...] = acc
        m_ref[...] = jnp.broadcast_to(m_new, m_ref.shape)

    key = lax.broadcasted_iota(jnp.int32, (PAGE_SIZE, LANE), 0)
    blocks_per_page = PAGE_SIZE // SEL_BLOCK
    is_last = (p == last).astype(F32)
    segments = []
    for k, page_ref in enumerate(page_refs):
        k_page = lambda g, r=page_ref: r[pl.ds(g, PAGE_SIZE, stride=G), :]
        v_page = lambda g, r=page_ref: r[pl.ds(PAGE_ROWS + g, PAGE_SIZE, stride=G), :]
        mask = seladd_ref[0, k, blocks_per_page - 1:blocks_per_page, :]
        for i in range(blocks_per_page - 2, -1, -1):
            mask = jnp.where(key < (i + 1) * SEL_BLOCK, seladd_ref[0, k, i:i + 1, :], mask)
        sT = scores(k_page) + mask
        if k == PAGES_PER_STEP - 1:
            sT = sT + blast_ref[...] * is_last
        segments.append((sT, v_page))
    accumulate(segments)

    @pl.when(p == last)
    def _():
        new = lambda ref: (lambda g: _pad_rows(ref[0, :, g * HEAD_DIM:(g + 1) * HEAD_DIM], PACK_ROWS))
        accumulate([(scores(new(ksn_ref)) + bnew_ref[...] + selnew_ref[0], new(vsn_ref))])
        o_slc = (acc_ref[...] / l_ref[0:1, :]).T

        k_win = lambda g: cw_ref[pl.ds(g, win_buf, stride=G), :]
        v_win = lambda g: cw_ref[pl.ds(win_buf * G + g, win_buf, stride=G), :]
        s_w = scores(k_win) + bwin_ref[...]
        s_n = scores(new(kwn_ref)) + bnew_ref[...]
        m = jnp.maximum(jnp.max(s_w, axis=0, keepdims=True), jnp.max(s_n, axis=0, keepdims=True))
        p_w = jnp.exp(s_w - m)
        p_n = jnp.exp(s_n - m)
        den = jnp.sum(p_w, axis=0, keepdims=True) + jnp.sum(p_n, axis=0, keepdims=True)
        o_win = ((weighted_values(p_w, v_win) + weighted_values(p_n, new(vwn_ref))) / den).T

        gates = jax.nn.sigmoid(gate_ref[0])
        o_ref[0] = gates[:, 0:1] * ocmp_ref[0] + gates[:, 1:2] * o_slc + gates[:, 2:3] * o_win


def nsa_sample(proj3, cols, ng, cache_cmp, cache_slc, cache_win, page_table, layer, rel_bias,
               pe_k, w1_k, w2_k, pe_v, w1_v, w2_v):
    B, T, _ = proj3.shape
    G, J, H = NSA_KV_HEADS, NSA_HPG, NSA_HEADS
    n_pages = page_table.shape[1]
    past = n_pages * PAGE_SIZE
    q0 = past
    win_buf = cache_win.shape[3]
    L = G * J * T
    assert L == LANE and T <= PACK_ROWS and T < CMP_STRIDE and past % SEL_BLOCK == 0 and T <= SEL_BLOCK
    assert win_buf == min(WINDOW, past)
    n_str = past // CMP_STRIDE
    n_sel = past // SEL_BLOCK + 1
    n_top = min(SEL_TOP, n_sel)
    n_sel_rows = _round_up(n_sel, 8)
    n_sel_lanes = _round_up(n_sel, LANE)
    flat = CMP_BLOCK * HEAD_DIM
    b_far = rel_bias.astype(F32)[_bucket_table()[MAX_DISTANCE - 1]]

    def lane_bias(dist, ok):
        b = _bias_of_distance(rel_bias, np.maximum(dist, 0)) - b_far[:, None, None]
        b = jnp.where(jnp.asarray(ok)[None], b, NEG)
        return jnp.moveaxis(b.reshape(G, J, dist.shape[0], T), 2, 0).reshape(dist.shape[0], L)

    ti = np.arange(T)[None, :]
    z = cmp_partial(_cache_rows(cache_cmp), page_table, layer, w1_k, w1_v)
    n = np.arange(n_str)[None, :]
    dist_c = q0 + np.arange(T)[:, None] - (n * CMP_STRIDE + CMP_BLOCK - 1)
    near = dist_c.min(axis=0) < MAX_DISTANCE
    n_far = int(np.argmax(near)) if near.any() else n_str
    bcmp = jnp.concatenate([jnp.broadcast_to(b_far[:, None, None], (H, T, n_far)),
                            _bias_of_distance(rel_bias, np.maximum(dist_c[:, n_far:], 0))], axis=2)
    cmp_start = np.arange(n_str) * CMP_STRIDE
    sel_start = np.arange(n_sel_lanes) * SEL_BLOCK
    ovl = ((cmp_start[:, None] < sel_start[None, :] + SEL_BLOCK) & (cmp_start[:, None] + CMP_BLOCK > sel_start[None, :])
           & (np.arange(n_str)[:, None] < n_str - 1) & (np.arange(n_sel_lanes)[None, :] < n_sel))
    wspec = [pl.BlockSpec((1, flat), lambda b, g: (0, 0)),
             pl.BlockSpec((flat, HEAD_DIM), lambda b, g: (0, 0)),
             pl.BlockSpec((HEAD_DIM, HEAD_DIM), lambda b, g: (0, 0))]
    o_cmp, p_sel = pl.pallas_call(
        functools.partial(_cmp_attn_sample_kernel, q0=q0),
        grid=(B, G),
        in_specs=[pl.BlockSpec((1, 2, 2, 1, n_str, HEAD_DIM), lambda b, g: (b, 0, 0, g, 0, 0))] + wspec + wspec + [
            pl.BlockSpec((1, T, J * HEAD_DIM), lambda b, g, o=cols['nq'] // (J * HEAD_DIM): (b, 0, o + g)),
            pl.BlockSpec((J, T, n_str), lambda b, g: (g, 0, 0)),
            pl.BlockSpec((n_str, n_sel_lanes), lambda b, g: (0, 0))],
        out_specs=(pl.BlockSpec((1, 1, J * T, HEAD_DIM), lambda b, g: (b, g, 0, 0)),
                   pl.BlockSpec((1, 1, T, n_sel_lanes), lambda b, g: (b, g, 0, 0))),
        out_shape=(jax.ShapeDtypeStruct((B, G, J * T, HEAD_DIM), F32),
                   jax.ShapeDtypeStruct((B, G, T, n_sel_lanes), F32)),
        compiler_params=pltpu.CompilerParams(dimension_semantics=("parallel", "parallel"),
                                             vmem_limit_bytes=V7X_VMEM_LIMIT_BYTES),
        name="nsa_cmp_attn_sample",
    )(z, pe_k.reshape(1, flat), w1_k.reshape(flat, HEAD_DIM).astype(BF16), w2_k.astype(BF16),
      pe_v.reshape(1, flat), w1_v.reshape(flat, HEAD_DIM).astype(BF16), w2_v.astype(BF16),
      proj3, bcmp, jnp.asarray(ovl, BF16))

    n_bgt = B * G * T
    psel_t = p_sel.reshape(n_bgt, n_sel_lanes)[:, :n_sel_rows].T
    tpos = jnp.asarray(np.tile(q0 + np.arange(T), B * G)[None, :], jnp.int32)
    seladd = pl.pallas_call(
        functools.partial(_rank_select_kernel, n_sel=n_sel, n_top=n_top),
        out_shape=jax.ShapeDtypeStruct((n_sel_rows, n_bgt), F32),
        scratch_shapes=[pltpu.VMEM((n_sel_rows, n_bgt), F32), pltpu.VMEM((n_sel_rows, n_bgt), F32)],
        name="nsa_rank_select",
    )(psel_t, tpos)
    seladd = seladd.T.reshape(B, G, 1, T, n_sel_rows)
    seladd = jnp.broadcast_to(seladd, (B, G, J, T, n_sel_rows)).reshape(B, L, n_sel_rows)
    bpp = PAGE_SIZE // SEL_BLOCK
    sel_past = seladd[:, :, :n_sel - 1].reshape(B, L, n_pages, bpp).transpose(0, 2, 3, 1)
    sel_new = seladd[:, :, n_sel - 1].reshape(B, 1, L)

    q = proj3[:, :, cols['nq']:cols['nq'] + NSA_WIDTH].reshape(B, T, G, J, HEAD_DIM) * HEAD_DIM ** -0.5
    q_t = q.transpose(0, 2, 4, 3, 1).reshape(B, G, HEAD_DIM, J * T)
    place = jnp.asarray(np.arange(G)[:, None, None] == (np.arange(L) // (J * T))[None, None, :])
    q_pad = jnp.where(place[None], jnp.tile(q_t, (1, 1, 1, G)), 0.0).astype(BF16)
    ki = np.arange(PAGE_SIZE)[:, None]
    b_last = lane_bias(PAGE_SIZE + ti - ki, np.ones((PAGE_SIZE, T), bool))
    kn = np.arange(PACK_ROWS)[:, None]
    b_new = lane_bias(ti - kn, (ti - kn >= 0) & (kn < T))
    kw = np.arange(win_buf)[:, None]
    b_win = lane_bias(win_buf + ti - kw, win_buf + ti - kw <= WINDOW)
    gates = ng.reshape(B, T, G, J, 3).transpose(0, 2, 3, 1, 4).reshape(B, L, 3)
    win_rows = 2 * win_buf * G
    new_spec = lambda name: pl.BlockSpec((1, T, KV_WIDTH), lambda b, p, pt, o=cols[name] // KV_WIDTH: (b, 0, o))
    const = lambda shape: pl.BlockSpec(shape, lambda b, p, pt: (0, 0))
    per_b = lambda shape: pl.BlockSpec((1,) + shape, lambda b, p, pt: (b,) + (0,) * len(shape))
    out = pl.pallas_call(
        functools.partial(_slc_win_sample_kernel, win_buf=win_buf),
        grid_spec=pltpu.PrefetchScalarGridSpec(
            num_scalar_prefetch=1,
            grid=(B, n_pages // PAGES_PER_STEP),
            in_specs=_page_specs(layer) + [
                per_b((G, HEAD_DIM, L)),
                pl.BlockSpec((1, PAGES_PER_STEP, bpp, L), lambda b, p, pt: (b, p, 0, 0)),
                const((PAGE_SIZE, L)), per_b((1, L)),
                new_spec('ks'), new_spec('vs'), new_spec('kw'), new_spec('vw'),
                const((PACK_ROWS, L)),
                pl.BlockSpec((win_rows, HEAD_DIM), lambda b, p, pt: (layer * B + b, 0)),
                const((win_buf, L)), per_b((L, HEAD_DIM)), per_b((L, 3))],
            out_specs=per_b((L, HEAD_DIM)),
            scratch_shapes=[pltpu.VMEM((8, L), F32), pltpu.VMEM((8, L), F32), pltpu.VMEM((HEAD_DIM, L), F32)]),
        out_shape=jax.ShapeDtypeStruct((B, L, HEAD_DIM), F32),
        compiler_params=pltpu.CompilerParams(dimension_semantics=("parallel", "arbitrary"),
                                             vmem_limit_bytes=V7X_VMEM_LIMIT_BYTES),
        name="nsa_slc_win_sample",
    )(page_table, *([_cache_rows(cache_slc)] * PAGES_PER_STEP), q_pad, sel_past, b_last, sel_new,
      proj3, proj3, proj3, proj3, b_new,
      cache_win.reshape(-1, HEAD_DIM), b_win, o_cmp.reshape(B, L, HEAD_DIM), gates)
    return out.reshape(B, G, J, T, HEAD_DIM).transpose(0, 3, 1, 2, 4).reshape(B, T, NSA_WIDTH)


def _retention_tables(T, q0):
    c = _largest_divisor(T, RET_CHUNK)
    cp = max(c, RET_CHUNK)
    lg = np.log1p(-(2.0 ** (-5.0 - np.arange(RET_HEADS, dtype=np.float32)))).astype(np.float32)
    i = np.arange(cp)
    rel = i[:, None] - i[None, :]
    inside = (i < c)[:, None] & (i < c)[None, :]
    decay = np.where((rel >= 0) & inside, np.exp(np.maximum(rel, 0)[None] * lg[:, None, None]), 0.0)
    q_dec = np.broadcast_to(np.exp((i + 1)[None, :, None] * lg[:, None, None]), (RET_HEADS, cp, HEAD_DIM))
    k_dec = np.where((i < c)[None, :, None], np.exp((c - 1 - i)[None, :, None] * lg[:, None, None]), 0.0)
    k_dec = np.broadcast_to(k_dec, (RET_HEADS, cp, HEAD_DIM))
    chunk_dec = np.broadcast_to(np.exp(c * lg)[:, None, None], (RET_HEADS, 8, HEAD_DIM))
    half = HEAD_DIM // 2
    inv = (1.0 / (10000.0 ** np.linspace(0.0, 1.0, half, dtype=np.float32))).astype(np.float32)
    ang = (q0 + np.arange(T)).astype(np.float32)[:, None] * inv[None]
    cos, sin = np.cos(ang), np.sin(ang)
    cosf = np.concatenate([cos, cos], axis=1)
    sinf = np.concatenate([-sin, sin], axis=1)
    f = lambda a: jnp.asarray(a, F32)
    return c, cp, f(decay), f(q_dec), f(k_dec), f(chunk_dec), f(cosf), f(sinf)


RET_HEADS_PER_STEP = 2


def _retention_kernel(q_ref, k_ref, v_ref, g_ref, s0_ref, cos_ref, sin_ref, dec_ref, qd_ref, kd_ref, cd_ref,
                      o_ref, s_ref, *, c, cp, n):
    half = HEAD_DIM // 2
    heads = range(RET_HEADS_PER_STEP)

    def load(ref, rows):
        return _pad_rows(ref[rows, :], cp)

    def rot(x, cos, sin):
        return x * cos + pltpu.roll(x, half, 1) * sin

    def body(i, states):
        rows = pl.ds(pl.multiple_of(i * c, c), c)
        cos, sin = load(cos_ref, rows), load(sin_ref, rows)
        q_all, k_all, v_all = load(q_ref.at[0], rows), load(k_ref.at[0], rows), load(v_ref.at[0], rows)
        new_states = []
        for hh in heads:
            cols = slice(hh * HEAD_DIM, (hh + 1) * HEAD_DIM)
            s = states[hh]
            q = rot(q_all[:, cols], cos, sin)
            k = rot(k_all[:, cols], cos, sin) * HEAD_DIM ** -0.5
            v = v_all[:, cols].astype(BF16)
            qb = q.astype(BF16)
            inner = _dot_nt(qb, k.astype(BF16)) * dec_ref[hh]
            o = (jnp.dot(inner.astype(BF16), v, preferred_element_type=F32)
                 + jnp.dot(qb, s.astype(BF16), preferred_element_type=F32) * qd_ref[hh])
            new_states.append(s * cd_ref[hh, 0:1, :] + _dot_tn((k * kd_ref[hh]).astype(BF16), v))
            o = o * lax.rsqrt(jnp.mean(o * o, axis=-1, keepdims=True) + EPS)
            g = g_ref[0, rows, cols]
            o_ref[0, rows, cols] = g * jax.nn.sigmoid(g) * o[:c]
        return tuple(new_states)

    final = lax.fori_loop(0, n, body, tuple(s0_ref[0, hh] for hh in heads))
    for hh in heads:
        s_ref[0, hh] = final[hh]


def retention(proj3, cols, s0, q0):
    B, T, _ = proj3.shape
    c, cp, decay, q_dec, k_dec, chunk_dec, cosf, sinf = _retention_tables(T, q0)
    hp = RET_HEADS_PER_STEP
    wide = hp * HEAD_DIM
    col = lambda name: pl.BlockSpec((1, T, wide), lambda b, h, o=cols[name] // wide: (b, 0, o + h))
    tab = lambda r: pl.BlockSpec((hp, r, HEAD_DIM), lambda b, h: (h, 0, 0))
    full = pl.BlockSpec((T, HEAD_DIM), lambda b, h: (0, 0))
    state = pl.BlockSpec((1, hp, HEAD_DIM, HEAD_DIM), lambda b, h: (b, h, 0, 0))
    return pl.pallas_call(
        functools.partial(_retention_kernel, c=c, cp=cp, n=T // c),
        grid=(B, RET_HEADS // hp),
        in_specs=[col('rq'), col('rk'), col('rv'), col('rg'), state, full, full,
                  tab(cp), tab(cp), tab(cp), tab(8)],
        out_specs=(pl.BlockSpec((1, T, wide), lambda b, h: (b, 0, h)), state),
        out_shape=(jax.ShapeDtypeStruct((B, T, RET_WIDTH), F32),
                   jax.ShapeDtypeStruct((B, RET_HEADS, HEAD_DIM, HEAD_DIM), F32)),
        compiler_params=pltpu.CompilerParams(dimension_semantics=("parallel", "parallel"),
                                             vmem_limit_bytes=V7X_VMEM_LIMIT_BYTES),
        name="retention",
    )(proj3, proj3, proj3, proj3, s0.astype(F32), cosf, sinf, decay, q_dec, k_dec, chunk_dec)


S5_BLK_GROUPS = 8
S5_BLK_STATE = S5_BLK_GROUPS * S5_STATE
S5_BLK_CH = S5_BLK_GROUPS * S5_GROUP
S5_SCAN_ROWS = 8


def _s5_params(lam_re, lam_im, log_step, b_re, b_im, c_re, c_im):
    nb = S5_GROUPS // S5_BLK_GROUPS
    lam = lax.complex(lam_re.astype(F32), lam_im.astype(F32))
    step = jnp.exp(log_step.astype(F32))[:, None]
    a_bar = jnp.exp(lam * step)
    b_bar = ((a_bar - 1.0) / lam)[..., None] * lax.complex(b_re.astype(F32), b_im.astype(F32))
    r = np.arange(S5_SCAN_ROWS)

    def powers(k, keep):
        p = jnp.exp(lam[None] * step[None] * jnp.asarray(k, F32)[:, None, None])
        return jnp.where(jnp.asarray(keep)[:, None, None], p, 0.0)

    tabs = [powers(np.full(S5_SCAN_ROWS, k), r >= k) for k in (1, 2, 4)]
    tabs.append(powers(r + 1, r >= 0))
    tab = jnp.stack(tabs)
    tab = tab.reshape(4, S5_SCAN_ROWS, nb, S5_BLK_STATE).transpose(2, 0, 1, 3)
    atab = jnp.concatenate([tab.real, tab.imag], axis=1)

    eye = jnp.eye(S5_BLK_GROUPS, dtype=F32)
    bb = b_bar.reshape(nb, S5_BLK_GROUPS, S5_STATE, S5_GROUP)

    def in_mat(x):
        return jnp.einsum('ngpc,gh->ngchp', x, eye).reshape(nb, S5_BLK_CH, S5_BLK_STATE)

    bmat = jnp.concatenate([in_mat(bb.real), in_mat(bb.imag)], axis=-1).astype(BF16)
    cr = c_re.astype(F32).reshape(nb, S5_BLK_GROUPS, S5_GROUP, S5_STATE)
    ci = c_im.astype(F32).reshape(nb, S5_BLK_GROUPS, S5_GROUP, S5_STATE)

    def out_mat(x):
        return jnp.einsum('ngcp,gh->ngphc', x, eye).reshape(nb, S5_BLK_STATE, S5_BLK_CH)

    cmat = jnp.concatenate([out_mat(cr), -out_mat(ci)], axis=1).astype(BF16)
    return atab, bmat, cmat


def _s5_scan_tile(xr, xi, cr, ci, atab_ref):
    for idx, k in enumerate((1, 2, 4)):
        pr, pi = atab_ref[0, idx], atab_ref[0, 4 + idx]
        sr, si = pltpu.roll(xr, k, 0), pltpu.roll(xi, k, 0)
        xr, xi = xr + pr * sr - pi * si, xi + pr * si + pi * sr
    pr, pi = atab_ref[0, 3], atab_ref[0, 7]
    xr, xi = xr + pr * cr - pi * ci, xi + pr * ci + pi * cr
    last = S5_SCAN_ROWS - 1
    cr = jnp.broadcast_to(xr[last:last + 1, :], xr.shape)
    ci = jnp.broadcast_to(xi[last:last + 1, :], xi.shape)
    return xr, xi, cr, ci


def _s5_kernel(u_ref, x0_ref, atab_ref, b_ref, c_ref, d_ref, y_ref, st_ref, xs_ref, *, T):
    u = u_ref[0]
    t_pad = _round_up(T, PACK_ROWS)
    xs_ref[...] = jnp.dot(_pad_rows(u, t_pad).astype(BF16), b_ref[0], preferred_element_type=F32)[:T]
    n = S5_BLK_STATE
    R = S5_SCAN_ROWS

    def body(i, carry):
        cr, ci = carry
        rows = pl.ds(pl.multiple_of(i * R, R), R)
        xr, xi, cr, ci = _s5_scan_tile(xs_ref[rows, :n], xs_ref[rows, n:], cr, ci, atab_ref)
        xs_ref[rows, :n] = xr
        xs_ref[rows, n:] = xi
        return cr, ci

    x0 = x0_ref[0, 0]
    cr0 = jnp.broadcast_to(x0[0:1, :], (R, n))
    ci0 = jnp.broadcast_to(x0[1:2, :], (R, n))
    cr, ci = lax.fori_loop(0, T // R, body, (cr0, ci0))
    st_ref[0, 0] = jnp.concatenate([cr[0:1], ci[0:1]], axis=0)
    y = jnp.dot(_pad_rows(xs_ref[...], t_pad).astype(BF16), c_ref[0], preferred_element_type=F32)[:T]
    y_ref[0] = _gelu_tanh(y + d_ref[...] * u)


def s5_scan(proj3, cols, x0, lam_re, lam_im, log_step, b_re, b_im, c_re, c_im, d):
    B, T, _ = proj3.shape
    assert T % S5_SCAN_ROWS == 0
    nb = S5_GROUPS // S5_BLK_GROUPS
    atab, bmat, cmat = _s5_params(lam_re, lam_im, log_step, b_re, b_im, c_re, c_im)
    x0b = x0.astype(F32).reshape(B, nb, S5_BLK_STATE, 2).transpose(0, 1, 3, 2)
    blk3 = lambda shape: pl.BlockSpec((1,) + shape, lambda b, j: (j, 0, 0))
    y, st = pl.pallas_call(
        functools.partial(_s5_kernel, T=T),
        grid=(B, nb),
        in_specs=[pl.BlockSpec((1, T, S5_BLK_CH), lambda b, j, o=cols['su'] // S5_BLK_CH: (b, 0, o + j)),
                  pl.BlockSpec((1, 1, 2, S5_BLK_STATE), lambda b, j: (b, j, 0, 0)),
                  pl.BlockSpec((1, 8, S5_SCAN_ROWS, S5_BLK_STATE), lambda b, j: (j, 0, 0, 0)),
                  blk3((S5_BLK_CH, 2 * S5_BLK_STATE)), blk3((2 * S5_BLK_STATE, S5_BLK_CH)),
                  pl.BlockSpec((1, S5_BLK_CH), lambda b, j: (0, j))],
        out_specs=(pl.BlockSpec((1, T, S5_BLK_CH), lambda b, j: (b, 0, j)),
                   pl.BlockSpec((1, 1, 2, S5_BLK_STATE), lambda b, j: (b, j, 0, 0))),
        out_shape=(jax.ShapeDtypeStruct((B, T, S5_WIDTH), F32),
                   jax.ShapeDtypeStruct((B, nb, 2, S5_BLK_STATE), F32)),
        scratch_shapes=[pltpu.VMEM((T, 2 * S5_BLK_STATE), F32)],
        compiler_params=pltpu.CompilerParams(dimension_semantics=("parallel", "parallel"),
                                             vmem_limit_bytes=V7X_VMEM_LIMIT_BYTES),
        name="s5_scan",
    )(proj3, x0b, atab, bmat, cmat, d.astype(F32).reshape(1, S5_WIDTH))
    st = st.transpose(0, 1, 3, 2).reshape(B, S5_GROUPS, S5_STATE, 2)
    return y, st


def _branch_norm_kernel(ro_ref, so_ref, no_ref, bn_ref, o_ref):
    off = 0
    for ref in (ro_ref, so_ref, no_ref):
        x = ref[...]
        w = x.shape[-1]
        y = x * lax.rsqrt(jnp.mean(x * x, axis=-1, keepdims=True) + EPS) * bn_ref[:, off:off + w]
        o_ref[:, off:off + w] = y.astype(o_ref.dtype)
        off += w


def branch_norm(ro, so, no, bn):
    m = ro.shape[0]
    tm = min(m, 256)
    assert m % tm == 0
    spec = lambda w: pl.BlockSpec((tm, w), lambda i: (i, 0))
    return pl.pallas_call(
        _branch_norm_kernel,
        grid=(m // tm,),
        in_specs=[spec(RET_WIDTH), spec(S5_WIDTH), spec(NSA_WIDTH), pl.BlockSpec((1, D_MODEL), lambda i: (0, 0))],
        out_specs=spec(D_MODEL),
        out_shape=jax.ShapeDtypeStruct((m, D_MODEL), BF16),
        compiler_params=pltpu.CompilerParams(dimension_semantics=("parallel",),
                                             vmem_limit_bytes=V7X_VMEM_LIMIT_BYTES),
        name="branch_norm",
    )(ro, so, no, bn.astype(F32).reshape(1, D_MODEL))


def _kv_rows_kernel(x_ref, o_ref):
    x = x_ref[0]
    for g in range(NSA_KV_HEADS):
        o_ref[pl.ds(g, x.shape[0], stride=NSA_KV_HEADS), :] = x[:, g * HEAD_DIM:(g + 1) * HEAD_DIM]


def kv_rows(proj3, col_k, t_start, t_len):
    B = proj3.shape[0]
    tT = min(t_len, 512)
    assert t_len % tT == 0 and t_start % tT == 0 and col_k % KV_WIDTH == 0
    n = t_len // tT
    rows = pl.pallas_call(
        _kv_rows_kernel,
        grid=(B, 2, n),
        in_specs=[pl.BlockSpec((1, tT, KV_WIDTH), lambda b, kv, i: (b, t_start // tT + i, col_k // KV_WIDTH + kv))],
        out_specs=pl.BlockSpec((tT * NSA_KV_HEADS, HEAD_DIM), lambda b, kv, i: ((b * 2 + kv) * n + i, 0)),
        out_shape=jax.ShapeDtypeStruct((B * 2 * t_len * NSA_KV_HEADS, HEAD_DIM), F32),
        compiler_params=pltpu.CompilerParams(dimension_semantics=("parallel", "parallel", "parallel"),
                                             vmem_limit_bytes=V7X_VMEM_LIMIT_BYTES),
        name="kv_rows",
    )(proj3)
    return rows.reshape(B, 2, t_len, NSA_KV_HEADS, HEAD_DIM)


_COL_NAMES = ('rq', 'rk', 'rv', 'rg', 'su', 'nq', 'kc', 'vc', 'ks', 'vs', 'kw', 'vw', 'ng')
COLS = {name: int(off) for name, off in zip(_COL_NAMES, np.concatenate([[0], np.cumsum(IN_SPLITS)]))}


def _block(x, layer, w, rel_bias, past, win_buf, big):
    B, T, _ = x.shape
    G = NSA_KV_HEADS
    M = B * T
    emitted = {}

    def project(a, name, **kw):
        if big is not None:
            return matmul(a, big[name], **kw)
        out, emitted[name] = matmul(a, w[name], layer, emit_bf16=True, **kw)
        return out

    x2 = x.reshape(M, D_MODEL)
    h = rmsnorm(x2, w['norm_mix'][layer], BF16)
    proj3 = project(h, 'w_in', n=COLS['ng']).reshape(B, T, -1)
    ng = matmul(h, w['w_gate'], layer)[:, :3 * NSA_HEADS].reshape(B, T, 3 * NSA_HEADS)
    s5_w = [w[k][layer] for k in ('s5_lambda_re', 's5_lambda_im', 's5_log_step', 's5_b_re', 's5_b_im',
                                  's5_c_re', 's5_c_im', 's5_d')]
    cmp_w = [w[k][layer] for k in ('cmp_pe_k', 'cmp_w1_k', 'cmp_w2_k', 'cmp_pe_v', 'cmp_w1_v', 'cmp_w2_v')]
    if past is None:
        q0 = 0
        ret_s0 = jnp.zeros((B, RET_HEADS, HEAD_DIM, HEAD_DIM), F32)
        s5_s0 = jnp.zeros((B, S5_GROUPS, S5_STATE, 2), F32)
        win_prev = jnp.zeros((B, 2, WINDOW, G, HEAD_DIM), x.dtype)
    else:
        cache_cmp, cache_slc, cache_win, state_ret, state_s5, page_table = past
        q0 = page_table.shape[1] * PAGE_SIZE
        ret_s0, s5_s0, win_prev = state_ret[layer], state_s5[layer], cache_win[layer]

    ro, ret_s = retention(proj3, COLS, ret_s0, q0)
    sy, s5_s = s5_scan(proj3, COLS, s5_s0, *s5_w)
    if past is None:
        kcmp, vcmp = compress_prompt(proj3, COLS['kc'], COLS['vc'], *cmp_w)
        no = nsa_prompt(proj3, COLS, ng, kcmp, vcmp, rel_bias)
    else:
        no = nsa_sample(proj3, COLS, ng, cache_cmp, cache_slc, cache_win, page_table, layer, rel_bias, *cmp_w)

    sy2 = sy.reshape(M, S5_WIDTH)
    so = matmul(sy2, w['s5_w_glu'], layer, res=sy2, act="glu")
    mix = branch_norm(ro.reshape(M, RET_WIDTH), so, no.reshape(M, NSA_WIDTH), w['branch_norm'][layer])
    x2 = project(mix, 'w_out', res=x2)
    h = rmsnorm(x2, w['norm_ffn'][layer], BF16)
    up = project(h, 'w_up', act="relu2", out_dtype=BF16)
    x2 = project(up, 'w_down', res=x2)

    cmp_rows = kv_rows(proj3, COLS['kc'], 0, T)
    slc_rows = kv_rows(proj3, COLS['ks'], 0, T)
    if T >= win_buf:
        win_new = kv_rows(proj3, COLS['kw'], T - win_buf, win_buf)
    else:
        win_new = jnp.concatenate([win_prev, kv_rows(proj3, COLS['kw'], 0, T)], axis=2)[:, :, -win_buf:]
    return x2.reshape(B, T, D_MODEL), cmp_rows, slc_rows, win_new, ret_s, s5_s, (big or emitted)


def kernel(x_prompt, x_sample, cache_cmp, cache_slc, cache_win, state_ret, state_s5, page_table,
           rel_bias, norm_mix, w_in, s5_lambda_re, s5_lambda_im, s5_log_step, s5_b_re, s5_b_im,
           s5_c_re, s5_c_im, s5_d, s5_w_glu, cmp_pe_k, cmp_w1_k, cmp_w2_k, cmp_pe_v, cmp_w1_v,
           cmp_w2_v, branch_norm, w_out, norm_ffn, w_up, w_down, norm_final):
    win_buf = cache_win.shape[3]
    n_main = COLS['ng']
    assert n_main % LANE == 0
    w = dict(
        norm_mix=norm_mix, norm_ffn=norm_ffn, branch_norm=branch_norm,
        w_in=w_in, w_out=w_out, w_up=w_up, w_down=w_down,
        w_gate=jnp.pad(w_in[:, :, n_main:].astype(BF16), ((0, 0), (0, 0), (0, LANE - (IN_COLS - n_main)))),
        s5_w_glu=s5_w_glu.astype(BF16),
        s5_lambda_re=s5_lambda_re, s5_lambda_im=s5_lambda_im, s5_log_step=s5_log_step, s5_b_re=s5_b_re,
        s5_b_im=s5_b_im, s5_c_re=s5_c_re, s5_c_im=s5_c_im, s5_d=s5_d,
        cmp_pe_k=cmp_pe_k, cmp_w1_k=cmp_w1_k, cmp_w2_k=cmp_w2_k,
        cmp_pe_v=cmp_pe_v, cmp_w1_v=cmp_w1_v, cmp_w2_v=cmp_w2_v)
    past = (cache_cmp, cache_slc, cache_win, state_ret, state_s5, page_table)
    xp, xs = x_prompt, x_sample
    written_p, written_s = [], []
    for layer in range(DEPTH):
        xs, *entries, big = _block(xs, layer, w, rel_bias, past, win_buf, None)
        written_s.append(entries)
        xp, *entries, _ = _block(xp, layer, w, rel_bias, None, win_buf, big)
        written_p.append(entries)
    y_prompt = rmsnorm(xp.reshape(-1, D_MODEL), norm_final, F32).reshape(xp.shape)
    y_sample = rmsnorm(xs.reshape(-1, D_MODEL), norm_final, F32).reshape(xs.shape)
    stacked = lambda written, i, axis: jnp.stack([entries[i] for entries in written], axis=axis)
    return (y_prompt, y_sample,
            stacked(written_p, 0, 1), stacked(written_s, 0, 1),
            stacked(written_p, 1, 1), stacked(written_s, 1, 1),
            stacked(written_p, 2, 0), stacked(written_s, 2, 0),
            stacked(written_p, 3, 0), stacked(written_s, 3, 0),
            stacked(written_p, 4, 0), stacked(written_s, 4, 0))
```

```python
import functools
import math

import jax
import jax.numpy as jnp
import numpy as np
from jax import lax
from jax.experimental import pallas as pl
from jax.experimental.pallas import tpu as pltpu

F32 = jnp.float32
BF16 = jnp.bfloat16

D_MODEL = 4096
DEPTH = 2
PAGE_SIZE = 128
HEAD_DIM = 128
RET_WIDTH = 1024
RET_HEADS = 8
RET_CHUNK = 128
S5_WIDTH = 1024
S5_GROUP = 16
S5_GROUPS = 64
S5_STATE = 64
NSA_WIDTH = 2048
NSA_HEADS = 16
NSA_KV_HEADS = 4
NSA_HPG = 4
KV_WIDTH = 512
CMP_BLOCK = 32
CMP_STRIDE = 16
SEL_BLOCK = 64
SEL_TOP = 16
WINDOW = 512
FORCE_SCORE = 1e4
NEG = -1e30
N_BUCKETS = 32
MAX_DISTANCE = 128
EPS = 1e-6
IN_SPLITS = (RET_WIDTH, RET_WIDTH, RET_WIDTH, RET_WIDTH, S5_WIDTH, NSA_WIDTH,
             KV_WIDTH, KV_WIDTH, KV_WIDTH, KV_WIDTH, KV_WIDTH, KV_WIDTH, 3 * NSA_HEADS)
IN_COLS = sum(IN_SPLITS)

V7X_VMEM_LIMIT_BYTES = 48 * 1024 * 1024
LANE = 128
PACK_ROWS = 16


def _round_up(n, m):
    return -(-n // m) * m


def _largest_divisor(n, cap):
    return max(d for d in range(1, min(n, cap) + 1) if n % d == 0)


def _pad_rows(x, rows):
    extra = rows - x.shape[0]
    return jnp.concatenate([x, jnp.zeros((extra, x.shape[1]), x.dtype)], axis=0) if extra else x


def _gelu_tanh(x):
    return 0.5 * x * (1.0 + jnp.tanh(math.sqrt(2.0 / math.pi) * (x + 0.044715 * (x * x * x))))


def _dot_nt(a, b):
    return lax.dot_general(a, b, (((1,), (1,)), ((), ())), preferred_element_type=F32)


def _dot_tn(a, b):
    return lax.dot_general(a, b, (((0,), (0,)), ((), ())), preferred_element_type=F32)


def _rmsnorm_kernel(x_ref, g_ref, o_ref):
    x = x_ref[...].astype(F32)
    ms = jnp.mean(x * x, axis=-1, keepdims=True)
    o_ref[...] = (x * lax.rsqrt(ms + EPS) * g_ref[...].astype(F32)).astype(o_ref.dtype)


def rmsnorm(x2d, gain, out_dtype):
    m, d = x2d.shape
    tm = min(m, 256)
    assert m % tm == 0
    return pl.pallas_call(
        _rmsnorm_kernel,
        grid=(m // tm,),
        in_specs=[pl.BlockSpec((tm, d), lambda i: (i, 0)),
                  pl.BlockSpec((1, d), lambda i: (0, 0))],
        out_specs=pl.BlockSpec((tm, d), lambda i: (i, 0)),
        out_shape=jax.ShapeDtypeStruct((m, d), out_dtype),
        compiler_params=pltpu.CompilerParams(dimension_semantics=("parallel",),
                                             vmem_limit_bytes=V7X_VMEM_LIMIT_BYTES),
        name="rmsnorm",
    )(x2d, gain.reshape(1, d))


def _mm_kernel(*refs, nk, act, has_res, emit_w):
    refs = list(refs)
    a_ref, w_ref = refs[:2]
    r_ref = refs[2] if has_res else None
    o_ref = refs[2 + has_res]
    wb_ref = refs[3 + has_res] if emit_w else None
    acc_ref = refs[-1]
    k = pl.program_id(2)

    @pl.when(k == 0)
    def _():
        acc_ref[...] = jnp.zeros_like(acc_ref)

    w = w_ref[...].astype(BF16)
    if emit_w:
        wb_ref[...] = w
    acc_ref[...] += jnp.dot(a_ref[...].astype(BF16), w, preferred_element_type=F32)

    @pl.when(k == nk - 1)
    def _():
        acc = acc_ref[...]
        if act == "relu2":
            acc = jnp.square(jnp.maximum(acc, 0.0))
        if act == "glu":
            acc = r_ref[...].astype(F32) * jax.nn.sigmoid(acc)
        elif has_res:
            acc = acc + r_ref[...].astype(F32)
        o_ref[...] = acc.astype(o_ref.dtype)


def matmul(a, w, layer=None, *, n=None, res=None, act=None, out_dtype=F32, emit_bf16=False):
    m, kdim = a.shape
    n = w.shape[-1] if n is None else n
    tm = min(m, 1024)
    tn = 512 if n % 512 == 0 else (256 if n % 256 == 0 else 128)
    tk = min(kdim, 4096)
    assert m % tm == 0 and n % tn == 0 and kdim % tk == 0
    assert not emit_bf16 or m == tm
    nk = kdim // tk
    if layer is None:
        w_spec = pl.BlockSpec((tk, tn), lambda i, j, k: (k, j))
    else:
        w_spec = pl.BlockSpec((None, tk, tn), lambda i, j, k: (layer, k, j))
    in_specs = [pl.BlockSpec((tm, tk), lambda i, j, k: (i, k)), w_spec]
    args = [a, w]
    if res is not None:
        in_specs.append(pl.BlockSpec((tm, tn), lambda i, j, k: (i, j)))
        args.append(res)
    out_specs = pl.BlockSpec((tm, tn), lambda i, j, k: (i, j))
    out_shape = jax.ShapeDtypeStruct((m, n), out_dtype)
    if emit_bf16:
        out_specs = (out_specs, pl.BlockSpec((tk, tn), lambda i, j, k: (k, j)))
        out_shape = (out_shape, jax.ShapeDtypeStruct((kdim, n), BF16))
    return pl.pallas_call(
        functools.partial(_mm_kernel, nk=nk, act=act, has_res=res is not None, emit_w=emit_bf16),
        grid=(m // tm, n // tn, nk),
        in_specs=in_specs,
        out_specs=out_specs,
        out_shape=out_shape,
        scratch_shapes=[pltpu.VMEM((tm, tn), F32)],
        compiler_params=pltpu.CompilerParams(
            dimension_semantics=("parallel", "parallel", "arbitrary"),
            vmem_limit_bytes=V7X_VMEM_LIMIT_BYTES),
        name="matmul",
    )(*args)


def _compress_rows(x_ref, n_full, pe_ref, w1_ref, w2_ref):
    pieces = [x_ref[0, pl.ds(s, n_full, stride=CMP_STRIDE), :] for s in range(CMP_STRIDE)]
    x = jnp.concatenate(pieces, axis=1).astype(BF16)
    half = CMP_STRIDE * HEAD_DIM
    z0 = jnp.dot(x, w1_ref[:half, :], preferred_element_type=F32)
    z1 = jnp.dot(x, w1_ref[half:, :], preferred_element_type=F32)
    z1 = pltpu.roll(z1, n_full - 1, 0)
    pe = jnp.broadcast_to(pe_ref[...], (8, CMP_BLOCK * HEAD_DIM)).astype(BF16)
    h0 = jnp.dot(pe, w1_ref[...], preferred_element_type=F32)[0:1, :]
    h = _gelu_tanh(z0 + z1 + h0)
    return jnp.dot(h.astype(BF16), w2_ref[...], preferred_element_type=F32)


def _compress_prompt_kernel(xk_ref, xv_ref, pek_ref, w1k_ref, w2k_ref, pev_ref, w1v_ref, w2v_ref,
                            kc_ref, vc_ref, *, n_full):
    for x_ref, pe_ref, w1_ref, w2_ref, o_ref in ((xk_ref, pek_ref, w1k_ref, w2k_ref, kc_ref),
                                                  (xv_ref, pev_ref, w1v_ref, w2v_ref, vc_ref)):
        out = _compress_rows(x_ref, n_full, pe_ref, w1_ref, w2_ref).astype(o_ref.dtype)
        n_pad = o_ref.shape[2]
        o_ref[0, 0, :n_full, :] = out
        if n_pad > n_full:
            o_ref[0, 0, n_full:, :] = jnp.zeros((n_pad - n_full, HEAD_DIM), o_ref.dtype)


def compress_prompt(proj3, col_k, col_v, pe_k, w1_k, w2_k, pe_v, w1_v, w2_v):
    B, T, _ = proj3.shape
    n_full = T // CMP_STRIDE
    n_pad = _round_up(n_full, LANE)
    flat = CMP_BLOCK * HEAD_DIM
    wspec = [pl.BlockSpec((1, flat), lambda b, g: (0, 0)),
             pl.BlockSpec((flat, HEAD_DIM), lambda b, g: (0, 0)),
             pl.BlockSpec((HEAD_DIM, HEAD_DIM), lambda b, g: (0, 0))]
    out_sds = jax.ShapeDtypeStruct((B, NSA_KV_HEADS, n_pad, HEAD_DIM), BF16)
    ospec = pl.BlockSpec((1, 1, n_pad, HEAD_DIM), lambda b, g: (b, g, 0, 0))
    return pl.pallas_call(
        functools.partial(_compress_prompt_kernel, n_full=n_full),
        grid=(B, NSA_KV_HEADS),
        in_specs=[pl.BlockSpec((1, T, HEAD_DIM), lambda b, g: (b, 0, col_k // HEAD_DIM + g)),
                  pl.BlockSpec((1, T, HEAD_DIM), lambda b, g: (b, 0, col_v // HEAD_DIM + g))] + wspec + wspec,
        out_specs=(ospec, ospec),
        out_shape=(out_sds, out_sds),
        compiler_params=pltpu.CompilerParams(dimension_semantics=("parallel", "parallel"),
                                             vmem_limit_bytes=V7X_VMEM_LIMIT_BYTES),
        name="nsa_compress_prompt",
    )(proj3, proj3,
      pe_k.reshape(1, flat), w1_k.reshape(flat, HEAD_DIM).astype(BF16), w2_k.astype(BF16),
      pe_v.reshape(1, flat), w1_v.reshape(flat, HEAD_DIM).astype(BF16), w2_v.astype(BF16))


def _bucket_table():
    d = np.arange(MAX_DISTANCE)
    max_exact = N_BUCKETS // 2
    large = max_exact + (np.log(np.maximum(d, 1).astype(np.float32) / np.float32(max_exact))
                         / np.float32(math.log(MAX_DISTANCE / max_exact))
                         * np.float32(N_BUCKETS - max_exact)).astype(np.int32)
    return np.where(d < max_exact, d, np.minimum(large, N_BUCKETS - 1)).astype(np.int32)


def _bias_of_distance(rel_bias, dist):
    bt = _bucket_table()
    buckets = bt[np.clip(dist, 0, MAX_DISTANCE - 1)]
    b = jnp.moveaxis(rel_bias.astype(F32)[buckets], -1, 0)
    return jnp.where(jnp.asarray(dist >= 0)[None], b, NEG)


def _bias_by_distance(rel_bias):
    return rel_bias.astype(F32)[_bucket_table()].T


def _toeplitz(v, n):
    h = v.shape[0]
    w = jnp.pad(v, ((0, 0), (0, 1)))
    m = jnp.tile(w, (1, n))[:, :n * (2 * n - 1)].reshape(h, n, 2 * n - 1)
    return m[:, :, n - 1:]


ATT_TILE = 128
N_BIAS_TILES = 4
SLC_CLASS_TILES = 2
MXU_TILES = 2


def _rows_softmax_pv(s_tiles, v_rows):
    m = s_tiles[0]
    for s in s_tiles[1:]:
        m = jnp.maximum(m, s)
    m = jnp.max(m, axis=-1, keepdims=True)
    l = None
    o = None
    for i in range(0, len(s_tiles), MXU_TILES):
        ps = [jnp.exp(s - m) for s in s_tiles[i:i + MXU_TILES]]
        for p in ps:
            l = p if l is None else l + p
        p_cat = ps[0] if len(ps) == 1 else jnp.concatenate(ps, axis=1)
        pv = jnp.dot(p_cat.astype(BF16), v_rows(i, len(ps)), preferred_element_type=F32)
        o = pv if o is None else o + pv
    return o / jnp.sum(l, axis=-1, keepdims=True)


def _score_tiles(q, k_rows, n_tiles):
    tiles = []
    for i in range(0, n_tiles, MXU_TILES):
        n = min(MXU_TILES, n_tiles - i)
        s = _dot_nt(q, k_rows(i, n))
        tiles += [s[:, j * LANE:(j + 1) * LANE] for j in range(n)]
    return tiles


def _nsa_prompt_kernel(q_ref, kc_ref, vc_ref, ks_ref, vs_ref, kw_ref, vw_ref, gate_ref,
                       bcmp_ref, btile_ref, ovl_ref, pick_ref, cvec_ref, kaug_s_ref, kaug_w_ref, o_ref,
                       ksb, vsb, kwb, vwb, s_ref, *, n_sel, n_top, T):
    tq = ATT_TILE
    J = NSA_HPG
    qi = pl.program_id(2)
    n_sel_pad = pick_ref.shape[0]
    nq = T // tq

    @pl.when(qi == 0)
    def _():
        ksb[:, :HEAD_DIM] = ks_ref[0].astype(BF16)
        ksb[:, HEAD_DIM:] = kaug_s_ref[...]
        vsb[...] = vs_ref[0].astype(BF16)
        kwb[:WINDOW, :HEAD_DIM] = jnp.zeros((WINDOW, HEAD_DIM), BF16)
        kwb[WINDOW:, :HEAD_DIM] = kw_ref[0].astype(BF16)
        kwb[:, HEAD_DIM:] = kaug_w_ref[...]
        vwb[:WINDOW, :] = jnp.zeros((WINDOW, HEAD_DIM), BF16)
        vwb[WINDOW:, :] = vw_ref[0].astype(BF16)

    def stack(f):
        return jnp.concatenate([f(j) for j in range(J)], axis=0)

    def put(vals, first):
        for j in range(J):
            cols = slice(j * HEAD_DIM, (j + 1) * HEAD_DIM)
            v = vals[j * tq:(j + 1) * tq]
            o_ref[0, :, cols] = v if first else o_ref[0, :, cols] + v

    gates = jax.nn.sigmoid(gate_ref[0, 0])
    gate = lambda c: stack(lambda j: gates[:, 3 * j + c:3 * j + c + 1])
    q_all = stack(lambda j: q_ref[0, :, j * HEAD_DIM:(j + 1) * HEAD_DIM] * HEAD_DIM ** -0.5).astype(BF16)

    row = lax.broadcasted_iota(jnp.int32, (tq, LANE), 0)
    lane = lax.broadcasted_iota(jnp.int32, (tq, LANE), 1)
    valid1 = qi * tq + row >= lane * CMP_STRIDE + (CMP_BLOCK - 1)
    valid = stack(lambda j: valid1)
    s = jnp.where(valid, _dot_nt(q_all, kc_ref[0, 0]) + stack(lambda j: bcmp_ref[j]), NEG)
    e = jnp.where(valid, jnp.exp(s - jnp.max(s, axis=-1, keepdims=True)), 0.0)
    den = jnp.sum(e, axis=-1, keepdims=True)
    p = e / jnp.where(den > 0.0, den, 1.0)
    put(gate(0) * jnp.dot(p.astype(BF16), vc_ref[0, 0], preferred_element_type=F32), True)
    psum = p[0:tq]
    for j in range(1, J):
        psum = psum + p[j * tq:(j + 1) * tq]

    bt = lambda i: stack(lambda j: btile_ref[j, i])

    pad_flag = jnp.broadcast_to(-cvec_ref[...], (tq, LANE)).astype(BF16)
    q_win = jnp.concatenate([q_all, stack(lambda j: pad_flag)], axis=1)
    n_w = WINDOW // tq + 1
    win_rows = lambda i, n: pl.ds(pl.multiple_of((qi + i) * tq, tq), n * tq)
    s_tiles = _score_tiles(q_win, lambda i, n: kwb[win_rows(i, n), :], n_w)
    s_tiles[0] = s_tiles[0] + bt(3)
    s_tiles[n_w - 2] = s_tiles[n_w - 2] + bt(1)
    s_tiles[n_w - 1] = s_tiles[n_w - 1] + bt(0)
    put(gate(2) * _rows_softmax_pv(s_tiles, lambda i, n: vwb[win_rows(i, n), :]), False)

    hi = psum.astype(BF16)
    r1 = psum - hi.astype(F32)
    mid = r1.astype(BF16)
    lo = (r1 - mid.astype(F32)).astype(BF16)
    ovl = ovl_ref[...]
    p_sel = _dot_nt(ovl, hi) + _dot_nt(ovl, mid) + _dot_nt(ovl, lo)
    blk = lax.broadcasted_iota(jnp.int32, (n_sel_pad, tq), 0)
    cur = (qi * tq + lax.broadcasted_iota(jnp.int32, (n_sel_pad, tq), 1)) // SEL_BLOCK
    forced = (blk == 0) | (blk == cur) | (blk == cur - 1)
    score = jnp.where(blk <= cur, p_sel + jnp.where(forced, FORCE_SCORE, 0.0), NEG)
    rank = jnp.zeros((n_sel_pad, tq), F32)
    for jb in range(n_sel):
        other = score[jb:jb + 1, :]
        beats = (other > score) | ((other == score) & (blk > jb))
        rank = rank + jnp.where(beats, 1.0, 0.0)
    sel = jnp.where((rank < n_top) & (blk <= cur) & (blk < n_sel), 1.0, 0.0).astype(BF16)
    aug = (_dot_tn(sel, pick_ref[...]) - cvec_ref[...]).astype(BF16)
    q_aug = jnp.concatenate([q_all, stack(lambda j: aug)], axis=1)

    for cls in range(-(-nq // SLC_CLASS_TILES)):
        n_t = min((cls + 1) * SLC_CLASS_TILES, nq)

        @pl.when(qi // SLC_CLASS_TILES == cls)
        def _(n_t=n_t):
            for kt, s in enumerate(_score_tiles(q_aug, lambda i, n: ksb[i * tq:(i + n) * tq, :], n_t)):
                s_ref[kt] = s
            s_ref[qi] = s_ref[qi] + bt(0)

            @pl.when(qi > 0)
            def _():
                s_ref[qi - 1] = s_ref[qi - 1] + bt(1)

            o = _rows_softmax_pv([s_ref[kt] for kt in range(n_t)], lambda i, n: vsb[i * tq:(i + n) * tq, :])
            put(gate(1) * o, False)


def nsa_prompt(proj3, cols, ng, kcmp, vcmp, rel_bias):
    B, T, _ = proj3.shape
    tq = ATT_TILE
    assert T % tq == 0 and T % SEL_BLOCK == 0 and WINDOW % tq == 0
    G, J = NSA_KV_HEADS, NSA_HPG
    nq = T // tq
    n_sel = T // SEL_BLOCK
    n_top = min(SEL_TOP, n_sel)
    n_sel_pad = _round_up(n_sel, 16)
    assert n_sel_pad < LANE
    n_pad = kcmp.shape[2]
    assert n_pad == LANE, "one lane tile of compressed blocks"

    bd = _bias_by_distance(rel_bias)
    n_a = T // CMP_STRIDE
    assert n_a == n_pad
    d_cmp = (CMP_STRIDE * ((n_a - 1) - np.arange(2 * n_a - 1))[None, :]
             + np.arange(CMP_STRIDE)[:, None] - (CMP_BLOCK - 1))
    gen_cmp = jnp.where(jnp.asarray(d_cmp >= 0)[None], bd[:, np.clip(d_cmp, 0, MAX_DISTANCE - 1)], 0.0)
    bcmp = _toeplitz(gen_cmp.reshape(NSA_HEADS * CMP_STRIDE, 2 * n_a - 1), n_a)
    bcmp = bcmp.reshape(NSA_HEADS, CMP_STRIDE, n_a, n_a).transpose(0, 2, 1, 3).reshape(NSA_HEADS, T, n_pad)
    d_diag = (tq - 1) - np.arange(2 * tq - 1)
    rel = bd - bd[:, -1:]
    gen = lambda d: jnp.where(jnp.asarray(d >= 0)[None], rel[:, np.clip(d, 0, MAX_DISTANCE - 1)], NEG)
    edge = jnp.broadcast_to(jnp.where(jnp.asarray(d_diag <= 0), 0.0, NEG)[None], (NSA_HEADS, 2 * tq - 1))
    btile = jnp.stack([_toeplitz(gen(d_diag), tq), _toeplitz(gen(d_diag + tq), tq),
                       jnp.zeros((NSA_HEADS, tq, tq), F32), _toeplitz(edge, tq)], axis=1)
    cmp_start = np.arange(n_pad) * CMP_STRIDE
    sel_start = np.arange(n_sel_pad) * SEL_BLOCK
    ovl = ((cmp_start[None, :] < sel_start[:, None] + SEL_BLOCK)
           & (cmp_start[None, :] + CMP_BLOCK > sel_start[:, None])
           & (np.arange(n_pad)[None, :] < T // CMP_STRIDE - 1))
    ovl = jnp.asarray(ovl, BF16)
    pick = jnp.asarray(np.arange(LANE)[None, :] == np.arange(n_sel_pad)[:, None], BF16)
    lane_i = np.arange(LANE)
    cvec = jnp.asarray(((lane_i < n_sel) | (lane_i == n_sel_pad))[None, :], F32)
    big = -NEG
    kaug_s = jnp.asarray(np.where(np.arange(T)[:, None] // SEL_BLOCK == lane_i[None, :], big, 0.0), BF16)
    kaug_w = jnp.asarray(np.where((np.arange(T + WINDOW)[:, None] < WINDOW) & (lane_i[None, :] == n_sel_pad),
                                  big, 0.0), BF16)
    gates = ng.reshape(B, T, G, 3 * J).transpose(0, 2, 1, 3)

    kv_spec = lambda name: pl.BlockSpec((1, T, HEAD_DIM),
                                        lambda b, g, i, o=cols[name] // HEAD_DIM: (b, 0, o + g))
    cmp_spec = pl.BlockSpec((1, 1, n_pad, HEAD_DIM), lambda b, g, i: (b, g, 0, 0))
    const2 = lambda shape: pl.BlockSpec(shape, lambda b, g, i: (0, 0))
    return pl.pallas_call(
        functools.partial(_nsa_prompt_kernel, n_sel=n_sel, n_top=n_top, T=T),
        grid=(B, G, nq),
        in_specs=[pl.BlockSpec((1, tq, J * HEAD_DIM),
                               lambda b, g, i, o=cols['nq'] // (J * HEAD_DIM): (b, i, o + g)),
                  cmp_spec, cmp_spec,
                  kv_spec('ks'), kv_spec('vs'), kv_spec('kw'), kv_spec('vw'),
                  pl.BlockSpec((1, 1, tq, 3 * J), lambda b, g, i: (b, g, i, 0)),
                  pl.BlockSpec((J, tq, n_pad), lambda b, g, i: (g, i, 0)),
                  pl.BlockSpec((J, N_BIAS_TILES, tq, tq), lambda b, g, i: (g, 0, 0, 0)),
                  const2((n_sel_pad, n_pad)), const2((n_sel_pad, LANE)), const2((1, LANE)),
                  const2((T, LANE)), const2((T + WINDOW, LANE))],
        out_specs=pl.BlockSpec((1, tq, J * HEAD_DIM), lambda b, g, i: (b, i, g)),
        out_shape=jax.ShapeDtypeStruct((B, T, NSA_WIDTH), F32),
        scratch_shapes=[pltpu.VMEM((T, 2 * HEAD_DIM), BF16), pltpu.VMEM((T, HEAD_DIM), BF16),
                        pltpu.VMEM((T + WINDOW, 2 * HEAD_DIM), BF16), pltpu.VMEM((T + WINDOW, HEAD_DIM), BF16),
                        pltpu.VMEM((nq, J * tq, tq), F32)],
        compiler_params=pltpu.CompilerParams(
            dimension_semantics=("parallel", "parallel", "arbitrary"),
            vmem_limit_bytes=V7X_VMEM_LIMIT_BYTES),
        name="nsa_prompt",
    )(proj3, kcmp, vcmp, proj3, proj3, proj3, proj3, gates, bcmp, btile, ovl, pick, cvec, kaug_s, kaug_w)


PAGE_ROWS = PAGE_SIZE * NSA_KV_HEADS
STRIDES_PER_PAGE = PAGE_SIZE // CMP_STRIDE
PAGES_PER_STEP = 8


def _cache_rows(cache):
    return cache.reshape(-1, HEAD_DIM)


def _page_specs(layer):
    return [pl.BlockSpec((2 * PAGE_ROWS, HEAD_DIM),
                         lambda b, p, pt, k=k: (pt[b, p * PAGES_PER_STEP + k] * DEPTH + layer, 0))
            for k in range(PAGES_PER_STEP)]


def _cmp_partial_kernel(pt_ref, *refs):
    page_refs = refs[:PAGES_PER_STEP]
    w1k_ref, w1v_ref, z_ref = refs[PAGES_PER_STEP:]
    G = NSA_KV_HEADS
    half = CMP_STRIDE * HEAD_DIM
    n_rows = PAGES_PER_STEP * STRIDES_PER_PAGE
    for kv, w1_ref in ((0, w1k_ref), (1, w1v_ref)):
        slabs = [jnp.transpose(r[kv * PAGE_ROWS:(kv + 1) * PAGE_ROWS, :].reshape(
            STRIDES_PER_PAGE, CMP_STRIDE * G, HEAD_DIM), (1, 0, 2)) for r in page_refs]
        xs = []
        for g in range(G):
            for slab in slabs:
                xs.append(jnp.concatenate([slab[s * G + g] for s in range(CMP_STRIDE)], axis=1))
        x = jnp.concatenate(xs, axis=0).astype(BF16)
        for j in range(CMP_BLOCK // CMP_STRIDE):
            z = jnp.dot(x, w1_ref[j * half:(j + 1) * half, :], preferred_element_type=F32)
            for g in range(G):
                z_ref[0, kv, j, g] = z[g * n_rows:(g + 1) * n_rows]


def cmp_partial(cache_rows, page_table, layer, w1_k, w1_v):
    B, n_pages = page_table.shape
    flat = CMP_BLOCK * HEAD_DIM
    n_str = n_pages * STRIDES_PER_PAGE
    assert n_pages % PAGES_PER_STEP == 0
    wspec = pl.BlockSpec((flat, HEAD_DIM), lambda b, p, pt: (0, 0))
    return pl.pallas_call(
        _cmp_partial_kernel,
        grid_spec=pltpu.PrefetchScalarGridSpec(
            num_scalar_prefetch=1,
            grid=(B, n_pages // PAGES_PER_STEP),
            in_specs=_page_specs(layer) + [wspec, wspec],
            out_specs=pl.BlockSpec((1, 2, 2, NSA_KV_HEADS, PAGES_PER_STEP * STRIDES_PER_PAGE, HEAD_DIM),
                                   lambda b, p, pt: (b, 0, 0, 0, p, 0))),
        out_shape=jax.ShapeDtypeStruct((B, 2, 2, NSA_KV_HEADS, n_str, HEAD_DIM), F32),
        compiler_params=pltpu.CompilerParams(dimension_semantics=("parallel", "arbitrary"),
                                             vmem_limit_bytes=V7X_VMEM_LIMIT_BYTES),
        name="nsa_cmp_partial",
    )(page_table, *([cache_rows] * PAGES_PER_STEP),
      w1_k.reshape(flat, HEAD_DIM).astype(BF16), w1_v.reshape(flat, HEAD_DIM).astype(BF16))


def _cmp_attn_sample_kernel(z_ref, pek_ref, w1k_ref, w2k_ref, pev_ref, w1v_ref, w2v_ref, q_ref, bias_ref,
                            ovl_ref, o_ref, psel_ref, *, q0):
    T = q_ref.shape[1]
    J = NSA_HPG
    n_str = z_ref.shape[4]

    def finish(kv, pe_ref, w1_ref, w2_ref):
        z1 = pltpu.roll(z_ref[0, kv, 1, 0], n_str - 1, 0)
        pe = jnp.broadcast_to(pe_ref[...], (PACK_ROWS, CMP_BLOCK * HEAD_DIM)).astype(BF16)
        h0 = jnp.dot(pe, w1_ref[...], preferred_element_type=F32)[0:1, :]
        h = _gelu_tanh(z_ref[0, kv, 0, 0] + z1 + h0)
        return jnp.dot(h.astype(BF16), w2_ref[...], preferred_element_type=F32).astype(BF16)

    kc = finish(0, pek_ref, w1k_ref, w2k_ref)
    vc = finish(1, pev_ref, w1v_ref, w2v_ref)
    stack = lambda f: jnp.concatenate([f(j) for j in range(J)], axis=0)
    q_all = stack(lambda j: q_ref[0, :, j * HEAD_DIM:(j + 1) * HEAD_DIM] * HEAD_DIM ** -0.5).astype(BF16)
    row = lax.broadcasted_iota(jnp.int32, (T, n_str), 0)
    lane = lax.broadcasted_iota(jnp.int32, (T, n_str), 1)
    valid1 = (q0 + row >= lane * CMP_STRIDE + (CMP_BLOCK - 1)) & (lane < n_str - 1)
    valid = stack(lambda j: valid1)
    s = jnp.where(valid, _dot_nt(q_all, kc) + stack(lambda j: bias_ref[j]), NEG)
    e = jnp.where(valid, jnp.exp(s - jnp.max(s, axis=-1, keepdims=True)), 0.0)
    den = jnp.sum(e, axis=-1, keepdims=True)
    p = e / jnp.where(den > 0.0, den, 1.0)
    o_ref[0, 0] = jnp.dot(p.astype(BF16), vc, preferred_element_type=F32)
    psum = p[0:T]
    for j in range(1, J):
        psum = psum + p[j * T:(j + 1) * T]
    psum = _pad_rows(psum, PACK_ROWS)
    hi = psum.astype(BF16)
    r1 = psum - hi.astype(F32)
    mid = r1.astype(BF16)
    lo = (r1 - mid.astype(F32)).astype(BF16)
    ovl = ovl_ref[...]
    p_sel = (jnp.dot(hi, ovl, preferred_element_type=F32) + jnp.dot(mid, ovl, preferred_element_type=F32)
             + jnp.dot(lo, ovl, preferred_element_type=F32))
    psel_ref[0, 0] = p_sel[:T]


def _rank_select_kernel(psel_ref, tpos_ref, out_ref, score_ref, rank_ref, *, n_sel, n_top):
    shape = psel_ref.shape
    blk = lax.broadcasted_iota(jnp.int32, shape, 0)
    cur = jnp.broadcast_to(tpos_ref[...], shape) // SEL_BLOCK
    forced = (blk == 0) | (blk == cur) | (blk == cur - 1)
    ok = (blk <= cur) & (blk < n_sel)
    score = jnp.where(ok, psel_ref[...] + jnp.where(forced, FORCE_SCORE, 0.0), NEG)
    score_ref[...] = score
    rank_ref[...] = jnp.zeros(shape, F32)

    def body(jb, c):
        other = jnp.broadcast_to(score_ref[pl.ds(jb, 1), :], shape)
        beats = (other > score) | ((other == score) & (blk > jb))
        rank_ref[...] = rank_ref[...] + jnp.where(beats, 1.0, 0.0)
        return c

    lax.fori_loop(0, n_sel, body, 0)
    out_ref[...] = jnp.where((rank_ref[...] < n_top) & ok, 0.0, NEG)


def _slc_win_sample_kernel(pt_ref, *refs, win_buf):
    page_refs = refs[:PAGES_PER_STEP]
    (qT_ref, seladd_ref, blast_ref, selnew_ref, ksn_ref, vsn_ref, kwn_ref, vwn_ref, bnew_ref, cw_ref, bwin_ref,
     ocmp_ref, gate_ref, o_ref, m_ref, l_ref, acc_ref) = refs[PAGES_PER_STEP:]
    G = NSA_KV_HEADS
    p = pl.program_id(1)
    last = pl.num_programs(1) - 1
    lane_group = lax.broadcasted_iota(jnp.int32, (1, LANE), 1) // (LANE // G)

    @pl.when(p == 0)
    def _():
        m_ref[...] = jnp.full(m_ref.shape, NEG, F32)
        l_ref[...] = jnp.zeros(l_ref.shape, F32)
        acc_ref[...] = jnp.zeros(acc_ref.shape, F32)

    def scores(k_of_g):
        s = None
        for g in range(G):
            sg = jnp.dot(k_of_g(g).astype(BF16), qT_ref[0, g], preferred_element_type=F32)
            s = sg if s is None else s + sg
        return s

    def weighted_values(pT, v_of_g):
        o = None
        for g in range(G):
            pg = jnp.where(lane_group == g, pT, 0.0).astype(BF16)
            og = _dot_tn(v_of_g(g).astype(BF16), pg)
            o = og if o is None else o + og
        return o

    def accumulate(segments):
        m_old = m_ref[0:1, :]
        m_new = m_old
        for sT, _ in segments:
            m_new = jnp.maximum(m_new, jnp.max(sT, axis=0, keepdims=True))
        alpha = jnp.exp(m_old - m_new)
        l_new = alpha * l_ref[0:1, :]
        acc = alpha * acc_ref[...]
        for sT, v_of_g in segments:
            pT = jnp.exp(sT - m_new)
            l_new = l_new + jnp.sum(pT, axis=0, keepdims=True)
            acc = acc + weighted_values(pT, v_of_g)
        l_ref[...] = jnp.broadcast_to(l_new, l_ref.shape)
        acc_ref[...] = acc
        m_ref[...] = jnp.broadcast_to(m_new, m_ref.shape)

    key = lax.broadcasted_iota(jnp.int32, (PAGE_SIZE, LANE), 0)
    blocks_per_page = PAGE_SIZE // SEL_BLOCK
    is_last = (p == last).astype(F32)
    segments = []
    for k, page_ref in enumerate(page_refs):
        k_page = lambda g, r=page_ref: r[pl.ds(g, PAGE_SIZE, stride=G), :]
        v_page = lambda g, r=page_ref: r[pl.ds(PAGE_ROWS + g, PAGE_SIZE, stride=G), :]
        mask = seladd_ref[0, k, blocks_per_page - 1:blocks_per_page, :]
        for i in range(blocks_per_page - 2, -1, -1):
            mask = jnp.where(key < (i + 1) * SEL_BLOCK, seladd_ref[0, k, i:i + 1, :], mask)
        sT = scores(k_page) + mask
        if k == PAGES_PER_STEP - 1:
            sT = sT + blast_ref[...] * is_last
        segments.append((sT, v_page))
    accumulate(segments)

    @pl.when(p == last)
    def _():
        new = lambda ref: (lambda g: _pad_rows(ref[0, :, g * HEAD_DIM:(g + 1) * HEAD_DIM], PACK_ROWS))
        accumulate([(scores(new(ksn_ref)) + bnew_ref[...] + selnew_ref[0], new(vsn_ref))])
        o_slc = (acc_ref[...] / l_ref[0:1, :]).T

        k_win = lambda g: cw_ref[pl.ds(g, win_buf, stride=G), :]
        v_win = lambda g: cw_ref[pl.ds(win_buf * G + g, win_buf, stride=G), :]
        s_w = scores(k_win) + bwin_ref[...]
        s_n = scores(new(kwn_ref)) + bnew_ref[...]
        m = jnp.maximum(jnp.max(s_w, axis=0, keepdims=True), jnp.max(s_n, axis=0, keepdims=True))
        p_w = jnp.exp(s_w - m)
        p_n = jnp.exp(s_n - m)
        den = jnp.sum(p_w, axis=0, keepdims=True) + jnp.sum(p_n, axis=0, keepdims=True)
        o_win = ((weighted_values(p_w, v_win) + weighted_values(p_n, new(vwn_ref))) / den).T

        gates = jax.nn.sigmoid(gate_ref[0])
        o_ref[0] = gates[:, 0:1] * ocmp_ref[0] + gates[:, 1:2] * o_slc + gates[:, 2:3] * o_win


def nsa_sample(proj3, cols, ng, cache_cmp, cache_slc, cache_win, page_table, layer, rel_bias,
               pe_k, w1_k, w2_k, pe_v, w1_v, w2_v):
    B, T, _ = proj3.shape
    G, J, H = NSA_KV_HEADS, NSA_HPG, NSA_HEADS
    n_pages = page_table.shape[1]
    past = n_pages * PAGE_SIZE
    q0 = past
    win_buf = cache_win.shape[3]
    L = G * J * T
    assert L == LANE and T <= PACK_ROWS and T < CMP_STRIDE and past % SEL_BLOCK == 0 and T <= SEL_BLOCK
    assert win_buf == min(WINDOW, past)
    n_str = past // CMP_STRIDE
    n_sel = past // SEL_BLOCK + 1
    n_top = min(SEL_TOP, n_sel)
    n_sel_rows = _round_up(n_sel, 8)
    n_sel_lanes = _round_up(n_sel, LANE)
    flat = CMP_BLOCK * HEAD_DIM
    b_far = rel_bias.astype(F32)[_bucket_table()[MAX_DISTANCE - 1]]

    def lane_bias(dist, ok):
        b = _bias_of_distance(rel_bias, np.maximum(dist, 0)) - b_far[:, None, None]
        b = jnp.where(jnp.asarray(ok)[None], b, NEG)
        return jnp.moveaxis(b.reshape(G, J, dist.shape[0], T), 2, 0).reshape(dist.shape[0], L)

    ti = np.arange(T)[None, :]
    z = cmp_partial(_cache_rows(cache_cmp), page_table, layer, w1_k, w1_v)
    n = np.arange(n_str)[None, :]
    dist_c = q0 + np.arange(T)[:, None] - (n * CMP_STRIDE + CMP_BLOCK - 1)
    near = dist_c.min(axis=0) < MAX_DISTANCE
    n_far = int(np.argmax(near)) if near.any() else n_str
    bcmp = jnp.concatenate([jnp.broadcast_to(b_far[:, None, None], (H, T, n_far)),
                            _bias_of_distance(rel_bias, np.maximum(dist_c[:, n_far:], 0))], axis=2)
    cmp_start = np.arange(n_str) * CMP_STRIDE
    sel_start = np.arange(n_sel_lanes) * SEL_BLOCK
    ovl = ((cmp_start[:, None] < sel_start[None, :] + SEL_BLOCK) & (cmp_start[:, None] + CMP_BLOCK > sel_start[None, :])
           & (np.arange(n_str)[:, None] < n_str - 1) & (np.arange(n_sel_lanes)[None, :] < n_sel))
    wspec = [pl.BlockSpec((1, flat), lambda b, g: (0, 0)),
             pl.BlockSpec((flat, HEAD_DIM), lambda b, g: (0, 0)),
             pl.BlockSpec((HEAD_DIM, HEAD_DIM), lambda b, g: (0, 0))]
    o_cmp, p_sel = pl.pallas_call(
        functools.partial(_cmp_attn_sample_kernel, q0=q0),
        grid=(B, G),
        in_specs=[pl.BlockSpec((1, 2, 2, 1, n_str, HEAD_DIM), lambda b, g: (b, 0, 0, g, 0, 0))] + wspec + wspec + [
            pl.BlockSpec((1, T, J * HEAD_DIM), lambda b, g, o=cols['nq'] // (J * HEAD_DIM): (b, 0, o + g)),
            pl.BlockSpec((J, T, n_str), lambda b, g: (g, 0, 0)),
            pl.BlockSpec((n_str, n_sel_lanes), lambda b, g: (0, 0))],
        out_specs=(pl.BlockSpec((1, 1, J * T, HEAD_DIM), lambda b, g: (b, g, 0, 0)),
                   pl.BlockSpec((1, 1, T, n_sel_lanes), lambda b, g: (b, g, 0, 0))),
        out_shape=(jax.ShapeDtypeStruct((B, G, J * T, HEAD_DIM), F32),
                   jax.ShapeDtypeStruct((B, G, T, n_sel_lanes), F32)),
        compiler_params=pltpu.CompilerParams(dimension_semantics=("parallel", "parallel"),
                                             vmem_limit_bytes=V7X_VMEM_LIMIT_BYTES),
        name="nsa_cmp_attn_sample",
    )(z, pe_k.reshape(1, flat), w1_k.reshape(flat, HEAD_DIM).astype(BF16), w2_k.astype(BF16),
      pe_v.reshape(1, flat), w1_v.reshape(flat, HEAD_DIM).astype(BF16), w2_v.astype(BF16),
      proj3, bcmp, jnp.asarray(ovl, BF16))

    n_bgt = B * G * T
    psel_t = p_sel.reshape(n_bgt, n_sel_lanes)[:, :n_sel_rows].T
    tpos = jnp.asarray(np.tile(q0 + np.arange(T), B * G)[None, :], jnp.int32)
    seladd = pl.pallas_call(
        functools.partial(_rank_select_kernel, n_sel=n_sel, n_top=n_top),
        out_shape=jax.ShapeDtypeStruct((n_sel_rows, n_bgt), F32),
        scratch_shapes=[pltpu.VMEM((n_sel_rows, n_bgt), F32), pltpu.VMEM((n_sel_rows, n_bgt), F32)],
        name="nsa_rank_select",
    )(psel_t, tpos)
    seladd = seladd.T.reshape(B, G, 1, T, n_sel_rows)
    seladd = jnp.broadcast_to(seladd, (B, G, J, T, n_sel_rows)).reshape(B, L, n_sel_rows)
    bpp = PAGE_SIZE // SEL_BLOCK
    sel_past = seladd[:, :, :n_sel - 1].reshape(B, L, n_pages, bpp).transpose(0, 2, 3, 1)
    sel_new = seladd[:, :, n_sel - 1].reshape(B, 1, L)

    q = proj3[:, :, cols['nq']:cols['nq'] + NSA_WIDTH].reshape(B, T, G, J, HEAD_DIM) * HEAD_DIM ** -0.5
    q_t = q.transpose(0, 2, 4, 3, 1).reshape(B, G, HEAD_DIM, J * T)
    place = jnp.asarray(np.arange(G)[:, None, None] == (np.arange(L) // (J * T))[None, None, :])
    q_pad = jnp.where(place[None], jnp.tile(q_t, (1, 1, 1, G)), 0.0).astype(BF16)
    ki = np.arange(PAGE_SIZE)[:, None]
    b_last = lane_bias(PAGE_SIZE + ti - ki, np.ones((PAGE_SIZE, T), bool))
    kn = np.arange(PACK_ROWS)[:, None]
    b_new = lane_bias(ti - kn, (ti - kn >= 0) & (kn < T))
    kw = np.arange(win_buf)[:, None]
    b_win = lane_bias(win_buf + ti - kw, win_buf + ti - kw <= WINDOW)
    gates = ng.reshape(B, T, G, J, 3).transpose(0, 2, 3, 1, 4).reshape(B, L, 3)
    win_rows = 2 * win_buf * G
    new_spec = lambda name: pl.BlockSpec((1, T, KV_WIDTH), lambda b, p, pt, o=cols[name] // KV_WIDTH: (b, 0, o))
    const = lambda shape: pl.BlockSpec(shape, lambda b, p, pt: (0, 0))
    per_b = lambda shape: pl.BlockSpec((1,) + shape, lambda b, p, pt: (b,) + (0,) * len(shape))
    out = pl.pallas_call(
        functools.partial(_slc_win_sample_kernel, win_buf=win_buf),
        grid_spec=pltpu.PrefetchScalarGridSpec(
            num_scalar_prefetch=1,
            grid=(B, n_pages // PAGES_PER_STEP),
            in_specs=_page_specs(layer) + [
                per_b((G, HEAD_DIM, L)),
                pl.BlockSpec((1, PAGES_PER_STEP, bpp, L), lambda b, p, pt: (b, p, 0, 0)),
                const((PAGE_SIZE, L)), per_b((1, L)),
                new_spec('ks'), new_spec('vs'), new_spec('kw'), new_spec('vw'),
                const((PACK_ROWS, L)),
                pl.BlockSpec((win_rows, HEAD_DIM), lambda b, p, pt: (layer * B + b, 0)),
                const((win_buf, L)), per_b((L, HEAD_DIM)), per_b((L, 3))],
            out_specs=per_b((L, HEAD_DIM)),
            scratch_shapes=[pltpu.VMEM((8, L), F32), pltpu.VMEM((8, L), F32), pltpu.VMEM((HEAD_DIM, L), F32)]),
        out_shape=jax.ShapeDtypeStruct((B, L, HEAD_DIM), F32),
        compiler_params=pltpu.CompilerParams(dimension_semantics=("parallel", "arbitrary"),
                                             vmem_limit_bytes=V7X_VMEM_LIMIT_BYTES),
        name="nsa_slc_win_sample",
    )(page_table, *([_cache_rows(cache_slc)] * PAGES_PER_STEP), q_pad, sel_past, b_last, sel_new,
      proj3, proj3, proj3, proj3, b_new,
      cache_win.reshape(-1, HEAD_DIM), b_win, o_cmp.reshape(B, L, HEAD_DIM), gates)
    return out.reshape(B, G, J, T, HEAD_DIM).transpose(0, 3, 1, 2, 4).reshape(B, T, NSA_WIDTH)


def _retention_tables(T, q0):
    c = _largest_divisor(T, RET_CHUNK)
    cp = max(c, RET_CHUNK)
    lg = np.log1p(-(2.0 ** (-5.0 - np.arange(RET_HEADS, dtype=np.float32)))).astype(np.float32)
    i = np.arange(cp)
    rel = i[:, None] - i[None, :]
    inside = (i < c)[:, None] & (i < c)[None, :]
    decay = np.where((rel >= 0) & inside, np.exp(np.maximum(rel, 0)[None] * lg[:, None, None]), 0.0)
    q_dec = np.broadcast_to(np.exp((i + 1)[None, :, None] * lg[:, None, None]), (RET_HEADS, cp, HEAD_DIM))
    k_dec = np.where((i < c)[None, :, None], np.exp((c - 1 - i)[None, :, None] * lg[:, None, None]), 0.0)
    k_dec = np.broadcast_to(k_dec, (RET_HEADS, cp, HEAD_DIM))
    chunk_dec = np.broadcast_to(np.exp(c * lg)[:, None, None], (RET_HEADS, 8, HEAD_DIM))
    half = HEAD_DIM // 2
    inv = (1.0 / (10000.0 ** np.linspace(0.0, 1.0, half, dtype=np.float32))).astype(np.float32)
    ang = (q0 + np.arange(T)).astype(np.float32)[:, None] * inv[None]
    cos, sin = np.cos(ang), np.sin(ang)
    cosf = np.concatenate([cos, cos], axis=1)
    sinf = np.concatenate([-sin, sin], axis=1)
    f = lambda a: jnp.asarray(a, F32)
    return c, cp, f(decay), f(q_dec), f(k_dec), f(chunk_dec), f(cosf), f(sinf)


RET_HEADS_PER_STEP = 2


def _retention_kernel(q_ref, k_ref, v_ref, g_ref, s0_ref, cos_ref, sin_ref, dec_ref, qd_ref, kd_ref, cd_ref,
                      o_ref, s_ref, *, c, cp, n):
    half = HEAD_DIM // 2
    heads = range(RET_HEADS_PER_STEP)

    def load(ref, rows):
        return _pad_rows(ref[rows, :], cp)

    def rot(x, cos, sin):
        return x * cos + pltpu.roll(x, half, 1) * sin

    def body(i, states):
        rows = pl.ds(pl.multiple_of(i * c, c), c)
        cos, sin = load(cos_ref, rows), load(sin_ref, rows)
        q_all, k_all, v_all = load(q_ref.at[0], rows), load(k_ref.at[0], rows), load(v_ref.at[0], rows)
        new_states = []
        for hh in heads:
            cols = slice(hh * HEAD_DIM, (hh + 1) * HEAD_DIM)
            s = states[hh]
            q = rot(q_all[:, cols], cos, sin)
            k = rot(k_all[:, cols], cos, sin) * HEAD_DIM ** -0.5
            v = v_all[:, cols].astype(BF16)
            qb = q.astype(BF16)
            inner = _dot_nt(qb, k.astype(BF16)) * dec_ref[hh]
            o = (jnp.dot(inner.astype(BF16), v, preferred_element_type=F32)
                 + jnp.dot(qb, s.astype(BF16), preferred_element_type=F32) * qd_ref[hh])
            new_states.append(s * cd_ref[hh, 0:1, :] + _dot_tn((k * kd_ref[hh]).astype(BF16), v))
            o = o * lax.rsqrt(jnp.mean(o * o, axis=-1, keepdims=True) + EPS)
            g = g_ref[0, rows, cols]
            o_ref[0, rows, cols] = g * jax.nn.sigmoid(g) * o[:c]
        return tuple(new_states)

    final = lax.fori_loop(0, n, body, tuple(s0_ref[0, hh] for hh in heads))
    for hh in heads:
        s_ref[0, hh] = final[hh]


def retention(proj3, cols, s0, q0):
    B, T, _ = proj3.shape
    c, cp, decay, q_dec, k_dec, chunk_dec, cosf, sinf = _retention_tables(T, q0)
    hp = RET_HEADS_PER_STEP
    wide = hp * HEAD_DIM
    col = lambda name: pl.BlockSpec((1, T, wide), lambda b, h, o=cols[name] // wide: (b, 0, o + h))
    tab = lambda r: pl.BlockSpec((hp, r, HEAD_DIM), lambda b, h: (h, 0, 0))
    full = pl.BlockSpec((T, HEAD_DIM), lambda b, h: (0, 0))
    state = pl.BlockSpec((1, hp, HEAD_DIM, HEAD_DIM), lambda b, h: (b, h, 0, 0))
    return pl.pallas_call(
        functools.partial(_retention_kernel, c=c, cp=cp, n=T // c),
        grid=(B, RET_HEADS // hp),
        in_specs=[col('rq'), col('rk'), col('rv'), col('rg'), state, full, full,
                  tab(cp), tab(cp), tab(cp), tab(8)],
        out_specs=(pl.BlockSpec((1, T, wide), lambda b, h: (b, 0, h)), state),
        out_shape=(jax.ShapeDtypeStruct((B, T, RET_WIDTH), F32),
                   jax.ShapeDtypeStruct((B, RET_HEADS, HEAD_DIM, HEAD_DIM), F32)),
        compiler_params=pltpu.CompilerParams(dimension_semantics=("parallel", "parallel"),
                                             vmem_limit_bytes=V7X_VMEM_LIMIT_BYTES),
        name="retention",
    )(proj3, proj3, proj3, proj3, s0.astype(F32), cosf, sinf, decay, q_dec, k_dec, chunk_dec)


S5_BLK_GROUPS = 8
S5_BLK_STATE = S5_BLK_GROUPS * S5_STATE
S5_BLK_CH = S5_BLK_GROUPS * S5_GROUP
S5_SCAN_ROWS = 8


def _s5_params(lam_re, lam_im, log_step, b_re, b_im, c_re, c_im):
    nb = S5_GROUPS // S5_BLK_GROUPS
    lam = lax.complex(lam_re.astype(F32), lam_im.astype(F32))
    step = jnp.exp(log_step.astype(F32))[:, None]
    a_bar = jnp.exp(lam * step)
    b_bar = ((a_bar - 1.0) / lam)[..., None] * lax.complex(b_re.astype(F32), b_im.astype(F32))
    r = np.arange(S5_SCAN_ROWS)

    def powers(k, keep):
        p = jnp.exp(lam[None] * step[None] * jnp.asarray(k, F32)[:, None, None])
        return jnp.where(jnp.asarray(keep)[:, None, None], p, 0.0)

    tabs = [powers(np.full(S5_SCAN_ROWS, k), r >= k) for k in (1, 2, 4)]
    tabs.append(powers(r + 1, r >= 0))
    tab = jnp.stack(tabs)
    tab = tab.reshape(4, S5_SCAN_ROWS, nb, S5_BLK_STATE).transpose(2, 0, 1, 3)
    atab = jnp.concatenate([tab.real, tab.imag], axis=1)

    eye = jnp.eye(S5_BLK_GROUPS, dtype=F32)
    bb = b_bar.reshape(nb, S5_BLK_GROUPS, S5_STATE, S5_GROUP)

    def in_mat(x):
        return jnp.einsum('ngpc,gh->ngchp', x, eye).reshape(nb, S5_BLK_CH, S5_BLK_STATE)

    bmat = jnp.concatenate([in_mat(bb.real), in_mat(bb.imag)], axis=-1).astype(BF16)
    cr = c_re.astype(F32).reshape(nb, S5_BLK_GROUPS, S5_GROUP, S5_STATE)
    ci = c_im.astype(F32).reshape(nb, S5_BLK_GROUPS, S5_GROUP, S5_STATE)

    def out_mat(x):
        return jnp.einsum('ngcp,gh->ngphc', x, eye).reshape(nb, S5_BLK_STATE, S5_BLK_CH)

    cmat = jnp.concatenate([out_mat(cr), -out_mat(ci)], axis=1).astype(BF16)
    return atab, bmat, cmat


def _s5_scan_tile(xr, xi, cr, ci, atab_ref):
    for idx, k in enumerate((1, 2, 4)):
        pr, pi = atab_ref[0, idx], atab_ref[0, 4 + idx]
        sr, si = pltpu.roll(xr, k, 0), pltpu.roll(xi, k, 0)
        xr, xi = xr + pr * sr - pi * si, xi + pr * si + pi * sr
    pr, pi = atab_ref[0, 3], atab_ref[0, 7]
    xr, xi = xr + pr * cr - pi * ci, xi + pr * ci + pi * cr
    last = S5_SCAN_ROWS - 1
    cr = jnp.broadcast_to(xr[last:last + 1, :], xr.shape)
    ci = jnp.broadcast_to(xi[last:last + 1, :], xi.shape)
    return xr, xi, cr, ci


def _s5_kernel(u_ref, x0_ref, atab_ref, b_ref, c_ref, d_ref, y_ref, st_ref, xs_ref, *, T):
    u = u_ref[0]
    t_pad = _round_up(T, PACK_ROWS)
    xs_ref[...] = jnp.dot(_pad_rows(u, t_pad).astype(BF16), b_ref[0], preferred_element_type=F32)[:T]
    n = S5_BLK_STATE
    R = S5_SCAN_ROWS

    def body(i, carry):
        cr, ci = carry
        rows = pl.ds(pl.multiple_of(i * R, R), R)
        xr, xi, cr, ci = _s5_scan_tile(xs_ref[rows, :n], xs_ref[rows, n:], cr, ci, atab_ref)
        xs_ref[rows, :n] = xr
        xs_ref[rows, n:] = xi
        return cr, ci

    x0 = x0_ref[0, 0]
    cr0 = jnp.broadcast_to(x0[0:1, :], (R, n))
    ci0 = jnp.broadcast_to(x0[1:2, :], (R, n))
    cr, ci = lax.fori_loop(0, T // R, body, (cr0, ci0))
    st_ref[0, 0] = jnp.concatenate([cr[0:1], ci[0:1]], axis=0)
    y = jnp.dot(_pad_rows(xs_ref[...], t_pad).astype(BF16), c_ref[0], preferred_element_type=F32)[:T]
    y_ref[0] = _gelu_tanh(y + d_ref[...] * u)


def s5_scan(proj3, cols, x0, lam_re, lam_im, log_step, b_re, b_im, c_re, c_im, d):
    B, T, _ = proj3.shape
    assert T % S5_SCAN_ROWS == 0
    nb = S5_GROUPS // S5_BLK_GROUPS
    atab, bmat, cmat = _s5_params(lam_re, lam_im, log_step, b_re, b_im, c_re, c_im)
    x0b = x0.astype(F32).reshape(B, nb, S5_BLK_STATE, 2).transpose(0, 1, 3, 2)
    blk3 = lambda shape: pl.BlockSpec((1,) + shape, lambda b, j: (j, 0, 0))
    y, st = pl.pallas_call(
        functools.partial(_s5_kernel, T=T),
        grid=(B, nb),
        in_specs=[pl.BlockSpec((1, T, S5_BLK_CH), lambda b, j, o=cols['su'] // S5_BLK_CH: (b, 0, o + j)),
                  pl.BlockSpec((1, 1, 2, S5_BLK_STATE), lambda b, j: (b, j, 0, 0)),
                  pl.BlockSpec((1, 8, S5_SCAN_ROWS, S5_BLK_STATE), lambda b, j: (j, 0, 0, 0)),
                  blk3((S5_BLK_CH, 2 * S5_BLK_STATE)), blk3((2 * S5_BLK_STATE, S5_BLK_CH)),
                  pl.BlockSpec((1, S5_BLK_CH), lambda b, j: (0, j))],
        out_specs=(pl.BlockSpec((1, T, S5_BLK_CH), lambda b, j: (b, 0, j)),
                   pl.BlockSpec((1, 1, 2, S5_BLK_STATE), lambda b, j: (b, j, 0, 0))),
        out_shape=(jax.ShapeDtypeStruct((B, T, S5_WIDTH), F32),
                   jax.ShapeDtypeStruct((B, nb, 2, S5_BLK_STATE), F32)),
        scratch_shapes=[pltpu.VMEM((T, 2 * S5_BLK_STATE), F32)],
        compiler_params=pltpu.CompilerParams(dimension_semantics=("parallel", "parallel"),
                                             vmem_limit_bytes=V7X_VMEM_LIMIT_BYTES),
        name="s5_scan",
    )(proj3, x0b, atab, bmat, cmat, d.astype(F32).reshape(1, S5_WIDTH))
    st = st.transpose(0, 1, 3, 2).reshape(B, S5_GROUPS, S5_STATE, 2)
    return y, st


def _branch_norm_kernel(ro_ref, so_ref, no_ref, bn_ref, o_ref):
    off = 0
    for ref in (ro_ref, so_ref, no_ref):
        x = ref[...]
        w = x.shape[-1]
        y = x * lax.rsqrt(jnp.mean(x * x, axis=-1, keepdims=True) + EPS) * bn_ref[:, off:off + w]
        o_ref[:, off:off + w] = y.astype(o_ref.dtype)
        off += w


def branch_norm(ro, so, no, bn):
    m = ro.shape[0]
    tm = min(m, 256)
    assert m % tm == 0
    spec = lambda w: pl.BlockSpec((tm, w), lambda i: (i, 0))
    return pl.pallas_call(
        _branch_norm_kernel,
        grid=(m // tm,),
        in_specs=[spec(RET_WIDTH), spec(S5_WIDTH), spec(NSA_WIDTH), pl.BlockSpec((1, D_MODEL), lambda i: (0, 0))],
        out_specs=spec(D_MODEL),
        out_shape=jax.ShapeDtypeStruct((m, D_MODEL), BF16),
        compiler_params=pltpu.CompilerParams(dimension_semantics=("parallel",),
                                             vmem_limit_bytes=V7X_VMEM_LIMIT_BYTES),
        name="branch_norm",
    )(ro, so, no, bn.astype(F32).reshape(1, D_MODEL))


def _kv_rows_kernel(x_ref, o_ref):
    x = x_ref[0]
    for g in range(NSA_KV_HEADS):
        o_ref[pl.ds(g, x.shape[0], stride=NSA_KV_HEADS), :] = x[:, g * HEAD_DIM:(g + 1) * HEAD_DIM]


def kv_rows(proj3, col_k, t_start, t_len):
    B = proj3.shape[0]
    tT = min(t_len, 512)
    assert t_len % tT == 0 and t_start % tT == 0 and col_k % KV_WIDTH == 0
    n = t_len // tT
    rows = pl.pallas_call(
        _kv_rows_kernel,
        grid=(B, 2, n),
        in_specs=[pl.BlockSpec((1, tT, KV_WIDTH), lambda b, kv, i: (b, t_start // tT + i, col_k // KV_WIDTH + kv))],
        out_specs=pl.BlockSpec((tT * NSA_KV_HEADS, HEAD_DIM), lambda b, kv, i: ((b * 2 + kv) * n + i, 0)),
        out_shape=jax.ShapeDtypeStruct((B * 2 * t_len * NSA_KV_HEADS, HEAD_DIM), F32),
        compiler_params=pltpu.CompilerParams(dimension_semantics=("parallel", "parallel", "parallel"),
                                             vmem_limit_bytes=V7X_VMEM_LIMIT_BYTES),
        name="kv_rows",
    )(proj3)
    return rows.reshape(B, 2, t_len, NSA_KV_HEADS, HEAD_DIM)


_COL_NAMES = ('rq', 'rk', 'rv', 'rg', 'su', 'nq', 'kc', 'vc', 'ks', 'vs', 'kw', 'vw', 'ng')
COLS = {name: int(off) for name, off in zip(_COL_NAMES, np.concatenate([[0], np.cumsum(IN_SPLITS)]))}


def _block(x, layer, w, rel_bias, past, win_buf, big):
    B, T, _ = x.shape
    G = NSA_KV_HEADS
    M = B * T
    emitted = {}

    def project(a, name, **kw):
        if big is not None:
            return matmul(a, big[name], **kw)
        out, emitted[name] = matmul(a, w[name], layer, emit_bf16=True, **kw)
        return out

    x2 = x.reshape(M, D_MODEL)
    h = rmsnorm(x2, w['norm_mix'][layer], BF16)
    proj3 = matmul(h, w['w_in'], layer, n=COLS['ng']).reshape(B, T, -1)
    ng = matmul(h, w['w_gate'], layer)[:, :3 * NSA_HEADS].reshape(B, T, 3 * NSA_HEADS)
    s5_w = [w[k][layer] for k in ('s5_lambda_re', 's5_lambda_im', 's5_log_step', 's5_b_re', 's5_b_im',
                                  's5_c_re', 's5_c_im', 's5_d')]
    cmp_w = [w[k][layer] for k in ('cmp_pe_k', 'cmp_w1_k', 'cmp_w2_k', 'cmp_pe_v', 'cmp_w1_v', 'cmp_w2_v')]
    if past is None:
        q0 = 0
        ret_s0 = jnp.zeros((B, RET_HEADS, HEAD_DIM, HEAD_DIM), F32)
        s5_s0 = jnp.zeros((B, S5_GROUPS, S5_STATE, 2), F32)
        win_prev = jnp.zeros((B, 2, WINDOW, G, HEAD_DIM), x.dtype)
    else:
        cache_cmp, cache_slc, cache_win, state_ret, state_s5, page_table = past
        q0 = page_table.shape[1] * PAGE_SIZE
        ret_s0, s5_s0, win_prev = state_ret[layer], state_s5[layer], cache_win[layer]

    ro, ret_s = retention(proj3, COLS, ret_s0, q0)
    sy, s5_s = s5_scan(proj3, COLS, s5_s0, *s5_w)
    if past is None:
        kcmp, vcmp = compress_prompt(proj3, COLS['kc'], COLS['vc'], *cmp_w)
        no = nsa_prompt(proj3, COLS, ng, kcmp, vcmp, rel_bias)
    else:
        no = nsa_sample(proj3, COLS, ng, cache_cmp, cache_slc, cache_win, page_table, layer, rel_bias, *cmp_w)

    sy2 = sy.reshape(M, S5_WIDTH)
    so = matmul(sy2, w['s5_w_glu'], layer, res=sy2, act="glu")
    mix = branch_norm(ro.reshape(M, RET_WIDTH), so, no.reshape(M, NSA_WIDTH), w['branch_norm'][layer])
    x2 = project(mix, 'w_out', res=x2)
    h = rmsnorm(x2, w['norm_ffn'][layer], BF16)
    up = project(h, 'w_up', act="relu2", out_dtype=BF16)
    x2 = project(up, 'w_down', res=x2)

    cmp_rows = kv_rows(proj3, COLS['kc'], 0, T)
    slc_rows = kv_rows(proj3, COLS['ks'], 0, T)
    if T >= win_buf:
        win_new = kv_rows(proj3, COLS['kw'], T - win_buf, win_buf)
    else:
        win_new = jnp.concatenate([win_prev, kv_rows(proj3, COLS['kw'], 0, T)], axis=2)[:, :, -win_buf:]
    return x2.reshape(B, T, D_MODEL), cmp_rows, slc_rows, win_new, ret_s, s5_s, (big or emitted)


def kernel(x_prompt, x_sample, cache_cmp, cache_slc, cache_win, state_ret, state_s5, page_table,
           rel_bias, norm_mix, w_in, s5_lambda_re, s5_lambda_im, s5_log_step, s5_b_re, s5_b_im,
           s5_c_re, s5_c_im, s5_d, s5_w_glu, cmp_pe_k, cmp_w1_k, cmp_w2_k, cmp_pe_v, cmp_w1_v,
           cmp_w2_v, branch_norm, w_out, norm_ffn, w_up, w_down, norm_final):
    win_buf = cache_win.shape[3]
    n_main = COLS['ng']
    assert n_main % LANE == 0
    w_in_b = w_in.astype(BF16)
    w = dict(
        norm_mix=norm_mix, norm_ffn=norm_ffn, branch_norm=branch_norm,
        w_in=w_in_b, w_out=w_out, w_up=w_up, w_down=w_down,
        w_gate=jnp.pad(w_in_b[:, :, n_main:], ((0, 0), (0, 0), (0, LANE - (IN_COLS - n_main)))),
        s5_w_glu=s5_w_glu.astype(BF16),
        s5_lambda_re=s5_lambda_re, s5_lambda_im=s5_lambda_im, s5_log_step=s5_log_step, s5_b_re=s5_b_re,
        s5_b_im=s5_b_im, s5_c_re=s5_c_re, s5_c_im=s5_c_im, s5_d=s5_d,
        cmp_pe_k=cmp_pe_k, cmp_w1_k=cmp_w1_k, cmp_w2_k=cmp_w2_k,
        cmp_pe_v=cmp_pe_v, cmp_w1_v=cmp_w1_v, cmp_w2_v=cmp_w2_v)
    past = (cache_cmp, cache_slc, cache_win, state_ret, state_s5, page_table)
    xp, xs = x_prompt, x_sample
    written_p, written_s = [], []
    for layer in range(DEPTH):
        xs, *entries, big = _block(xs, layer, w, rel_bias, past, win_buf, None)
        written_s.append(entries)
        xp, *entries, _ = _block(xp, layer, w, rel_bias, None, win_buf, big)
        written_p.append(entries)
    y_prompt = rmsnorm(xp.reshape(-1, D_MODEL), norm_final, F32).reshape(xp.shape)
    y_sample = rmsnorm(xs.reshape(-1, D_MODEL), norm_final, F32).reshape(xs.shape)
    stacked = lambda written, i, axis: jnp.stack([entries[i] for entries in written], axis=axis)
    return (y_prompt, y_sample,
            stacked(written_p, 0, 1), stacked(written_s, 0, 1),
            stacked(written_p, 1, 1), stacked(written_s, 1, 1),
            stacked(written_p, 2, 0), stacked(written_s, 2, 0),
            stacked(written_p, 3, 0), stacked(written_s, 3, 0),
            stacked(written_p, 4, 0), stacked(written_s, 4, 0))
```

```python
import functools
import math

import jax
import jax.numpy as jnp
import numpy as np
from jax import lax
from jax.experimental import pallas as pl
from jax.experimental.pallas import tpu as pltpu

F32 = jnp.float32
BF16 = jnp.bfloat16

D_MODEL = 4096
DEPTH = 2
PAGE_SIZE = 128
HEAD_DIM = 128
RET_WIDTH = 1024
RET_HEADS = 8
RET_CHUNK = 128
S5_WIDTH = 1024
S5_GROUP = 16
S5_GROUPS = 64
S5_STATE = 64
NSA_WIDTH = 2048
NSA_HEADS = 16
NSA_KV_HEADS = 4
NSA_HPG = 4
KV_WIDTH = 512
CMP_BLOCK = 32
CMP_STRIDE = 16
SEL_BLOCK = 64
SEL_TOP = 16
WINDOW = 512
FORCE_SCORE = 1e4
NEG = -1e30
N_BUCKETS = 32
MAX_DISTANCE = 128
EPS = 1e-6
IN_SPLITS = (RET_WIDTH, RET_WIDTH, RET_WIDTH, RET_WIDTH, S5_WIDTH, NSA_WIDTH,
             KV_WIDTH, KV_WIDTH, KV_WIDTH, KV_WIDTH, KV_WIDTH, KV_WIDTH, 3 * NSA_HEADS)
IN_COLS = sum(IN_SPLITS)

V7X_VMEM_LIMIT_BYTES = 48 * 1024 * 1024
LANE = 128
PACK_ROWS = 16


def _round_up(n, m):
    return -(-n // m) * m


def _largest_divisor(n, cap):
    return max(d for d in range(1, min(n, cap) + 1) if n % d == 0)


def _pad_rows(x, rows):
    extra = rows - x.shape[0]
    return jnp.concatenate([x, jnp.zeros((extra, x.shape[1]), x.dtype)], axis=0) if extra else x


def _gelu_tanh(x):
    return 0.5 * x * (1.0 + jnp.tanh(math.sqrt(2.0 / math.pi) * (x + 0.044715 * (x * x * x))))


def _dot_nt(a, b):
    return lax.dot_general(a, b, (((1,), (1,)), ((), ())), preferred_element_type=F32)


def _dot_tn(a, b):
    return lax.dot_general(a, b, (((0,), (0,)), ((), ())), preferred_element_type=F32)


def _rmsnorm_kernel(x_ref, g_ref, o_ref):
    x = x_ref[...].astype(F32)
    ms = jnp.mean(x * x, axis=-1, keepdims=True)
    o_ref[...] = (x * lax.rsqrt(ms + EPS) * g_ref[...].astype(F32)).astype(o_ref.dtype)


def rmsnorm(x2d, gain, out_dtype):
    m, d = x2d.shape
    tm = min(m, 256)
    assert m % tm == 0
    return pl.pallas_call(
        _rmsnorm_kernel,
        grid=(m // tm,),
        in_specs=[pl.BlockSpec((tm, d), lambda i: (i, 0)),
                  pl.BlockSpec((1, d), lambda i: (0, 0))],
        out_specs=pl.BlockSpec((tm, d), lambda i: (i, 0)),
        out_shape=jax.ShapeDtypeStruct((m, d), out_dtype),
        compiler_params=pltpu.CompilerParams(dimension_semantics=("parallel",),
                                             vmem_limit_bytes=V7X_VMEM_LIMIT_BYTES),
        name="rmsnorm",
    )(x2d, gain.reshape(1, d))


def _mm_kernel(*refs, nk, act, has_res, emit_w):
    refs = list(refs)
    a_ref, w_ref = refs[:2]
    r_ref = refs[2] if has_res else None
    o_ref = refs[2 + has_res]
    wb_ref = refs[3 + has_res] if emit_w else None
    acc_ref = refs[-1]
    k = pl.program_id(2)

    @pl.when(k == 0)
    def _():
        acc_ref[...] = jnp.zeros_like(acc_ref)

    w = w_ref[...].astype(BF16)
    if emit_w:
        wb_ref[...] = w
    acc_ref[...] += jnp.dot(a_ref[...].astype(BF16), w, preferred_element_type=F32)

    @pl.when(k == nk - 1)
    def _():
        acc = acc_ref[...]
        if act == "relu2":
            acc = jnp.square(jnp.maximum(acc, 0.0))
        if act == "glu":
            acc = r_ref[...].astype(F32) * jax.nn.sigmoid(acc)
        elif has_res:
            acc = acc + r_ref[...].astype(F32)
        o_ref[...] = acc.astype(o_ref.dtype)


def matmul(a, w, layer=None, *, n=None, res=None, act=None, out_dtype=F32, emit_bf16=False):
    m, kdim = a.shape
    n = w.shape[-1] if n is None else n
    tm = min(m, 1024)
    tn = 512 if n % 512 == 0 else (256 if n % 256 == 0 else 128)
    tk = min(kdim, 4096)
    assert m % tm == 0 and n % tn == 0 and kdim % tk == 0
    assert not emit_bf16 or m == tm
    nk = kdim // tk
    if layer is None:
        w_spec = pl.BlockSpec((tk, tn), lambda i, j, k: (k, j))
    else:
        w_spec = pl.BlockSpec((None, tk, tn), lambda i, j, k: (layer, k, j))
    in_specs = [pl.BlockSpec((tm, tk), lambda i, j, k: (i, k)), w_spec]
    args = [a, w]
    if res is not None:
        in_specs.append(pl.BlockSpec((tm, tn), lambda i, j, k: (i, j)))
        args.append(res)
    out_specs = pl.BlockSpec((tm, tn), lambda i, j, k: (i, j))
    out_shape = jax.ShapeDtypeStruct((m, n), out_dtype)
    if emit_bf16:
        out_specs = (out_specs, pl.BlockSpec((tk, tn), lambda i, j, k: (k, j)))
        out_shape = (out_shape, jax.ShapeDtypeStruct((kdim, n), BF16))
    return pl.pallas_call(
        functools.partial(_mm_kernel, nk=nk, act=act, has_res=res is not None, emit_w=emit_bf16),
        grid=(m // tm, n // tn, nk),
        in_specs=in_specs,
        out_specs=out_specs,
        out_shape=out_shape,
        scratch_shapes=[pltpu.VMEM((tm, tn), F32)],
        compiler_params=pltpu.CompilerParams(
            dimension_semantics=("parallel", "parallel", "arbitrary"),
            vmem_limit_bytes=V7X_VMEM_LIMIT_BYTES),
        name="matmul",
    )(*args)


def _compress_rows(x_ref, n_full, pe_ref, w1_ref, w2_ref):
    pieces = [x_ref[0, pl.ds(s, n_full, stride=CMP_STRIDE), :] for s in range(CMP_STRIDE)]
    x = jnp.concatenate(pieces, axis=1).astype(BF16)
    half = CMP_STRIDE * HEAD_DIM
    z0 = jnp.dot(x, w1_ref[:half, :], preferred_element_type=F32)
    z1 = jnp.dot(x, w1_ref[half:, :], preferred_element_type=F32)
    z1 = pltpu.roll(z1, n_full - 1, 0)
    pe = jnp.broadcast_to(pe_ref[...], (8, CMP_BLOCK * HEAD_DIM)).astype(BF16)
    h0 = jnp.dot(pe, w1_ref[...], preferred_element_type=F32)[0:1, :]
    h = _gelu_tanh(z0 + z1 + h0)
    return jnp.dot(h.astype(BF16), w2_ref[...], preferred_element_type=F32)


def _compress_prompt_kernel(xk_ref, xv_ref, pek_ref, w1k_ref, w2k_ref, pev_ref, w1v_ref, w2v_ref,
                            kc_ref, vc_ref, *, n_full):
    for x_ref, pe_ref, w1_ref, w2_ref, o_ref in ((xk_ref, pek_ref, w1k_ref, w2k_ref, kc_ref),
                                                  (xv_ref, pev_ref, w1v_ref, w2v_ref, vc_ref)):
        out = _compress_rows(x_ref, n_full, pe_ref, w1_ref, w2_ref).astype(o_ref.dtype)
        n_pad = o_ref.shape[2]
        o_ref[0, 0, :n_full, :] = out
        if n_pad > n_full:
            o_ref[0, 0, n_full:, :] = jnp.zeros((n_pad - n_full, HEAD_DIM), o_ref.dtype)


def compress_prompt(proj3, col_k, col_v, pe_k, w1_k, w2_k, pe_v, w1_v, w2_v):
    B, T, _ = proj3.shape
    n_full = T // CMP_STRIDE
    n_pad = _round_up(n_full, LANE)
    flat = CMP_BLOCK * HEAD_DIM
    wspec = [pl.BlockSpec((1, flat), lambda b, g: (0, 0)),
             pl.BlockSpec((flat, HEAD_DIM), lambda b, g: (0, 0)),
             pl.BlockSpec((HEAD_DIM, HEAD_DIM), lambda b, g: (0, 0))]
    out_sds = jax.ShapeDtypeStruct((B, NSA_KV_HEADS, n_pad, HEAD_DIM), BF16)
    ospec = pl.BlockSpec((1, 1, n_pad, HEAD_DIM), lambda b, g: (b, g, 0, 0))
    return pl.pallas_call(
        functools.partial(_compress_prompt_kernel, n_full=n_full),
        grid=(B, NSA_KV_HEADS),
        in_specs=[pl.BlockSpec((1, T, HEAD_DIM), lambda b, g: (b, 0, col_k // HEAD_DIM + g)),
                  pl.BlockSpec((1, T, HEAD_DIM), lambda b, g: (b, 0, col_v // HEAD_DIM + g))] + wspec + wspec,
        out_specs=(ospec, ospec),
        out_shape=(out_sds, out_sds),
        compiler_params=pltpu.CompilerParams(dimension_semantics=("parallel", "parallel"),
                                             vmem_limit_bytes=V7X_VMEM_LIMIT_BYTES),
        name="nsa_compress_prompt",
    )(proj3, proj3,
      pe_k.reshape(1, flat), w1_k.reshape(flat, HEAD_DIM).astype(BF16), w2_k.astype(BF16),
      pe_v.reshape(1, flat), w1_v.reshape(flat, HEAD_DIM).astype(BF16), w2_v.astype(BF16))


def _bucket_table():
    d = np.arange(MAX_DISTANCE)
    max_exact = N_BUCKETS // 2
    large = max_exact + (np.log(np.maximum(d, 1).astype(np.float32) / np.float32(max_exact))
                         / np.float32(math.log(MAX_DISTANCE / max_exact))
                         * np.float32(N_BUCKETS - max_exact)).astype(np.int32)
    return np.where(d < max_exact, d, np.minimum(large, N_BUCKETS - 1)).astype(np.int32)


def _bias_of_distance(rel_bias, dist):
    bt = _bucket_table()
    buckets = bt[np.clip(dist, 0, MAX_DISTANCE - 1)]
    b = jnp.moveaxis(rel_bias.astype(F32)[buckets], -1, 0)
    return jnp.where(jnp.asarray(dist >= 0)[None], b, NEG)


def _bias_by_distance(rel_bias):
    return rel_bias.astype(F32)[_bucket_table()].T


def _toeplitz(v, n):
    h = v.shape[0]
    w = jnp.pad(v, ((0, 0), (0, 1)))
    m = jnp.tile(w, (1, n))[:, :n * (2 * n - 1)].reshape(h, n, 2 * n - 1)
    return m[:, :, n - 1:]


ATT_TILE = 128
N_BIAS_TILES = 4
SLC_CLASS_TILES = 2
MXU_TILES = 2


def _rows_softmax_pv(s_tiles, v_rows):
    m = s_tiles[0]
    for s in s_tiles[1:]:
        m = jnp.maximum(m, s)
    m = jnp.max(m, axis=-1, keepdims=True)
    l = None
    o = None
    for i in range(0, len(s_tiles), MXU_TILES):
        ps = [jnp.exp(s - m) for s in s_tiles[i:i + MXU_TILES]]
        for p in ps:
            l = p if l is None else l + p
        p_cat = ps[0] if len(ps) == 1 else jnp.concatenate(ps, axis=1)
        pv = jnp.dot(p_cat.astype(BF16), v_rows(i, len(ps)), preferred_element_type=F32)
        o = pv if o is None else o + pv
    return o / jnp.sum(l, axis=-1, keepdims=True)


def _score_tiles(q, k_rows, n_tiles):
    tiles = []
    for i in range(0, n_tiles, MXU_TILES):
        n = min(MXU_TILES, n_tiles - i)
        s = _dot_nt(q, k_rows(i, n))
        tiles += [s[:, j * LANE:(j + 1) * LANE] for j in range(n)]
    return tiles


def _nsa_prompt_kernel(q_ref, kc_ref, vc_ref, ks_ref, vs_ref, kw_ref, vw_ref, gate_ref,
                       bcmp_ref, btile_ref, ovl_ref, pick_ref, cvec_ref, kaug_s_ref, kaug_w_ref, o_ref,
                       ksb, vsb, kwb, vwb, s_ref, *, n_sel, n_top, T):
    tq = ATT_TILE
    J = NSA_HPG
    qi = pl.program_id(2)
    n_sel_pad = pick_ref.shape[0]
    nq = T // tq

    @pl.when(qi == 0)
    def _():
        ksb[:, :HEAD_DIM] = ks_ref[0].astype(BF16)
        ksb[:, HEAD_DIM:] = kaug_s_ref[...]
        vsb[...] = vs_ref[0].astype(BF16)
        kwb[:WINDOW, :HEAD_DIM] = jnp.zeros((WINDOW, HEAD_DIM), BF16)
        kwb[WINDOW:, :HEAD_DIM] = kw_ref[0].astype(BF16)
        kwb[:, HEAD_DIM:] = kaug_w_ref[...]
        vwb[:WINDOW, :] = jnp.zeros((WINDOW, HEAD_DIM), BF16)
        vwb[WINDOW:, :] = vw_ref[0].astype(BF16)

    def stack(f):
        return jnp.concatenate([f(j) for j in range(J)], axis=0)

    def put(vals, first):
        for j in range(J):
            cols = slice(j * HEAD_DIM, (j + 1) * HEAD_DIM)
            v = vals[j * tq:(j + 1) * tq]
            o_ref[0, :, cols] = v if first else o_ref[0, :, cols] + v

    gates = jax.nn.sigmoid(gate_ref[0, 0])
    gate = lambda c: stack(lambda j: gates[:, 3 * j + c:3 * j + c + 1])
    q_all = stack(lambda j: q_ref[0, :, j * HEAD_DIM:(j + 1) * HEAD_DIM] * HEAD_DIM ** -0.5).astype(BF16)

    row = lax.broadcasted_iota(jnp.int32, (tq, LANE), 0)
    lane = lax.broadcasted_iota(jnp.int32, (tq, LANE), 1)
    valid1 = qi * tq + row >= lane * CMP_STRIDE + (CMP_BLOCK - 1)
    valid = stack(lambda j: valid1)
    s = jnp.where(valid, _dot_nt(q_all, kc_ref[0, 0]) + stack(lambda j: bcmp_ref[j]), NEG)
    e = jnp.where(valid, jnp.exp(s - jnp.max(s, axis=-1, keepdims=True)), 0.0)
    den = jnp.sum(e, axis=-1, keepdims=True)
    p = e / jnp.where(den > 0.0, den, 1.0)
    put(gate(0) * jnp.dot(p.astype(BF16), vc_ref[0, 0], preferred_element_type=F32), True)
    psum = p[0:tq]
    for j in range(1, J):
        psum = psum + p[j * tq:(j + 1) * tq]

    bt = lambda i: stack(lambda j: btile_ref[j, i])

    pad_flag = jnp.broadcast_to(-cvec_ref[...], (tq, LANE)).astype(BF16)
    q_win = jnp.concatenate([q_all, stack(lambda j: pad_flag)], axis=1)
    n_w = WINDOW // tq + 1
    win_rows = lambda i, n: pl.ds(pl.multiple_of((qi + i) * tq, tq), n * tq)
    s_tiles = _score_tiles(q_win, lambda i, n: kwb[win_rows(i, n), :], n_w)
    s_tiles[0] = s_tiles[0] + bt(3)
    s_tiles[n_w - 2] = s_tiles[n_w - 2] + bt(1)
    s_tiles[n_w - 1] = s_tiles[n_w - 1] + bt(0)
    put(gate(2) * _rows_softmax_pv(s_tiles, lambda i, n: vwb[win_rows(i, n), :]), False)

    hi = psum.astype(BF16)
    r1 = psum - hi.astype(F32)
    mid = r1.astype(BF16)
    lo = (r1 - mid.astype(F32)).astype(BF16)
    ovl = ovl_ref[...]
    p_sel = _dot_nt(ovl, hi) + _dot_nt(ovl, mid) + _dot_nt(ovl, lo)
    blk = lax.broadcasted_iota(jnp.int32, (n_sel_pad, tq), 0)
    cur = (qi * tq + lax.broadcasted_iota(jnp.int32, (n_sel_pad, tq), 1)) // SEL_BLOCK
    forced = (blk == 0) | (blk == cur) | (blk == cur - 1)
    score = jnp.where(blk <= cur, p_sel + jnp.where(forced, FORCE_SCORE, 0.0), NEG)
    rank = jnp.zeros((n_sel_pad, tq), F32)
    for jb in range(n_sel):
        other = score[jb:jb + 1, :]
        beats = (other > score) | ((other == score) & (blk > jb))
        rank = rank + jnp.where(beats, 1.0, 0.0)
    sel = jnp.where((rank < n_top) & (blk <= cur) & (blk < n_sel), 1.0, 0.0).astype(BF16)
    aug = (_dot_tn(sel, pick_ref[...]) - cvec_ref[...]).astype(BF16)
    q_aug = jnp.concatenate([q_all, stack(lambda j: aug)], axis=1)

    for cls in range(-(-nq // SLC_CLASS_TILES)):
        n_t = min((cls + 1) * SLC_CLASS_TILES, nq)

        @pl.when(qi // SLC_CLASS_TILES == cls)
        def _(n_t=n_t):
            for kt, s in enumerate(_score_tiles(q_aug, lambda i, n: ksb[i * tq:(i + n) * tq, :], n_t)):
                s_ref[kt] = s
            s_ref[qi] = s_ref[qi] + bt(0)

            @pl.when(qi > 0)
            def _():
                s_ref[qi - 1] = s_ref[qi - 1] + bt(1)

            o = _rows_softmax_pv([s_ref[kt] for kt in range(n_t)], lambda i, n: vsb[i * tq:(i + n) * tq, :])
            put(gate(1) * o, False)


def nsa_prompt(proj3, cols, ng, kcmp, vcmp, rel_bias):
    B, T, _ = proj3.shape
    tq = ATT_TILE
    assert T % tq == 0 and T % SEL_BLOCK == 0 and WINDOW % tq == 0
    G, J = NSA_KV_HEADS, NSA_HPG
    nq = T // tq
    n_sel = T // SEL_BLOCK
    n_top = min(SEL_TOP, n_sel)
    n_sel_pad = _round_up(n_sel, 16)
    assert n_sel_pad < LANE
    n_pad = kcmp.shape[2]
    assert n_pad == LANE, "one lane tile of compressed blocks"

    bd = _bias_by_distance(rel_bias)
    n_a = T // CMP_STRIDE
    assert n_a == n_pad
    d_cmp = (CMP_STRIDE * ((n_a - 1) - np.arange(2 * n_a - 1))[None, :]
             + np.arange(CMP_STRIDE)[:, None] - (CMP_BLOCK - 1))
    gen_cmp = jnp.where(jnp.asarray(d_cmp >= 0)[None], bd[:, np.clip(d_cmp, 0, MAX_DISTANCE - 1)], 0.0)
    bcmp = _toeplitz(gen_cmp.reshape(NSA_HEADS * CMP_STRIDE, 2 * n_a - 1), n_a)
    bcmp = bcmp.reshape(NSA_HEADS, CMP_STRIDE, n_a, n_a).transpose(0, 2, 1, 3).reshape(NSA_HEADS, T, n_pad)
    d_diag = (tq - 1) - np.arange(2 * tq - 1)
    rel = bd - bd[:, -1:]
    gen = lambda d: jnp.where(jnp.asarray(d >= 0)[None], rel[:, np.clip(d, 0, MAX_DISTANCE - 1)], NEG)
    edge = jnp.broadcast_to(jnp.where(jnp.asarray(d_diag <= 0), 0.0, NEG)[None], (NSA_HEADS, 2 * tq - 1))
    btile = jnp.stack([_toeplitz(gen(d_diag), tq), _toeplitz(gen(d_diag + tq), tq),
                       jnp.zeros((NSA_HEADS, tq, tq), F32), _toeplitz(edge, tq)], axis=1)
    cmp_start = np.arange(n_pad) * CMP_STRIDE
    sel_start = np.arange(n_sel_pad) * SEL_BLOCK
    ovl = ((cmp_start[None, :] < sel_start[:, None] + SEL_BLOCK)
           & (cmp_start[None, :] + CMP_BLOCK > sel_start[:, None])
           & (np.arange(n_pad)[None, :] < T // CMP_STRIDE - 1))
    ovl = jnp.asarray(ovl, BF16)
    pick = jnp.asarray(np.arange(LANE)[None, :] == np.arange(n_sel_pad)[:, None], BF16)
    lane_i = np.arange(LANE)
    cvec = jnp.asarray(((lane_i < n_sel) | (lane_i == n_sel_pad))[None, :], F32)
    big = -NEG
    kaug_s = jnp.asarray(np.where(np.arange(T)[:, None] // SEL_BLOCK == lane_i[None, :], big, 0.0), BF16)
    kaug_w = jnp.asarray(np.where((np.arange(T + WINDOW)[:, None] < WINDOW) & (lane_i[None, :] == n_sel_pad),
                                  big, 0.0), BF16)
    gates = ng.reshape(B, T, G, 3 * J).transpose(0, 2, 1, 3)

    kv_spec = lambda name: pl.BlockSpec((1, T, HEAD_DIM),
                                        lambda b, g, i, o=cols[name] // HEAD_DIM: (b, 0, o + g))
    cmp_spec = pl.BlockSpec((1, 1, n_pad, HEAD_DIM), lambda b, g, i: (b, g, 0, 0))
    const2 = lambda shape: pl.BlockSpec(shape, lambda b, g, i: (0, 0))
    return pl.pallas_call(
        functools.partial(_nsa_prompt_kernel, n_sel=n_sel, n_top=n_top, T=T),
        grid=(B, G, nq),
        in_specs=[pl.BlockSpec((1, tq, J * HEAD_DIM),
                               lambda b, g, i, o=cols['nq'] // (J * HEAD_DIM): (b, i, o + g)),
                  cmp_spec, cmp_spec,
                  kv_spec('ks'), kv_spec('vs'), kv_spec('kw'), kv_spec('vw'),
                  pl.BlockSpec((1, 1, tq, 3 * J), lambda b, g, i: (b, g, i, 0)),
                  pl.BlockSpec((J, tq, n_pad), lambda b, g, i: (g, i, 0)),
                  pl.BlockSpec((J, N_BIAS_TILES, tq, tq), lambda b, g, i: (g, 0, 0, 0)),
                  const2((n_sel_pad, n_pad)), const2((n_sel_pad, LANE)), const2((1, LANE)),
                  const2((T, LANE)), const2((T + WINDOW, LANE))],
        out_specs=pl.BlockSpec((1, tq, J * HEAD_DIM), lambda b, g, i: (b, i, g)),
        out_shape=jax.ShapeDtypeStruct((B, T, NSA_WIDTH), F32),
        scratch_shapes=[pltpu.VMEM((T, 2 * HEAD_DIM), BF16), pltpu.VMEM((T, HEAD_DIM), BF16),
                        pltpu.VMEM((T + WINDOW, 2 * HEAD_DIM), BF16), pltpu.VMEM((T + WINDOW, HEAD_DIM), BF16),
                        pltpu.VMEM((nq, J * tq, tq), F32)],
        compiler_params=pltpu.CompilerParams(
            dimension_semantics=("parallel", "parallel", "arbitrary"),
            vmem_limit_bytes=V7X_VMEM_LIMIT_BYTES),
        name="nsa_prompt",
    )(proj3, kcmp, vcmp, proj3, proj3, proj3, proj3, gates, bcmp, btile, ovl, pick, cvec, kaug_s, kaug_w)


PAGE_ROWS = PAGE_SIZE * NSA_KV_HEADS
STRIDES_PER_PAGE = PAGE_SIZE // CMP_STRIDE
PAGES_PER_STEP = 16


def _cache_rows(cache):
    return cache.reshape(-1, HEAD_DIM)


def _page_specs(layer):
    return [pl.BlockSpec((2 * PAGE_ROWS, HEAD_DIM),
                         lambda b, p, pt, k=k: (pt[b, p * PAGES_PER_STEP + k] * DEPTH + layer, 0))
            for k in range(PAGES_PER_STEP)]


def _cmp_partial_kernel(pt_ref, *refs):
    page_refs = refs[:PAGES_PER_STEP]
    w1k_ref, w1v_ref, z_ref = refs[PAGES_PER_STEP:]
    G = NSA_KV_HEADS
    half = CMP_STRIDE * HEAD_DIM
    n_rows = PAGES_PER_STEP * STRIDES_PER_PAGE
    for kv, w1_ref in ((0, w1k_ref), (1, w1v_ref)):
        slabs = [jnp.transpose(r[kv * PAGE_ROWS:(kv + 1) * PAGE_ROWS, :].reshape(
            STRIDES_PER_PAGE, CMP_STRIDE * G, HEAD_DIM), (1, 0, 2)) for r in page_refs]
        xs = []
        for g in range(G):
            for slab in slabs:
                xs.append(jnp.concatenate([slab[s * G + g] for s in range(CMP_STRIDE)], axis=1))
        x = jnp.concatenate(xs, axis=0).astype(BF16)
        for j in range(CMP_BLOCK // CMP_STRIDE):
            z = jnp.dot(x, w1_ref[j * half:(j + 1) * half, :], preferred_element_type=F32)
            for g in range(G):
                z_ref[0, kv, j, g] = z[g * n_rows:(g + 1) * n_rows]


def cmp_partial(cache_rows, page_table, layer, w1_k, w1_v):
    B, n_pages = page_table.shape
    flat = CMP_BLOCK * HEAD_DIM
    n_str = n_pages * STRIDES_PER_PAGE
    assert n_pages % PAGES_PER_STEP == 0
    wspec = pl.BlockSpec((flat, HEAD_DIM), lambda b, p, pt: (0, 0))
    return pl.pallas_call(
        _cmp_partial_kernel,
        grid_spec=pltpu.PrefetchScalarGridSpec(
            num_scalar_prefetch=1,
            grid=(B, n_pages // PAGES_PER_STEP),
            in_specs=_page_specs(layer) + [wspec, wspec],
            out_specs=pl.BlockSpec((1, 2, 2, NSA_KV_HEADS, PAGES_PER_STEP * STRIDES_PER_PAGE, HEAD_DIM),
                                   lambda b, p, pt: (b, 0, 0, 0, p, 0))),
        out_shape=jax.ShapeDtypeStruct((B, 2, 2, NSA_KV_HEADS, n_str, HEAD_DIM), F32),
        compiler_params=pltpu.CompilerParams(dimension_semantics=("parallel", "arbitrary"),
                                             vmem_limit_bytes=V7X_VMEM_LIMIT_BYTES),
        name="nsa_cmp_partial",
    )(page_table, *([cache_rows] * PAGES_PER_STEP),
      w1_k.reshape(flat, HEAD_DIM).astype(BF16), w1_v.reshape(flat, HEAD_DIM).astype(BF16))


def _cmp_attn_sample_kernel(z_ref, pek_ref, w1k_ref, w2k_ref, pev_ref, w1v_ref, w2v_ref, q_ref, bias_ref,
                            ovl_ref, o_ref, psel_ref, *, q0):
    T = q_ref.shape[1]
    J = NSA_HPG
    n_str = z_ref.shape[4]

    def finish(kv, pe_ref, w1_ref, w2_ref):
        z1 = pltpu.roll(z_ref[0, kv, 1, 0], n_str - 1, 0)
        pe = jnp.broadcast_to(pe_ref[...], (PACK_ROWS, CMP_BLOCK * HEAD_DIM)).astype(BF16)
        h0 = jnp.dot(pe, w1_ref[...], preferred_element_type=F32)[0:1, :]
        h = _gelu_tanh(z_ref[0, kv, 0, 0] + z1 + h0)
        return jnp.dot(h.astype(BF16), w2_ref[...], preferred_element_type=F32).astype(BF16)

    kc = finish(0, pek_ref, w1k_ref, w2k_ref)
    vc = finish(1, pev_ref, w1v_ref, w2v_ref)
    stack = lambda f: jnp.concatenate([f(j) for j in range(J)], axis=0)
    q_all = stack(lambda j: q_ref[0, :, j * HEAD_DIM:(j + 1) * HEAD_DIM] * HEAD_DIM ** -0.5).astype(BF16)
    row = lax.broadcasted_iota(jnp.int32, (T, n_str), 0)
    lane = lax.broadcasted_iota(jnp.int32, (T, n_str), 1)
    valid1 = (q0 + row >= lane * CMP_STRIDE + (CMP_BLOCK - 1)) & (lane < n_str - 1)
    valid = stack(lambda j: valid1)
    s = jnp.where(valid, _dot_nt(q_all, kc) + stack(lambda j: bias_ref[j]), NEG)
    e = jnp.where(valid, jnp.exp(s - jnp.max(s, axis=-1, keepdims=True)), 0.0)
    den = jnp.sum(e, axis=-1, keepdims=True)
    p = e / jnp.where(den > 0.0, den, 1.0)
    o_ref[0, 0] = jnp.dot(p.astype(BF16), vc, preferred_element_type=F32)
    psum = p[0:T]
    for j in range(1, J):
        psum = psum + p[j * T:(j + 1) * T]
    psum = _pad_rows(psum, PACK_ROWS)
    hi = psum.astype(BF16)
    r1 = psum - hi.astype(F32)
    mid = r1.astype(BF16)
    lo = (r1 - mid.astype(F32)).astype(BF16)
    ovl = ovl_ref[...]
    p_sel = (jnp.dot(hi, ovl, preferred_element_type=F32) + jnp.dot(mid, ovl, preferred_element_type=F32)
             + jnp.dot(lo, ovl, preferred_element_type=F32))
    psel_ref[0, 0] = p_sel[:T]


def _rank_select_kernel(psel_ref, tpos_ref, out_ref, score_ref, rank_ref, *, n_sel, n_top):
    shape = psel_ref.shape
    blk = lax.broadcasted_iota(jnp.int32, shape, 0)
    cur = jnp.broadcast_to(tpos_ref[...], shape) // SEL_BLOCK
    forced = (blk == 0) | (blk == cur) | (blk == cur - 1)
    ok = (blk <= cur) & (blk < n_sel)
    score = jnp.where(ok, psel_ref[...] + jnp.where(forced, FORCE_SCORE, 0.0), NEG)
    score_ref[...] = score
    rank_ref[...] = jnp.zeros(shape, F32)

    def body(jb, c):
        other = jnp.broadcast_to(score_ref[pl.ds(jb, 1), :], shape)
        beats = (other > score) | ((other == score) & (blk > jb))
        rank_ref[...] = rank_ref[...] + jnp.where(beats, 1.0, 0.0)
        return c

    lax.fori_loop(0, n_sel, body, 0)
    out_ref[...] = jnp.where((rank_ref[...] < n_top) & ok, 0.0, NEG)


def _slc_win_sample_kernel(pt_ref, *refs, win_buf):
    page_refs = refs[:PAGES_PER_STEP]
    (qT_ref, seladd_ref, blast_ref, selnew_ref, ksn_ref, vsn_ref, kwn_ref, vwn_ref, bnew_ref, cw_ref, bwin_ref,
     ocmp_ref, gate_ref, o_ref, m_ref, l_ref, acc_ref) = refs[PAGES_PER_STEP:]
    G = NSA_KV_HEADS
    p = pl.program_id(1)
    last = pl.num_programs(1) - 1
    lane_group = lax.broadcasted_iota(jnp.int32, (1, LANE), 1) // (LANE // G)

    @pl.when(p == 0)
    def _():
        m_ref[...] = jnp.full(m_ref.shape, NEG, F32)
        l_ref[...] = jnp.zeros(l_ref.shape, F32)
        acc_ref[...] = jnp.zeros(acc_ref.shape, F32)

    def scores(k_of_g):
        s = None
        for g in range(G):
            sg = jnp.dot(k_of_g(g).astype(BF16), qT_ref[0, g], preferred_element_type=F32)
            s = sg if s is None else s + sg
        return s

    def weighted_values(pT, v_of_g):
        o = None
        for g in range(G):
            pg = jnp.where(lane_group == g, pT, 0.0).astype(BF16)
            og = _dot_tn(v_of_g(g).astype(BF16), pg)
            o = og if o is None else o + og
        return o

    def accumulate(segments):
        m_old = m_ref[0:1, :]
        m_new = m_old
        for sT, _ in segments:
            m_new = jnp.maximum(m_new, jnp.max(sT, axis=0, keepdims=True))
        alpha = jnp.exp(m_old - m_new)
        l_new = alpha * l_ref[0:1, :]
        acc = alpha * acc_ref[...]
        for sT, v_of_g in segments:
            pT = jnp.exp(sT - m_new)
            l_new = l_new + jnp.sum(pT, axis=0, keepdims=True)
            acc = acc + weighted_values(pT, v_of_g)
        l_ref[...] = jnp.broadcast_to(l_new, l_ref.shape)
        acc_ref[...] = acc
        m_ref[...] = jnp.broadcast_to(m_new, m_ref.shape)

    key = lax.broadcasted_iota(jnp.int32, (PAGE_SIZE, LANE), 0)
    blocks_per_page = PAGE_SIZE // SEL_BLOCK
    is_last = (p == last).astype(F32)
    segments = []
    for k, page_ref in enumerate(page_refs):
        k_page = lambda g, r=page_ref: r[pl.ds(g, PAGE_SIZE, stride=G), :]
        v_page = lambda g, r=page_ref: r[pl.ds(PAGE_ROWS + g, PAGE_SIZE, stride=G), :]
        mask = seladd_ref[0, k, blocks_per_page - 1:blocks_per_page, :]
        for i in range(blocks_per_page - 2, -1, -1):
            mask = jnp.where(key < (i + 1) * SEL_BLOCK, seladd_ref[0, k, i:i + 1, :], mask)
        sT = scores(k_page) + mask
        if k == PAGES_PER_STEP - 1:
            sT = sT + blast_ref[...] * is_last
        segments.append((sT, v_page))
    accumulate(segments)

    @pl.when(p == last)
    def _():
        new = lambda ref: (lambda g: _pad_rows(ref[0, :, g * HEAD_DIM:(g + 1) * HEAD_DIM], PACK_ROWS))
        accumulate([(scores(new(ksn_ref)) + bnew_ref[...] + selnew_ref[0], new(vsn_ref))])
        o_slc = (acc_ref[...] / l_ref[0:1, :]).T

        k_win = lambda g: cw_ref[pl.ds(g, win_buf, stride=G), :]
        v_win = lambda g: cw_ref[pl.ds(win_buf * G + g, win_buf, stride=G), :]
        s_w = scores(k_win) + bwin_ref[...]
        s_n = scores(new(kwn_ref)) + bnew_ref[...]
        m = jnp.maximum(jnp.max(s_w, axis=0, keepdims=True), jnp.max(s_n, axis=0, keepdims=True))
        p_w = jnp.exp(s_w - m)
        p_n = jnp.exp(s_n - m)
        den = jnp.sum(p_w, axis=0, keepdims=True) + jnp.sum(p_n, axis=0, keepdims=True)
        o_win = ((weighted_values(p_w, v_win) + weighted_values(p_n, new(vwn_ref))) / den).T

        gates = jax.nn.sigmoid(gate_ref[0])
        o_ref[0] = gates[:, 0:1] * ocmp_ref[0] + gates[:, 1:2] * o_slc + gates[:, 2:3] * o_win


def nsa_sample(proj3, cols, ng, cache_cmp, cache_slc, cache_win, page_table, layer, rel_bias,
               pe_k, w1_k, w2_k, pe_v, w1_v, w2_v):
    B, T, _ = proj3.shape
    G, J, H = NSA_KV_HEADS, NSA_HPG, NSA_HEADS
    n_pages = page_table.shape[1]
    past = n_pages * PAGE_SIZE
    q0 = past
    win_buf = cache_win.shape[3]
    L = G * J * T
    assert L == LANE and T <= PACK_ROWS and T < CMP_STRIDE and past % SEL_BLOCK == 0 and T <= SEL_BLOCK
    assert win_buf == min(WINDOW, past)
    n_str = past // CMP_STRIDE
    n_sel = past // SEL_BLOCK + 1
    n_top = min(SEL_TOP, n_sel)
    n_sel_rows = _round_up(n_sel, 8)
    n_sel_lanes = _round_up(n_sel, LANE)
    flat = CMP_BLOCK * HEAD_DIM
    b_far = rel_bias.astype(F32)[_bucket_table()[MAX_DISTANCE - 1]]

    def lane_bias(dist, ok):
        b = _bias_of_distance(rel_bias, np.maximum(dist, 0)) - b_far[:, None, None]
        b = jnp.where(jnp.asarray(ok)[None], b, NEG)
        return jnp.moveaxis(b.reshape(G, J, dist.shape[0], T), 2, 0).reshape(dist.shape[0], L)

    ti = np.arange(T)[None, :]
    z = cmp_partial(_cache_rows(cache_cmp), page_table, layer, w1_k, w1_v)
    n = np.arange(n_str)[None, :]
    dist_c = q0 + np.arange(T)[:, None] - (n * CMP_STRIDE + CMP_BLOCK - 1)
    near = dist_c.min(axis=0) < MAX_DISTANCE
    n_far = int(np.argmax(near)) if near.any() else n_str
    bcmp = jnp.concatenate([jnp.broadcast_to(b_far[:, None, None], (H, T, n_far)),
                            _bias_of_distance(rel_bias, np.maximum(dist_c[:, n_far:], 0))], axis=2)
    cmp_start = np.arange(n_str) * CMP_STRIDE
    sel_start = np.arange(n_sel_lanes) * SEL_BLOCK
    ovl = ((cmp_start[:, None] < sel_start[None, :] + SEL_BLOCK) & (cmp_start[:, None] + CMP_BLOCK > sel_start[None, :])
           & (np.arange(n_str)[:, None] < n_str - 1) & (np.arange(n_sel_lanes)[None, :] < n_sel))
    wspec = [pl.BlockSpec((1, flat), lambda b, g: (0, 0)),
             pl.BlockSpec((flat, HEAD_DIM), lambda b, g: (0, 0)),
             pl.BlockSpec((HEAD_DIM, HEAD_DIM), lambda b, g: (0, 0))]
    o_cmp, p_sel = pl.pallas_call(
        functools.partial(_cmp_attn_sample_kernel, q0=q0),
        grid=(B, G),
        in_specs=[pl.BlockSpec((1, 2, 2, 1, n_str, HEAD_DIM), lambda b, g: (b, 0, 0, g, 0, 0))] + wspec + wspec + [
            pl.BlockSpec((1, T, J * HEAD_DIM), lambda b, g, o=cols['nq'] // (J * HEAD_DIM): (b, 0, o + g)),
            pl.BlockSpec((J, T, n_str), lambda b, g: (g, 0, 0)),
            pl.BlockSpec((n_str, n_sel_lanes), lambda b, g: (0, 0))],
        out_specs=(pl.BlockSpec((1, 1, J * T, HEAD_DIM), lambda b, g: (b, g, 0, 0)),
                   pl.BlockSpec((1, 1, T, n_sel_lanes), lambda b, g: (b, g, 0, 0))),
        out_shape=(jax.ShapeDtypeStruct((B, G, J * T, HEAD_DIM), F32),
                   jax.ShapeDtypeStruct((B, G, T, n_sel_lanes), F32)),
        compiler_params=pltpu.CompilerParams(dimension_semantics=("parallel", "parallel"),
                                             vmem_limit_bytes=V7X_VMEM_LIMIT_BYTES),
        name="nsa_cmp_attn_sample",
    )(z, pe_k.reshape(1, flat), w1_k.reshape(flat, HEAD_DIM).astype(BF16), w2_k.astype(BF16),
      pe_v.reshape(1, flat), w1_v.reshape(flat, HEAD_DIM).astype(BF16), w2_v.astype(BF16),
      proj3, bcmp, jnp.asarray(ovl, BF16))

    n_bgt = B * G * T
    psel_t = p_sel.reshape(n_bgt, n_sel_lanes)[:, :n_sel_rows].T
    tpos = jnp.asarray(np.tile(q0 + np.arange(T), B * G)[None, :], jnp.int32)
    seladd = pl.pallas_call(
        functools.partial(_rank_select_kernel, n_sel=n_sel, n_top=n_top),
        out_shape=jax.ShapeDtypeStruct((n_sel_rows, n_bgt), F32),
        scratch_shapes=[pltpu.VMEM((n_sel_rows, n_bgt), F32), pltpu.VMEM((n_sel_rows, n_bgt), F32)],
        name="nsa_rank_select",
    )(psel_t, tpos)
    seladd = seladd.T.reshape(B, G, 1, T, n_sel_rows)
    seladd = jnp.broadcast_to(seladd, (B, G, J, T, n_sel_rows)).reshape(B, L, n_sel_rows)
    bpp = PAGE_SIZE // SEL_BLOCK
    sel_past = seladd[:, :, :n_sel - 1].reshape(B, L, n_pages, bpp).transpose(0, 2, 3, 1)
    sel_new = seladd[:, :, n_sel - 1].reshape(B, 1, L)

    q = proj3[:, :, cols['nq']:cols['nq'] + NSA_WIDTH].reshape(B, T, G, J, HEAD_DIM) * HEAD_DIM ** -0.5
    q_t = q.transpose(0, 2, 4, 3, 1).reshape(B, G, HEAD_DIM, J * T)
    place = jnp.asarray(np.arange(G)[:, None, None] == (np.arange(L) // (J * T))[None, None, :])
    q_pad = jnp.where(place[None], jnp.tile(q_t, (1, 1, 1, G)), 0.0).astype(BF16)
    ki = np.arange(PAGE_SIZE)[:, None]
    b_last = lane_bias(PAGE_SIZE + ti - ki, np.ones((PAGE_SIZE, T), bool))
    kn = np.arange(PACK_ROWS)[:, None]
    b_new = lane_bias(ti - kn, (ti - kn >= 0) & (kn < T))
    kw = np.arange(win_buf)[:, None]
    b_win = lane_bias(win_buf + ti - kw, win_buf + ti - kw <= WINDOW)
    gates = ng.reshape(B, T, G, J, 3).transpose(0, 2, 3, 1, 4).reshape(B, L, 3)
    win_rows = 2 * win_buf * G
    new_spec = lambda name: pl.BlockSpec((1, T, KV_WIDTH), lambda b, p, pt, o=cols[name] // KV_WIDTH: (b, 0, o))
    const = lambda shape: pl.BlockSpec(shape, lambda b, p, pt: (0, 0))
    per_b = lambda shape: pl.BlockSpec((1,) + shape, lambda b, p, pt: (b,) + (0,) * len(shape))
    out = pl.pallas_call(
        functools.partial(_slc_win_sample_kernel, win_buf=win_buf),
        grid_spec=pltpu.PrefetchScalarGridSpec(
            num_scalar_prefetch=1,
            grid=(B, n_pages // PAGES_PER_STEP),
            in_specs=_page_specs(layer) + [
                per_b((G, HEAD_DIM, L)),
                pl.BlockSpec((1, PAGES_PER_STEP, bpp, L), lambda b, p, pt: (b, p, 0, 0)),
                const((PAGE_SIZE, L)), per_b((1, L)),
                new_spec('ks'), new_spec('vs'), new_spec('kw'), new_spec('vw'),
                const((PACK_ROWS, L)),
                pl.BlockSpec((win_rows, HEAD_DIM), lambda b, p, pt: (layer * B + b, 0)),
                const((win_buf, L)), per_b((L, HEAD_DIM)), per_b((L, 3))],
            out_specs=per_b((L, HEAD_DIM)),
            scratch_shapes=[pltpu.VMEM((8, L), F32), pltpu.VMEM((8, L), F32), pltpu.VMEM((HEAD_DIM, L), F32)]),
        out_shape=jax.ShapeDtypeStruct((B, L, HEAD_DIM), F32),
        compiler_params=pltpu.CompilerParams(dimension_semantics=("parallel", "arbitrary"),
                                             vmem_limit_bytes=V7X_VMEM_LIMIT_BYTES),
        name="nsa_slc_win_sample",
    )(page_table, *([_cache_rows(cache_slc)] * PAGES_PER_STEP), q_pad, sel_past, b_last, sel_new,
      proj3, proj3, proj3, proj3, b_new,
      cache_win.reshape(-1, HEAD_DIM), b_win, o_cmp.reshape(B, L, HEAD_DIM), gates)
    return out.reshape(B, G, J, T, HEAD_DIM).transpose(0, 3, 1, 2, 4).reshape(B, T, NSA_WIDTH)


def _retention_tables(T, q0):
    c = _largest_divisor(T, RET_CHUNK)
    cp = max(c, RET_CHUNK)
    lg = np.log1p(-(2.0 ** (-5.0 - np.arange(RET_HEADS, dtype=np.float32)))).astype(np.float32)
    i = np.arange(cp)
    rel = i[:, None] - i[None, :]
    inside = (i < c)[:, None] & (i < c)[None, :]
    decay = np.where((rel >= 0) & inside, np.exp(np.maximum(rel, 0)[None] * lg[:, None, None]), 0.0)
    q_dec = np.broadcast_to(np.exp((i + 1)[None, :, None] * lg[:, None, None]), (RET_HEADS, cp, HEAD_DIM))
    k_dec = np.where((i < c)[None, :, None], np.exp((c - 1 - i)[None, :, None] * lg[:, None, None]), 0.0)
    k_dec = np.broadcast_to(k_dec, (RET_HEADS, cp, HEAD_DIM))
    chunk_dec = np.broadcast_to(np.exp(c * lg)[:, None, None], (RET_HEADS, 8, HEAD_DIM))
    half = HEAD_DIM // 2
    inv = (1.0 / (10000.0 ** np.linspace(0.0, 1.0, half, dtype=np.float32))).astype(np.float32)
    ang = (q0 + np.arange(T)).astype(np.float32)[:, None] * inv[None]
    cos, sin = np.cos(ang), np.sin(ang)
    cosf = np.concatenate([cos, cos], axis=1)
    sinf = np.concatenate([-sin, sin], axis=1)
    f = lambda a: jnp.asarray(a, F32)
    return c, cp, f(decay), f(q_dec), f(k_dec), f(chunk_dec), f(cosf), f(sinf)


RET_HEADS_PER_STEP = 2


def _retention_kernel(q_ref, k_ref, v_ref, g_ref, s0_ref, cos_ref, sin_ref, dec_ref, qd_ref, kd_ref, cd_ref,
                      o_ref, s_ref, *, c, cp, n):
    half = HEAD_DIM // 2
    heads = range(RET_HEADS_PER_STEP)

    def load(ref, rows):
        return _pad_rows(ref[rows, :], cp)

    def rot(x, cos, sin):
        return x * cos + pltpu.roll(x, half, 1) * sin

    def body(i, states):
        rows = pl.ds(pl.multiple_of(i * c, c), c)
        cos, sin = load(cos_ref, rows), load(sin_ref, rows)
        q_all, k_all, v_all = load(q_ref.at[0], rows), load(k_ref.at[0], rows), load(v_ref.at[0], rows)
        new_states = []
        for hh in heads:
            cols = slice(hh * HEAD_DIM, (hh + 1) * HEAD_DIM)
            s = states[hh]
            q = rot(q_all[:, cols], cos, sin)
            k = rot(k_all[:, cols], cos, sin) * HEAD_DIM ** -0.5
            v = v_all[:, cols].astype(BF16)
            qb = q.astype(BF16)
            inner = _dot_nt(qb, k.astype(BF16)) * dec_ref[hh]
            o = (jnp.dot(inner.astype(BF16), v, preferred_element_type=F32)
                 + jnp.dot(qb, s.astype(BF16), preferred_element_type=F32) * qd_ref[hh])
            new_states.append(s * cd_ref[hh, 0:1, :] + _dot_tn((k * kd_ref[hh]).astype(BF16), v))
            o = o * lax.rsqrt(jnp.mean(o * o, axis=-1, keepdims=True) + EPS)
            g = g_ref[0, rows, cols]
            o_ref[0, rows, cols] = g * jax.nn.sigmoid(g) * o[:c]
        return tuple(new_states)

    final = lax.fori_loop(0, n, body, tuple(s0_ref[0, hh] for hh in heads))
    for hh in heads:
        s_ref[0, hh] = final[hh]


def retention(proj3, cols, s0, q0):
    B, T, _ = proj3.shape
    c, cp, decay, q_dec, k_dec, chunk_dec, cosf, sinf = _retention_tables(T, q0)
    hp = RET_HEADS_PER_STEP
    wide = hp * HEAD_DIM
    col = lambda name: pl.BlockSpec((1, T, wide), lambda b, h, o=cols[name] // wide: (b, 0, o + h))
    tab = lambda r: pl.BlockSpec((hp, r, HEAD_DIM), lambda b, h: (h, 0, 0))
    full = pl.BlockSpec((T, HEAD_DIM), lambda b, h: (0, 0))
    state = pl.BlockSpec((1, hp, HEAD_DIM, HEAD_DIM), lambda b, h: (b, h, 0, 0))
    return pl.pallas_call(
        functools.partial(_retention_kernel, c=c, cp=cp, n=T // c),
        grid=(B, RET_HEADS // hp),
        in_specs=[col('rq'), col('rk'), col('rv'), col('rg'), state, full, full,
                  tab(cp), tab(cp), tab(cp), tab(8)],
        out_specs=(pl.BlockSpec((1, T, wide), lambda b, h: (b, 0, h)), state),
        out_shape=(jax.ShapeDtypeStruct((B, T, RET_WIDTH), F32),
                   jax.ShapeDtypeStruct((B, RET_HEADS, HEAD_DIM, HEAD_DIM), F32)),
        compiler_params=pltpu.CompilerParams(dimension_semantics=("parallel", "parallel"),
                                             vmem_limit_bytes=V7X_VMEM_LIMIT_BYTES),
        name="retention",
    )(proj3, proj3, proj3, proj3, s0.astype(F32), cosf, sinf, decay, q_dec, k_dec, chunk_dec)


S5_BLK_GROUPS = 8
S5_BLK_STATE = S5_BLK_GROUPS * S5_STATE
S5_BLK_CH = S5_BLK_GROUPS * S5_GROUP
S5_SCAN_ROWS = 8


def _s5_params(lam_re, lam_im, log_step, b_re, b_im, c_re, c_im):
    nb = S5_GROUPS // S5_BLK_GROUPS
    lam = lax.complex(lam_re.astype(F32), lam_im.astype(F32))
    step = jnp.exp(log_step.astype(F32))[:, None]
    a_bar = jnp.exp(lam * step)
    b_bar = ((a_bar - 1.0) / lam)[..., None] * lax.complex(b_re.astype(F32), b_im.astype(F32))
    r = np.arange(S5_SCAN_ROWS)

    def powers(k, keep):
        p = jnp.exp(lam[None] * step[None] * jnp.asarray(k, F32)[:, None, None])
        return jnp.where(jnp.asarray(keep)[:, None, None], p, 0.0)

    tabs = [powers(np.full(S5_SCAN_ROWS, k), r >= k) for k in (1, 2, 4)]
    tabs.append(powers(r + 1, r >= 0))
    tab = jnp.stack(tabs)
    tab = tab.reshape(4, S5_SCAN_ROWS, nb, S5_BLK_STATE).transpose(2, 0, 1, 3)
    atab = jnp.concatenate([tab.real, tab.imag], axis=1)

    eye = jnp.eye(S5_BLK_GROUPS, dtype=F32)
    bb = b_bar.reshape(nb, S5_BLK_GROUPS, S5_STATE, S5_GROUP)

    def in_mat(x):
        return jnp.einsum('ngpc,gh->ngchp', x, eye).reshape(nb, S5_BLK_CH, S5_BLK_STATE)

    bmat = jnp.concatenate([in_mat(bb.real), in_mat(bb.imag)], axis=-1).astype(BF16)
    cr = c_re.astype(F32).reshape(nb, S5_BLK_GROUPS, S5_GROUP, S5_STATE)
    ci = c_im.astype(F32).reshape(nb, S5_BLK_GROUPS, S5_GROUP, S5_STATE)

    def out_mat(x):
        return jnp.einsum('ngcp,gh->ngphc', x, eye).reshape(nb, S5_BLK_STATE, S5_BLK_CH)

    cmat = jnp.concatenate([out_mat(cr), -out_mat(ci)], axis=1).astype(BF16)
    return atab, bmat, cmat


def _s5_scan_tile(xr, xi, cr, ci, atab_ref):
    for idx, k in enumerate((1, 2, 4)):
        pr, pi = atab_ref[0, idx], atab_ref[0, 4 + idx]
        sr, si = pltpu.roll(xr, k, 0), pltpu.roll(xi, k, 0)
        xr, xi = xr + pr * sr - pi * si, xi + pr * si + pi * sr
    pr, pi = atab_ref[0, 3], atab_ref[0, 7]
    xr, xi = xr + pr * cr - pi * ci, xi + pr * ci + pi * cr
    last = S5_SCAN_ROWS - 1
    cr = jnp.broadcast_to(xr[last:last + 1, :], xr.shape)
    ci = jnp.broadcast_to(xi[last:last + 1, :], xi.shape)
    return xr, xi, cr, ci


def _s5_kernel(u_ref, x0_ref, atab_ref, b_ref, c_ref, d_ref, y_ref, st_ref, xs_ref, *, T):
    u = u_ref[0]
    t_pad = _round_up(T, PACK_ROWS)
    xs_ref[...] = jnp.dot(_pad_rows(u, t_pad).astype(BF16), b_ref[0], preferred_element_type=F32)[:T]
    n = S5_BLK_STATE
    R = S5_SCAN_ROWS

    def body(i, carry):
        cr, ci = carry
        rows = pl.ds(pl.multiple_of(i * R, R), R)
        xr, xi, cr, ci = _s5_scan_tile(xs_ref[rows, :n], xs_ref[rows, n:], cr, ci, atab_ref)
        xs_ref[rows, :n] = xr
        xs_ref[rows, n:] = xi
        return cr, ci

    x0 = x0_ref[0, 0]
    cr0 = jnp.broadcast_to(x0[0:1, :], (R, n))
    ci0 = jnp.broadcast_to(x0[1:2, :], (R, n))
    cr, ci = lax.fori_loop(0, T // R, body, (cr0, ci0))
    st_ref[0, 0] = jnp.concatenate([cr[0:1], ci[0:1]], axis=0)
    y = jnp.dot(_pad_rows(xs_ref[...], t_pad).astype(BF16), c_ref[0], preferred_element_type=F32)[:T]
    y_ref[0] = _gelu_tanh(y + d_ref[...] * u)


def s5_scan(proj3, cols, x0, lam_re, lam_im, log_step, b_re, b_im, c_re, c_im, d):
    B, T, _ = proj3.shape
    assert T % S5_SCAN_ROWS == 0
    nb = S5_GROUPS // S5_BLK_GROUPS
    atab, bmat, cmat = _s5_params(lam_re, lam_im, log_step, b_re, b_im, c_re, c_im)
    x0b = x0.astype(F32).reshape(B, nb, S5_BLK_STATE, 2).transpose(0, 1, 3, 2)
    blk3 = lambda shape: pl.BlockSpec((1,) + shape, lambda b, j: (j, 0, 0))
    y, st = pl.pallas_call(
        functools.partial(_s5_kernel, T=T),
        grid=(B, nb),
        in_specs=[pl.BlockSpec((1, T, S5_BLK_CH), lambda b, j, o=cols['su'] // S5_BLK_CH: (b, 0, o + j)),
                  pl.BlockSpec((1, 1, 2, S5_BLK_STATE), lambda b, j: (b, j, 0, 0)),
                  pl.BlockSpec((1, 8, S5_SCAN_ROWS, S5_BLK_STATE), lambda b, j: (j, 0, 0, 0)),
                  blk3((S5_BLK_CH, 2 * S5_BLK_STATE)), blk3((2 * S5_BLK_STATE, S5_BLK_CH)),
                  pl.BlockSpec((1, S5_BLK_CH), lambda b, j: (0, j))],
        out_specs=(pl.BlockSpec((1, T, S5_BLK_CH), lambda b, j: (b, 0, j)),
                   pl.BlockSpec((1, 1, 2, S5_BLK_STATE), lambda b, j: (b, j, 0, 0))),
        out_shape=(jax.ShapeDtypeStruct((B, T, S5_WIDTH), F32),
                   jax.ShapeDtypeStruct((B, nb, 2, S5_BLK_STATE), F32)),
        scratch_shapes=[pltpu.VMEM((T, 2 * S5_BLK_STATE), F32)],
        compiler_params=pltpu.CompilerParams(dimension_semantics=("parallel", "parallel"),
                                             vmem_limit_bytes=V7X_VMEM_LIMIT_BYTES),
        name="s5_scan",
    )(proj3, x0b, atab, bmat, cmat, d.astype(F32).reshape(1, S5_WIDTH))
    st = st.transpose(0, 1, 3, 2).reshape(B, S5_GROUPS, S5_STATE, 2)
    return y, st


def _branch_norm_kernel(ro_ref, so_ref, no_ref, bn_ref, o_ref):
    off = 0
    for ref in (ro_ref, so_ref, no_ref):
        x = ref[...]
        w = x.shape[-1]
        y = x * lax.rsqrt(jnp.mean(x * x, axis=-1, keepdims=True) + EPS) * bn_ref[:, off:off + w]
        o_ref[:, off:off + w] = y.astype(o_ref.dtype)
        off += w


def branch_norm(ro, so, no, bn):
    m = ro.shape[0]
    tm = min(m, 256)
    assert m % tm == 0
    spec = lambda w: pl.BlockSpec((tm, w), lambda i: (i, 0))
    return pl.pallas_call(
        _branch_norm_kernel,
        grid=(m // tm,),
        in_specs=[spec(RET_WIDTH), spec(S5_WIDTH), spec(NSA_WIDTH), pl.BlockSpec((1, D_MODEL), lambda i: (0, 0))],
        out_specs=spec(D_MODEL),
        out_shape=jax.ShapeDtypeStruct((m, D_MODEL), BF16),
        compiler_params=pltpu.CompilerParams(dimension_semantics=("parallel",),
                                             vmem_limit_bytes=V7X_VMEM_LIMIT_BYTES),
        name="branch_norm",
    )(ro, so, no, bn.astype(F32).reshape(1, D_MODEL))


def _kv_rows_kernel(x_ref, o_ref):
    x = x_ref[0]
    for g in range(NSA_KV_HEADS):
        o_ref[pl.ds(g, x.shape[0], stride=NSA_KV_HEADS), :] = x[:, g * HEAD_DIM:(g + 1) * HEAD_DIM]


def kv_rows(proj3, col_k, t_start, t_len):
    B = proj3.shape[0]
    tT = min(t_len, 512)
    assert t_len % tT == 0 and t_start % tT == 0 and col_k % KV_WIDTH == 0
    n = t_len // tT
    rows = pl.pallas_call(
        _kv_rows_kernel,
        grid=(B, 2, n),
        in_specs=[pl.BlockSpec((1, tT, KV_WIDTH), lambda b, kv, i: (b, t_start // tT + i, col_k // KV_WIDTH + kv))],
        out_specs=pl.BlockSpec((tT * NSA_KV_HEADS, HEAD_DIM), lambda b, kv, i: ((b * 2 + kv) * n + i, 0)),
        out_shape=jax.ShapeDtypeStruct((B * 2 * t_len * NSA_KV_HEADS, HEAD_DIM), F32),
        compiler_params=pltpu.CompilerParams(dimension_semantics=("parallel", "parallel", "parallel"),
                                             vmem_limit_bytes=V7X_VMEM_LIMIT_BYTES),
        name="kv_rows",
    )(proj3)
    return rows.reshape(B, 2, t_len, NSA_KV_HEADS, HEAD_DIM)


_COL_NAMES = ('rq', 'rk', 'rv', 'rg', 'su', 'nq', 'kc', 'vc', 'ks', 'vs', 'kw', 'vw', 'ng')
COLS = {name: int(off) for name, off in zip(_COL_NAMES, np.concatenate([[0], np.cumsum(IN_SPLITS)]))}


def _block(x, layer, w, rel_bias, past, win_buf, big):
    B, T, _ = x.shape
    G = NSA_KV_HEADS
    M = B * T
    emitted = {}

    def project(a, name, **kw):
        if big is not None:
            return matmul(a, big[name], **kw)
        out, emitted[name] = matmul(a, w[name], layer, emit_bf16=True, **kw)
        return out

    x2 = x.reshape(M, D_MODEL)
    h = rmsnorm(x2, w['norm_mix'][layer], BF16)
    proj3 = matmul(h, w['w_in'], layer, n=COLS['ng']).reshape(B, T, -1)
    ng = matmul(h, w['w_gate'], layer)[:, :3 * NSA_HEADS].reshape(B, T, 3 * NSA_HEADS)
    s5_w = [w[k][layer] for k in ('s5_lambda_re', 's5_lambda_im', 's5_log_step', 's5_b_re', 's5_b_im',
                                  's5_c_re', 's5_c_im', 's5_d')]
    cmp_w = [w[k][layer] for k in ('cmp_pe_k', 'cmp_w1_k', 'cmp_w2_k', 'cmp_pe_v', 'cmp_w1_v', 'cmp_w2_v')]
    if past is None:
        q0 = 0
        ret_s0 = jnp.zeros((B, RET_HEADS, HEAD_DIM, HEAD_DIM), F32)
        s5_s0 = jnp.zeros((B, S5_GROUPS, S5_STATE, 2), F32)
        win_prev = jnp.zeros((B, 2, WINDOW, G, HEAD_DIM), x.dtype)
    else:
        cache_cmp, cache_slc, cache_win, state_ret, state_s5, page_table = past
        q0 = page_table.shape[1] * PAGE_SIZE
        ret_s0, s5_s0, win_prev = state_ret[layer], state_s5[layer], cache_win[layer]

    ro, ret_s = retention(proj3, COLS, ret_s0, q0)
    sy, s5_s = s5_scan(proj3, COLS, s5_s0, *s5_w)
    if past is None:
        kcmp, vcmp = compress_prompt(proj3, COLS['kc'], COLS['vc'], *cmp_w)
        no = nsa_prompt(proj3, COLS, ng, kcmp, vcmp, rel_bias)
    else:
        no = nsa_sample(proj3, COLS, ng, cache_cmp, cache_slc, cache_win, page_table, layer, rel_bias, *cmp_w)

    sy2 = sy.reshape(M, S5_WIDTH)
    so = matmul(sy2, w['s5_w_glu'], layer, res=sy2, act="glu")
    mix = branch_norm(ro.reshape(M, RET_WIDTH), so, no.reshape(M, NSA_WIDTH), w['branch_norm'][layer])
    x2 = project(mix, 'w_out', res=x2)
    h = rmsnorm(x2, w['norm_ffn'][layer], BF16)
    up = project(h, 'w_up', act="relu2", out_dtype=BF16)
    x2 = project(up, 'w_down', res=x2)

    cmp_rows = kv_rows(proj3, COLS['kc'], 0, T)
    slc_rows = kv_rows(proj3, COLS['ks'], 0, T)
    if T >= win_buf:
        win_new = kv_rows(proj3, COLS['kw'], T - win_buf, win_buf)
    else:
        win_new = jnp.concatenate([win_prev, kv_rows(proj3, COLS['kw'], 0, T)], axis=2)[:, :, -win_buf:]
    return x2.reshape(B, T, D_MODEL), cmp_rows, slc_rows, win_new, ret_s, s5_s, (big or emitted)


def kernel(x_prompt, x_sample, cache_cmp, cache_slc, cache_win, state_ret, state_s5, page_table,
           rel_bias, norm_mix, w_in, s5_lambda_re, s5_lambda_im, s5_log_step, s5_b_re, s5_b_im,
           s5_c_re, s5_c_im, s5_d, s5_w_glu, cmp_pe_k, cmp_w1_k, cmp_w2_k, cmp_pe_v, cmp_w1_v,
           cmp_w2_v, branch_norm, w_out, norm_ffn, w_up, w_down, norm_final):
    win_buf = cache_win.shape[3]
    n_main = COLS['ng']
    assert n_main % LANE == 0
    w_in_b = w_in.astype(BF16)
    w = dict(
        norm_mix=norm_mix, norm_ffn=norm_ffn, branch_norm=branch_norm,
        w_in=w_in_b, w_out=w_out, w_up=w_up, w_down=w_down,
        w_gate=jnp.pad(w_in_b[:, :, n_main:], ((0, 0), (0, 0), (0, LANE - (IN_COLS - n_main)))),
        s5_w_glu=s5_w_glu.astype(BF16),
        s5_lambda_re=s5_lambda_re, s5_lambda_im=s5_lambda_im, s5_log_step=s5_log_step, s5_b_re=s5_b_re,
        s5_b_im=s5_b_im, s5_c_re=s5_c_re, s5_c_im=s5_c_im, s5_d=s5_d,
        cmp_pe_k=cmp_pe_k, cmp_w1_k=cmp_w1_k, cmp_w2_k=cmp_w2_k,
        cmp_pe_v=cmp_pe_v, cmp_w1_v=cmp_w1_v, cmp_w2_v=cmp_w2_v)
    past = (cache_cmp, cache_slc, cache_win, state_ret, state_s5, page_table)
    xp, xs = x_prompt, x_sample
    written_p, written_s = [], []
    for layer in range(DEPTH):
        xs, *entries, big = _block(xs, layer, w, rel_bias, past, win_buf, None)
        written_s.append(entries)
        xp, *entries, _ = _block(xp, layer, w, rel_bias, None, win_buf, big)
        written_p.append(entries)
    y_prompt = rmsnorm(xp.reshape(-1, D_MODEL), norm_final, F32).reshape(xp.shape)
    y_sample = rmsnorm(xs.reshape(-1, D_MODEL), norm_final, F32).reshape(xs.shape)
    stacked = lambda written, i, axis: jnp.stack([entries[i] for entries in written], axis=axis)
    return (y_prompt, y_sample,
            stacked(written_p, 0, 1), stacked(written_s, 0, 1),
            stacked(written_p, 1, 1), stacked(written_s, 1, 1),
            stacked(written_p, 2, 0), stacked(written_s, 2, 0),
            stacked(written_p, 3, 0), stacked(written_s, 3, 0),
            stacked(written_p, 4, 0), stacked(written_s, 4, 0))
```

```python
import functools
import math

import jax
import jax.numpy as jnp
import numpy as np
from jax import lax
from jax.experimental import pallas as pl
from jax.experimental.pallas import tpu as pltpu

F32 = jnp.float32
BF16 = jnp.bfloat16

D_MODEL = 4096
DEPTH = 2
PAGE_SIZE = 128
HEAD_DIM = 128
RET_WIDTH = 1024
RET_HEADS = 8
RET_CHUNK = 128
S5_WIDTH = 1024
S5_GROUP = 16
S5_GROUPS = 64
S5_STATE = 64
NSA_WIDTH = 2048
NSA_HEADS = 16
NSA_KV_HEADS = 4
NSA_HPG = 4
KV_WIDTH = 512
CMP_BLOCK = 32
CMP_STRIDE = 16
SEL_BLOCK = 64
SEL_TOP = 16
WINDOW = 512
FORCE_SCORE = 1e4
NEG = -1e30
N_BUCKETS = 32
MAX_DISTANCE = 128
EPS = 1e-6
IN_SPLITS = (RET_WIDTH, RET_WIDTH, RET_WIDTH, RET_WIDTH, S5_WIDTH, NSA_WIDTH,
             KV_WIDTH, KV_WIDTH, KV_WIDTH, KV_WIDTH, KV_WIDTH, KV_WIDTH, 3 * NSA_HEADS)
IN_COLS = sum(IN_SPLITS)

V7X_VMEM_LIMIT_BYTES = 48 * 1024 * 1024
LANE = 128
PACK_ROWS = 16


def _round_up(n, m):
    return -(-n // m) * m


def _largest_divisor(n, cap):
    return max(d for d in range(1, min(n, cap) + 1) if n % d == 0)


def _pad_rows(x, rows):
    extra = rows - x.shape[0]
    return jnp.concatenate([x, jnp.zeros((extra, x.shape[1]), x.dtype)], axis=0) if extra else x


def _gelu_tanh(x):
    return 0.5 * x * (1.0 + jnp.tanh(math.sqrt(2.0 / math.pi) * (x + 0.044715 * (x * x * x))))


def _dot_nt(a, b):
    return lax.dot_general(a, b, (((1,), (1,)), ((), ())), preferred_element_type=F32)


def _dot_tn(a, b):
    return lax.dot_general(a, b, (((0,), (0,)), ((), ())), preferred_element_type=F32)


def _rmsnorm_kernel(x_ref, g_ref, o_ref):
    x = x_ref[...].astype(F32)
    ms = jnp.mean(x * x, axis=-1, keepdims=True)
    o_ref[...] = (x * lax.rsqrt(ms + EPS) * g_ref[...].astype(F32)).astype(o_ref.dtype)


def rmsnorm(x2d, gain, out_dtype):
    m, d = x2d.shape
    tm = min(m, 256)
    assert m % tm == 0
    return pl.pallas_call(
        _rmsnorm_kernel,
        grid=(m // tm,),
        in_specs=[pl.BlockSpec((tm, d), lambda i: (i, 0)),
                  pl.BlockSpec((1, d), lambda i: (0, 0))],
        out_specs=pl.BlockSpec((tm, d), lambda i: (i, 0)),
        out_shape=jax.ShapeDtypeStruct((m, d), out_dtype),
        compiler_params=pltpu.CompilerParams(dimension_semantics=("parallel",),
                                             vmem_limit_bytes=V7X_VMEM_LIMIT_BYTES),
        name="rmsnorm",
    )(x2d, gain.reshape(1, d))


def _mm_kernel(*refs, nk, act, has_res, emit_w):
    refs = list(refs)
    a_ref, w_ref = refs[:2]
    r_ref = refs[2] if has_res else None
    o_ref = refs[2 + has_res]
    wb_ref = refs[3 + has_res] if emit_w else None
    acc_ref = refs[-1]
    k = pl.program_id(2)

    @pl.when(k == 0)
    def _():
        acc_ref[...] = jnp.zeros_like(acc_ref)

    w = w_ref[...].astype(BF16)
    if emit_w:
        wb_ref[...] = w
    acc_ref[...] += jnp.dot(a_ref[...].astype(BF16), w, preferred_element_type=F32)

    @pl.when(k == nk - 1)
    def _():
        acc = acc_ref[...]
        if act == "relu2":
            acc = jnp.square(jnp.maximum(acc, 0.0))
        if act == "glu":
            acc = r_ref[...].astype(F32) * jax.nn.sigmoid(acc)
        elif has_res:
            acc = acc + r_ref[...].astype(F32)
        o_ref[...] = acc.astype(o_ref.dtype)


def matmul(a, w, layer=None, *, n=None, res=None, act=None, out_dtype=F32, emit_bf16=False):
    m, kdim = a.shape
    n = w.shape[-1] if n is None else n
    tm = min(m, 1024)
    tn = 512 if n % 512 == 0 else (256 if n % 256 == 0 else 128)
    tk = min(kdim, 4096)
    assert m % tm == 0 and n % tn == 0 and kdim % tk == 0
    assert not emit_bf16 or m == tm
    nk = kdim // tk
    if layer is None:
        w_spec = pl.BlockSpec((tk, tn), lambda i, j, k: (k, j))
    else:
        w_spec = pl.BlockSpec((None, tk, tn), lambda i, j, k: (layer, k, j))
    in_specs = [pl.BlockSpec((tm, tk), lambda i, j, k: (i, k)), w_spec]
    args = [a, w]
    if res is not None:
        in_specs.append(pl.BlockSpec((tm, tn), lambda i, j, k: (i, j)))
        args.append(res)
    out_specs = pl.BlockSpec((tm, tn), lambda i, j, k: (i, j))
    out_shape = jax.ShapeDtypeStruct((m, n), out_dtype)
    if emit_bf16:
        out_specs = (out_specs, pl.BlockSpec((tk, tn), lambda i, j, k: (k, j)))
        out_shape = (out_shape, jax.ShapeDtypeStruct((kdim, n), BF16))
    return pl.pallas_call(
        functools.partial(_mm_kernel, nk=nk, act=act, has_res=res is not None, emit_w=emit_bf16),
        grid=(m // tm, n // tn, nk),
        in_specs=in_specs,
        out_specs=out_specs,
        out_shape=out_shape,
        scratch_shapes=[pltpu.VMEM((tm, tn), F32)],
        compiler_params=pltpu.CompilerParams(
            dimension_semantics=("parallel", "parallel", "arbitrary"),
            vmem_limit_bytes=V7X_VMEM_LIMIT_BYTES),
        name="matmul",
    )(*args)


def _compress_rows(x_ref, n_full, pe_ref, w1_ref, w2_ref):
    pieces = [x_ref[0, pl.ds(s, n_full, stride=CMP_STRIDE), :] for s in range(CMP_STRIDE)]
    x = jnp.concatenate(pieces, axis=1).astype(BF16)
    half = CMP_STRIDE * HEAD_DIM
    z0 = jnp.dot(x, w1_ref[:half, :], preferred_element_type=F32)
    z1 = jnp.dot(x, w1_ref[half:, :], preferred_element_type=F32)
    z1 = pltpu.roll(z1, n_full - 1, 0)
    pe = jnp.broadcast_to(pe_ref[...], (8, CMP_BLOCK * HEAD_DIM)).astype(BF16)
    h0 = jnp.dot(pe, w1_ref[...], preferred_element_type=F32)[0:1, :]
    h = _gelu_tanh(z0 + z1 + h0)
    return jnp.dot(h.astype(BF16), w2_ref[...], preferred_element_type=F32)


def _compress_prompt_kernel(xk_ref, xv_ref, pek_ref, w1k_ref, w2k_ref, pev_ref, w1v_ref, w2v_ref,
                            kc_ref, vc_ref, *, n_full):
    for x_ref, pe_ref, w1_ref, w2_ref, o_ref in ((xk_ref, pek_ref, w1k_ref, w2k_ref, kc_ref),
                                                  (xv_ref, pev_ref, w1v_ref, w2v_ref, vc_ref)):
        out = _compress_rows(x_ref, n_full, pe_ref, w1_ref, w2_ref).astype(o_ref.dtype)
        n_pad = o_ref.shape[2]
        o_ref[0, 0, :n_full, :] = out
        if n_pad > n_full:
            o_ref[0, 0, n_full:, :] = jnp.zeros((n_pad - n_full, HEAD_DIM), o_ref.dtype)


def compress_prompt(proj3, col_k, col_v, pe_k, w1_k, w2_k, pe_v, w1_v, w2_v):
    B, T, _ = proj3.shape
    n_full = T // CMP_STRIDE
    n_pad = _round_up(n_full, LANE)
    flat = CMP_BLOCK * HEAD_DIM
    wspec = [pl.BlockSpec((1, flat), lambda b, g: (0, 0)),
             pl.BlockSpec((flat, HEAD_DIM), lambda b, g: (0, 0)),
             pl.BlockSpec((HEAD_DIM, HEAD_DIM), lambda b, g: (0, 0))]
    out_sds = jax.ShapeDtypeStruct((B, NSA_KV_HEADS, n_pad, HEAD_DIM), BF16)
    ospec = pl.BlockSpec((1, 1, n_pad, HEAD_DIM), lambda b, g: (b, g, 0, 0))
    return pl.pallas_call(
        functools.partial(_compress_prompt_kernel, n_full=n_full),
        grid=(B, NSA_KV_HEADS),
        in_specs=[pl.BlockSpec((1, T, HEAD_DIM), lambda b, g: (b, 0, col_k // HEAD_DIM + g)),
                  pl.BlockSpec((1, T, HEAD_DIM), lambda b, g: (b, 0, col_v // HEAD_DIM + g))] + wspec + wspec,
        out_specs=(ospec, ospec),
        out_shape=(out_sds, out_sds),
        compiler_params=pltpu.CompilerParams(dimension_semantics=("parallel", "parallel"),
                                             vmem_limit_bytes=V7X_VMEM_LIMIT_BYTES),
        name="nsa_compress_prompt",
    )(proj3, proj3,
      pe_k.reshape(1, flat), w1_k.reshape(flat, HEAD_DIM).astype(BF16), w2_k.astype(BF16),
      pe_v.reshape(1, flat), w1_v.reshape(flat, HEAD_DIM).astype(BF16), w2_v.astype(BF16))


def _bucket_table():
    d = np.arange(MAX_DISTANCE)
    max_exact = N_BUCKETS // 2
    large = max_exact + (np.log(np.maximum(d, 1).astype(np.float32) / np.float32(max_exact))
                         / np.float32(math.log(MAX_DISTANCE / max_exact))
                         * np.float32(N_BUCKETS - max_exact)).astype(np.int32)
    return np.where(d < max_exact, d, np.minimum(large, N_BUCKETS - 1)).astype(np.int32)


def _bias_of_distance(rel_bias, dist):
    bt = _bucket_table()
    buckets = bt[np.clip(dist, 0, MAX_DISTANCE - 1)]
    b = jnp.moveaxis(rel_bias.astype(F32)[buckets], -1, 0)
    return jnp.where(jnp.asarray(dist >= 0)[None], b, NEG)


def _bias_by_distance(rel_bias):
    return rel_bias.astype(F32)[_bucket_table()].T


def _toeplitz(v, n):
    h = v.shape[0]
    w = jnp.pad(v, ((0, 0), (0, 1)))
    m = jnp.tile(w, (1, n))[:, :n * (2 * n - 1)].reshape(h, n, 2 * n - 1)
    return m[:, :, n - 1:]


ATT_TILE = 128
N_BIAS_TILES = 4
SLC_CLASS_TILES = 2
MXU_TILES = 2


def _rows_softmax_pv(s_tiles, v_rows):
    m = s_tiles[0]
    for s in s_tiles[1:]:
        m = jnp.maximum(m, s)
    m = jnp.max(m, axis=-1, keepdims=True)
    l = None
    o = None
    for i in range(0, len(s_tiles), MXU_TILES):
        ps = [jnp.exp(s - m) for s in s_tiles[i:i + MXU_TILES]]
        for p in ps:
            l = p if l is None else l + p
        p_cat = ps[0] if len(ps) == 1 else jnp.concatenate(ps, axis=1)
        pv = jnp.dot(p_cat.astype(BF16), v_rows(i, len(ps)), preferred_element_type=F32)
        o = pv if o is None else o + pv
    return o / jnp.sum(l, axis=-1, keepdims=True)


def _score_tiles(q, k_rows, n_tiles):
    tiles = []
    for i in range(0, n_tiles, MXU_TILES):
        n = min(MXU_TILES, n_tiles - i)
        s = _dot_nt(q, k_rows(i, n))
        tiles += [s[:, j * LANE:(j + 1) * LANE] for j in range(n)]
    return tiles


def _nsa_prompt_kernel(q_ref, kc_ref, vc_ref, ks_ref, vs_ref, kw_ref, vw_ref, gate_ref,
                       bcmp_ref, btile_ref, ovl_ref, pick_ref, cvec_ref, kaug_s_ref, kaug_w_ref, o_ref,
                       ksb, vsb, kwb, vwb, s_ref, *, n_sel, n_top, T):
    tq = ATT_TILE
    J = NSA_HPG
    qi = pl.program_id(2)
    n_sel_pad = pick_ref.shape[0]
    nq = T // tq

    @pl.when(qi == 0)
    def _():
        ksb[:, :HEAD_DIM] = ks_ref[0].astype(BF16)
        ksb[:, HEAD_DIM:] = kaug_s_ref[...]
        vsb[...] = vs_ref[0].astype(BF16)
        kwb[:WINDOW, :HEAD_DIM] = jnp.zeros((WINDOW, HEAD_DIM), BF16)
        kwb[WINDOW:, :HEAD_DIM] = kw_ref[0].astype(BF16)
        kwb[:, HEAD_DIM:] = kaug_w_ref[...]
        vwb[:WINDOW, :] = jnp.zeros((WINDOW, HEAD_DIM), BF16)
        vwb[WINDOW:, :] = vw_ref[0].astype(BF16)

    def stack(f):
        return jnp.concatenate([f(j) for j in range(J)], axis=0)

    def put(vals, first):
        for j in range(J):
            cols = slice(j * HEAD_DIM, (j + 1) * HEAD_DIM)
            v = vals[j * tq:(j + 1) * tq]
            o_ref[0, :, cols] = v if first else o_ref[0, :, cols] + v

    gates = jax.nn.sigmoid(gate_ref[0, 0])
    gate = lambda c: stack(lambda j: gates[:, 3 * j + c:3 * j + c + 1])
    q_all = stack(lambda j: q_ref[0, :, j * HEAD_DIM:(j + 1) * HEAD_DIM] * HEAD_DIM ** -0.5).astype(BF16)

    row = lax.broadcasted_iota(jnp.int32, (tq, LANE), 0)
    lane = lax.broadcasted_iota(jnp.int32, (tq, LANE), 1)
    valid1 = qi * tq + row >= lane * CMP_STRIDE + (CMP_BLOCK - 1)
    valid = stack(lambda j: valid1)
    s = jnp.where(valid, _dot_nt(q_all, kc_ref[0, 0]) + stack(lambda j: bcmp_ref[j]), NEG)
    e = jnp.where(valid, jnp.exp(s - jnp.max(s, axis=-1, keepdims=True)), 0.0)
    den = jnp.sum(e, axis=-1, keepdims=True)
    p = e / jnp.where(den > 0.0, den, 1.0)
    put(gate(0) * jnp.dot(p.astype(BF16), vc_ref[0, 0], preferred_element_type=F32), True)
    psum = p[0:tq]
    for j in range(1, J):
        psum = psum + p[j * tq:(j + 1) * tq]

    bt = lambda i: stack(lambda j: btile_ref[j, i])

    pad_flag = jnp.broadcast_to(-cvec_ref[...], (tq, LANE)).astype(BF16)
    q_win = jnp.concatenate([q_all, stack(lambda j: pad_flag)], axis=1)
    n_w = WINDOW // tq + 1
    win_rows = lambda i, n: pl.ds(pl.multiple_of((qi + i) * tq, tq), n * tq)
    s_tiles = _score_tiles(q_win, lambda i, n: kwb[win_rows(i, n), :], n_w)
    s_tiles[0] = s_tiles[0] + bt(3)
    s_tiles[n_w - 2] = s_tiles[n_w - 2] + bt(1)
    s_tiles[n_w - 1] = s_tiles[n_w - 1] + bt(0)
    put(gate(2) * _rows_softmax_pv(s_tiles, lambda i, n: vwb[win_rows(i, n), :]), False)

    hi = psum.astype(BF16)
    r1 = psum - hi.astype(F32)
    mid = r1.astype(BF16)
    lo = (r1 - mid.astype(F32)).astype(BF16)
    ovl = ovl_ref[...]
    p_sel = _dot_nt(ovl, hi) + _dot_nt(ovl, mid) + _dot_nt(ovl, lo)
    blk = lax.broadcasted_iota(jnp.int32, (n_sel_pad, tq), 0)
    cur = (qi * tq + lax.broadcasted_iota(jnp.int32, (n_sel_pad, tq), 1)) // SEL_BLOCK
    forced = (blk == 0) | (blk == cur) | (blk == cur - 1)
    score = jnp.where(blk <= cur, p_sel + jnp.where(forced, FORCE_SCORE, 0.0), NEG)
    rank = jnp.zeros((n_sel_pad, tq), F32)
    for jb in range(n_sel):
        other = score[jb:jb + 1, :]
        beats = (other > score) | ((other == score) & (blk > jb))
        rank = rank + jnp.where(beats, 1.0, 0.0)
    sel = jnp.where((rank < n_top) & (blk <= cur) & (blk < n_sel), 1.0, 0.0).astype(BF16)
    aug = (_dot_tn(sel, pick_ref[...]) - cvec_ref[...]).astype(BF16)
    q_aug = jnp.concatenate([q_all, stack(lambda j: aug)], axis=1)

    for cls in range(-(-nq // SLC_CLASS_TILES)):
        n_t = min((cls + 1) * SLC_CLASS_TILES, nq)

        @pl.when(qi // SLC_CLASS_TILES == cls)
        def _(n_t=n_t):
            for kt, s in enumerate(_score_tiles(q_aug, lambda i, n: ksb[i * tq:(i + n) * tq, :], n_t)):
                s_ref[kt] = s
            s_ref[qi] = s_ref[qi] + bt(0)

            @pl.when(qi > 0)
            def _():
                s_ref[qi - 1] = s_ref[qi - 1] + bt(1)

            o = _rows_softmax_pv([s_ref[kt] for kt in range(n_t)], lambda i, n: vsb[i * tq:(i + n) * tq, :])
            put(gate(1) * o, False)


def nsa_prompt(proj3, cols, ng, kcmp, vcmp, rel_bias):
    B, T, _ = proj3.shape
    tq = ATT_TILE
    assert T % tq == 0 and T % SEL_BLOCK == 0 and WINDOW % tq == 0
    G, J = NSA_KV_HEADS, NSA_HPG
    nq = T // tq
    n_sel = T // SEL_BLOCK
    n_top = min(SEL_TOP, n_sel)
    n_sel_pad = _round_up(n_sel, 16)
    assert n_sel_pad < LANE
    n_pad = kcmp.shape[2]
    assert n_pad == LANE, "one lane tile of compressed blocks"

    bd = _bias_by_distance(rel_bias)
    n_a = T // CMP_STRIDE
    assert n_a == n_pad
    d_cmp = (CMP_STRIDE * ((n_a - 1) - np.arange(2 * n_a - 1))[None, :]
             + np.arange(CMP_STRIDE)[:, None] - (CMP_BLOCK - 1))
    gen_cmp = jnp.where(jnp.asarray(d_cmp >= 0)[None], bd[:, np.clip(d_cmp, 0, MAX_DISTANCE - 1)], 0.0)
    bcmp = _toeplitz(gen_cmp.reshape(NSA_HEADS * CMP_STRIDE, 2 * n_a - 1), n_a)
    bcmp = bcmp.reshape(NSA_HEADS, CMP_STRIDE, n_a, n_a).transpose(0, 2, 1, 3).reshape(NSA_HEADS, T, n_pad)
    d_diag = (tq - 1) - np.arange(2 * tq - 1)
    rel = bd - bd[:, -1:]
    gen = lambda d: jnp.where(jnp.asarray(d >= 0)[None], rel[:, np.clip(d, 0, MAX_DISTANCE - 1)], NEG)
    edge = jnp.broadcast_to(jnp.where(jnp.asarray(d_diag <= 0), 0.0, NEG)[None], (NSA_HEADS, 2 * tq - 1))
    btile = jnp.stack([_toeplitz(gen(d_diag), tq), _toeplitz(gen(d_diag + tq), tq),
                       jnp.zeros((NSA_HEADS, tq, tq), F32), _toeplitz(edge, tq)], axis=1)
    cmp_start = np.arange(n_pad) * CMP_STRIDE
    sel_start = np.arange(n_sel_pad) * SEL_BLOCK
    ovl = ((cmp_start[None, :] < sel_start[:, None] + SEL_BLOCK)
           & (cmp_start[None, :] + CMP_BLOCK > sel_start[:, None])
           & (np.arange(n_pad)[None, :] < T // CMP_STRIDE - 1))
    ovl = jnp.asarray(ovl, BF16)
    pick = jnp.asarray(np.arange(LANE)[None, :] == np.arange(n_sel_pad)[:, None], BF16)
    lane_i = np.arange(LANE)
    cvec = jnp.asarray(((lane_i < n_sel) | (lane_i == n_sel_pad))[None, :], F32)
    big = -NEG
    kaug_s = jnp.asarray(np.where(np.arange(T)[:, None] // SEL_BLOCK == lane_i[None, :], big, 0.0), BF16)
    kaug_w = jnp.asarray(np.where((np.arange(T + WINDOW)[:, None] < WINDOW) & (lane_i[None, :] == n_sel_pad),
                                  big, 0.0), BF16)
    gates = ng.reshape(B, T, G, 3 * J).transpose(0, 2, 1, 3)

    kv_spec = lambda name: pl.BlockSpec((1, T, HEAD_DIM),
                                        lambda b, g, i, o=cols[name] // HEAD_DIM: (b, 0, o + g))
    cmp_spec = pl.BlockSpec((1, 1, n_pad, HEAD_DIM), lambda b, g, i: (b, g, 0, 0))
    const2 = lambda shape: pl.BlockSpec(shape, lambda b, g, i: (0, 0))
    return pl.pallas_call(
        functools.partial(_nsa_prompt_kernel, n_sel=n_sel, n_top=n_top, T=T),
        grid=(B, G, nq),
        in_specs=[pl.BlockSpec((1, tq, J * HEAD_DIM),
                               lambda b, g, i, o=cols['nq'] // (J * HEAD_DIM): (b, i, o + g)),
                  cmp_spec, cmp_spec,
                  kv_spec('ks'), kv_spec('vs'), kv_spec('kw'), kv_spec('vw'),
                  pl.BlockSpec((1, 1, tq, 3 * J), lambda b, g, i: (b, g, i, 0)),
                  pl.BlockSpec((J, tq, n_pad), lambda b, g, i: (g, i, 0)),
                  pl.BlockSpec((J, N_BIAS_TILES, tq, tq), lambda b, g, i: (g, 0, 0, 0)),
                  const2((n_sel_pad, n_pad)), const2((n_sel_pad, LANE)), const2((1, LANE)),
                  const2((T, LANE)), const2((T + WINDOW, LANE))],
        out_specs=pl.BlockSpec((1, tq, J * HEAD_DIM), lambda b, g, i: (b, i, g)),
        out_shape=jax.ShapeDtypeStruct((B, T, NSA_WIDTH), F32),
        scratch_shapes=[pltpu.VMEM((T, 2 * HEAD_DIM), BF16), pltpu.VMEM((T, HEAD_DIM), BF16),
                        pltpu.VMEM((T + WINDOW, 2 * HEAD_DIM), BF16), pltpu.VMEM((T + WINDOW, HEAD_DIM), BF16),
                        pltpu.VMEM((nq, J * tq, tq), F32)],
        compiler_params=pltpu.CompilerParams(
            dimension_semantics=("parallel", "parallel", "arbitrary"),
            vmem_limit_bytes=V7X_VMEM_LIMIT_BYTES),
        name="nsa_prompt",
    )(proj3, kcmp, vcmp, proj3, proj3, proj3, proj3, gates, bcmp, btile, ovl, pick, cvec, kaug_s, kaug_w)


PAGE_ROWS = PAGE_SIZE * NSA_KV_HEADS
STRIDES_PER_PAGE = PAGE_SIZE // CMP_STRIDE
PAGES_PER_STEP = 16


def _cache_rows(cache):
    return cache.reshape(-1, HEAD_DIM)


def _page_specs(layer):
    return [pl.BlockSpec((2 * PAGE_ROWS, HEAD_DIM),
                         lambda b, p, pt, k=k: (pt[b, p * PAGES_PER_STEP + k] * DEPTH + layer, 0))
            for k in range(PAGES_PER_STEP)]


def _cmp_partial_kernel(pt_ref, *refs):
    page_refs = refs[:PAGES_PER_STEP]
    w1k_ref, w1v_ref, z_ref = refs[PAGES_PER_STEP:]
    G = NSA_KV_HEADS
    half = CMP_STRIDE * HEAD_DIM
    n_rows = PAGES_PER_STEP * STRIDES_PER_PAGE
    for kv, w1_ref in ((0, w1k_ref), (1, w1v_ref)):
        slabs = [jnp.transpose(r[kv * PAGE_ROWS:(kv + 1) * PAGE_ROWS, :].reshape(
            STRIDES_PER_PAGE, CMP_STRIDE * G, HEAD_DIM), (1, 0, 2)) for r in page_refs]
        xs = []
        for g in range(G):
            for slab in slabs:
                xs.append(jnp.concatenate([slab[s * G + g] for s in range(CMP_STRIDE)], axis=1))
        x = jnp.concatenate(xs, axis=0).astype(BF16)
        for j in range(CMP_BLOCK // CMP_STRIDE):
            z = jnp.dot(x, w1_ref[j * half:(j + 1) * half, :], preferred_element_type=F32)
            for g in range(G):
                z_ref[0, kv, j, g] = z[g * n_rows:(g + 1) * n_rows]


def cmp_partial(cache_rows, page_table, layer, w1_k, w1_v):
    B, n_pages = page_table.shape
    flat = CMP_BLOCK * HEAD_DIM
    n_str = n_pages * STRIDES_PER_PAGE
    assert n_pages % PAGES_PER_STEP == 0
    wspec = pl.BlockSpec((flat, HEAD_DIM), lambda b, p, pt: (0, 0))
    return pl.pallas_call(
        _cmp_partial_kernel,
        grid_spec=pltpu.PrefetchScalarGridSpec(
            num_scalar_prefetch=1,
            grid=(B, n_pages // PAGES_PER_STEP),
            in_specs=_page_specs(layer) + [wspec, wspec],
            out_specs=pl.BlockSpec((1, 2, 2, NSA_KV_HEADS, PAGES_PER_STEP * STRIDES_PER_PAGE, HEAD_DIM),
                                   lambda b, p, pt: (b, 0, 0, 0, p, 0))),
        out_shape=jax.ShapeDtypeStruct((B, 2, 2, NSA_KV_HEADS, n_str, HEAD_DIM), F32),
        compiler_params=pltpu.CompilerParams(dimension_semantics=("parallel", "arbitrary"),
                                             vmem_limit_bytes=V7X_VMEM_LIMIT_BYTES),
        name="nsa_cmp_partial",
    )(page_table, *([cache_rows] * PAGES_PER_STEP),
      w1_k.reshape(flat, HEAD_DIM).astype(BF16), w1_v.reshape(flat, HEAD_DIM).astype(BF16))


def _cmp_attn_sample_kernel(z_ref, pek_ref, w1k_ref, w2k_ref, pev_ref, w1v_ref, w2v_ref, q_ref, bias_ref,
                            ovl_ref, o_ref, psel_ref, *, q0):
    T = q_ref.shape[1]
    J = NSA_HPG
    n_str = z_ref.shape[4]

    def finish(kv, pe_ref, w1_ref, w2_ref):
        z1 = pltpu.roll(z_ref[0, kv, 1, 0], n_str - 1, 0)
        pe = jnp.broadcast_to(pe_ref[...], (PACK_ROWS, CMP_BLOCK * HEAD_DIM)).astype(BF16)
        h0 = jnp.dot(pe, w1_ref[...], preferred_element_type=F32)[0:1, :]
        h = _gelu_tanh(z_ref[0, kv, 0, 0] + z1 + h0)
        return jnp.dot(h.astype(BF16), w2_ref[...], preferred_element_type=F32).astype(BF16)

    kc = finish(0, pek_ref, w1k_ref, w2k_ref)
    vc = finish(1, pev_ref, w1v_ref, w2v_ref)
    stack = lambda f: jnp.concatenate([f(j) for j in range(J)], axis=0)
    q_all = stack(lambda j: q_ref[0, :, j * HEAD_DIM:(j + 1) * HEAD_DIM] * HEAD_DIM ** -0.5).astype(BF16)
    row = lax.broadcasted_iota(jnp.int32, (T, n_str), 0)
    lane = lax.broadcasted_iota(jnp.int32, (T, n_str), 1)
    valid1 = (q0 + row >= lane * CMP_STRIDE + (CMP_BLOCK - 1)) & (lane < n_str - 1)
    valid = stack(lambda j: valid1)
    s = jnp.where(valid, _dot_nt(q_all, kc) + stack(lambda j: bias_ref[j]), NEG)
    e = jnp.where(valid, jnp.exp(s - jnp.max(s, axis=-1, keepdims=True)), 0.0)
    den = jnp.sum(e, axis=-1, keepdims=True)
    p = e / jnp.where(den > 0.0, den, 1.0)
    o_ref[0, 0] = jnp.dot(p.astype(BF16), vc, preferred_element_type=F32)
    psum = p[0:T]
    for j in range(1, J):
        psum = psum + p[j * T:(j + 1) * T]
    psum = _pad_rows(psum, PACK_ROWS)
    hi = psum.astype(BF16)
    r1 = psum - hi.astype(F32)
    mid = r1.astype(BF16)
    lo = (r1 - mid.astype(F32)).astype(BF16)
    ovl = ovl_ref[...]
    p_sel = (jnp.dot(hi, ovl, preferred_element_type=F32) + jnp.dot(mid, ovl, preferred_element_type=F32)
             + jnp.dot(lo, ovl, preferred_element_type=F32))
    psel_ref[0, 0] = p_sel[:T]


def _rank_select_kernel(psel_ref, tpos_ref, out_ref, score_ref, rank_ref, *, n_sel, n_top):
    shape = psel_ref.shape
    blk = lax.broadcasted_iota(jnp.int32, shape, 0)
    cur = jnp.broadcast_to(tpos_ref[...], shape) // SEL_BLOCK
    forced = (blk == 0) | (blk == cur) | (blk == cur - 1)
    ok = (blk <= cur) & (blk < n_sel)
    score = jnp.where(ok, psel_ref[...] + jnp.where(forced, FORCE_SCORE, 0.0), NEG)
    score_ref[...] = score
    rank_ref[...] = jnp.zeros(shape, F32)

    def body(jb, c):
        other = jnp.broadcast_to(score_ref[pl.ds(jb, 1), :], shape)
        beats = (other > score) | ((other == score) & (blk > jb))
        rank_ref[...] = rank_ref[...] + jnp.where(beats, 1.0, 0.0)
        return c

    lax.fori_loop(0, n_sel, body, 0)
    out_ref[...] = jnp.where((rank_ref[...] < n_top) & ok, 0.0, NEG)


def _slc_win_sample_kernel(pt_ref, *refs, win_buf):
    page_refs = refs[:PAGES_PER_STEP]
    (qT_ref, seladd_ref, blast_ref, selnew_ref, ksn_ref, vsn_ref, kwn_ref, vwn_ref, bnew_ref, cw_ref, bwin_ref,
     ocmp_ref, gate_ref, o_ref, m_ref, l_ref, acc_ref) = refs[PAGES_PER_STEP:]
    G = NSA_KV_HEADS
    p = pl.program_id(1)
    last = pl.num_programs(1) - 1
    lane_group = lax.broadcasted_iota(jnp.int32, (1, LANE), 1) // (LANE // G)

    @pl.when(p == 0)
    def _():
        m_ref[...] = jnp.full(m_ref.shape, NEG, F32)
        l_ref[...] = jnp.zeros(l_ref.shape, F32)
        acc_ref[...] = jnp.zeros(acc_ref.shape, F32)

    def scores(k_of_g):
        s = None
        for g in range(G):
            sg = jnp.dot(k_of_g(g).astype(BF16), qT_ref[0, g], preferred_element_type=F32)
            s = sg if s is None else s + sg
        return s

    def weighted_values(pT, v_of_g):
        o = None
        for g in range(G):
            pg = jnp.where(lane_group == g, pT, 0.0).astype(BF16)
            og = _dot_tn(v_of_g(g).astype(BF16), pg)
            o = og if o is None else o + og
        return o

    def accumulate(segments):
        m_old = m_ref[0:1, :]
        m_new = m_old
        for sT, _ in segments:
            m_new = jnp.maximum(m_new, jnp.max(sT, axis=0, keepdims=True))
        alpha = jnp.exp(m_old - m_new)
        l_new = alpha * l_ref[0:1, :]
        acc = alpha * acc_ref[...]
        for sT, v_of_g in segments:
            pT = jnp.exp(sT - m_new)
            l_new = l_new + jnp.sum(pT, axis=0, keepdims=True)
            acc = acc + weighted_values(pT, v_of_g)
        l_ref[...] = jnp.broadcast_to(l_new, l_ref.shape)
        acc_ref[...] = acc
        m_ref[...] = jnp.broadcast_to(m_new, m_ref.shape)

    key = lax.broadcasted_iota(jnp.int32, (PAGE_SIZE, LANE), 0)
    blocks_per_page = PAGE_SIZE // SEL_BLOCK
    is_last = (p == last).astype(F32)
    segments = []
    for k, page_ref in enumerate(page_refs):
        k_page = lambda g, r=page_ref: r[pl.ds(g, PAGE_SIZE, stride=G), :]
        v_page = lambda g, r=page_ref: r[pl.ds(PAGE_ROWS + g, PAGE_SIZE, stride=G), :]
        mask = seladd_ref[0, k, blocks_per_page - 1:blocks_per_page, :]
        for i in range(blocks_per_page - 2, -1, -1):
            mask = jnp.where(key < (i + 1) * SEL_BLOCK, seladd_ref[0, k, i:i + 1, :], mask)
        sT = scores(k_page) + mask
        if k == PAGES_PER_STEP - 1:
            sT = sT + blast_ref[...] * is_last
        segments.append((sT, v_page))
    accumulate(segments)

    @pl.when(p == last)
    def _():
        new = lambda ref: (lambda g: _pad_rows(ref[0, :, g * HEAD_DIM:(g + 1) * HEAD_DIM], PACK_ROWS))
        accumulate([(scores(new(ksn_ref)) + bnew_ref[...] + selnew_ref[0], new(vsn_ref))])
        o_slc = (acc_ref[...] / l_ref[0:1, :]).T

        k_win = lambda g: cw_ref[pl.ds(g, win_buf, stride=G), :]
        v_win = lambda g: cw_ref[pl.ds(win_buf * G + g, win_buf, stride=G), :]
        s_w = scores(k_win) + bwin_ref[...]
        s_n = scores(new(kwn_ref)) + bnew_ref[...]
        m = jnp.maximum(jnp.max(s_w, axis=0, keepdims=True), jnp.max(s_n, axis=0, keepdims=True))
        p_w = jnp.exp(s_w - m)
        p_n = jnp.exp(s_n - m)
        den = jnp.sum(p_w, axis=0, keepdims=True) + jnp.sum(p_n, axis=0, keepdims=True)
        o_win = ((weighted_values(p_w, v_win) + weighted_values(p_n, new(vwn_ref))) / den).T

        gates = jax.nn.sigmoid(gate_ref[0])
        o_ref[0] = gates[:, 0:1] * ocmp_ref[0] + gates[:, 1:2] * o_slc + gates[:, 2:3] * o_win


def nsa_sample(proj3, cols, ng, cache_cmp, cache_slc, cache_win, page_table, layer, rel_bias,
               pe_k, w1_k, w2_k, pe_v, w1_v, w2_v):
    B, T, _ = proj3.shape
    G, J, H = NSA_KV_HEADS, NSA_HPG, NSA_HEADS
    n_pages = page_table.shape[1]
    past = n_pages * PAGE_SIZE
    q0 = past
    win_buf = cache_win.shape[3]
    L = G * J * T
    assert L == LANE and T <= PACK_ROWS and T < CMP_STRIDE and past % SEL_BLOCK == 0 and T <= SEL_BLOCK
    assert win_buf == min(WINDOW, past)
    n_str = past // CMP_STRIDE
    n_sel = past // SEL_BLOCK + 1
    n_top = min(SEL_TOP, n_sel)
    n_sel_rows = _round_up(n_sel, 8)
    n_sel_lanes = _round_up(n_sel, LANE)
    flat = CMP_BLOCK * HEAD_DIM
    b_far = rel_bias.astype(F32)[_bucket_table()[MAX_DISTANCE - 1]]

    def lane_bias(dist, ok):
        b = _bias_of_distance(rel_bias, np.maximum(dist, 0)) - b_far[:, None, None]
        b = jnp.where(jnp.asarray(ok)[None], b, NEG)
        return jnp.moveaxis(b.reshape(G, J, dist.shape[0], T), 2, 0).reshape(dist.shape[0], L)

    ti = np.arange(T)[None, :]
    z = cmp_partial(_cache_rows(cache_cmp), page_table, layer, w1_k, w1_v)
    n = np.arange(n_str)[None, :]
    dist_c = q0 + np.arange(T)[:, None] - (n * CMP_STRIDE + CMP_BLOCK - 1)
    near = dist_c.min(axis=0) < MAX_DISTANCE
    n_far = int(np.argmax(near)) if near.any() else n_str
    bcmp = jnp.concatenate([jnp.broadcast_to(b_far[:, None, None], (H, T, n_far)),
                            _bias_of_distance(rel_bias, np.maximum(dist_c[:, n_far:], 0))], axis=2)
    cmp_start = np.arange(n_str) * CMP_STRIDE
    sel_start = np.arange(n_sel_lanes) * SEL_BLOCK
    ovl = ((cmp_start[:, None] < sel_start[None, :] + SEL_BLOCK) & (cmp_start[:, None] + CMP_BLOCK > sel_start[None, :])
           & (np.arange(n_str)[:, None] < n_str - 1) & (np.arange(n_sel_lanes)[None, :] < n_sel))
    wspec = [pl.BlockSpec((1, flat), lambda b, g: (0, 0)),
             pl.BlockSpec((flat, HEAD_DIM), lambda b, g: (0, 0)),
             pl.BlockSpec((HEAD_DIM, HEAD_DIM), lambda b, g: (0, 0))]
    o_cmp, p_sel = pl.pallas_call(
        functools.partial(_cmp_attn_sample_kernel, q0=q0),
        grid=(B, G),
        in_specs=[pl.BlockSpec((1, 2, 2, 1, n_str, HEAD_DIM), lambda b, g: (b, 0, 0, g, 0, 0))] + wspec + wspec + [
            pl.BlockSpec((1, T, J * HEAD_DIM), lambda b, g, o=cols['nq'] // (J * HEAD_DIM): (b, 0, o + g)),
            pl.BlockSpec((J, T, n_str), lambda b, g: (g, 0, 0)),
            pl.BlockSpec((n_str, n_sel_lanes), lambda b, g: (0, 0))],
        out_specs=(pl.BlockSpec((1, 1, J * T, HEAD_DIM), lambda b, g: (b, g, 0, 0)),
                   pl.BlockSpec((1, 1, T, n_sel_lanes), lambda b, g: (b, g, 0, 0))),
        out_shape=(jax.ShapeDtypeStruct((B, G, J * T, HEAD_DIM), F32),
                   jax.ShapeDtypeStruct((B, G, T, n_sel_lanes), F32)),
        compiler_params=pltpu.CompilerParams(dimension_semantics=("parallel", "parallel"),
                                             vmem_limit_bytes=V7X_VMEM_LIMIT_BYTES),
        name="nsa_cmp_attn_sample",
    )(z, pe_k.reshape(1, flat), w1_k.reshape(flat, HEAD_DIM).astype(BF16), w2_k.astype(BF16),
      pe_v.reshape(1, flat), w1_v.reshape(flat, HEAD_DIM).astype(BF16), w2_v.astype(BF16),
      proj3, bcmp, jnp.asarray(ovl, BF16))

    n_bgt = B * G * T
    psel_t = p_sel.reshape(n_bgt, n_sel_lanes)[:, :n_sel_rows].T
    tpos = jnp.asarray(np.tile(q0 + np.arange(T), B * G)[None, :], jnp.int32)
    seladd = pl.pallas_call(
        functools.partial(_rank_select_kernel, n_sel=n_sel, n_top=n_top),
        out_shape=jax.ShapeDtypeStruct((n_sel_rows, n_bgt), F32),
        scratch_shapes=[pltpu.VMEM((n_sel_rows, n_bgt), F32), pltpu.VMEM((n_sel_rows, n_bgt), F32)],
        name="nsa_rank_select",
    )(psel_t, tpos)
    seladd = seladd.T.reshape(B, G, 1, T, n_sel_rows)
    seladd = jnp.broadcast_to(seladd, (B, G, J, T, n_sel_rows)).reshape(B, L, n_sel_rows)
    bpp = PAGE_SIZE // SEL_BLOCK
    sel_past = seladd[:, :, :n_sel - 1].reshape(B, L, n_pages, bpp).transpose(0, 2, 3, 1)
    sel_new = seladd[:, :, n_sel - 1].reshape(B, 1, L)

    q = proj3[:, :, cols['nq']:cols['nq'] + NSA_WIDTH].reshape(B, T, G, J, HEAD_DIM) * HEAD_DIM ** -0.5
    q_t = q.transpose(0, 2, 4, 3, 1).reshape(B, G, HEAD_DIM, J * T)
    place = jnp.asarray(np.arange(G)[:, None, None] == (np.arange(L) // (J * T))[None, None, :])
    q_pad = jnp.where(place[None], jnp.tile(q_t, (1, 1, 1, G)), 0.0).astype(BF16)
    ki = np.arange(PAGE_SIZE)[:, None]
    b_last = lane_bias(PAGE_SIZE + ti - ki, np.ones((PAGE_SIZE, T), bool))
    kn = np.arange(PACK_ROWS)[:, None]
    b_new = lane_bias(ti - kn, (ti - kn >= 0) & (kn < T))
    kw = np.arange(win_buf)[:, None]
    b_win = lane_bias(win_buf + ti - kw, win_buf + ti - kw <= WINDOW)
    gates = ng.reshape(B, T, G, J, 3).transpose(0, 2, 3, 1, 4).reshape(B, L, 3)
    win_rows = 2 * win_buf * G
    new_spec = lambda name: pl.BlockSpec((1, T, KV_WIDTH), lambda b, p, pt, o=cols[name] // KV_WIDTH: (b, 0, o))
    const = lambda shape: pl.BlockSpec(shape, lambda b, p, pt: (0, 0))
    per_b = lambda shape: pl.BlockSpec((1,) + shape, lambda b, p, pt: (b,) + (0,) * len(shape))
    out = pl.pallas_call(
        functools.partial(_slc_win_sample_kernel, win_buf=win_buf),
        grid_spec=pltpu.PrefetchScalarGridSpec(
            num_scalar_prefetch=1,
            grid=(B, n_pages // PAGES_PER_STEP),
            in_specs=_page_specs(layer) + [
                per_b((G, HEAD_DIM, L)),
                pl.BlockSpec((1, PAGES_PER_STEP, bpp, L), lambda b, p, pt: (b, p, 0, 0)),
                const((PAGE_SIZE, L)), per_b((1, L)),
                new_spec('ks'), new_spec('vs'), new_spec('kw'), new_spec('vw'),
                const((PACK_ROWS, L)),
                pl.BlockSpec((win_rows, HEAD_DIM), lambda b, p, pt: (layer * B + b, 0)),
                const((win_buf, L)), per_b((L, HEAD_DIM)), per_b((L, 3))],
            out_specs=per_b((L, HEAD_DIM)),
            scratch_shapes=[pltpu.VMEM((8, L), F32), pltpu.VMEM((8, L), F32), pltpu.VMEM((HEAD_DIM, L), F32)]),
        out_shape=jax.ShapeDtypeStruct((B, L, HEAD_DIM), F32),
        compiler_params=pltpu.CompilerParams(dimension_semantics=("parallel", "arbitrary"),
                                             vmem_limit_bytes=V7X_VMEM_LIMIT_BYTES),
        name="nsa_slc_win_sample",
    )(page_table, *([_cache_rows(cache_slc)] * PAGES_PER_STEP), q_pad, sel_past, b_last, sel_new,
      proj3, proj3, proj3, proj3, b_new,
      cache_win.reshape(-1, HEAD_DIM), b_win, o_cmp.reshape(B, L, HEAD_DIM), gates)
    return out.reshape(B, G, J, T, HEAD_DIM).transpose(0, 3, 1, 2, 4).reshape(B, T, NSA_WIDTH)


def _retention_tables(T, q0):
    c = _largest_divisor(T, RET_CHUNK)
    cp = max(c, RET_CHUNK)
    lg = np.log1p(-(2.0 ** (-5.0 - np.arange(RET_HEADS, dtype=np.float32)))).astype(np.float32)
    i = np.arange(cp)
    rel = i[:, None] - i[None, :]
    inside = (i < c)[:, None] & (i < c)[None, :]
    decay = np.where((rel >= 0) & inside, np.exp(np.maximum(rel, 0)[None] * lg[:, None, None]), 0.0)
    q_dec = np.broadcast_to(np.exp((i + 1)[None, :, None] * lg[:, None, None]), (RET_HEADS, cp, HEAD_DIM))
    k_dec = np.where((i < c)[None, :, None], np.exp((c - 1 - i)[None, :, None] * lg[:, None, None]), 0.0)
    k_dec = np.broadcast_to(k_dec, (RET_HEADS, cp, HEAD_DIM))
    chunk_dec = np.broadcast_to(np.exp(c * lg)[:, None, None], (RET_HEADS, 8, HEAD_DIM))
    half = HEAD_DIM // 2
    inv = (1.0 / (10000.0 ** np.linspace(0.0, 1.0, half, dtype=np.float32))).astype(np.float32)
    ang = (q0 + np.arange(T)).astype(np.float32)[:, None] * inv[None]
    cos, sin = np.cos(ang), np.sin(ang)
    cosf = np.concatenate([cos, cos], axis=1)
    sinf = np.concatenate([-sin, sin], axis=1)
    f = lambda a: jnp.asarray(a, F32)
    return c, cp, f(decay), f(q_dec), f(k_dec), f(chunk_dec), f(cosf), f(sinf)


RET_HEADS_PER_STEP = 4


def _retention_kernel(q_ref, k_ref, v_ref, g_ref, s0_ref, cos_ref, sin_ref, dec_ref, qd_ref, kd_ref, cd_ref,
                      o_ref, s_ref, *, c, cp, n):
    half = HEAD_DIM // 2
    heads = range(RET_HEADS_PER_STEP)

    def load(ref, rows):
        return _pad_rows(ref[rows, :], cp)

    def rot(x, cos, sin):
        return x * cos + pltpu.roll(x, half, 1) * sin

    def body(i, states):
        rows = pl.ds(pl.multiple_of(i * c, c), c)
        cos, sin = load(cos_ref, rows), load(sin_ref, rows)
        q_all, k_all, v_all = load(q_ref.at[0], rows), load(k_ref.at[0], rows), load(v_ref.at[0], rows)
        new_states = []
        for hh in heads:
            cols = slice(hh * HEAD_DIM, (hh + 1) * HEAD_DIM)
            s = states[hh]
            q = rot(q_all[:, cols], cos, sin)
            k = rot(k_all[:, cols], cos, sin) * HEAD_DIM ** -0.5
            v = v_all[:, cols].astype(BF16)
            qb = q.astype(BF16)
            inner = _dot_nt(qb, k.astype(BF16)) * dec_ref[hh]
            o = (jnp.dot(inner.astype(BF16), v, preferred_element_type=F32)
                 + jnp.dot(qb, s.astype(BF16), preferred_element_type=F32) * qd_ref[hh])
            new_states.append(s * cd_ref[hh, 0:1, :] + _dot_tn((k * kd_ref[hh]).astype(BF16), v))
            o = o * lax.rsqrt(jnp.mean(o * o, axis=-1, keepdims=True) + EPS)
            g = g_ref[0, rows, cols]
            o_ref[0, rows, cols] = g * jax.nn.sigmoid(g) * o[:c]
        return tuple(new_states)

    final = lax.fori_loop(0, n, body, tuple(s0_ref[0, hh] for hh in heads))
    for hh in heads:
        s_ref[0, hh] = final[hh]


def retention(proj3, cols, s0, q0):
    B, T, _ = proj3.shape
    c, cp, decay, q_dec, k_dec, chunk_dec, cosf, sinf = _retention_tables(T, q0)
    hp = RET_HEADS_PER_STEP
    wide = hp * HEAD_DIM
    col = lambda name: pl.BlockSpec((1, T, wide), lambda b, h, o=cols[name] // wide: (b, 0, o + h))
    tab = lambda r: pl.BlockSpec((hp, r, HEAD_DIM), lambda b, h: (h, 0, 0))
    full = pl.BlockSpec((T, HEAD_DIM), lambda b, h: (0, 0))
    state = pl.BlockSpec((1, hp, HEAD_DIM, HEAD_DIM), lambda b, h: (b, h, 0, 0))
    return pl.pallas_call(
        functools.partial(_retention_kernel, c=c, cp=cp, n=T // c),
        grid=(B, RET_HEADS // hp),
        in_specs=[col('rq'), col('rk'), col('rv'), col('rg'), state, full, full,
                  tab(cp), tab(cp), tab(cp), tab(8)],
        out_specs=(pl.BlockSpec((1, T, wide), lambda b, h: (b, 0, h)), state),
        out_shape=(jax.ShapeDtypeStruct((B, T, RET_WIDTH), F32),
                   jax.ShapeDtypeStruct((B, RET_HEADS, HEAD_DIM, HEAD_DIM), F32)),
        compiler_params=pltpu.CompilerParams(dimension_semantics=("parallel", "parallel"),
                                             vmem_limit_bytes=V7X_VMEM_LIMIT_BYTES),
        name="retention",
    )(proj3, proj3, proj3, proj3, s0.astype(F32), cosf, sinf, decay, q_dec, k_dec, chunk_dec)


S5_BLK_GROUPS = 8
S5_BLK_STATE = S5_BLK_GROUPS * S5_STATE
S5_BLK_CH = S5_BLK_GROUPS * S5_GROUP
S5_SCAN_ROWS = 8


def _s5_params(lam_re, lam_im, log_step, b_re, b_im, c_re, c_im):
    nb = S5_GROUPS // S5_BLK_GROUPS
    lam = lax.complex(lam_re.astype(F32), lam_im.astype(F32))
    step = jnp.exp(log_step.astype(F32))[:, None]
    a_bar = jnp.exp(lam * step)
    b_bar = ((a_bar - 1.0) / lam)[..., None] * lax.complex(b_re.astype(F32), b_im.astype(F32))
    r = np.arange(S5_SCAN_ROWS)

    def powers(k, keep):
        p = jnp.exp(lam[None] * step[None] * jnp.asarray(k, F32)[:, None, None])
        return jnp.where(jnp.asarray(keep)[:, None, None], p, 0.0)

    tabs = [powers(np.full(S5_SCAN_ROWS, k), r >= k) for k in (1, 2, 4)]
    tabs.append(powers(r + 1, r >= 0))
    tab = jnp.stack(tabs)
    tab = tab.reshape(4, S5_SCAN_ROWS, nb, S5_BLK_STATE).transpose(2, 0, 1, 3)
    atab = jnp.concatenate([tab.real, tab.imag], axis=1)

    eye = jnp.eye(S5_BLK_GROUPS, dtype=F32)
    bb = b_bar.reshape(nb, S5_BLK_GROUPS, S5_STATE, S5_GROUP)

    def in_mat(x):
        return jnp.einsum('ngpc,gh->ngchp', x, eye).reshape(nb, S5_BLK_CH, S5_BLK_STATE)

    bmat = jnp.concatenate([in_mat(bb.real), in_mat(bb.imag)], axis=-1).astype(BF16)
    cr = c_re.astype(F32).reshape(nb, S5_BLK_GROUPS, S5_GROUP, S5_STATE)
    ci = c_im.astype(F32).reshape(nb, S5_BLK_GROUPS, S5_GROUP, S5_STATE)

    def out_mat(x):
        return jnp.einsum('ngcp,gh->ngphc', x, eye).reshape(nb, S5_BLK_STATE, S5_BLK_CH)

    cmat = jnp.concatenate([out_mat(cr), -out_mat(ci)], axis=1).astype(BF16)
    return atab, bmat, cmat


def _s5_scan_tile(xr, xi, cr, ci, atab_ref):
    for idx, k in enumerate((1, 2, 4)):
        pr, pi = atab_ref[0, idx], atab_ref[0, 4 + idx]
        sr, si = pltpu.roll(xr, k, 0), pltpu.roll(xi, k, 0)
        xr, xi = xr + pr * sr - pi * si, xi + pr * si + pi * sr
    pr, pi = atab_ref[0, 3], atab_ref[0, 7]
    xr, xi = xr + pr * cr - pi * ci, xi + pr * ci + pi * cr
    last = S5_SCAN_ROWS - 1
    cr = jnp.broadcast_to(xr[last:last + 1, :], xr.shape)
    ci = jnp.broadcast_to(xi[last:last + 1, :], xi.shape)
    return xr, xi, cr, ci


def _s5_kernel(u_ref, x0_ref, atab_ref, b_ref, c_ref, d_ref, y_ref, st_ref, xs_ref, *, T):
    u = u_ref[0]
    t_pad = _round_up(T, PACK_ROWS)
    xs_ref[...] = jnp.dot(_pad_rows(u, t_pad).astype(BF16), b_ref[0], preferred_element_type=F32)[:T]
    n = S5_BLK_STATE
    R = S5_SCAN_ROWS

    def body(i, carry):
        cr, ci = carry
        rows = pl.ds(pl.multiple_of(i * R, R), R)
        xr, xi, cr, ci = _s5_scan_tile(xs_ref[rows, :n], xs_ref[rows, n:], cr, ci, atab_ref)
        xs_ref[rows, :n] = xr
        xs_ref[rows, n:] = xi
        return cr, ci

    x0 = x0_ref[0, 0]
    cr0 = jnp.broadcast_to(x0[0:1, :], (R, n))
    ci0 = jnp.broadcast_to(x0[1:2, :], (R, n))
    cr, ci = lax.fori_loop(0, T // R, body, (cr0, ci0))
    st_ref[0, 0] = jnp.concatenate([cr[0:1], ci[0:1]], axis=0)
    y = jnp.dot(_pad_rows(xs_ref[...], t_pad).astype(BF16), c_ref[0], preferred_element_type=F32)[:T]
    y_ref[0] = _gelu_tanh(y + d_ref[...] * u)


def s5_scan(proj3, cols, x0, lam_re, lam_im, log_step, b_re, b_im, c_re, c_im, d):
    B, T, _ = proj3.shape
    assert T % S5_SCAN_ROWS == 0
    nb = S5_GROUPS // S5_BLK_GROUPS
    atab, bmat, cmat = _s5_params(lam_re, lam_im, log_step, b_re, b_im, c_re, c_im)
    x0b = x0.astype(F32).reshape(B, nb, S5_BLK_STATE, 2).transpose(0, 1, 3, 2)
    blk3 = lambda shape: pl.BlockSpec((1,) + shape, lambda b, j: (j, 0, 0))
    y, st = pl.pallas_call(
        functools.partial(_s5_kernel, T=T),
        grid=(B, nb),
        in_specs=[pl.BlockSpec((1, T, S5_BLK_CH), lambda b, j, o=cols['su'] // S5_BLK_CH: (b, 0, o + j)),
                  pl.BlockSpec((1, 1, 2, S5_BLK_STATE), lambda b, j: (b, j, 0, 0)),
                  pl.BlockSpec((1, 8, S5_SCAN_ROWS, S5_BLK_STATE), lambda b, j: (j, 0, 0, 0)),
                  blk3((S5_BLK_CH, 2 * S5_BLK_STATE)), blk3((2 * S5_BLK_STATE, S5_BLK_CH)),
                  pl.BlockSpec((1, S5_BLK_CH), lambda b, j: (0, j))],
        out_specs=(pl.BlockSpec((1, T, S5_BLK_CH), lambda b, j: (b, 0, j)),
                   pl.BlockSpec((1, 1, 2, S5_BLK_STATE), lambda b, j: (b, j, 0, 0))),
        out_shape=(jax.ShapeDtypeStruct((B, T, S5_WIDTH), F32),
                   jax.ShapeDtypeStruct((B, nb, 2, S5_BLK_STATE), F32)),
        scratch_shapes=[pltpu.VMEM((T, 2 * S5_BLK_STATE), F32)],
        compiler_params=pltpu.CompilerParams(dimension_semantics=("parallel", "parallel"),
                                             vmem_limit_bytes=V7X_VMEM_LIMIT_BYTES),
        name="s5_scan",
    )(proj3, x0b, atab, bmat, cmat, d.astype(F32).reshape(1, S5_WIDTH))
    st = st.transpose(0, 1, 3, 2).reshape(B, S5_GROUPS, S5_STATE, 2)
    return y, st


def _branch_norm_kernel(ro_ref, so_ref, no_ref, bn_ref, o_ref):
    off = 0
    for ref in (ro_ref, so_ref, no_ref):
        x = ref[...]
        w = x.shape[-1]
        y = x * lax.rsqrt(jnp.mean(x * x, axis=-1, keepdims=True) + EPS) * bn_ref[:, off:off + w]
        o_ref[:, off:off + w] = y.astype(o_ref.dtype)
        off += w


def branch_norm(ro, so, no, bn):
    m = ro.shape[0]
    tm = min(m, 256)
    assert m % tm == 0
    spec = lambda w: pl.BlockSpec((tm, w), lambda i: (i, 0))
    return pl.pallas_call(
        _branch_norm_kernel,
        grid=(m // tm,),
        in_specs=[spec(RET_WIDTH), spec(S5_WIDTH), spec(NSA_WIDTH), pl.BlockSpec((1, D_MODEL), lambda i: (0, 0))],
        out_specs=spec(D_MODEL),
        out_shape=jax.ShapeDtypeStruct((m, D_MODEL), BF16),
        compiler_params=pltpu.CompilerParams(dimension_semantics=("parallel",),
                                             vmem_limit_bytes=V7X_VMEM_LIMIT_BYTES),
        name="branch_norm",
    )(ro, so, no, bn.astype(F32).reshape(1, D_MODEL))


def _kv_rows_kernel(x_ref, o_ref):
    x = x_ref[0]
    for g in range(NSA_KV_HEADS):
        o_ref[pl.ds(g, x.shape[0], stride=NSA_KV_HEADS), :] = x[:, g * HEAD_DIM:(g + 1) * HEAD_DIM]


def kv_rows(proj3, col_k, t_start, t_len):
    B = proj3.shape[0]
    tT = min(t_len, 512)
    assert t_len % tT == 0 and t_start % tT == 0 and col_k % KV_WIDTH == 0
    n = t_len // tT
    rows = pl.pallas_call(
        _kv_rows_kernel,
        grid=(B, 2, n),
        in_specs=[pl.BlockSpec((1, tT, KV_WIDTH), lambda b, kv, i: (b, t_start // tT + i, col_k // KV_WIDTH + kv))],
        out_specs=pl.BlockSpec((tT * NSA_KV_HEADS, HEAD_DIM), lambda b, kv, i: ((b * 2 + kv) * n + i, 0)),
        out_shape=jax.ShapeDtypeStruct((B * 2 * t_len * NSA_KV_HEADS, HEAD_DIM), F32),
        compiler_params=pltpu.CompilerParams(dimension_semantics=("parallel", "parallel", "parallel"),
                                             vmem_limit_bytes=V7X_VMEM_LIMIT_BYTES),
        name="kv_rows",
    )(proj3)
    return rows.reshape(B, 2, t_len, NSA_KV_HEADS, HEAD_DIM)


_COL_NAMES = ('rq', 'rk', 'rv', 'rg', 'su', 'nq', 'kc', 'vc', 'ks', 'vs', 'kw', 'vw', 'ng')
COLS = {name: int(off) for name, off in zip(_COL_NAMES, np.concatenate([[0], np.cumsum(IN_SPLITS)]))}


def _block(x, layer, w, rel_bias, past, win_buf, big):
    B, T, _ = x.shape
    G = NSA_KV_HEADS
    M = B * T
    emitted = {}

    def project(a, name, **kw):
        if big is not None:
            return matmul(a, big[name], **kw)
        out, emitted[name] = matmul(a, w[name], layer, emit_bf16=True, **kw)
        return out

    x2 = x.reshape(M, D_MODEL)
    h = rmsnorm(x2, w['norm_mix'][layer], BF16)
    proj3 = matmul(h, w['w_in'], layer, n=COLS['ng']).reshape(B, T, -1)
    ng = matmul(h, w['w_gate'], layer)[:, :3 * NSA_HEADS].reshape(B, T, 3 * NSA_HEADS)
    s5_w = [w[k][layer] for k in ('s5_lambda_re', 's5_lambda_im', 's5_log_step', 's5_b_re', 's5_b_im',
                                  's5_c_re', 's5_c_im', 's5_d')]
    cmp_w = [w[k][layer] for k in ('cmp_pe_k', 'cmp_w1_k', 'cmp_w2_k', 'cmp_pe_v', 'cmp_w1_v', 'cmp_w2_v')]
    if past is None:
        q0 = 0
        ret_s0 = jnp.zeros((B, RET_HEADS, HEAD_DIM, HEAD_DIM), F32)
        s5_s0 = jnp.zeros((B, S5_GROUPS, S5_STATE, 2), F32)
        win_prev = jnp.zeros((B, 2, WINDOW, G, HEAD_DIM), x.dtype)
    else:
        cache_cmp, cache_slc, cache_win, state_ret, state_s5, page_table = past
        q0 = page_table.shape[1] * PAGE_SIZE
        ret_s0, s5_s0, win_prev = state_ret[layer], state_s5[layer], cache_win[layer]

    ro, ret_s = retention(proj3, COLS, ret_s0, q0)
    sy, s5_s = s5_scan(proj3, COLS, s5_s0, *s5_w)
    if past is None:
        kcmp, vcmp = compress_prompt(proj3, COLS['kc'], COLS['vc'], *cmp_w)
        no = nsa_prompt(proj3, COLS, ng, kcmp, vcmp, rel_bias)
    else:
        no = nsa_sample(proj3, COLS, ng, cache_cmp, cache_slc, cache_win, page_table, layer, rel_bias, *cmp_w)

    sy2 = sy.reshape(M, S5_WIDTH)
    so = matmul(sy2, w['s5_w_glu'], layer, res=sy2, act="glu")
    mix = branch_norm(ro.reshape(M, RET_WIDTH), so, no.reshape(M, NSA_WIDTH), w['branch_norm'][layer])
    x2 = project(mix, 'w_out', res=x2)
    h = rmsnorm(x2, w['norm_ffn'][layer], BF16)
    up = project(h, 'w_up', act="relu2", out_dtype=BF16)
    x2 = project(up, 'w_down', res=x2)

    cmp_rows = kv_rows(proj3, COLS['kc'], 0, T)
    slc_rows = kv_rows(proj3, COLS['ks'], 0, T)
    if T >= win_buf:
        win_new = kv_rows(proj3, COLS['kw'], T - win_buf, win_buf)
    else:
        win_new = jnp.concatenate([win_prev, kv_rows(proj3, COLS['kw'], 0, T)], axis=2)[:, :, -win_buf:]
    return x2.reshape(B, T, D_MODEL), cmp_rows, slc_rows, win_new, ret_s, s5_s, (big or emitted)


def kernel(x_prompt, x_sample, cache_cmp, cache_slc, cache_win, state_ret, state_s5, page_table,
           rel_bias, norm_mix, w_in, s5_lambda_re, s5_lambda_im, s5_log_step, s5_b_re, s5_b_im,
           s5_c_re, s5_c_im, s5_d, s5_w_glu, cmp_pe_k, cmp_w1_k, cmp_w2_k, cmp_pe_v, cmp_w1_v,
           cmp_w2_v, branch_norm, w_out, norm_ffn, w_up, w_down, norm_final):
    win_buf = cache_win.shape[3]
    n_main = COLS['ng']
    assert n_main % LANE == 0
    w_in_b = w_in.astype(BF16)
    w = dict(
        norm_mix=norm_mix, norm_ffn=norm_ffn, branch_norm=branch_norm,
        w_in=w_in_b, w_out=w_out, w_up=w_up, w_down=w_down,
        w_gate=jnp.pad(w_in_b[:, :, n_main:], ((0, 0), (0, 0), (0, LANE - (IN_COLS - n_main)))),
        s5_w_glu=s5_w_glu.astype(BF16),
        s5_lambda_re=s5_lambda_re, s5_lambda_im=s5_lambda_im, s5_log_step=s5_log_step, s5_b_re=s5_b_re,
        s5_b_im=s5_b_im, s5_c_re=s5_c_re, s5_c_im=s5_c_im, s5_d=s5_d,
        cmp_pe_k=cmp_pe_k, cmp_w1_k=cmp_w1_k, cmp_w2_k=cmp_w2_k,
        cmp_pe_v=cmp_pe_v, cmp_w1_v=cmp_w1_v, cmp_w2_v=cmp_w2_v)
    past = (cache_cmp, cache_slc, cache_win, state_ret, state_s5, page_table)
    xp, xs = x_prompt, x_sample
    written_p, written_s = [], []
    for layer in range(DEPTH):
        xs, *entries, big = _block(xs, layer, w, rel_bias, past, win_buf, None)
        written_s.append(entries)
        xp, *entries, _ = _block(xp, layer, w, rel_bias, None, win_buf, big)
        written_p.append(entries)
    y_prompt = rmsnorm(xp.reshape(-1, D_MODEL), norm_final, F32).reshape(xp.shape)
    y_sample = rmsnorm(xs.reshape(-1, D_MODEL), norm_final, F32).reshape(xs.shape)
    stacked = lambda written, i, axis: jnp.stack([entries[i] for entries in written], axis=axis)
    return (y_prompt, y_sample,
            stacked(written_p, 0, 1), stacked(written_s, 0, 1),
            stacked(written_p, 1, 1), stacked(written_s, 1, 1),
            stacked(written_p, 2, 0), stacked(written_s, 2, 0),
            stacked(written_p, 3, 0), stacked(written_s, 3, 0),
            stacked(written_p, 4, 0), stacked(written_s, 4, 0))
```
